```python
import jax, jax.numpy as jnp
from jax import lax
import numpy as np

D_MODEL = 1024
BATCH = 8
SEQ = 4096
DEPTH = 2

GRID_W = 64
CTX_LEN = 256
HEAD_DIM = 64
D_MIX = D_MODEL
D_RWKV = D_MIX // 2
D_CONV = D_MIX // 4
D_FOURIER = D_MIX - D_RWKV - D_CONV
N_RWKV_HEADS = D_RWKV // HEAD_DIM
N_CONV_GROUPS = D_CONV // HEAD_DIM
N_FOURIER_GROUPS = D_FOURIER // HEAD_DIM
D_DECAY_LORA = 64
D_AAA_LORA = 64
D_GATE_LORA = 128
D_FF = 2816
N_EXPERTS = 8
TOP_K = 2
D_FF_EXPERT = 3584
N_DENSE = (DEPTH + 1) // 2
N_MOE = DEPTH // 2
NORM_EPS = 1e-6
GN_EPS = 64e-5
POS_BASE = 10000.0
COL_SIZES = (D_RWKV, D_RWKV, D_DECAY_LORA, D_DECAY_LORA, D_AAA_LORA, D_AAA_LORA, D_RWKV, D_GATE_LORA,
             D_CONV, D_CONV, D_CONV, D_FOURIER)
D_STATE_COLS = 2 * D_RWKV + 2 * D_DECAY_LORA + 2 * D_AAA_LORA
D_IN = 4 * D_RWKV // 2 * 1 + D_RWKV + 2 * D_DECAY_LORA + 2 * D_AAA_LORA + D_GATE_LORA + 3 * D_CONV + D_FOURIER

kernel_name = "hybrid_rwkv7_shortconv_fourier_moe_dit"


def split_cols(z, sizes):
    out, o = [], 0
    for s in sizes:
        out.append(z[..., o:o + s])
        o += s
    return out


def rms_norm(x, g):
    xf = x.astype(jnp.float32)
    y = xf * lax.rsqrt(jnp.mean(xf * xf, axis=-1, keepdims=True) + NORM_EPS)
    return (y * g.astype(jnp.float32)).astype(x.dtype)


def adaln(silu_c, w, b):
    return jnp.split(silu_c @ w + b, 6, axis=-1)


def sincos_2d(rows, cols, dim):
    quarter = dim // 4
    omega = 1.0 / (POS_BASE ** (jnp.arange(quarter, dtype=jnp.float32) / quarter))

    def axis_emb(n):
        ang = jnp.arange(n, dtype=jnp.float32)[:, None] * omega[None, :]
        return jnp.concatenate([jnp.sin(ang), jnp.cos(ang)], axis=-1)

    er, ec = axis_emb(rows), axis_emb(cols)
    emb = jnp.concatenate([jnp.broadcast_to(er[:, None, :], (rows, cols, dim // 2)),
                           jnp.broadcast_to(ec[None, :, :], (rows, cols, dim // 2))], axis=-1)
    return emb.reshape(rows * cols, dim)


def token_shift(z, reverse):
    if reverse:
        return jnp.concatenate([z[:, 1:], jnp.zeros_like(z[:, :1])], axis=1)
    return jnp.concatenate([jnp.zeros_like(z[:, :1]), z[:, :-1]], axis=1)


def to_heads(z):
    return z.reshape(z.shape[0], z.shape[1], -1, HEAD_DIM)


def rwkv_direction(k, v, wlo, alo, r, mu, mu_w, mu_a, w0, w2, a0, a2, k_k, k_a, reverse):
    f32 = jnp.float32
    lerp = lambda z, m: z + (token_shift(z, reverse) - z) * m.astype(f32)
    k = lerp(k.astype(f32), mu[0])
    v = lerp(v.astype(f32), mu[1])
    wlo = lerp(wlo.astype(f32), mu_w)
    alo = lerp(alo.astype(f32), mu_a)
    w_log = -jax.nn.softplus(-(w0.astype(f32) + jnp.tanh(wlo) @ w2.astype(f32))) - 0.5
    decay = jnp.exp(-jnp.exp(w_log))
    a = jax.nn.sigmoid(a0.astype(f32) + alo @ a2.astype(f32))
    kk = to_heads(k * k_k.astype(f32))
    kk = kk / jnp.maximum(jnp.linalg.norm(kk, axis=-1, keepdims=True), 1e-12)
    k = k * (1.0 + (a - 1.0) * k_a.astype(f32))
    rh = None if r is None else to_heads(lerp(r.astype(f32), mu[2]))
    return rh, to_heads(decay), to_heads(k), to_heads(v), -kk, kk * to_heads(a)


def wkv_scan(r, w, k, v, a, b, state0, reverse):
    def update(S, w_t, k_t, v_t, a_t, b_t):
        sa = jnp.einsum("bhvk,bhk->bhv", S, a_t)
        return S * w_t[:, :, None, :] + sa[..., None] * b_t[:, :, None, :] + v_t[..., None] * k_t[:, :, None, :]

    tm = lambda t: jnp.moveaxis(t, 1, 0)
    if r is None:
        def step_state(S, inp):
            return update(S, *inp), None
        S, _ = lax.scan(step_state, state0, tuple(map(tm, (w, k, v, a, b))), reverse=reverse)
        return None, S

    def step(S, inp):
        S = update(S, *inp[1:])
        return S, jnp.einsum("bhvk,bhk->bhv", S, inp[0])

    S, y = lax.scan(step, state0, tuple(map(tm, (r, w, k, v, a, b))), reverse=reverse)
    return jnp.moveaxis(y, 0, 1), S


def rwkv_readout(y, r, k, v, r_k, gn_w, gn_b):
    mean = jnp.mean(y, axis=-1, keepdims=True)
    var = jnp.mean(jnp.square(y - mean), axis=-1, keepdims=True)
    yn = ((y - mean) * lax.rsqrt(var + GN_EPS)).reshape(y.shape[0], y.shape[1], D_RWKV)
    yn = yn * gn_w.astype(jnp.float32) + gn_b.astype(jnp.float32)
    bonus = jnp.sum(r * k * r_k.astype(jnp.float32), axis=-1, keepdims=True) * v
    return yn + bonus.reshape(yn.shape)


def rwkv_mixer(k, v, wlos, alos, r, glo, rp, states0):
    ys, states = [], []
    for d, reverse in enumerate((False, True)):
        rh, wh, kh, vh, ah, bh = rwkv_direction(k, v, wlos[d], alos[d], r, rp["mu"][d], rp["mu_w"][d],
                                                rp["mu_a"][d], rp["w0"][d], rp["w2"][d], rp["a0"][d],
                                                rp["a2"][d], rp["k_k"], rp["k_a"], reverse)
        y, S = wkv_scan(rh, wh, kh, vh, ah, bh, states0[d], reverse)
        states.append(S)
        if r is not None:
            ys.append(rwkv_readout(y, rh, kh, vh, rp["r_k"], rp["gn_w"], rp["gn_b"]))
    if r is None:
        return None, states
    gate = jax.nn.sigmoid(glo.astype(jnp.float32)) @ rp["g2"].astype(jnp.float32)
    return ((ys[0] + ys[1]) * gate).astype(k.dtype), states


def conv3_centred(h, w):
    hp = jnp.pad(h, ((0, 0), (1, 1), (0, 0)))
    return hp[:, :-2] * w[0] + hp[:, 1:-1] * w[1] + hp[:, 2:] * w[2]


def fourier_mix(f):
    bsz, length, _ = f.shape
    fg = f.astype(jnp.float32).reshape(bsz, length, N_FOURIER_GROUPS, HEAD_DIM)
    out = jnp.fft.fft2(fg, axes=(1, 3), norm="ortho").real
    return out.reshape(bsz, length, D_FOURIER).astype(f.dtype)


def token_mixers(z, rp, conv_w, states0, on_grid):
    k, v, wlo_f, wlo_b, alo_f, alo_b, r, glo, u, gate_b, gate_c, f = split_cols(z, COL_SIZES)
    y_rwkv, states = rwkv_mixer(k, v, (wlo_f, wlo_b), (alo_f, alo_b), r, glo, rp, states0)
    hc = gate_c * u
    if on_grid:
        bsz, length, ch = hc.shape
        rows = length // GRID_W
        conv = conv3_centred(hc.reshape(bsz * rows, GRID_W, ch), conv_w).reshape(bsz, length, ch)
    else:
        conv = conv3_centred(hc, conv_w)
    y_conv = gate_b * conv
    y_four = fourier_mix(f)
    return jnp.concatenate([y_rwkv, y_conv, y_four], axis=-1), states


def swiglu(h, wg, wu, wd):
    return (jax.nn.silu(h @ wg) * (h @ wu)) @ wd


def moe_swiglu(h, router_w, router_b, wg, wu, wd):
    logits = h.astype(jnp.float32) @ router_w.astype(jnp.float32) + router_b.astype(jnp.float32)
    top_val, top_idx = lax.top_k(logits, TOP_K)
    top_p = jax.nn.softmax(top_val, axis=-1)
    gates = jnp.sum(jax.nn.one_hot(top_idx, N_EXPERTS, dtype=jnp.float32) * top_p[..., None], axis=-2)
    gates = gates.astype(h.dtype)
    out = jnp.zeros_like(h)
    for e in range(N_EXPERTS):
        out = out + gates[..., e:e + 1] * swiglu(h, wg[e], wu[e], wd[e])
    return out


def setup_inputs(seed: int = 0) -> dict:
    key = jax.random.key(seed)
    ks = iter(jax.random.split(key, 40))
    nrm = lambda shape, s: jax.random.normal(next(ks), shape, jnp.float32) * s
    uni = lambda shape, lo, hi: jax.random.uniform(next(ks), shape, jnp.float32, lo, hi)
    L = DEPTH
    return {
        "x": nrm((BATCH, SEQ, D_MODEL), 1.0),
        "c": nrm((BATCH, D_MODEL), 1.0),
        "ctx": nrm((BATCH, CTX_LEN, D_MODEL), 1.0),
        "c_ctx": nrm((D_MODEL,), 1.0),
        "ada_w": nrm((L, D_MODEL, 6 * D_MODEL), 0.5 * D_MODEL ** -0.5),
        "ada_b": nrm((L, 6 * D_MODEL), 0.02),
        "norm_g": 1.0 + nrm((L, 4, D_MODEL), 0.02),
        "w_in": nrm((L, D_MODEL, D_IN), D_MODEL ** -0.5),
        "w_out": nrm((L, D_MIX, D_MODEL), D_MIX ** -0.5),
        "rwkv_mu": uni((L, 2, 3, D_RWKV), 0.0, 1.0),
        "rwkv_mu_w": uni((L, 2, D_DECAY_LORA), 0.0, 1.0),
        "rwkv_mu_a": uni((L, 2, D_AAA_LORA), 0.0, 1.0),
        "rwkv_w0": uni((L, 2, D_RWKV), -6.0, 1.0),
        "rwkv_w2": nrm((L, 2, D_DECAY_LORA, D_RWKV), 0.1),
        "rwkv_a0": nrm((L, 2, D_RWKV), 0.5),
        "rwkv_a2": nrm((L, 2, D_AAA_LORA, D_RWKV), 0.1),
        "rwkv_g2": nrm((L, D_GATE_LORA, D_RWKV), D_GATE_LORA ** -0.5),
        "rwkv_k_k": 0.85 + nrm((L, D_RWKV), 0.05),
        "rwkv_k_a": 1.0 + nrm((L, D_RWKV), 0.05),
        "rwkv_r_k": nrm((L, N_RWKV_HEADS, HEAD_DIM), 0.1),
        "rwkv_gn_w": 1.0 + nrm((L, D_RWKV), 0.02),
        "rwkv_gn_b": nrm((L, D_RWKV), 0.02),
        "conv_w": nrm((L, 3, D_CONV), 3.0 ** -0.5),
        "ffn_w_gate": nrm((N_DENSE, D_MODEL, D_FF), D_MODEL ** -0.5),
        "ffn_w_up": nrm((N_DENSE, D_MODEL, D_FF), D_MODEL ** -0.5),
        "ffn_w_down": nrm((N_DENSE, D_FF, D_MODEL), D_FF ** -0.5),
        "router_w": nrm((N_MOE, D_MODEL, N_EXPERTS), D_MODEL ** -0.5),
        "router_b": nrm((N_MOE, N_EXPERTS), 0.01),
        "moe_w_gate": nrm((N_MOE, N_EXPERTS, D_MODEL, D_FF_EXPERT), D_MODEL ** -0.5),
        "moe_w_up": nrm((N_MOE, N_EXPERTS, D_MODEL, D_FF_EXPERT), D_MODEL ** -0.5),
        "moe_w_down": nrm((N_MOE, N_EXPERTS, D_FF_EXPERT, D_MODEL), D_FF_EXPERT ** -0.5),
    }


def reference(x, c, ctx, c_ctx, ada_w, ada_b, norm_g, w_in, w_out, rwkv_mu, rwkv_mu_w, rwkv_mu_a,
              rwkv_w0, rwkv_w2, rwkv_a0, rwkv_a2, rwkv_g2, rwkv_k_k, rwkv_k_a, rwkv_r_k, rwkv_gn_w,
              rwkv_gn_b, conv_w, ffn_w_gate, ffn_w_up, ffn_w_down, router_w, router_b, moe_w_gate,
              moe_w_up, moe_w_down):
    bsz, length, dim = x.shape
    rows = length // GRID_W
    x = x + sincos_2d(rows, GRID_W, dim).astype(x.dtype)[None]
    xc = ctx
    silu_c, silu_cc = jax.nn.silu(c), jax.nn.silu(c_ctx)
    zero_state = jnp.zeros((bsz, N_RWKV_HEADS, HEAD_DIM, HEAD_DIM), jnp.float32)

    def channel_mixer(l, h):
        if l % 2 == 0:
            i = l // 2
            return swiglu(h, ffn_w_gate[i], ffn_w_up[i], ffn_w_down[i])
        i = l // 2
        return moe_swiglu(h, router_w[i], router_b[i], moe_w_gate[i], moe_w_up[i], moe_w_down[i])

    for l in range(DEPTH):
        last = l == DEPTH - 1
        rp = {"mu": rwkv_mu[l], "mu_w": rwkv_mu_w[l], "mu_a": rwkv_mu_a[l], "w0": rwkv_w0[l],
              "w2": rwkv_w2[l], "a0": rwkv_a0[l], "a2": rwkv_a2[l], "g2": rwkv_g2[l],
              "k_k": rwkv_k_k[l], "k_a": rwkv_k_a[l], "r_k": rwkv_r_k[l], "gn_w": rwkv_gn_w[l],
              "gn_b": rwkv_gn_b[l]}
        sh1, sc1, g1, sh2, sc2, g2 = adaln(silu_c[:, None, :], ada_w[l], ada_b[l])
        csh1, csc1, cg1, csh2, csc2, cg2 = adaln(silu_cc, ada_w[l], ada_b[l])

        hc = rms_norm(xc, norm_g[l, 0]) * (1.0 + csc1) + csh1
        if last:
            kc, vc, wfc, wbc, afc, abc = split_cols(hc @ w_in[l][:, :D_STATE_COLS], COL_SIZES[:6])
            _, ctx_states = rwkv_mixer(kc, vc, (wfc, wbc), (afc, abc), None, None, rp,
                                       (zero_state, zero_state))
        else:
            yc, ctx_states = token_mixers(hc @ w_in[l], rp, conv_w[l], (zero_state, zero_state), False)
            xc = xc + cg1 * rms_norm(yc @ w_out[l], norm_g[l, 1])
            hc2 = rms_norm(xc, norm_g[l, 2]) * (1.0 + csc2) + csh2
            xc = xc + cg2 * rms_norm(channel_mixer(l, hc2), norm_g[l, 3])

        h = rms_norm(x, norm_g[l, 0]) * (1.0 + sc1) + sh1
        y, _ = token_mixers(h @ w_in[l], rp, conv_w[l], ctx_states, True)
        x = x + g1 * rms_norm(y @ w_out[l], norm_g[l, 1])
        h2 = rms_norm(x, norm_g[l, 2]) * (1.0 + sc2) + sh2
        x = x + g2 * rms_norm(channel_mixer(l, h2), norm_g[l, 3])
    return x
```

```python
import functools

import jax
import jax.numpy as jnp
from jax import lax
from jax.experimental import pallas as pl
from jax.experimental.pallas import tpu as pltpu

F32 = jnp.float32
BF16 = jnp.bfloat16

HEAD_DIM = 64
GRID_W = 64
NORM_EPS = 1e-6
GN_EPS = 64e-5
POS_BASE = 10000.0
N_EXPERTS = 8
CHUNK = 64
LANES = 128
VMEM_LIMIT = 56 * 1024 * 1024


def _params(*sem):
    return pltpu.CompilerParams(dimension_semantics=sem, vmem_limit_bytes=VMEM_LIMIT)


def _dot(a, b):
    return jnp.dot(a.astype(BF16), b.astype(BF16), preferred_element_type=F32)


def _dot_nt(a, b):
    return lax.dot_general(a.astype(BF16), b.astype(BF16), (((1,), (1,)), ((), ())),
                           preferred_element_type=F32)


def _split(x):
    hi = x.astype(BF16)
    lo = (x - hi.astype(F32)).astype(BF16)
    return hi, lo


def _dot_split_lhs(x, w):
    hi, lo = _split(x)
    return (jnp.dot(hi, w, preferred_element_type=F32) + jnp.dot(lo, w, preferred_element_type=F32))


def _dot_split_rhs(w, x):
    hi, lo = _split(x)
    return (jnp.dot(w, hi, preferred_element_type=F32) + jnp.dot(w, lo, preferred_element_type=F32))


def _rms(x, g):
    ms = jnp.mean(x * x, axis=-1, keepdims=True)
    return x * lax.rsqrt(ms + NORM_EPS) * g


def _silu(x):
    return x * jax.nn.sigmoid(x)


def _softplus(x):
    return jnp.maximum(x, 0.0) + jnp.log(1.0 + jnp.exp(-jnp.abs(x)))


def _adaln_kernel(c_ref, w_ref, b_ref, o_ref):
    s = _silu(c_ref[...])
    o_ref[0] = _dot(s, w_ref[0]) + b_ref[0]


def _adaln(cvec, ada_w, ada_b):
    nl, d, n = ada_w.shape
    rows = cvec.shape[0]
    tn = 1536
    return pl.pallas_call(
        _adaln_kernel,
        grid=(nl, n // tn),
        in_specs=[pl.BlockSpec((rows, d), lambda l, j: (0, 0)),
                  pl.BlockSpec((1, d, tn), lambda l, j: (l, 0, j)),
                  pl.BlockSpec((1, 1, tn), lambda l, j: (l, 0, j))],
        out_specs=pl.BlockSpec((1, rows, tn), lambda l, j: (l, 0, j)),
        out_shape=jax.ShapeDtypeStruct((nl, rows, n), F32),
        compiler_params=_params("arbitrary", "arbitrary"),
        name="adaln",
    )(cvec, ada_w, ada_b.reshape(nl, 1, n))


N_RWKV_COLS = 1920
N_CONV_COLS = 768
N_FOUR_COLS = 256


def _inproj_kernel(add_pos, *refs):
    if add_pos:
        x_ref, pos_ref, g_ref, sc_ref, sh_ref, w_ref, zr_ref, zc_ref, zf_ref, xp_ref = refs
        x = x_ref[0] + pos_ref[...]
        xp_ref[0] = x
    else:
        x_ref, g_ref, sc_ref, sh_ref, w_ref, zr_ref, zc_ref, zf_ref = refs
        x = x_ref[0]
    h = (_rms(x, g_ref[...]) * (1.0 + sc_ref[0]) + sh_ref[0]).astype(BF16)
    a, b = N_RWKV_COLS, N_RWKV_COLS + N_CONV_COLS
    zr_ref[0] = jnp.dot(h, w_ref[:, :a], preferred_element_type=F32)
    zc_ref[0] = jnp.dot(h, w_ref[:, a:b], preferred_element_type=F32)
    zf_ref[0] = jnp.dot(h, w_ref[:, b:], preferred_element_type=F32)


def _inproj(x, pos, g, sc, sh, w, tm):
    bsz, length, d = x.shape
    n = w.shape[1]
    add_pos = pos is not None
    row = lambda b, i: (b, i, 0)
    in_specs = [pl.BlockSpec((1, tm, d), row)]
    args = [x]
    if add_pos:
        in_specs.append(pl.BlockSpec((tm, d), lambda b, i: (i, 0)))
        args.append(pos)
    in_specs += [pl.BlockSpec((1, d), lambda b, i: (0, 0)),
                 pl.BlockSpec((1, 1, d), lambda b, i: (b, 0, 0)),
                 pl.BlockSpec((1, 1, d), lambda b, i: (b, 0, 0)),
                 pl.BlockSpec((d, n), lambda b, i: (0, 0))]
    args += [g, sc, sh, w]
    out_shape = [jax.ShapeDtypeStruct((bsz, length, N_RWKV_COLS), F32),
                 jax.ShapeDtypeStruct((bsz, length, N_CONV_COLS), F32),
                 jax.ShapeDtypeStruct((bsz, length, N_FOUR_COLS), F32)]
    out_specs = [pl.BlockSpec((1, tm, N_RWKV_COLS), row),
                 pl.BlockSpec((1, tm, N_CONV_COLS), row),
                 pl.BlockSpec((1, tm, N_FOUR_COLS), row)]
    if add_pos:
        out_shape.append(jax.ShapeDtypeStruct((bsz, length, d), F32))
        out_specs.append(pl.BlockSpec((1, tm, d), row))
    return pl.pallas_call(
        functools.partial(_inproj_kernel, add_pos),
        grid=(bsz, length // tm),
        in_specs=in_specs, out_specs=out_specs, out_shape=out_shape,
        compiler_params=_params("parallel", "parallel"),
        name="inproj",
    )(*args)


HALO = 8


def _prep_kernel(reverse, tm, z_ref, halo_ref, mu_ref, muw_ref, mua_ref, w0_ref, w2_ref, a0_ref, a2_ref,
                 kk_ref, ka_ref, bd_ref, r_ref, lw_ref, k_ref, v_ref, a_ref, b_ref):
    i = pl.program_id(1)
    last = pl.num_programs(1) - 1
    z = z_ref[0]
    rows = lax.broadcasted_iota(jnp.int32, (tm, 1), 0)
    if reverse:
        edge = jnp.where(i == last, 0.0, 1.0)
        hrow = halo_ref[0, 0:1, :] * edge
        zs = jnp.where(rows == tm - 1, hrow, pltpu.roll(z, tm - 1, 0))
    else:
        edge = jnp.where(i == 0, 0.0, 1.0)
        hrow = halo_ref[0, HALO - 1:HALO, :] * edge
        zs = jnp.where(rows == 0, hrow, pltpu.roll(z, 1, 0))
    dz = zs - z
    k = z[:, 0:512] + dz[:, 0:512] * mu_ref[0:1, :]
    v = z[:, 512:1024] + dz[:, 512:1024] * mu_ref[1:2, :]
    r = z[:, 1280:1792] + dz[:, 1280:1792] * mu_ref[2:3, :]
    wl = z[:, 1024:1152] + dz[:, 1024:1152] * muw_ref[...]
    al = z[:, 1152:1280] + dz[:, 1152:1280] * mua_ref[...]
    w_log = -_softplus(-(w0_ref[...] + _dot(jnp.tanh(wl), w2_ref[...]))) - 0.5
    lw = -jnp.exp(w_log)
    a = jax.nn.sigmoid(a0_ref[...] + _dot(al, a2_ref[...]))
    kk = k * kk_ref[...]
    ss = _dot_split_lhs(kk * kk, bd_ref[...])
    kk = kk / jnp.maximum(jnp.sqrt(ss), 1e-12)
    r_ref[0] = r
    lw_ref[0] = lw
    k_ref[0] = k * (1.0 + (a - 1.0) * ka_ref[...])
    v_ref[0] = v
    a_ref[0] = -kk
    b_ref[0] = kk * a


def _prep(zr, reverse, p, tm):
    bsz, length, ncol = zr.shape
    nt = length // tm
    hb = tm // HALO
    nhb = length // HALO
    if reverse:
        halo_map = lambda b, i: (b, jnp.minimum((i + 1) * hb, nhb - 1), 0)
    else:
        halo_map = lambda b, i: (b, jnp.maximum(i * hb - 1, 0), 0)
    full = lambda shape: pl.BlockSpec(shape, lambda b, i: (0,) * len(shape))
    row = lambda b, i: (b, i, 0)
    out = jax.ShapeDtypeStruct((bsz, length, 512), F32)
    return pl.pallas_call(
        functools.partial(_prep_kernel, reverse, tm),
        grid=(bsz, nt),
        in_specs=[pl.BlockSpec((1, tm, ncol), row), pl.BlockSpec((1, HALO, ncol), halo_map),
                  full((3, 512)), full((1, 128)), full((1, 128)), full((1, 512)), full((128, 512)),
                  full((1, 512)), full((128, 512)), full((1, 512)), full((1, 512)), full((512, 512))],
        out_specs=[pl.BlockSpec((1, tm, 512), row)] * 6,
        out_shape=[out] * 6,
        compiler_params=_params("parallel", "parallel"),
        name="rwkv_prep_bwd" if reverse else "rwkv_prep_fwd",
    )(zr, zr, p["mu"], p["mu_w"], p["mu_a"], p["w0"], p["w2"], p["a0"], p["a2"], p["k_k"], p["k_a"], p["bd"])


SOLVE_BLOCK = 16


def _dot3(x, y):
    xh, xl = _split(x)
    yh, yl = _split(y)
    d = lambda p, q: jnp.dot(p, q, preferred_element_type=F32)
    return d(xh, yh) + d(xh, yl) + d(xl, yh)


def _unit_lower_solve(a, x, diag_blk, eye):
    a_d = jnp.where(diag_blk, a, 0.0)
    a_o = a - a_d
    dinv = eye + a_d
    apow = a_d
    steps = SOLVE_BLOCK.bit_length() - 1
    for _ in range(steps - 1):
        apow = _dot3(apow, apow)
        dinv = dinv + _dot3(apow, dinv)
    b = _dot(dinv, a_o)
    x = _dot(dinv, x)
    x = x + _dot(b, x)
    return x + _dot(_dot(b, b), x)


def _scan_kernel(reverse, r_ref, lw_ref, k_ref, v_ref, a_ref, b_ref, s0_ref, rk_ref, gw_ref, gb_ref, bd_ref,
                 y_ref, st_ref):
    c = pl.program_id(1)

    @pl.when(c == 0)
    def _():
        st_ref[...] = s0_ref[...]

    n = CHUNK
    t_i = lax.broadcasted_iota(jnp.int32, (n, n), 0)
    s_i = lax.broadcasted_iota(jnp.int32, (n, n), 1)
    incl = (s_i >= t_i) if reverse else (s_i <= t_i)
    blk_shift = SOLVE_BLOCK.bit_length() - 1
    diag_blk = (s_i >> blk_shift) == (t_i >> blk_shift)
    eye = jnp.where(s_i == t_i, 1.0, 0.0)
    lane2 =lax.broadcasted_iota(jnp.int32, (n, LANES), 1)
    t_2 = lax.broadcasted_iota(jnp.int32, (n, LANES), 0)
    s_2 = lane2 & (n - 1)
    incl2 = (s_2 >= t_2) if reverse else (s_2 <= t_2)
    strict2 = (s_2 > t_2) if reverse else (s_2 < t_2)
    first = lane2 < HEAD_DIM
    r_bd = lax.broadcasted_iota(jnp.int32, (LANES, LANES), 0) < HEAD_DIM
    c_bd = lax.broadcasted_iota(jnp.int32, (LANES, LANES), 1) < HEAD_DIM
    blockdiag = r_bd == c_bd

    r = r_ref[0]
    lw = lw_ref[0]
    k = k_ref[0]
    v = v_ref[0]
    a = a_ref[0]
    b = b_ref[0]

    cum = _dot_split_rhs(jnp.where(incl, 1.0, 0.0).astype(BF16), lw)
    tot = cum[0:1, :] if reverse else cum[n - 1:n, :]
    w_in = jnp.exp(cum)
    w_ex = jnp.exp(cum - lw)
    w_inv = jnp.exp(-cum)
    w_rem = jnp.exp(tot - cum)
    w_tot = jnp.exp(tot)
    rt = r * w_in
    at = a * w_ex
    bt = b * w_inv
    kt = k * w_inv
    bh = b * w_rem
    kh = k * w_rem

    ys = []
    for g in range(r.shape[1] // LANES):
        sl = slice(g * LANES, (g + 1) * LANES)
        at_g, rt_g, v_g = at[:, sl], rt[:, sl], v[:, sl]
        at0 = jnp.where(first, at_g, 0.0)
        at1 = jnp.where(first, 0.0, at_g)
        rt0 = jnp.where(first, rt_g, 0.0)
        rt1 = jnp.where(first, 0.0, rt_g)
        lhs = jnp.concatenate([at0, rt0, at1, rt1], axis=0)
        rhs = jnp.concatenate([bt[:, sl], kt[:, sl]], axis=0)
        s = _dot_nt(lhs, rhs)
        sa = (jnp.where(strict2, s[0:n], 0.0), jnp.where(strict2, s[2 * n:3 * n], 0.0))
        sr = (jnp.where(incl2, s[n:2 * n], 0.0), jnp.where(incl2, s[3 * n:4 * n], 0.0))
        vv = jnp.concatenate([v_g, v_g], axis=0)
        pq = []
        for j, at_j in enumerate((at0, at1)):
            akv = _dot(jnp.where(first, 0.0, sa[j]), vv)
            x = jnp.concatenate([at_j, akv], axis=1)
            pq.append(_unit_lower_solve(sa[j][:, 0:n], x, diag_blk, eye))
        t_p = st_ref[0, g]
        p01 = jnp.concatenate([pq[0][:, :LANES], pq[1][:, :LANES]], axis=0)
        pt = _dot_split_rhs(p01.astype(BF16), t_p)
        u = jnp.where(first, pt[0:n] + pq[0][:, LANES:], pt[n:] + pq[1][:, LANES:])
        uv = jnp.concatenate([u, v_g], axis=0)
        rtt = _dot_split_rhs(jnp.concatenate([rt0, rt1], axis=0).astype(BF16), t_p)
        y = jnp.where(first, rtt[0:n] + _dot(sr[0], uv), rtt[n:] + _dot(sr[1], uv))
        ys.append(y)
        bk = jnp.concatenate([bh[:, sl], kh[:, sl]], axis=0)
        t_new = _dot(bk.T, uv)
        w_col = jnp.broadcast_to(w_tot[:, sl], (LANES, LANES)).T
        st_ref[0, g] = jnp.where(blockdiag, w_col * t_p + t_new, 0.0)

    y = jnp.concatenate(ys, axis=1)
    bd = bd_ref[...]
    inv = 1.0 / HEAD_DIM
    mean = _dot_split_lhs(y, bd) * inv
    d = y - mean
    var = _dot_split_lhs(d * d, bd) * inv
    yn = d * lax.rsqrt(var + GN_EPS) * gw_ref[...] + gb_ref[...]
    bonus = _dot_split_lhs(r * k * rk_ref[...], bd) * v
    y_ref[0] = yn + bonus


def _scan(seq, state0, reverse, p):
    r = seq[0]
    bsz, length, width = r.shape
    nc = length // CHUNK
    npairs = width // LANES
    cmap = (lambda b, c: (b, nc - 1 - c, 0)) if reverse else (lambda b, c: (b, c, 0))
    full = lambda shape: pl.BlockSpec(shape, lambda b, c: (0,) * len(shape))
    st_spec = pl.BlockSpec((1, npairs, LANES, LANES), lambda b, c: (b, 0, 0, 0))
    return pl.pallas_call(
        functools.partial(_scan_kernel, reverse),
        grid=(bsz, nc),
        in_specs=[pl.BlockSpec((1, CHUNK, width), cmap)] * 6 + [st_spec] +
                 [full((1, width))] * 3 + [full((width, width))],
        out_specs=[pl.BlockSpec((1, CHUNK, width), cmap), st_spec],
        out_shape=[jax.ShapeDtypeStruct((bsz, length, width), F32),
                   jax.ShapeDtypeStruct((bsz, npairs, LANES, LANES), F32)],
        compiler_params=_params("parallel", "arbitrary"),
        name="wkv_scan_bwd" if reverse else "wkv_scan_fwd",
    )(*seq, state0, p["r_k"], p["gn_w"], p["gn_b"], p["bd"])


def _mix_kernel(period, tm, yf_ref, yb_ref, glo_ref, zc_ref, g2_ref, cw_ref, o_ref):
    gate = _dot(jax.nn.sigmoid(glo_ref[0]), g2_ref[...])
    yr = (yf_ref[0] + yb_ref[0]) * gate
    zc = zc_ref[0]
    u, gate_b, gate_c = zc[:, 0:256], zc[:, 256:512], zc[:, 512:768]
    hc = gate_c * u
    pos = lax.broadcasted_iota(jnp.int32, (tm, 1), 0) % period
    prev = jnp.where(pos == 0, 0.0, pltpu.roll(hc, 1, 0))
    nxt = jnp.where(pos == period - 1, 0.0, pltpu.roll(hc, tm - 1, 0))
    conv = prev * cw_ref[0:1, :] + hc * cw_ref[1:2, :] + nxt * cw_ref[2:3, :]
    o_ref[0, :, 0:512] = yr.astype(BF16)
    o_ref[0, :, 512:768] = (gate_b * conv).astype(BF16)


def _mix(ys_f, ys_b, zr, zc, g2, conv_w, period, tm):
    bsz, length, _ = zc.shape
    row = lambda b, i: (b, i, 0)
    glo_block = (N_RWKV_COLS - 128) // 128
    return pl.pallas_call(
        functools.partial(_mix_kernel, period, tm),
        grid=(bsz, length // tm),
        in_specs=[pl.BlockSpec((1, tm, 512), row), pl.BlockSpec((1, tm, 512), row),
                  pl.BlockSpec((1, tm, 128), lambda b, i: (b, i, glo_block)),
                  pl.BlockSpec((1, tm, N_CONV_COLS), row),
                  pl.BlockSpec((128, 512), lambda b, i: (0, 0)),
                  pl.BlockSpec((3, 256), lambda b, i: (0, 0))],
        out_specs=pl.BlockSpec((1, tm, 768), row),
        out_shape=jax.ShapeDtypeStruct((bsz, length, 768), BF16),
        compiler_params=_params("parallel", "parallel"),
        name="gate_conv",
    )(ys_f, ys_b, zr, zc, g2, conv_w)


def _chan_dft_kernel(f_ref, w_ref, o_ref):
    xcs = jnp.dot(f_ref[0].astype(BF16), w_ref[...], preferred_element_type=F32)
    o_ref[0, 0] = xcs[:, 0:256].astype(BF16)
    o_ref[0, 1] = xcs[:, 256:512].astype(BF16)


def _pos_dft_kernel(nb, cs_ref, x_ref, o_ref, acc_ref):
    kstep = pl.program_id(1)

    @pl.when(kstep == 0)
    def _():
        acc_ref[...] = jnp.zeros_like(acc_ref)

    cs = cs_ref[...]
    for b in range(nb):
        acc_ref[b] += jnp.dot(cs, x_ref[b], preferred_element_type=F32)

    @pl.when(kstep == pl.num_programs(1) - 1)
    def _():
        o_ref[...] = acc_ref[...].astype(BF16)


def _fourier(zf, chan_w, pos_cs, tm, tk):
    bsz, length, ch = zf.shape
    tr = min(512, length)
    xcs = pl.pallas_call(
        _chan_dft_kernel,
        grid=(bsz, length // tr),
        in_specs=[pl.BlockSpec((1, tr, ch), lambda b, i: (b, i, 0)),
                  pl.BlockSpec((ch, 2 * ch), lambda b, i: (0, 0))],
        out_specs=pl.BlockSpec((1, 2, tr, ch), lambda b, i: (b, 0, i, 0)),
        out_shape=jax.ShapeDtypeStruct((bsz, 2, length, ch), BF16),
        compiler_params=_params("parallel", "parallel"),
        name="chan_dft",
    )(zf, chan_w)
    xcs = xcs.reshape(bsz, 2 * length, ch)
    return pl.pallas_call(
        functools.partial(_pos_dft_kernel, bsz),
        grid=(length // tm, 2 * length // tk),
        in_specs=[pl.BlockSpec((tm, tk), lambda i, kk: (i, kk)),
                  pl.BlockSpec((bsz, tk, ch), lambda i, kk: (0, kk, 0))],
        out_specs=pl.BlockSpec((bsz, tm, ch), lambda i, kk: (0, i, 0)),
        out_shape=jax.ShapeDtypeStruct((bsz, length, ch), BF16),
        scratch_shapes=[pltpu.VMEM((bsz, tm, ch), F32)],
        compiler_params=_params("parallel", "arbitrary"),
        name="pos_dft",
    )(pos_cs, xcs)


def _dft_tables(length):
    m = jnp.arange(length, dtype=jnp.int32)[None, :]
    l1 = jnp.arange(length // 64, dtype=jnp.int32)[:, None] * 64
    l2 = jnp.arange(64, dtype=jnp.int32)[:, None]
    ang = lambda l: ((l * m) % length).astype(F32) * (2.0 * jnp.pi / length)
    c1, s1, c2, s2 = jnp.cos(ang(l1)), jnp.sin(ang(l1)), jnp.cos(ang(l2)), jnp.sin(ang(l2))
    scale = 1.0 / jnp.sqrt(jnp.float32(length))
    cos = (c1[:, None, :] * c2[None] - s1[:, None, :] * s2[None]).reshape(length, length) * scale
    sin = (s1[:, None, :] * c2[None] + c1[:, None, :] * s2[None]).reshape(length, length) * scale
    return jnp.concatenate([cos, sin], axis=1).astype(BF16)


def _chan_dft_weights(ngroups):
    j = jnp.arange(HEAD_DIM, dtype=jnp.int32)
    ang = ((j[:, None] * j[None, :]) % HEAD_DIM).astype(F32) * (2.0 * jnp.pi / HEAD_DIM)
    eye = jnp.eye(ngroups, dtype=F32)
    scale = 1.0 / jnp.sqrt(jnp.float32(HEAD_DIM))
    cc = jnp.kron(eye, jnp.cos(ang) * scale)
    sc = jnp.kron(eye, jnp.sin(ang) * scale)
    return jnp.concatenate([cc, -sc], axis=1).astype(BF16)


def _outproj_kernel(yrc_ref, yf_ref, w_ref, x_ref, gn1_ref, gate_ref, gn2_ref, sc_ref, sh_ref, xo_ref, h_ref):
    nrc = yrc_ref.shape[2]
    o = (jnp.dot(yrc_ref[0], w_ref[0:nrc, :], preferred_element_type=F32) +
         jnp.dot(yf_ref[0], w_ref[nrc:, :], preferred_element_type=F32))
    xn = x_ref[0] + gate_ref[0] * _rms(o, gn1_ref[...])
    xo_ref[0] = xn
    h_ref[0] = (_rms(xn, gn2_ref[...]) * (1.0 + sc_ref[0]) + sh_ref[0]).astype(BF16)


def _outproj(yrc, yf, w, x, gn1, gate, gn2, sc, sh, tm):
    bsz, length, d = x.shape
    row = lambda b, i: (b, i, 0)
    vec = pl.BlockSpec((1, d), lambda b, i: (0, 0))
    mod = pl.BlockSpec((1, 1, d), lambda b, i: (b, 0, 0))
    return pl.pallas_call(
        _outproj_kernel,
        grid=(bsz, length // tm),
        in_specs=[pl.BlockSpec((1, tm, yrc.shape[2]), row), pl.BlockSpec((1, tm, yf.shape[2]), row),
                  pl.BlockSpec(w.shape, lambda b, i: (0, 0)), pl.BlockSpec((1, tm, d), row),
                  vec, mod, vec, mod, mod],
        out_specs=[pl.BlockSpec((1, tm, d), row), pl.BlockSpec((1, tm, d), row)],
        out_shape=[jax.ShapeDtypeStruct((bsz, length, d), F32), jax.ShapeDtypeStruct((bsz, length, d), BF16)],
        compiler_params=_params("parallel", "parallel"),
        name="outproj",
    )(yrc, yf, w, x, gn1, gate, gn2, sc, sh)


def _ffn_kernel(h_ref, wg_ref, wu_ref, wd_ref, x_ref, gn_ref, gate_ref, o_ref, acc_ref):
    j = pl.program_id(2)

    @pl.when(j == 0)
    def _():
        acc_ref[...] = jnp.zeros_like(acc_ref)

    h = h_ref[0]
    act = _silu(jnp.dot(h, wg_ref[...], preferred_element_type=F32)) * jnp.dot(h, wu_ref[...],
                                                                              preferred_element_type=F32)
    acc_ref[...] += jnp.dot(act.astype(BF16), wd_ref[...], preferred_element_type=F32)

    @pl.when(j == pl.num_programs(2) - 1)
    def _():
        o_ref[0] = x_ref[0] + gate_ref[0] * _rms(acc_ref[...], gn_ref[...])


def _ffn(h, wg, wu, wd, x, gn, gate, tm, tf):
    bsz, length, d = x.shape
    ff = wg.shape[1]
    row = lambda b, i, j: (b, i, 0)
    return pl.pallas_call(
        _ffn_kernel,
        grid=(bsz, length // tm, ff // tf),
        in_specs=[pl.BlockSpec((1, tm, d), row),
                  pl.BlockSpec((d, tf), lambda b, i, j: (0, j)),
                  pl.BlockSpec((d, tf), lambda b, i, j: (0, j)),
                  pl.BlockSpec((tf, d), lambda b, i, j: (j, 0)),
                  pl.BlockSpec((1, tm, d), row),
                  pl.BlockSpec((1, d), lambda b, i, j: (0, 0)),
                  pl.BlockSpec((1, 1, d), lambda b, i, j: (b, 0, 0))],
        out_specs=pl.BlockSpec((1, tm, d), row),
        out_shape=jax.ShapeDtypeStruct((bsz, length, d), F32),
        scratch_shapes=[pltpu.VMEM((tm, d), F32)],
        compiler_params=_params("parallel", "parallel", "arbitrary"),
        name="ffn",
    )(h, wg, wu, wd, x, gn, gate)


def _router_kernel(x_ref, gn_ref, sc_ref, sh_ref, w_ref, b_ref, o_ref):
    h = _rms(x_ref[0], gn_ref[...]) * (1.0 + sc_ref[0]) + sh_ref[0]
    logits = jnp.dot(h, w_ref[...], preferred_element_type=F32, precision=lax.Precision.HIGHEST) + b_ref[...]
    ne = float(logits.shape[1])
    idx = lax.broadcasted_iota(jnp.int32, logits.shape, 1).astype(F32)
    m1 = jnp.max(logits, axis=1, keepdims=True)
    i1 = jnp.min(jnp.where(logits == m1, idx, ne), axis=1, keepdims=True)
    rest = jnp.where(idx == i1, -jnp.inf, logits)
    m2 = jnp.max(rest, axis=1, keepdims=True)
    i2 = jnp.min(jnp.where(rest == m2, idx, ne), axis=1, keepdims=True)
    e2 = jnp.exp(m2 - m1)
    p1 = 1.0 / (1.0 + e2)
    p2 = e2 / (1.0 + e2)
    o_ref[0] = jnp.where(idx == i1, p1, 0.0) + jnp.where(idx == i2, p2, 0.0)


def _router(x, gn, sc, sh, w, b, tm):
    bsz, length, d = x.shape
    pad = LANES - w.shape[1]
    w = jnp.pad(w, ((0, 0), (0, pad)))
    b = jnp.pad(b, ((0, 0), (0, pad)), constant_values=-1e30)
    ne = LANES
    row = lambda bb, i: (bb, i, 0)
    vec = pl.BlockSpec((1, d), lambda bb, i: (0, 0))
    mod = pl.BlockSpec((1, 1, d), lambda bb, i: (bb, 0, 0))
    return pl.pallas_call(
        _router_kernel,
        grid=(bsz, length // tm),
        in_specs=[pl.BlockSpec((1, tm, d), row), vec, mod, mod,
                  pl.BlockSpec((d, ne), lambda bb, i: (0, 0)), pl.BlockSpec((1, ne), lambda bb, i: (0, 0))],
        out_specs=pl.BlockSpec((1, tm, ne), row),
        out_shape=jax.ShapeDtypeStruct((bsz, length, ne), F32),
        compiler_params=_params("parallel", "parallel"),
        name="router",
    )(x, gn, sc, sh, w, b)


def _moe_kernel(h_ref, gates_ref, wg_ref, wu_ref, wd_ref, x_ref, gn_ref, gate_ref, o_ref, acc_ref):
    e = pl.program_id(2)
    j = pl.program_id(3)

    @pl.when((e == 0) & (j == 0))
    def _():
        acc_ref[...] = jnp.zeros_like(acc_ref)

    gates = gates_ref[0]
    sel = lax.broadcasted_iota(jnp.int32, gates.shape, 1) == e
    ge = jnp.sum(jnp.where(sel, gates, 0.0), axis=1, keepdims=True)
    h = h_ref[0]
    act = _silu(jnp.dot(h, wg_ref[0], preferred_element_type=F32)) * jnp.dot(h, wu_ref[0],
                                                                            preferred_element_type=F32)
    acc_ref[...] += jnp.dot((act * ge).astype(BF16), wd_ref[0], preferred_element_type=F32)

    @pl.when((e == pl.num_programs(2) - 1) & (j == pl.num_programs(3) - 1))
    def _():
        o_ref[0] = x_ref[0] + gate_ref[0] * _rms(acc_ref[...], gn_ref[...])


def _moe(h, gates, wg, wu, wd, x, gn, gate, tm, tf):
    bsz, length, d = x.shape
    ne, _, ff = wg.shape
    row = lambda b, i, e, j: (b, i, 0)
    return pl.pallas_call(
        _moe_kernel,
        grid=(bsz, length // tm, ne, ff // tf),
        in_specs=[pl.BlockSpec((1, tm, d), row),
                  pl.BlockSpec((1, tm, gates.shape[2]), row),
                  pl.BlockSpec((1, d, tf), lambda b, i, e, j: (e, 0, j)),
                  pl.BlockSpec((1, d, tf), lambda b, i, e, j: (e, 0, j)),
                  pl.BlockSpec((1, tf, d), lambda b, i, e, j: (e, j, 0)),
                  pl.BlockSpec((1, tm, d), row),
                  pl.BlockSpec((1, d), lambda b, i, e, j: (0, 0)),
                  pl.BlockSpec((1, 1, d), lambda b, i, e, j: (b, 0, 0))],
        out_specs=pl.BlockSpec((1, tm, d), row),
        out_shape=jax.ShapeDtypeStruct((bsz, length, d), F32),
        scratch_shapes=[pltpu.VMEM((tm, d), F32)],
        compiler_params=_params("parallel", "parallel", "arbitrary", "arbitrary"),
        name="moe",
    )(h, gates, wg, wu, wd, x, gn, gate)


def _sincos_2d(rows, cols, dim):
    quarter = dim // 4
    omega = 1.0 / (POS_BASE ** (jnp.arange(quarter, dtype=F32) / quarter))

    def axis_emb(n):
        ang = jnp.arange(n, dtype=F32)[:, None] * omega[None, :]
        return jnp.concatenate([jnp.sin(ang), jnp.cos(ang)], axis=-1)

    er, ec = axis_emb(rows), axis_emb(cols)
    emb = jnp.concatenate([jnp.broadcast_to(er[:, None, :], (rows, cols, dim // 2)),
                           jnp.broadcast_to(ec[None, :, :], (rows, cols, dim // 2))], axis=-1)
    return emb.reshape(rows * cols, dim)


def _head_block_ones(width):
    h = jnp.arange(width, dtype=jnp.int32) // HEAD_DIM
    return (h[:, None] == h[None, :]).astype(BF16)


def _pad_lora(w2, d):
    z = jnp.zeros_like(w2[0])
    return jnp.concatenate([w2[0], z], axis=0) if d == 0 else jnp.concatenate([z, w2[1]], axis=0)


def _tile(length, pref):
    return pref if length % pref == 0 else length


def kernel(x, c, ctx, c_ctx, ada_w, ada_b, norm_g, w_in, w_out, rwkv_mu, rwkv_mu_w, rwkv_mu_a, rwkv_w0, rwkv_w2,
           rwkv_a0, rwkv_a2, rwkv_g2, rwkv_k_k, rwkv_k_a, rwkv_r_k, rwkv_gn_w, rwkv_gn_b, conv_w, ffn_w_gate,
           ffn_w_up, ffn_w_down, router_w, router_b, moe_w_gate, moe_w_up, moe_w_down):
    bsz, length, dim = x.shape
    ctx_len = ctx.shape[1]
    depth = ada_w.shape[0]
    d_rwkv = rwkv_w0.shape[-1]

    pos = _sincos_2d(length // GRID_W, GRID_W, dim).astype(x.dtype)
    bd = _head_block_ones(d_rwkv)
    chan_w = _chan_dft_weights(N_FOUR_COLS // HEAD_DIM)
    dft_lat = _dft_tables(length)
    dft_ctx = _dft_tables(ctx_len)

    cvec = jnp.zeros((16, dim), F32).at[:bsz].set(c).at[bsz].set(c_ctx)
    mod = _adaln(cvec, ada_w, ada_b)
    zero_state = jnp.zeros((bsz, d_rwkv // LANES, LANES, LANES), F32)
    xc = ctx

    def mods(l, lo, hi):
        return [mod[l, lo:hi, i * dim:(i + 1) * dim][:, None, :] for i in range(6)]

    def mixers(l, zr, zc, zf, states0, period, dft, need_y=True):
        tm = _tile(zr.shape[1], 512)
        ys, states = [], []
        for d, reverse in enumerate((False, True)):
            p = {"mu": rwkv_mu[l, d], "mu_w": rwkv_mu_w[l].reshape(1, -1), "mu_a": rwkv_mu_a[l].reshape(1, -1),
                 "w0": rwkv_w0[l, d][None], "w2": _pad_lora(rwkv_w2[l], d), "a0": rwkv_a0[l, d][None],
                 "a2": _pad_lora(rwkv_a2[l], d), "k_k": rwkv_k_k[l][None], "k_a": rwkv_k_a[l][None],
                 "r_k": rwkv_r_k[l].reshape(1, -1), "gn_w": rwkv_gn_w[l][None], "gn_b": rwkv_gn_b[l][None],
                 "bd": bd}
            seq = _prep(zr, reverse, p, tm)
            y, st = _scan(seq, states0[d], reverse, p)
            ys.append(y)
            states.append(st)
        if not need_y:
            return None, None, states
        yrc = _mix(ys[0], ys[1], zr, zc, rwkv_g2[l], conv_w[l], period, tm)
        yf = _fourier(zf, chan_w, dft, _tile(zf.shape[1], 512), 512)
        return yrc, yf, states

    def channel_mixer(l, h2, xn, gn3, gate2, sc2, sh2, gn2):
        tm = _tile(xn.shape[1], 512)
        i = l // 2
        if l % 2 == 0:
            return _ffn(h2, ffn_w_gate[i].astype(BF16), ffn_w_up[i].astype(BF16), ffn_w_down[i].astype(BF16),
                        xn, gn3, gate2, tm, 1408)
        gates = _router(xn, gn2, sc2, sh2, router_w[i], router_b[i][None], tm)
        return _moe(h2, gates, moe_w_gate[i].astype(BF16), moe_w_up[i].astype(BF16), moe_w_down[i].astype(BF16),
                    xn, gn3, gate2, tm, 896)

    for l in range(depth):
        last = l == depth - 1
        w_in_l = w_in[l].astype(BF16)
        w_out_l = w_out[l].astype(BF16)
        gn = [norm_g[l, i][None] for i in range(4)]
        sh1, sc1, g1, sh2, sc2, g2 = mods(l, 0, bsz)
        csh1, csc1, cg1, csh2, csc2, cg2 = mods(l, bsz, bsz + 1)

        xc_flat = xc.reshape(1, bsz * ctx_len, dim)
        tmc = _tile(bsz * ctx_len, 512)
        zr, zc, zf = _inproj(xc_flat, None, gn[0], csc1, csh1, w_in_l, tmc)
        unflat = lambda t: t.reshape(bsz, ctx_len, t.shape[-1])
        yrc, yf, ctx_states = mixers(l, unflat(zr), unflat(zc), unflat(zf), (zero_state, zero_state), ctx_len,
                                     dft_ctx, need_y=not last)
        if not last:
            flat = lambda t: t.reshape(1, bsz * ctx_len, t.shape[-1])
            xcn, hc2 = _outproj(flat(yrc), flat(yf), w_out_l, xc_flat, gn[1], cg1, gn[2], csc2, csh2, tmc)
            xc = channel_mixer(l, hc2, xcn, gn[3], cg2, csc2, csh2, gn[2]).reshape(bsz, ctx_len, dim)

        tm = _tile(length, 256)
        if l == 0:
            zr, zc, zf, x = _inproj(x, pos, gn[0], sc1, sh1, w_in_l, tm)
        else:
            zr, zc, zf = _inproj(x, None, gn[0], sc1, sh1, w_in_l, tm)
        yrc, yf, _ = mixers(l, zr, zc, zf, ctx_states, GRID_W, dft_lat)
        xn, h2 = _outproj(yrc, yf, w_out_l, x, gn[1], g1, gn[2], sc2, sh2, _tile(length, 512))
        x = channel_mixer(l, h2, xn, gn[3], g2, sc2, sh2, gn[2])
    return x
```

```python
import functools

import jax
import jax.numpy as jnp
from jax import lax
from jax.experimental import pallas as pl
from jax.experimental.pallas import tpu as pltpu

F32 = jnp.float32
BF16 = jnp.bfloat16

HEAD_DIM = 64
GRID_W = 64
NORM_EPS = 1e-6
GN_EPS = 64e-5
POS_BASE = 10000.0
N_EXPERTS = 8
CHUNK = 64
LANES = 128
VMEM_LIMIT = 56 * 1024 * 1024


def _params(*sem):
    return pltpu.CompilerParams(dimension_semantics=sem, vmem_limit_bytes=VMEM_LIMIT)


def _dot(a, b):
    return jnp.dot(a.astype(BF16), b.astype(BF16), preferred_element_type=F32)


def _dot_nt(a, b):
    return lax.dot_general(a.astype(BF16), b.astype(BF16), (((1,), (1,)), ((), ())),
                           preferred_element_type=F32)


def _split(x):
    hi = x.astype(BF16)
    lo = (x - hi.astype(F32)).astype(BF16)
    return hi, lo


def _dot_split_lhs(x, w):
    hi, lo = _split(x)
    return (jnp.dot(hi, w, preferred_element_type=F32) + jnp.dot(lo, w, preferred_element_type=F32))


def _dot_split_rhs(w, x):
    hi, lo = _split(x)
    return (jnp.dot(w, hi, preferred_element_type=F32) + jnp.dot(w, lo, preferred_element_type=F32))


def _rms(x, g):
    ms = jnp.mean(x * x, axis=-1, keepdims=True)
    return x * lax.rsqrt(ms + NORM_EPS) * g


def _silu(x):
    return x * jax.nn.sigmoid(x)


def _softplus(x):
    return jnp.maximum(x, 0.0) + jnp.log(1.0 + jnp.exp(-jnp.abs(x)))


def _adaln_kernel(c_ref, w_ref, b_ref, o_ref):
    s = _silu(c_ref[...])
    o_ref[0] = _dot(s, w_ref[0]) + b_ref[0]


def _adaln(cvec, ada_w, ada_b):
    nl, d, n = ada_w.shape
    rows = cvec.shape[0]
    tn = 1536
    return pl.pallas_call(
        _adaln_kernel,
        grid=(nl, n // tn),
        in_specs=[pl.BlockSpec((rows, d), lambda l, j: (0, 0)),
                  pl.BlockSpec((1, d, tn), lambda l, j: (l, 0, j)),
                  pl.BlockSpec((1, 1, tn), lambda l, j: (l, 0, j))],
        out_specs=pl.BlockSpec((1, rows, tn), lambda l, j: (l, 0, j)),
        out_shape=jax.ShapeDtypeStruct((nl, rows, n), F32),
        compiler_params=_params("arbitrary", "arbitrary"),
        name="adaln",
    )(cvec, ada_w, ada_b.reshape(nl, 1, n))


N_RWKV_COLS = 1920
N_CONV_COLS = 768
N_FOUR_COLS = 256


def _inproj_kernel(add_pos, *refs):
    if add_pos:
        x_ref, pos_ref, g_ref, sc_ref, sh_ref, w_ref, zr_ref, zc_ref, zf_ref, xp_ref = refs
        x = x_ref[0] + pos_ref[...]
        xp_ref[0] = x
    else:
        x_ref, g_ref, sc_ref, sh_ref, w_ref, zr_ref, zc_ref, zf_ref = refs
        x = x_ref[0]
    h = (_rms(x, g_ref[...]) * (1.0 + sc_ref[0]) + sh_ref[0]).astype(BF16)
    a, b = N_RWKV_COLS, N_RWKV_COLS + N_CONV_COLS
    zr_ref[0] = jnp.dot(h, w_ref[:, :a], preferred_element_type=F32)
    zc_ref[0] = jnp.dot(h, w_ref[:, a:b], preferred_element_type=F32)
    zf_ref[0] = jnp.dot(h, w_ref[:, b:], preferred_element_type=F32)


def _inproj(x, pos, g, sc, sh, w, tm):
    bsz, length, d = x.shape
    n = w.shape[1]
    add_pos = pos is not None
    row = lambda b, i: (b, i, 0)
    in_specs = [pl.BlockSpec((1, tm, d), row)]
    args = [x]
    if add_pos:
        in_specs.append(pl.BlockSpec((tm, d), lambda b, i: (i, 0)))
        args.append(pos)
    in_specs += [pl.BlockSpec((1, d), lambda b, i: (0, 0)),
                 pl.BlockSpec((1, 1, d), lambda b, i: (b, 0, 0)),
                 pl.BlockSpec((1, 1, d), lambda b, i: (b, 0, 0)),
                 pl.BlockSpec((d, n), lambda b, i: (0, 0))]
    args += [g, sc, sh, w]
    out_shape = [jax.ShapeDtypeStruct((bsz, length, N_RWKV_COLS), F32),
                 jax.ShapeDtypeStruct((bsz, length, N_CONV_COLS), F32),
                 jax.ShapeDtypeStruct((bsz, length, N_FOUR_COLS), F32)]
    out_specs = [pl.BlockSpec((1, tm, N_RWKV_COLS), row),
                 pl.BlockSpec((1, tm, N_CONV_COLS), row),
                 pl.BlockSpec((1, tm, N_FOUR_COLS), row)]
    if add_pos:
        out_shape.append(jax.ShapeDtypeStruct((bsz, length, d), F32))
        out_specs.append(pl.BlockSpec((1, tm, d), row))
    return pl.pallas_call(
        functools.partial(_inproj_kernel, add_pos),
        grid=(bsz, length // tm),
        in_specs=in_specs, out_specs=out_specs, out_shape=out_shape,
        compiler_params=_params("parallel", "parallel"),
        name="inproj",
    )(*args)


HALO = 8


def _prep_kernel(reverse, tm, z_ref, halo_ref, mu_ref, muw_ref, mua_ref, w0_ref, w2_ref, a0_ref, a2_ref,
                 kk_ref, ka_ref, bd_ref, r_ref, lw_ref, k_ref, v_ref, a_ref, b_ref):
    i = pl.program_id(1)
    last = pl.num_programs(1) - 1
    z = z_ref[0]
    rows = lax.broadcasted_iota(jnp.int32, (tm, 1), 0)
    if reverse:
        edge = jnp.where(i == last, 0.0, 1.0)
        hrow = halo_ref[0, 0:1, :] * edge
        zs = jnp.where(rows == tm - 1, hrow, pltpu.roll(z, tm - 1, 0))
    else:
        edge = jnp.where(i == 0, 0.0, 1.0)
        hrow = halo_ref[0, HALO - 1:HALO, :] * edge
        zs = jnp.where(rows == 0, hrow, pltpu.roll(z, 1, 0))
    dz = zs - z
    k = z[:, 0:512] + dz[:, 0:512] * mu_ref[0:1, :]
    v = z[:, 512:1024] + dz[:, 512:1024] * mu_ref[1:2, :]
    r = z[:, 1280:1792] + dz[:, 1280:1792] * mu_ref[2:3, :]
    wl = z[:, 1024:1152] + dz[:, 1024:1152] * muw_ref[...]
    al = z[:, 1152:1280] + dz[:, 1152:1280] * mua_ref[...]
    w_log = -_softplus(-(w0_ref[...] + _dot(jnp.tanh(wl), w2_ref[...]))) - 0.5
    lw = -jnp.exp(w_log)
    a = jax.nn.sigmoid(a0_ref[...] + _dot(al, a2_ref[...]))
    kk = k * kk_ref[...]
    ss = _dot_split_lhs(kk * kk, bd_ref[...])
    kk = kk / jnp.maximum(jnp.sqrt(ss), 1e-12)
    r_ref[0] = r
    lw_ref[0] = lw
    k_ref[0] = k * (1.0 + (a - 1.0) * ka_ref[...])
    v_ref[0] = v
    a_ref[0] = -kk
    b_ref[0] = kk * a


def _prep(zr, reverse, p, tm):
    bsz, length, ncol = zr.shape
    nt = length // tm
    hb = tm // HALO
    nhb = length // HALO
    if reverse:
        halo_map = lambda b, i: (b, jnp.minimum((i + 1) * hb, nhb - 1), 0)
    else:
        halo_map = lambda b, i: (b, jnp.maximum(i * hb - 1, 0), 0)
    full = lambda shape: pl.BlockSpec(shape, lambda b, i: (0,) * len(shape))
    row = lambda b, i: (b, i, 0)
    out = jax.ShapeDtypeStruct((bsz, length, 512), F32)
    return pl.pallas_call(
        functools.partial(_prep_kernel, reverse, tm),
        grid=(bsz, nt),
        in_specs=[pl.BlockSpec((1, tm, ncol), row), pl.BlockSpec((1, HALO, ncol), halo_map),
                  full((3, 512)), full((1, 128)), full((1, 128)), full((1, 512)), full((128, 512)),
                  full((1, 512)), full((128, 512)), full((1, 512)), full((1, 512)), full((512, 512))],
        out_specs=[pl.BlockSpec((1, tm, 512), row)] * 6,
        out_shape=[out] * 6,
        compiler_params=_params("parallel", "parallel"),
        name="rwkv_prep_bwd" if reverse else "rwkv_prep_fwd",
    )(zr, zr, p["mu"], p["mu_w"], p["mu_a"], p["w0"], p["w2"], p["a0"], p["a2"], p["k_k"], p["k_a"], p["bd"])


SOLVE_BLOCK = 16


def _dot3(x, y):
    xh, xl = _split(x)
    yh, yl = _split(y)
    d = lambda p, q: jnp.dot(p, q, preferred_element_type=F32)
    return d(xh, yh) + d(xh, yl) + d(xl, yh)


def _unit_lower_solve(a_list, x_list, diag_blk, eye):
    a_d = [jnp.where(diag_blk, a, 0.0) for a in a_list]
    a_o = [a - d for a, d in zip(a_list, a_d)]
    dinv = [eye + d for d in a_d]
    apow = a_d
    steps = SOLVE_BLOCK.bit_length() - 1
    for _ in range(steps - 1):
        apow = [_dot3(p, p) for p in apow]
        dinv = [d + _dot3(p, d) for p, d in zip(apow, dinv)]
    b = [_dot(d, o) for d, o in zip(dinv, a_o)]
    x = [_dot(d, x) for d, x in zip(dinv, x_list)]
    b2 = [_dot(m, m) for m in b]
    x = [v + _dot(m, v) for m, v in zip(b, x)]
    return [v + _dot(m, v) for m, v in zip(b2, x)]


def _scan_kernel(reverse, r_ref, lw_ref, k_ref, v_ref, a_ref, b_ref, s0_ref, rk_ref, gw_ref, gb_ref, bd_ref,
                 y_ref, st_ref):
    c = pl.program_id(1)

    @pl.when(c == 0)
    def _():
        st_ref[...] = s0_ref[...]

    n = CHUNK
    t_i = lax.broadcasted_iota(jnp.int32, (n, n), 0)
    s_i = lax.broadcasted_iota(jnp.int32, (n, n), 1)
    incl = (s_i >= t_i) if reverse else (s_i <= t_i)
    blk_shift = SOLVE_BLOCK.bit_length() - 1
    diag_blk = (s_i >> blk_shift) == (t_i >> blk_shift)
    eye = jnp.where(s_i == t_i, 1.0, 0.0)
    lane2 =lax.broadcasted_iota(jnp.int32, (n, LANES), 1)
    t_2 = lax.broadcasted_iota(jnp.int32, (n, LANES), 0)
    s_2 = lane2 & (n - 1)
    incl2 = (s_2 >= t_2) if reverse else (s_2 <= t_2)
    strict2 = (s_2 > t_2) if reverse else (s_2 < t_2)
    first = lane2 < HEAD_DIM
    r_bd = lax.broadcasted_iota(jnp.int32, (LANES, LANES), 0) < HEAD_DIM
    c_bd = lax.broadcasted_iota(jnp.int32, (LANES, LANES), 1) < HEAD_DIM
    blockdiag = r_bd == c_bd

    r = r_ref[0]
    lw = lw_ref[0]
    k = k_ref[0]
    v = v_ref[0]
    a = a_ref[0]
    b = b_ref[0]

    cum = _dot_split_rhs(jnp.where(incl, 1.0, 0.0).astype(BF16), lw)
    tot = cum[0:1, :] if reverse else cum[n - 1:n, :]
    w_in = jnp.exp(cum)
    w_ex = jnp.exp(cum - lw)
    w_inv = jnp.exp(-cum)
    w_rem = jnp.exp(tot - cum)
    w_tot = jnp.exp(tot)
    rt = r * w_in
    at = a * w_ex
    bt = b * w_inv
    kt = k * w_inv
    bh = b * w_rem
    kh = k * w_rem

    npairs = r.shape[1] // LANES
    states = [st_ref[0, g] for g in range(npairs)]
    pairs = range(npairs)
    sls = [slice(g * LANES, (g + 1) * LANES) for g in pairs]
    v_g = [v[:, sl] for sl in sls]
    at_h = [(jnp.where(first, at[:, sl], 0.0), jnp.where(first, 0.0, at[:, sl])) for sl in sls]
    rt_h = [(jnp.where(first, rt[:, sl], 0.0), jnp.where(first, 0.0, rt[:, sl])) for sl in sls]
    s = [_dot_nt(jnp.concatenate([at_h[g][0], rt_h[g][0], at_h[g][1], rt_h[g][1]], axis=0),
                 jnp.concatenate([bt[:, sls[g]], kt[:, sls[g]]], axis=0))
         for g in pairs]
    sa = [(jnp.where(strict2, s[g][0:n], 0.0), jnp.where(strict2, s[g][2 * n:3 * n], 0.0)) for g in pairs]
    sr = [(jnp.where(incl2, s[g][n:2 * n], 0.0), jnp.where(incl2, s[g][3 * n:4 * n], 0.0)) for g in pairs]
    heads = [(g, j) for g in pairs for j in range(2)]
    vv = [jnp.concatenate([v_g[g], v_g[g]], axis=0) for g in pairs]
    akv = [_dot(jnp.where(first, 0.0, sa[g][j]), vv[g]) for g, j in heads]
    pq = _unit_lower_solve([sa[g][j][:, 0:n] for g, j in heads],
                           [jnp.concatenate([at_h[g][j], akv[2 * g + j]], axis=1) for g, j in heads],
                           diag_blk, eye)
    pt = [_dot_split_rhs(jnp.concatenate([pq[2 * g][:, :LANES], pq[2 * g + 1][:, :LANES]], axis=0).astype(BF16),
                         states[g]) for g in pairs]
    rtt = [_dot_split_rhs(jnp.concatenate(rt_h[g], axis=0).astype(BF16), states[g]) for g in pairs]
    u = [jnp.where(first, pt[g][0:n] + pq[2 * g][:, LANES:], pt[g][n:] + pq[2 * g + 1][:, LANES:]) for g in pairs]
    uv = [jnp.concatenate([u[g], v_g[g]], axis=0) for g in pairs]
    yh = [_dot(sr[g][j], uv[g]) for g, j in heads]
    ys = [jnp.where(first, rtt[g][0:n] + yh[2 * g], rtt[g][n:] + yh[2 * g + 1]) for g in pairs]
    t_new = [_dot(jnp.concatenate([bh[:, sls[g]], kh[:, sls[g]]], axis=0).T, uv[g]) for g in pairs]
    for g in pairs:
        w_col = jnp.broadcast_to(w_tot[:, sls[g]], (LANES, LANES)).T
        st_ref[0, g] = jnp.where(blockdiag, w_col * states[g] + t_new[g], 0.0)
    y = jnp.concatenate(ys, axis=1)
    bd = bd_ref[...]
    inv = 1.0 / HEAD_DIM
    mean = _dot_split_lhs(y, bd) * inv
    d = y - mean
    var = _dot_split_lhs(d * d, bd) * inv
    yn = d * lax.rsqrt(var + GN_EPS) * gw_ref[...] + gb_ref[...]
    bonus = _dot_split_lhs(r * k * rk_ref[...], bd) * v
    y_ref[0] = yn + bonus


def _scan(seq, state0, reverse, p):
    r = seq[0]
    bsz, length, width = r.shape
    nc = length // CHUNK
    npairs = width // LANES
    cmap = (lambda b, c: (b, nc - 1 - c, 0)) if reverse else (lambda b, c: (b, c, 0))
    full = lambda shape: pl.BlockSpec(shape, lambda b, c: (0,) * len(shape))
    st_spec = pl.BlockSpec((1, npairs, LANES, LANES), lambda b, c: (b, 0, 0, 0))
    return pl.pallas_call(
        functools.partial(_scan_kernel, reverse),
        grid=(bsz, nc),
        in_specs=[pl.BlockSpec((1, CHUNK, width), cmap)] * 6 + [st_spec] +
                 [full((1, width))] * 3 + [full((width, width))],
        out_specs=[pl.BlockSpec((1, CHUNK, width), cmap), st_spec],
        out_shape=[jax.ShapeDtypeStruct((bsz, length, width), F32),
                   jax.ShapeDtypeStruct((bsz, npairs, LANES, LANES), F32)],
        compiler_params=_params("parallel", "arbitrary"),
        name="wkv_scan_bwd" if reverse else "wkv_scan_fwd",
    )(*seq, state0, p["r_k"], p["gn_w"], p["gn_b"], p["bd"])


def _mix_kernel(period, tm, yf_ref, yb_ref, glo_ref, zc_ref, g2_ref, cw_ref, o_ref):
    gate = _dot(jax.nn.sigmoid(glo_ref[0]), g2_ref[...])
    yr = (yf_ref[0] + yb_ref[0]) * gate
    zc = zc_ref[0]
    u, gate_b, gate_c = zc[:, 0:256], zc[:, 256:512], zc[:, 512:768]
    hc = gate_c * u
    pos = lax.broadcasted_iota(jnp.int32, (tm, 1), 0) % period
    prev = jnp.where(pos == 0, 0.0, pltpu.roll(hc, 1, 0))
    nxt = jnp.where(pos == period - 1, 0.0, pltpu.roll(hc, tm - 1, 0))
    conv = prev * cw_ref[0:1, :] + hc * cw_ref[1:2, :] + nxt * cw_ref[2:3, :]
    o_ref[0, :, 0:512] = yr.astype(BF16)
    o_ref[0, :, 512:768] = (gate_b * conv).astype(BF16)


def _mix(ys_f, ys_b, zr, zc, g2, conv_w, period, tm):
    bsz, length, _ = zc.shape
    row = lambda b, i: (b, i, 0)
    glo_block = (N_RWKV_COLS - 128) // 128
    return pl.pallas_call(
        functools.partial(_mix_kernel, period, tm),
        grid=(bsz, length // tm),
        in_specs=[pl.BlockSpec((1, tm, 512), row), pl.BlockSpec((1, tm, 512), row),
                  pl.BlockSpec((1, tm, 128), lambda b, i: (b, i, glo_block)),
                  pl.BlockSpec((1, tm, N_CONV_COLS), row),
                  pl.BlockSpec((128, 512), lambda b, i: (0, 0)),
                  pl.BlockSpec((3, 256), lambda b, i: (0, 0))],
        out_specs=pl.BlockSpec((1, tm, 768), row),
        out_shape=jax.ShapeDtypeStruct((bsz, length, 768), BF16),
        compiler_params=_params("parallel", "parallel"),
        name="gate_conv",
    )(ys_f, ys_b, zr, zc, g2, conv_w)


def _chan_dft_kernel(f_ref, w_ref, o_ref):
    xcs = jnp.dot(f_ref[0].astype(BF16), w_ref[...], preferred_element_type=F32)
    o_ref[0, 0] = xcs[:, 0:256].astype(BF16)
    o_ref[0, 1] = xcs[:, 256:512].astype(BF16)


def _pos_dft_kernel(nb, cs_ref, x_ref, o_ref, acc_ref):
    kstep = pl.program_id(1)

    @pl.when(kstep == 0)
    def _():
        acc_ref[...] = jnp.zeros_like(acc_ref)

    cs = cs_ref[...]
    for b in range(nb):
        acc_ref[b] += jnp.dot(cs, x_ref[b], preferred_element_type=F32)

    @pl.when(kstep == pl.num_programs(1) - 1)
    def _():
        o_ref[...] = acc_ref[...].astype(BF16)


def _fourier(zf, chan_w, pos_cs, tm, tk):
    bsz, length, ch = zf.shape
    tr = min(512, length)
    xcs = pl.pallas_call(
        _chan_dft_kernel,
        grid=(bsz, length // tr),
        in_specs=[pl.BlockSpec((1, tr, ch), lambda b, i: (b, i, 0)),
                  pl.BlockSpec((ch, 2 * ch), lambda b, i: (0, 0))],
        out_specs=pl.BlockSpec((1, 2, tr, ch), lambda b, i: (b, 0, i, 0)),
        out_shape=jax.ShapeDtypeStruct((bsz, 2, length, ch), BF16),
        compiler_params=_params("parallel", "parallel"),
        name="chan_dft",
    )(zf, chan_w)
    xcs = xcs.reshape(bsz, 2 * length, ch)
    return pl.pallas_call(
        functools.partial(_pos_dft_kernel, bsz),
        grid=(length // tm, 2 * length // tk),
        in_specs=[pl.BlockSpec((tm, tk), lambda i, kk: (i, kk)),
                  pl.BlockSpec((bsz, tk, ch), lambda i, kk: (0, kk, 0))],
        out_specs=pl.BlockSpec((bsz, tm, ch), lambda i, kk: (0, i, 0)),
        out_shape=jax.ShapeDtypeStruct((bsz, length, ch), BF16),
        scratch_shapes=[pltpu.VMEM((bsz, tm, ch), F32)],
        compiler_params=_params("parallel", "arbitrary"),
        name="pos_dft",
    )(pos_cs, xcs)


def _dft_tables(length):
    m = jnp.arange(length, dtype=jnp.int32)[None, :]
    l1 = jnp.arange(length // 64, dtype=jnp.int32)[:, None] * 64
    l2 = jnp.arange(64, dtype=jnp.int32)[:, None]
    ang = lambda l: ((l * m) % length).astype(F32) * (2.0 * jnp.pi / length)
    c1, s1, c2, s2 = jnp.cos(ang(l1)), jnp.sin(ang(l1)), jnp.cos(ang(l2)), jnp.sin(ang(l2))
    scale = 1.0 / jnp.sqrt(jnp.float32(length))
    cos = (c1[:, None, :] * c2[None] - s1[:, None, :] * s2[None]).reshape(length, length) * scale
    sin = (s1[:, None, :] * c2[None] + c1[:, None, :] * s2[None]).reshape(length, length) * scale
    return jnp.concatenate([cos, sin], axis=1).astype(BF16)


def _chan_dft_weights(ngroups):
    j = jnp.arange(HEAD_DIM, dtype=jnp.int32)
    ang = ((j[:, None] * j[None, :]) % HEAD_DIM).astype(F32) * (2.0 * jnp.pi / HEAD_DIM)
    eye = jnp.eye(ngroups, dtype=F32)
    scale = 1.0 / jnp.sqrt(jnp.float32(HEAD_DIM))
    cc = jnp.kron(eye, jnp.cos(ang) * scale)
    sc = jnp.kron(eye, jnp.sin(ang) * scale)
    return jnp.concatenate([cc, -sc], axis=1).astype(BF16)


def _outproj_kernel(yrc_ref, yf_ref, w_ref, x_ref, gn1_ref, gate_ref, gn2_ref, sc_ref, sh_ref, xo_ref, h_ref):
    nrc = yrc_ref.shape[2]
    o = (jnp.dot(yrc_ref[0], w_ref[0:nrc, :], preferred_element_type=F32) +
         jnp.dot(yf_ref[0], w_ref[nrc:, :], preferred_element_type=F32))
    xn = x_ref[0] + gate_ref[0] * _rms(o, gn1_ref[...])
    xo_ref[0] = xn
    h_ref[0] = (_rms(xn, gn2_ref[...]) * (1.0 + sc_ref[0]) + sh_ref[0]).astype(BF16)


def _outproj(yrc, yf, w, x, gn1, gate, gn2, sc, sh, tm):
    bsz, length, d = x.shape
    row = lambda b, i: (b, i, 0)
    vec = pl.BlockSpec((1, d), lambda b, i: (0, 0))
    mod = pl.BlockSpec((1, 1, d), lambda b, i: (b, 0, 0))
    return pl.pallas_call(
        _outproj_kernel,
        grid=(bsz, length // tm),
        in_specs=[pl.BlockSpec((1, tm, yrc.shape[2]), row), pl.BlockSpec((1, tm, yf.shape[2]), row),
                  pl.BlockSpec(w.shape, lambda b, i: (0, 0)), pl.BlockSpec((1, tm, d), row),
                  vec, mod, vec, mod, mod],
        out_specs=[pl.BlockSpec((1, tm, d), row), pl.BlockSpec((1, tm, d), row)],
        out_shape=[jax.ShapeDtypeStruct((bsz, length, d), F32), jax.ShapeDtypeStruct((bsz, length, d), BF16)],
        compiler_params=_params("parallel", "parallel"),
        name="outproj",
    )(yrc, yf, w, x, gn1, gate, gn2, sc, sh)


def _ffn_kernel(h_ref, wg_ref, wu_ref, wd_ref, x_ref, gn_ref, gate_ref, o_ref, acc_ref):
    j = pl.program_id(2)

    @pl.when(j == 0)
    def _():
        acc_ref[...] = jnp.zeros_like(acc_ref)

    h = h_ref[0]
    act = _silu(jnp.dot(h, wg_ref[...], preferred_element_type=F32)) * jnp.dot(h, wu_ref[...],
                                                                              preferred_element_type=F32)
    acc_ref[...] += jnp.dot(act.astype(BF16), wd_ref[...], preferred_element_type=F32)

    @pl.when(j == pl.num_programs(2) - 1)
    def _():
        o_ref[0] = x_ref[0] + gate_ref[0] * _rms(acc_ref[...], gn_ref[...])


def _ffn(h, wg, wu, wd, x, gn, gate, tm, tf):
    bsz, length, d = x.shape
    ff = wg.shape[1]
    row = lambda b, i, j: (b, i, 0)
    return pl.pallas_call(
        _ffn_kernel,
        grid=(bsz, length // tm, ff // tf),
        in_specs=[pl.BlockSpec((1, tm, d), row),
                  pl.BlockSpec((d, tf), lambda b, i, j: (0, j)),
                  pl.BlockSpec((d, tf), lambda b, i, j: (0, j)),
                  pl.BlockSpec((tf, d), lambda b, i, j: (j, 0)),
                  pl.BlockSpec((1, tm, d), row),
                  pl.BlockSpec((1, d), lambda b, i, j: (0, 0)),
                  pl.BlockSpec((1, 1, d), lambda b, i, j: (b, 0, 0))],
        out_specs=pl.BlockSpec((1, tm, d), row),
        out_shape=jax.ShapeDtypeStruct((bsz, length, d), F32),
        scratch_shapes=[pltpu.VMEM((tm, d), F32)],
        compiler_params=_params("parallel", "parallel", "arbitrary"),
        name="ffn",
    )(h, wg, wu, wd, x, gn, gate)


def _router_kernel(x_ref, gn_ref, sc_ref, sh_ref, w_ref, b_ref, o_ref):
    h = _rms(x_ref[0], gn_ref[...]) * (1.0 + sc_ref[0]) + sh_ref[0]
    logits = jnp.dot(h, w_ref[...], preferred_element_type=F32, precision=lax.Precision.HIGHEST) + b_ref[...]
    ne = float(logits.shape[1])
    idx = lax.broadcasted_iota(jnp.int32, logits.shape, 1).astype(F32)
    m1 = jnp.max(logits, axis=1, keepdims=True)
    i1 = jnp.min(jnp.where(logits == m1, idx, ne), axis=1, keepdims=True)
    rest = jnp.where(idx == i1, -jnp.inf, logits)
    m2 = jnp.max(rest, axis=1, keepdims=True)
    i2 = jnp.min(jnp.where(rest == m2, idx, ne), axis=1, keepdims=True)
    e2 = jnp.exp(m2 - m1)
    p1 = 1.0 / (1.0 + e2)
    p2 = e2 / (1.0 + e2)
    o_ref[0] = jnp.where(idx == i1, p1, 0.0) + jnp.where(idx == i2, p2, 0.0)


def _router(x, gn, sc, sh, w, b, tm):
    bsz, length, d = x.shape
    pad = LANES - w.shape[1]
    w = jnp.pad(w, ((0, 0), (0, pad)))
    b = jnp.pad(b, ((0, 0), (0, pad)), constant_values=-1e30)
    ne = LANES
    row = lambda bb, i: (bb, i, 0)
    vec = pl.BlockSpec((1, d), lambda bb, i: (0, 0))
    mod = pl.BlockSpec((1, 1, d), lambda bb, i: (bb, 0, 0))
    return pl.pallas_call(
        _router_kernel,
        grid=(bsz, length // tm),
        in_specs=[pl.BlockSpec((1, tm, d), row), vec, mod, mod,
                  pl.BlockSpec((d, ne), lambda bb, i: (0, 0)), pl.BlockSpec((1, ne), lambda bb, i: (0, 0))],
        out_specs=pl.BlockSpec((1, tm, ne), row),
        out_shape=jax.ShapeDtypeStruct((bsz, length, ne), F32),
        compiler_params=_params("parallel", "parallel"),
        name="router",
    )(x, gn, sc, sh, w, b)


def _moe_kernel(h_ref, gates_ref, wg_ref, wu_ref, wd_ref, x_ref, gn_ref, gate_ref, o_ref, acc_ref):
    e = pl.program_id(2)
    j = pl.program_id(3)

    @pl.when((e == 0) & (j == 0))
    def _():
        acc_ref[...] = jnp.zeros_like(acc_ref)

    gates = gates_ref[0]
    sel = lax.broadcasted_iota(jnp.int32, gates.shape, 1) == e
    ge = jnp.sum(jnp.where(sel, gates, 0.0), axis=1, keepdims=True)
    h = h_ref[0]
    act = _silu(jnp.dot(h, wg_ref[0], preferred_element_type=F32)) * jnp.dot(h, wu_ref[0],
                                                                            preferred_element_type=F32)
    acc_ref[...] += jnp.dot((act * ge).astype(BF16), wd_ref[0], preferred_element_type=F32)

    @pl.when((e == pl.num_programs(2) - 1) & (j == pl.num_programs(3) - 1))
    def _():
        o_ref[0] = x_ref[0] + gate_ref[0] * _rms(acc_ref[...], gn_ref[...])


def _moe(h, gates, wg, wu, wd, x, gn, gate, tm, tf):
    bsz, length, d = x.shape
    ne, _, ff = wg.shape
    row = lambda b, i, e, j: (b, i, 0)
    return pl.pallas_call(
        _moe_kernel,
        grid=(bsz, length // tm, ne, ff // tf),
        in_specs=[pl.BlockSpec((1, tm, d), row),
                  pl.BlockSpec((1, tm, gates.shape[2]), row),
                  pl.BlockSpec((1, d, tf), lambda b, i, e, j: (e, 0, j)),
                  pl.BlockSpec((1, d, tf), lambda b, i, e, j: (e, 0, j)),
                  pl.BlockSpec((1, tf, d), lambda b, i, e, j: (e, j, 0)),
                  pl.BlockSpec((1, tm, d), row),
                  pl.BlockSpec((1, d), lambda b, i, e, j: (0, 0)),
                  pl.BlockSpec((1, 1, d), lambda b, i, e, j: (b, 0, 0))],
        out_specs=pl.BlockSpec((1, tm, d), row),
        out_shape=jax.ShapeDtypeStruct((bsz, length, d), F32),
        scratch_shapes=[pltpu.VMEM((tm, d), F32)],
        compiler_params=_params("parallel", "parallel", "arbitrary", "arbitrary"),
        name="moe",
    )(h, gates, wg, wu, wd, x, gn, gate)


def _sincos_2d(rows, cols, dim):
    quarter = dim // 4
    omega = 1.0 / (POS_BASE ** (jnp.arange(quarter, dtype=F32) / quarter))

    def axis_emb(n):
        ang = jnp.arange(n, dtype=F32)[:, None] * omega[None, :]
        return jnp.concatenate([jnp.sin(ang), jnp.cos(ang)], axis=-1)

    er, ec = axis_emb(rows), axis_emb(cols)
    emb = jnp.concatenate([jnp.broadcast_to(er[:, None, :], (rows, cols, dim // 2)),
                           jnp.broadcast_to(ec[None, :, :], (rows, cols, dim // 2))], axis=-1)
    return emb.reshape(rows * cols, dim)


def _head_block_ones(width):
    h = jnp.arange(width, dtype=jnp.int32) // HEAD_DIM
    return (h[:, None] == h[None, :]).astype(BF16)


def _pad_lora(w2, d):
    z = jnp.zeros_like(w2[0])
    return jnp.concatenate([w2[0], z], axis=0) if d == 0 else jnp.concatenate([z, w2[1]], axis=0)


def _tile(length, pref):
    return pref if length % pref == 0 else length


def kernel(x, c, ctx, c_ctx, ada_w, ada_b, norm_g, w_in, w_out, rwkv_mu, rwkv_mu_w, rwkv_mu_a, rwkv_w0, rwkv_w2,
           rwkv_a0, rwkv_a2, rwkv_g2, rwkv_k_k, rwkv_k_a, rwkv_r_k, rwkv_gn_w, rwkv_gn_b, conv_w, ffn_w_gate,
           ffn_w_up, ffn_w_down, router_w, router_b, moe_w_gate, moe_w_up, moe_w_down):
    bsz, length, dim = x.shape
    ctx_len = ctx.shape[1]
    depth = ada_w.shape[0]
    d_rwkv = rwkv_w0.shape[-1]

    pos = _sincos_2d(length // GRID_W, GRID_W, dim).astype(x.dtype)
    bd = _head_block_ones(d_rwkv)
    chan_w = _chan_dft_weights(N_FOUR_COLS // HEAD_DIM)
    dft_lat = _dft_tables(length)
    dft_ctx = _dft_tables(ctx_len)

    cvec = jnp.zeros((16, dim), F32).at[:bsz].set(c).at[bsz].set(c_ctx)
    mod = _adaln(cvec, ada_w, ada_b)
    zero_state = jnp.zeros((bsz, d_rwkv // LANES, LANES, LANES), F32)
    xc = ctx

    def mods(l, lo, hi):
        return [mod[l, lo:hi, i * dim:(i + 1) * dim][:, None, :] for i in range(6)]

    def mixers(l, zr, zc, zf, states0, period, dft, need_y=True):
        tm = _tile(zr.shape[1], 512)
        ys, states = [], []
        for d, reverse in enumerate((False, True)):
            p = {"mu": rwkv_mu[l, d], "mu_w": rwkv_mu_w[l].reshape(1, -1), "mu_a": rwkv_mu_a[l].reshape(1, -1),
                 "w0": rwkv_w0[l, d][None], "w2": _pad_lora(rwkv_w2[l], d), "a0": rwkv_a0[l, d][None],
                 "a2": _pad_lora(rwkv_a2[l], d), "k_k": rwkv_k_k[l][None], "k_a": rwkv_k_a[l][None],
                 "r_k": rwkv_r_k[l].reshape(1, -1), "gn_w": rwkv_gn_w[l][None], "gn_b": rwkv_gn_b[l][None],
                 "bd": bd}
            seq = _prep(zr, reverse, p, tm)
            y, st = _scan(seq, states0[d], reverse, p)
            ys.append(y)
            states.append(st)
        if not need_y:
            return None, None, states
        yrc = _mix(ys[0], ys[1], zr, zc, rwkv_g2[l], conv_w[l], period, tm)
        yf = _fourier(zf, chan_w, dft, _tile(zf.shape[1], 512), 512)
        return yrc, yf, states

    def channel_mixer(l, h2, xn, gn3, gate2, sc2, sh2, gn2):
        tm = _tile(xn.shape[1], 512)
        i = l // 2
        if l % 2 == 0:
            return _ffn(h2, ffn_w_gate[i].astype(BF16), ffn_w_up[i].astype(BF16), ffn_w_down[i].astype(BF16),
                        xn, gn3, gate2, tm, 1408)
        gates = _router(xn, gn2, sc2, sh2, router_w[i], router_b[i][None], tm)
        return _moe(h2, gates, moe_w_gate[i].astype(BF16), moe_w_up[i].astype(BF16), moe_w_down[i].astype(BF16),
                    xn, gn3, gate2, tm, 896)

    for l in range(depth):
        last = l == depth - 1
        w_in_l = w_in[l].astype(BF16)
        w_out_l = w_out[l].astype(BF16)
        gn = [norm_g[l, i][None] for i in range(4)]
        sh1, sc1, g1, sh2, sc2, g2 = mods(l, 0, bsz)
        csh1, csc1, cg1, csh2, csc2, cg2 = mods(l, bsz, bsz + 1)

        xc_flat = xc.reshape(1, bsz * ctx_len, dim)
        tmc = _tile(bsz * ctx_len, 512)
        zr, zc, zf = _inproj(xc_flat, None, gn[0], csc1, csh1, w_in_l, tmc)
        unflat = lambda t: t.reshape(bsz, ctx_len, t.shape[-1])
        yrc, yf, ctx_states = mixers(l, unflat(zr), unflat(zc), unflat(zf), (zero_state, zero_state), ctx_len,
                                     dft_ctx, need_y=not last)
        if not last:
            flat = lambda t: t.reshape(1, bsz * ctx_len, t.shape[-1])
            xcn, hc2 = _outproj(flat(yrc), flat(yf), w_out_l, xc_flat, gn[1], cg1, gn[2], csc2, csh2, tmc)
            xc = channel_mixer(l, hc2, xcn, gn[3], cg2, csc2, csh2, gn[2]).reshape(bsz, ctx_len, dim)

        tm = _tile(length, 256)
        if l == 0:
            zr, zc, zf, x = _inproj(x, pos, gn[0], sc1, sh1, w_in_l, tm)
        else:
            zr, zc, zf = _inproj(x, None, gn[0], sc1, sh1, w_in_l, tm)
        yrc, yf, _ = mixers(l, zr, zc, zf, ctx_states, GRID_W, dft_lat)
        xn, h2 = _outproj(yrc, yf, w_out_l, x, gn[1], g1, gn[2], sc2, sh2, _tile(length, 512))
        x = channel_mixer(l, h2, xn, gn[3], g2, sc2, sh2, gn[2])
    return x
```

```python
import functools

import jax
import jax.numpy as jnp
from jax import lax
from jax.experimental import pallas as pl
from jax.experimental.pallas import tpu as pltpu

F32 = jnp.float32
BF16 = jnp.bfloat16

HEAD_DIM = 64
GRID_W = 64
NORM_EPS = 1e-6
GN_EPS = 64e-5
POS_BASE = 10000.0
N_EXPERTS = 8
CHUNK = 64
LANES = 128
VMEM_LIMIT = 56 * 1024 * 1024


def _params(*sem):
    return pltpu.CompilerParams(dimension_semantics=sem, vmem_limit_bytes=VMEM_LIMIT)


def _dot(a, b):
    return jnp.dot(a.astype(BF16), b.astype(BF16), preferred_element_type=F32)


def _dot_nt(a, b):
    return lax.dot_general(a.astype(BF16), b.astype(BF16), (((1,), (1,)), ((), ())),
                           preferred_element_type=F32)


def _split(x):
    hi = x.astype(BF16)
    lo = (x - hi.astype(F32)).astype(BF16)
    return hi, lo


def _dot_split_lhs(x, w):
    hi, lo = _split(x)
    return (jnp.dot(hi, w, preferred_element_type=F32) + jnp.dot(lo, w, preferred_element_type=F32))


def _dot_split_rhs(w, x):
    hi, lo = _split(x)
    return (jnp.dot(w, hi, preferred_element_type=F32) + jnp.dot(w, lo, preferred_element_type=F32))


def _rms(x, g):
    ms = jnp.mean(x * x, axis=-1, keepdims=True)
    return x * lax.rsqrt(ms + NORM_EPS) * g


def _silu(x):
    return x * jax.nn.sigmoid(x)


def _softplus(x):
    return jnp.maximum(x, 0.0) + jnp.log(1.0 + jnp.exp(-jnp.abs(x)))


def _adaln_kernel(c_ref, w_ref, b_ref, o_ref):
    s = _silu(c_ref[...])
    o_ref[0] = _dot(s, w_ref[0]) + b_ref[0]


def _adaln(cvec, ada_w, ada_b):
    nl, d, n = ada_w.shape
    rows = cvec.shape[0]
    tn = 1536
    return pl.pallas_call(
        _adaln_kernel,
        grid=(nl, n // tn),
        in_specs=[pl.BlockSpec((rows, d), lambda l, j: (0, 0)),
                  pl.BlockSpec((1, d, tn), lambda l, j: (l, 0, j)),
                  pl.BlockSpec((1, 1, tn), lambda l, j: (l, 0, j))],
        out_specs=pl.BlockSpec((1, rows, tn), lambda l, j: (l, 0, j)),
        out_shape=jax.ShapeDtypeStruct((nl, rows, n), F32),
        compiler_params=_params("arbitrary", "arbitrary"),
        name="adaln",
    )(cvec, ada_w, ada_b.reshape(nl, 1, n))


N_RWKV_COLS = 1920
N_CONV_COLS = 768
N_FOUR_COLS = 256


def _inproj_kernel(add_pos, *refs):
    if add_pos:
        x_ref, pos_ref, g_ref, sc_ref, sh_ref, w_ref, zr_ref, zc_ref, zf_ref, xp_ref = refs
        x = x_ref[0] + pos_ref[...]
        xp_ref[0] = x
    else:
        x_ref, g_ref, sc_ref, sh_ref, w_ref, zr_ref, zc_ref, zf_ref = refs
        x = x_ref[0]
    h = (_rms(x, g_ref[...]) * (1.0 + sc_ref[0]) + sh_ref[0]).astype(BF16)
    a, b = N_RWKV_COLS, N_RWKV_COLS + N_CONV_COLS
    zr_ref[0] = jnp.dot(h, w_ref[:, :a], preferred_element_type=F32)
    zc_ref[0] = jnp.dot(h, w_ref[:, a:b], preferred_element_type=F32)
    zf_ref[0] = jnp.dot(h, w_ref[:, b:], preferred_element_type=F32)


def _inproj(x, pos, g, sc, sh, w, tm):
    bsz, length, d = x.shape
    n = w.shape[1]
    add_pos = pos is not None
    row = lambda b, i: (b, i, 0)
    in_specs = [pl.BlockSpec((1, tm, d), row)]
    args = [x]
    if add_pos:
        in_specs.append(pl.BlockSpec((tm, d), lambda b, i: (i, 0)))
        args.append(pos)
    in_specs += [pl.BlockSpec((1, d), lambda b, i: (0, 0)),
                 pl.BlockSpec((1, 1, d), lambda b, i: (b, 0, 0)),
                 pl.BlockSpec((1, 1, d), lambda b, i: (b, 0, 0)),
                 pl.BlockSpec((d, n), lambda b, i: (0, 0))]
    args += [g, sc, sh, w]
    out_shape = [jax.ShapeDtypeStruct((bsz, length, N_RWKV_COLS), F32),
                 jax.ShapeDtypeStruct((bsz, length, N_CONV_COLS), F32),
                 jax.ShapeDtypeStruct((bsz, length, N_FOUR_COLS), F32)]
    out_specs = [pl.BlockSpec((1, tm, N_RWKV_COLS), row),
                 pl.BlockSpec((1, tm, N_CONV_COLS), row),
                 pl.BlockSpec((1, tm, N_FOUR_COLS), row)]
    if add_pos:
        out_shape.append(jax.ShapeDtypeStruct((bsz, length, d), F32))
        out_specs.append(pl.BlockSpec((1, tm, d), row))
    return pl.pallas_call(
        functools.partial(_inproj_kernel, add_pos),
        grid=(bsz, length // tm),
        in_specs=in_specs, out_specs=out_specs, out_shape=out_shape,
        compiler_params=_params("parallel", "parallel"),
        name="inproj",
    )(*args)


HALO = 8


def _prep_kernel(reverse, tm, z_ref, halo_ref, mu_ref, muw_ref, mua_ref, w0_ref, w2_ref, a0_ref, a2_ref,
                 kk_ref, ka_ref, bd_ref, r_ref, lw_ref, k_ref, v_ref, a_ref, b_ref):
    i = pl.program_id(1)
    last = pl.num_programs(1) - 1
    z = z_ref[0]
    rows = lax.broadcasted_iota(jnp.int32, (tm, 1), 0)
    if reverse:
        edge = jnp.where(i == last, 0.0, 1.0)
        hrow = halo_ref[0, 0:1, :] * edge
        zs = jnp.where(rows == tm - 1, hrow, pltpu.roll(z, tm - 1, 0))
    else:
        edge = jnp.where(i == 0, 0.0, 1.0)
        hrow = halo_ref[0, HALO - 1:HALO, :] * edge
        zs = jnp.where(rows == 0, hrow, pltpu.roll(z, 1, 0))
    dz = zs - z
    k = z[:, 0:512] + dz[:, 0:512] * mu_ref[0:1, :]
    v = z[:, 512:1024] + dz[:, 512:1024] * mu_ref[1:2, :]
    r = z[:, 1280:1792] + dz[:, 1280:1792] * mu_ref[2:3, :]
    wl = z[:, 1024:1152] + dz[:, 1024:1152] * muw_ref[...]
    al = z[:, 1152:1280] + dz[:, 1152:1280] * mua_ref[...]
    w_log = -_softplus(-(w0_ref[...] + _dot(jnp.tanh(wl), w2_ref[...]))) - 0.5
    lw = -jnp.exp(w_log)
    a = jax.nn.sigmoid(a0_ref[...] + _dot(al, a2_ref[...]))
    kk = k * kk_ref[...]
    ss = _dot_split_lhs(kk * kk, bd_ref[...])
    kk = kk / jnp.maximum(jnp.sqrt(ss), 1e-12)
    r_ref[0] = r
    lw_ref[0] = lw
    k_ref[0] = k * (1.0 + (a - 1.0) * ka_ref[...])
    v_ref[0] = v
    a_ref[0] = -kk
    b_ref[0] = kk * a


def _prep(zr, reverse, p, tm):
    bsz, length, ncol = zr.shape
    nt = length // tm
    hb = tm // HALO
    nhb = length // HALO
    if reverse:
        halo_map = lambda b, i: (b, jnp.minimum((i + 1) * hb, nhb - 1), 0)
    else:
        halo_map = lambda b, i: (b, jnp.maximum(i * hb - 1, 0), 0)
    full = lambda shape: pl.BlockSpec(shape, lambda b, i: (0,) * len(shape))
    row = lambda b, i: (b, i, 0)
    out = jax.ShapeDtypeStruct((bsz, length, 512), F32)
    return pl.pallas_call(
        functools.partial(_prep_kernel, reverse, tm),
        grid=(bsz, nt),
        in_specs=[pl.BlockSpec((1, tm, ncol), row), pl.BlockSpec((1, HALO, ncol), halo_map),
                  full((3, 512)), full((1, 128)), full((1, 128)), full((1, 512)), full((128, 512)),
                  full((1, 512)), full((128, 512)), full((1, 512)), full((1, 512)), full((512, 512))],
        out_specs=[pl.BlockSpec((1, tm, 512), row)] * 6,
        out_shape=[out] * 6,
        compiler_params=_params("parallel", "parallel"),
        name="rwkv_prep_bwd" if reverse else "rwkv_prep_fwd",
    )(zr, zr, p["mu"], p["mu_w"], p["mu_a"], p["w0"], p["w2"], p["a0"], p["a2"], p["k_k"], p["k_a"], p["bd"])


SOLVE_BLOCK = 16


def _dot3(x, y):
    xh, xl = _split(x)
    yh, yl = _split(y)
    d = lambda p, q: jnp.dot(p, q, preferred_element_type=F32)
    return d(xh, yh) + d(xh, yl) + d(xl, yh)


def _unit_lower_solve(a_list, x_list, diag_blk, eye):
    a_d = [jnp.where(diag_blk, a, 0.0) for a in a_list]
    a_o = [a - d for a, d in zip(a_list, a_d)]
    dinv = [eye + d for d in a_d]
    apow = a_d
    steps = SOLVE_BLOCK.bit_length() - 1
    for _ in range(steps - 1):
        apow = [_dot3(p, p) for p in apow]
        dinv = [d + _dot3(p, d) for p, d in zip(apow, dinv)]
    b = [_dot(d, o) for d, o in zip(dinv, a_o)]
    x = [_dot(d, x) for d, x in zip(dinv, x_list)]
    b2 = [_dot(m, m) for m in b]
    x = [v + _dot(m, v) for m, v in zip(b, x)]
    return [v + _dot(m, v) for m, v in zip(b2, x)]


def _scan_kernel(reverse, r_ref, lw_ref, k_ref, v_ref, a_ref, b_ref, s0_ref, rk_ref, gw_ref, gb_ref, bd_ref,
                 y_ref, st_ref):
    c = pl.program_id(1)

    @pl.when(c == 0)
    def _():
        st_ref[...] = s0_ref[...]

    n = CHUNK
    t_i = lax.broadcasted_iota(jnp.int32, (n, n), 0)
    s_i = lax.broadcasted_iota(jnp.int32, (n, n), 1)
    incl = (s_i >= t_i) if reverse else (s_i <= t_i)
    blk_shift = SOLVE_BLOCK.bit_length() - 1
    diag_blk = (s_i >> blk_shift) == (t_i >> blk_shift)
    eye = jnp.where(s_i == t_i, 1.0, 0.0)
    lane2 =lax.broadcasted_iota(jnp.int32, (n, LANES), 1)
    t_2 = lax.broadcasted_iota(jnp.int32, (n, LANES), 0)
    s_2 = lane2 & (n - 1)
    incl2 = (s_2 >= t_2) if reverse else (s_2 <= t_2)
    strict2 = (s_2 > t_2) if reverse else (s_2 < t_2)
    first = lane2 < HEAD_DIM
    r_bd = lax.broadcasted_iota(jnp.int32, (LANES, LANES), 0) < HEAD_DIM
    c_bd = lax.broadcasted_iota(jnp.int32, (LANES, LANES), 1) < HEAD_DIM
    blockdiag = r_bd == c_bd

    r = r_ref[0]
    lw = lw_ref[0]
    k = k_ref[0]
    v = v_ref[0]
    a = a_ref[0]
    b = b_ref[0]

    cum = _dot_split_rhs(jnp.where(incl, 1.0, 0.0).astype(BF16), lw)
    tot = cum[0:1, :] if reverse else cum[n - 1:n, :]
    w_in = jnp.exp(cum)
    w_ex = jnp.exp(cum - lw)
    w_inv = jnp.exp(-cum)
    w_rem = jnp.exp(tot - cum)
    w_tot = jnp.exp(tot)
    rt = r * w_in
    at = a * w_ex
    bt = b * w_inv
    kt = k * w_inv
    bh = b * w_rem
    kh = k * w_rem

    npairs = r.shape[1] // LANES
    states = [st_ref[0, g] for g in range(npairs)]
    pairs = range(npairs)
    sls = [slice(g * LANES, (g + 1) * LANES) for g in pairs]
    v_g = [v[:, sl] for sl in sls]
    at_h = [(jnp.where(first, at[:, sl], 0.0), jnp.where(first, 0.0, at[:, sl])) for sl in sls]
    rt_h = [(jnp.where(first, rt[:, sl], 0.0), jnp.where(first, 0.0, rt[:, sl])) for sl in sls]
    s = [_dot_nt(jnp.concatenate([at_h[g][0], rt_h[g][0], at_h[g][1], rt_h[g][1]], axis=0),
                 jnp.concatenate([bt[:, sls[g]], kt[:, sls[g]]], axis=0))
         for g in pairs]
    sa = [(jnp.where(strict2, s[g][0:n], 0.0), jnp.where(strict2, s[g][2 * n:3 * n], 0.0)) for g in pairs]
    sr = [(jnp.where(incl2, s[g][n:2 * n], 0.0), jnp.where(incl2, s[g][3 * n:4 * n], 0.0)) for g in pairs]
    heads = [(g, j) for g in pairs for j in range(2)]
    vv = [jnp.concatenate([v_g[g], v_g[g]], axis=0) for g in pairs]
    akv = [_dot(jnp.where(first, 0.0, sa[g][j]), vv[g]) for g, j in heads]
    pq = _unit_lower_solve([sa[g][j][:, 0:n] for g, j in heads],
                           [jnp.concatenate([at_h[g][j], akv[2 * g + j]], axis=1) for g, j in heads],
                           diag_blk, eye)
    pt = [_dot_split_rhs(jnp.concatenate([pq[2 * g][:, :LANES], pq[2 * g + 1][:, :LANES]], axis=0).astype(BF16),
                         states[g]) for g in pairs]
    rtt = [_dot_split_rhs(jnp.concatenate(rt_h[g], axis=0).astype(BF16), states[g]) for g in pairs]
    u = [jnp.where(first, pt[g][0:n] + pq[2 * g][:, LANES:], pt[g][n:] + pq[2 * g + 1][:, LANES:]) for g in pairs]
    uv = [jnp.concatenate([u[g], v_g[g]], axis=0) for g in pairs]
    yh = [_dot(sr[g][j], uv[g]) for g, j in heads]
    ys = [jnp.where(first, rtt[g][0:n] + yh[2 * g], rtt[g][n:] + yh[2 * g + 1]) for g in pairs]
    t_new = [_dot(jnp.concatenate([bh[:, sls[g]], kh[:, sls[g]]], axis=0).T, uv[g]) for g in pairs]
    for g in pairs:
        w_col = jnp.broadcast_to(w_tot[:, sls[g]], (LANES, LANES)).T
        st_ref[0, g] = jnp.where(blockdiag, w_col * states[g] + t_new[g], 0.0)
    y = jnp.concatenate(ys, axis=1)
    bd = bd_ref[...]
    inv = 1.0 / HEAD_DIM
    mean = _dot_split_lhs(y, bd) * inv
    d = y - mean
    var = _dot_split_lhs(d * d, bd) * inv
    yn = d * lax.rsqrt(var + GN_EPS) * gw_ref[...] + gb_ref[...]
    bonus = _dot_split_lhs(r * k * rk_ref[...], bd) * v
    y_ref[0] = yn + bonus


def _scan(seq, state0, reverse, p):
    r = seq[0]
    bsz, length, width = r.shape
    nc = length // CHUNK
    npairs = width // LANES
    cmap = (lambda b, c: (b, nc - 1 - c, 0)) if reverse else (lambda b, c: (b, c, 0))
    full = lambda shape: pl.BlockSpec(shape, lambda b, c: (0,) * len(shape))
    st_spec = pl.BlockSpec((1, npairs, LANES, LANES), lambda b, c: (b, 0, 0, 0))
    return pl.pallas_call(
        functools.partial(_scan_kernel, reverse),
        grid=(bsz, nc),
        in_specs=[pl.BlockSpec((1, CHUNK, width), cmap)] * 6 + [st_spec] +
                 [full((1, width))] * 3 + [full((width, width))],
        out_specs=[pl.BlockSpec((1, CHUNK, width), cmap), st_spec],
        out_shape=[jax.ShapeDtypeStruct((bsz, length, width), F32),
                   jax.ShapeDtypeStruct((bsz, npairs, LANES, LANES), F32)],
        compiler_params=_params("parallel", "arbitrary"),
        name="wkv_scan_bwd" if reverse else "wkv_scan_fwd",
    )(*seq, state0, p["r_k"], p["gn_w"], p["gn_b"], p["bd"])


def _mix_kernel(period, tm, yf_ref, yb_ref, glo_ref, zc_ref, g2_ref, cw_ref, o_ref):
    gate = _dot(jax.nn.sigmoid(glo_ref[0]), g2_ref[...])
    yr = (yf_ref[0] + yb_ref[0]) * gate
    zc = zc_ref[0]
    u, gate_b, gate_c = zc[:, 0:256], zc[:, 256:512], zc[:, 512:768]
    hc = gate_c * u
    pos = lax.broadcasted_iota(jnp.int32, (tm, 1), 0) % period
    prev = jnp.where(pos == 0, 0.0, pltpu.roll(hc, 1, 0))
    nxt = jnp.where(pos == period - 1, 0.0, pltpu.roll(hc, tm - 1, 0))
    conv = prev * cw_ref[0:1, :] + hc * cw_ref[1:2, :] + nxt * cw_ref[2:3, :]
    o_ref[0, :, 0:512] = yr.astype(BF16)
    o_ref[0, :, 512:768] = (gate_b * conv).astype(BF16)


def _mix(ys_f, ys_b, zr, zc, g2, conv_w, period, tm):
    bsz, length, _ = zc.shape
    row = lambda b, i: (b, i, 0)
    glo_block = (N_RWKV_COLS - 128) // 128
    return pl.pallas_call(
        functools.partial(_mix_kernel, period, tm),
        grid=(bsz, length // tm),
        in_specs=[pl.BlockSpec((1, tm, 512), row), pl.BlockSpec((1, tm, 512), row),
                  pl.BlockSpec((1, tm, 128), lambda b, i: (b, i, glo_block)),
                  pl.BlockSpec((1, tm, N_CONV_COLS), row),
                  pl.BlockSpec((128, 512), lambda b, i: (0, 0)),
                  pl.BlockSpec((3, 256), lambda b, i: (0, 0))],
        out_specs=pl.BlockSpec((1, tm, 768), row),
        out_shape=jax.ShapeDtypeStruct((bsz, length, 768), BF16),
        compiler_params=_params("parallel", "parallel"),
        name="gate_conv",
    )(ys_f, ys_b, zr, zc, g2, conv_w)


def _chan_dft_kernel(f_ref, w_ref, o_ref):
    xcs = jnp.dot(f_ref[0].astype(BF16), w_ref[...], preferred_element_type=F32)
    o_ref[0, 0] = xcs[:, 0:256].astype(BF16)
    o_ref[0, 1] = xcs[:, 256:512].astype(BF16)


def _pos_dft_kernel(nb, cs_ref, x_ref, o_ref, acc_ref):
    kstep = pl.program_id(1)

    @pl.when(kstep == 0)
    def _():
        acc_ref[...] = jnp.zeros_like(acc_ref)

    cs = cs_ref[...]
    for b in range(nb):
        acc_ref[b] += jnp.dot(cs, x_ref[b], preferred_element_type=F32)

    @pl.when(kstep == pl.num_programs(1) - 1)
    def _():
        o_ref[...] = acc_ref[...].astype(BF16)


def _fourier(zf, chan_w, pos_cs, tm, tk):
    bsz, length, ch = zf.shape
    tr = min(512, length)
    xcs = pl.pallas_call(
        _chan_dft_kernel,
        grid=(bsz, length // tr),
        in_specs=[pl.BlockSpec((1, tr, ch), lambda b, i: (b, i, 0)),
                  pl.BlockSpec((ch, 2 * ch), lambda b, i: (0, 0))],
        out_specs=pl.BlockSpec((1, 2, tr, ch), lambda b, i: (b, 0, i, 0)),
        out_shape=jax.ShapeDtypeStruct((bsz, 2, length, ch), BF16),
        compiler_params=_params("parallel", "parallel"),
        name="chan_dft",
    )(zf, chan_w)
    xcs = xcs.reshape(bsz, 2 * length, ch)
    return pl.pallas_call(
        functools.partial(_pos_dft_kernel, bsz),
        grid=(length // tm, 2 * length // tk),
        in_specs=[pl.BlockSpec((tm, tk), lambda i, kk: (i, kk)),
                  pl.BlockSpec((bsz, tk, ch), lambda i, kk: (0, kk, 0))],
        out_specs=pl.BlockSpec((bsz, tm, ch), lambda i, kk: (0, i, 0)),
        out_shape=jax.ShapeDtypeStruct((bsz, length, ch), BF16),
        scratch_shapes=[pltpu.VMEM((bsz, tm, ch), F32)],
        compiler_params=_params("parallel", "arbitrary"),
        name="pos_dft",
    )(pos_cs, xcs)


def _dft_tables(length):
    m = jnp.arange(length, dtype=jnp.int32)[None, :]
    l1 = jnp.arange(length // 64, dtype=jnp.int32)[:, None] * 64
    l2 = jnp.arange(64, dtype=jnp.int32)[:, None]
    ang = lambda l: ((l * m) % length).astype(F32) * (2.0 * jnp.pi / length)
    c1, s1, c2, s2 = jnp.cos(ang(l1)), jnp.sin(ang(l1)), jnp.cos(ang(l2)), jnp.sin(ang(l2))
    scale = 1.0 / jnp.sqrt(jnp.float32(length))
    cos = (c1[:, None, :] * c2[None] - s1[:, None, :] * s2[None]).reshape(length, length) * scale
    sin = (s1[:, None, :] * c2[None] + c1[:, None, :] * s2[None]).reshape(length, length) * scale
    return jnp.concatenate([cos, sin], axis=1).astype(BF16)


def _chan_dft_weights(ngroups):
    j = jnp.arange(HEAD_DIM, dtype=jnp.int32)
    ang = ((j[:, None] * j[None, :]) % HEAD_DIM).astype(F32) * (2.0 * jnp.pi / HEAD_DIM)
    eye = jnp.eye(ngroups, dtype=F32)
    scale = 1.0 / jnp.sqrt(jnp.float32(HEAD_DIM))
    cc = jnp.kron(eye, jnp.cos(ang) * scale)
    sc = jnp.kron(eye, jnp.sin(ang) * scale)
    return jnp.concatenate([cc, -sc], axis=1).astype(BF16)


def _outproj_kernel(yrc_ref, yf_ref, w_ref, x_ref, gn1_ref, gate_ref, gn2_ref, sc_ref, sh_ref, xo_ref, h_ref):
    nrc = yrc_ref.shape[2]
    o = (jnp.dot(yrc_ref[0], w_ref[0:nrc, :], preferred_element_type=F32) +
         jnp.dot(yf_ref[0], w_ref[nrc:, :], preferred_element_type=F32))
    xn = x_ref[0] + gate_ref[0] * _rms(o, gn1_ref[...])
    xo_ref[0] = xn
    h_ref[0] = (_rms(xn, gn2_ref[...]) * (1.0 + sc_ref[0]) + sh_ref[0]).astype(BF16)


def _outproj(yrc, yf, w, x, gn1, gate, gn2, sc, sh, tm):
    bsz, length, d = x.shape
    row = lambda b, i: (b, i, 0)
    vec = pl.BlockSpec((1, d), lambda b, i: (0, 0))
    mod = pl.BlockSpec((1, 1, d), lambda b, i: (b, 0, 0))
    return pl.pallas_call(
        _outproj_kernel,
        grid=(bsz, length // tm),
        in_specs=[pl.BlockSpec((1, tm, yrc.shape[2]), row), pl.BlockSpec((1, tm, yf.shape[2]), row),
                  pl.BlockSpec(w.shape, lambda b, i: (0, 0)), pl.BlockSpec((1, tm, d), row),
                  vec, mod, vec, mod, mod],
        out_specs=[pl.BlockSpec((1, tm, d), row), pl.BlockSpec((1, tm, d), row)],
        out_shape=[jax.ShapeDtypeStruct((bsz, length, d), F32), jax.ShapeDtypeStruct((bsz, length, d), BF16)],
        compiler_params=_params("parallel", "parallel"),
        name="outproj",
    )(yrc, yf, w, x, gn1, gate, gn2, sc, sh)


FF_SUB = 512


def _swiglu_partial(h, wg, wu, wd, tf):
    def up(c, n):
        return (jnp.dot(h, wg(c, n), preferred_element_type=F32), jnp.dot(h, wu(c, n), preferred_element_type=F32))

    def down(gu, c, n):
        return jnp.dot((_silu(gu[0]) * gu[1]).astype(BF16), wd(c, n), preferred_element_type=F32)

    out = None
    prev = None
    for cut in [(c, min(FF_SUB, tf - c)) for c in range(0, tf, FF_SUB)]:
        gu = up(*cut)
        if prev is not None:
            part = down(*prev)
            out = part if out is None else out + part
        prev = (gu,) + cut
    part = down(*prev)
    return part if out is None else out + part


def _ffn_kernel(h_ref, wg_ref, wu_ref, wd_ref, x_ref, gn_ref, gate_ref, o_ref, acc_ref):
    j = pl.program_id(2)

    @pl.when(j == 0)
    def _():
        acc_ref[...] = jnp.zeros_like(acc_ref)

    acc_ref[...] += _swiglu_partial(h_ref[0], lambda c, n: wg_ref[:, c:c + n], lambda c, n: wu_ref[:, c:c + n],
                                    lambda c, n: wd_ref[c:c + n, :], wg_ref.shape[1])

    @pl.when(j == pl.num_programs(2) - 1)
    def _():
        o_ref[0] = x_ref[0] + gate_ref[0] * _rms(acc_ref[...], gn_ref[...])


def _ffn(h, wg, wu, wd, x, gn, gate, tm, tf):
    bsz, length, d = x.shape
    ff = wg.shape[1]
    row = lambda b, i, j: (b, i, 0)
    return pl.pallas_call(
        _ffn_kernel,
        grid=(bsz, length // tm, ff // tf),
        in_specs=[pl.BlockSpec((1, tm, d), row),
                  pl.BlockSpec((d, tf), lambda b, i, j: (0, j)),
                  pl.BlockSpec((d, tf), lambda b, i, j: (0, j)),
                  pl.BlockSpec((tf, d), lambda b, i, j: (j, 0)),
                  pl.BlockSpec((1, tm, d), row),
                  pl.BlockSpec((1, d), lambda b, i, j: (0, 0)),
                  pl.BlockSpec((1, 1, d), lambda b, i, j: (b, 0, 0))],
        out_specs=pl.BlockSpec((1, tm, d), row),
        out_shape=jax.ShapeDtypeStruct((bsz, length, d), F32),
        scratch_shapes=[pltpu.VMEM((tm, d), F32)],
        compiler_params=_params("parallel", "parallel", "arbitrary"),
        name="ffn",
    )(h, wg, wu, wd, x, gn, gate)


MOE_TILE = 512
MOE_BLOCK = 128
MOE_ALIGN = 16
MOE_TM = 1024


def _router_kernel(x_ref, gn_ref, sc_ref, sh_ref, w_ref, b_ref, tri_ref, o_ref, rank_ref, rankt_ref, cnt_ref):
    h = _rms(x_ref[0], gn_ref[...]) * (1.0 + sc_ref[0]) + sh_ref[0]
    logits = jnp.dot(h, w_ref[...], preferred_element_type=F32, precision=lax.Precision.HIGHEST) + b_ref[...]
    ne = float(logits.shape[1])
    idx = lax.broadcasted_iota(jnp.int32, logits.shape, 1).astype(F32)
    m1 = jnp.max(logits, axis=1, keepdims=True)
    i1 = jnp.min(jnp.where(logits == m1, idx, ne), axis=1, keepdims=True)
    rest = jnp.where(idx == i1, -jnp.inf, logits)
    m2 = jnp.max(rest, axis=1, keepdims=True)
    i2 = jnp.min(jnp.where(rest == m2, idx, ne), axis=1, keepdims=True)
    e2 = jnp.exp(m2 - m1)
    p1 = 1.0 / (1.0 + e2)
    p2 = e2 / (1.0 + e2)
    o_ref[0] = jnp.where(idx == i1, p1, 0.0) + jnp.where(idx == i2, p2, 0.0)
    sel = jnp.where(idx == i1, 1.0, jnp.where(idx == i2, 1.0, 0.0))
    before = jnp.dot(tri_ref[...], sel.astype(BF16), preferred_element_type=F32)
    rank = jnp.where(sel > 0.0, before, -1.0)
    rank_ref[0] = rank
    rankt_ref[0] = jnp.transpose(rank)[0:N_EXPERTS, :]
    cnt_ref[0, 0] = jnp.sum(sel, axis=0, keepdims=True)


def _router(x, gn, sc, sh, w, b):
    bsz, length, d = x.shape
    tm = MOE_TILE
    pad = LANES - w.shape[1]
    w = jnp.pad(w, ((0, 0), (0, pad)))
    b = jnp.pad(b, ((0, 0), (0, pad)), constant_values=-1e30)
    ne = LANES
    t = jnp.arange(tm, dtype=jnp.int32)
    tri = (t[None, :] < t[:, None]).astype(BF16)
    row = lambda bb, i: (bb, i, 0)
    vec = pl.BlockSpec((1, d), lambda bb, i: (0, 0))
    mod = pl.BlockSpec((1, 1, d), lambda bb, i: (bb, 0, 0))
    nt = length // tm
    return pl.pallas_call(
        _router_kernel,
        grid=(bsz, nt),
        in_specs=[pl.BlockSpec((1, tm, d), row), vec, mod, mod,
                  pl.BlockSpec((d, ne), lambda bb, i: (0, 0)), pl.BlockSpec((1, ne), lambda bb, i: (0, 0)),
                  pl.BlockSpec((tm, tm), lambda bb, i: (0, 0))],
        out_specs=[pl.BlockSpec((1, tm, ne), row), pl.BlockSpec((1, tm, ne), row),
                   pl.BlockSpec((1, N_EXPERTS, tm), lambda bb, i: (bb, 0, i)),
                   pl.BlockSpec((1, 1, 1, ne), lambda bb, i: (bb, i, 0, 0))],
        out_shape=[jax.ShapeDtypeStruct((bsz, length, ne), F32), jax.ShapeDtypeStruct((bsz, length, ne), F32),
                   jax.ShapeDtypeStruct((bsz, N_EXPERTS, length), F32),
                   jax.ShapeDtypeStruct((bsz, nt, 1, ne), F32)],
        compiler_params=_params("parallel", "parallel"),
        name="router",
    )(x, gn, sc, sh, w, b, tri)


_MOE_SHIFT = MOE_BLOCK.bit_length() - 1
_MOE_BUF_BLOCKS = (2 * MOE_TILE + N_EXPERTS * (MOE_ALIGN - 1 + MOE_BLOCK - 1)) // MOE_BLOCK + 1


def _align_rows(c):
    return ((c + (MOE_ALIGN - 1)) >> (MOE_ALIGN.bit_length() - 1)) << (MOE_ALIGN.bit_length() - 1)


def _blocks(c):
    return (c + (MOE_BLOCK - 1)) >> _MOE_SHIFT


def _dispatch_copies(start, tile, off_ref, cnt_ref, buf_ref, hs_ref, sem):
    def run(src_row, dst_row, nrows):
        cp = pltpu.make_async_copy(buf_ref.at[pl.ds(pl.multiple_of(src_row, MOE_ALIGN), nrows)],
                                   hs_ref.at[pl.ds(pl.multiple_of(dst_row, MOE_ALIGN), nrows)], sem)
        if start:
            cp.start()
        else:
            cp.wait()

    slot = jnp.int32(0)
    for e in range(N_EXPERTS):
        rows = _align_rows(cnt_ref[tile * N_EXPERTS + e])
        dst0 = off_ref[tile * N_EXPERTS + e]
        src0 = slot * MOE_BLOCK
        nfull = rows >> _MOE_SHIFT

        def full(rb, carry, src0=src0, dst0=dst0):
            run(src0 + rb * MOE_BLOCK, dst0 + rb * MOE_BLOCK, MOE_BLOCK)
            return carry

        lax.fori_loop(0, nfull, full, 0)
        done = nfull * MOE_BLOCK
        piece = MOE_BLOCK // 2
        while piece >= MOE_ALIGN:
            has = (rows & piece) != 0

            @pl.when(has)
            def _(done=done, piece=piece, src0=src0, dst0=dst0):
                run(src0 + done, dst0 + done, piece)

            done = done + jnp.where(has, piece, 0)
            piece //= 2
        slot = slot + _blocks(rows)


def _dispatch_kernel(off_ref, cnt_ref, h_ref, rankt_ref, hs_init_ref, hs_ref, buf_ref, sem):
    del hs_init_ref
    tile = pl.program_id(0)
    h = h_ref[...]
    rows = lax.broadcasted_iota(jnp.int32, (MOE_BLOCK, 1), 0)
    slot = jnp.int32(0)
    for e in range(N_EXPERTS):
        nb = _blocks(_align_rows(cnt_ref[tile * N_EXPERTS + e]))
        rank_row = rankt_ref[0, e:e + 1, :]

        def body(rb, carry, slot=slot, rank_row=rank_row):
            want = (rows + rb * MOE_BLOCK).astype(F32)
            onehot = jnp.where(rank_row == want, 1.0, 0.0).astype(BF16)
            dst = pl.multiple_of((slot + rb) * MOE_BLOCK, MOE_BLOCK)
            buf_ref[pl.ds(dst, MOE_BLOCK), :] = jnp.dot(onehot, h, preferred_element_type=F32).astype(BF16)
            return carry

        lax.fori_loop(0, nb, body, 0)
        slot = slot + nb
    _dispatch_copies(True, tile, off_ref, cnt_ref, buf_ref, hs_ref, sem)
    _dispatch_copies(False, tile, off_ref, cnt_ref, buf_ref, hs_ref, sem)


def _dispatch(h, rankt, off, cnt, nrows):
    n, d = h.shape
    ntb = rankt.shape[2] // MOE_TILE
    grid_spec = pltpu.PrefetchScalarGridSpec(
        num_scalar_prefetch=2, grid=(n // MOE_TILE,),
        in_specs=[pl.BlockSpec((MOE_TILE, d), lambda t, o, c: (t, 0)),
                  pl.BlockSpec((1, N_EXPERTS, MOE_TILE), lambda t, o, c: (t // ntb, 0, t % ntb)),
                  pl.BlockSpec(memory_space=pl.ANY)],
        out_specs=pl.BlockSpec(memory_space=pl.ANY),
        scratch_shapes=[pltpu.VMEM((_MOE_BUF_BLOCKS * MOE_BLOCK, d), BF16), pltpu.SemaphoreType.DMA(())])
    return pl.pallas_call(
        _dispatch_kernel, grid_spec=grid_spec,
        out_shape=jax.ShapeDtypeStruct((nrows, d), BF16),
        input_output_aliases={4: 0},
        compiler_params=_params("arbitrary"),
        name="moe_dispatch",
    )(off, cnt, h, rankt, jnp.zeros((nrows, d), BF16))


def _gffn_kernel(te_ref, tv_ref, h_ref, wg_ref, wu_ref, wd_ref, o_ref, acc_ref):
    q = pl.program_id(0)
    j = pl.program_id(1)

    @pl.when(j == 0)
    def _():
        acc_ref[...] = jnp.zeros_like(acc_ref)

    @pl.when(tv_ref[q] > 0)
    def _():
        acc_ref[...] += _swiglu_partial(h_ref[...], lambda c, n: wg_ref[0, :, c:c + n],
                                        lambda c, n: wu_ref[0, :, c:c + n], lambda c, n: wd_ref[0, c:c + n, :],
                                        wg_ref.shape[2])

    @pl.when(j == pl.num_programs(1) - 1)
    def _():
        o_ref[...] = acc_ref[...].astype(BF16)


def _gffn(hs, te, tv, wg, wu, wd, tf):
    nrows, d = hs.shape
    ff = wg.shape[2]
    grid_spec = pltpu.PrefetchScalarGridSpec(
        num_scalar_prefetch=2, grid=(nrows // MOE_TM, ff // tf),
        in_specs=[pl.BlockSpec((MOE_TM, d), lambda q, j, te, tv: (q, 0)),
                  pl.BlockSpec((1, d, tf), lambda q, j, te, tv: (te[q], 0, j)),
                  pl.BlockSpec((1, d, tf), lambda q, j, te, tv: (te[q], 0, j)),
                  pl.BlockSpec((1, tf, d), lambda q, j, te, tv: (te[q], j, 0))],
        out_specs=pl.BlockSpec((MOE_TM, d), lambda q, j, te, tv: (q, 0)),
        scratch_shapes=[pltpu.VMEM((MOE_TM, d), F32)])
    return pl.pallas_call(
        _gffn_kernel, grid_spec=grid_spec,
        out_shape=jax.ShapeDtypeStruct((nrows, d), BF16),
        compiler_params=_params("arbitrary", "arbitrary"),
        name="moe_experts",
    )(te, tv, hs, wg, wu, wd)


def _combine_copies(start, tile, off_ref, cnt_ref, ys_ref, buf_ref, sem):
    slot = jnp.int32(0)
    for e in range(N_EXPERTS):
        nb = _blocks(cnt_ref[tile * N_EXPERTS + e])
        src0 = off_ref[tile * N_EXPERTS + e]

        def body(rb, carry, slot=slot, src0=src0):
            cp = pltpu.make_async_copy(
                ys_ref.at[pl.ds(pl.multiple_of(src0 + rb * MOE_BLOCK, MOE_ALIGN), MOE_BLOCK)],
                buf_ref.at[pl.ds(pl.multiple_of((slot + rb) * MOE_BLOCK, MOE_BLOCK), MOE_BLOCK)], sem)
            if start:
                cp.start()
            else:
                cp.wait()
            return carry

        lax.fori_loop(0, nb, body, 0)
        slot = slot + nb


def _combine_kernel(off_ref, cnt_ref, ys_ref, rank_ref, gates_ref, x_ref, gn_ref, gate_ref, o_ref, buf_ref, acc_ref,
                    sem):
    tile = pl.program_id(0)
    _combine_copies(True, tile, off_ref, cnt_ref, ys_ref, buf_ref, sem)
    acc_ref[...] = jnp.zeros_like(acc_ref)
    _combine_copies(False, tile, off_ref, cnt_ref, ys_ref, buf_ref, sem)
    lanes = lax.broadcasted_iota(jnp.int32, (1, MOE_BLOCK), 1)
    rank = rank_ref[0]
    gates = gates_ref[0]
    slot = jnp.int32(0)
    for e in range(N_EXPERTS):
        nb = _blocks(cnt_ref[tile * N_EXPERTS + e])
        rank_col = rank[:, e:e + 1]
        gate_col = gates[:, e:e + 1]

        def body(rb, carry, slot=slot, rank_col=rank_col, gate_col=gate_col):
            want = (lanes + rb * MOE_BLOCK).astype(F32)
            scatter = jnp.where(rank_col == want, gate_col, 0.0).astype(BF16)
            src = pl.multiple_of((slot + rb) * MOE_BLOCK, MOE_BLOCK)
            acc_ref[...] += jnp.dot(scatter, buf_ref[pl.ds(src, MOE_BLOCK), :], preferred_element_type=F32)
            return carry

        lax.fori_loop(0, nb, body, 0)
        slot = slot + nb
    o_ref[0] = x_ref[0] + gate_ref[0] * _rms(acc_ref[...], gn_ref[...])


def _combine(ys, rank, gates, x, gn, gate, off, cnt):
    bsz, length, d = x.shape
    ntb = length // MOE_TILE
    tok = lambda t, o, c: (t // ntb, t % ntb, 0)
    grid_spec = pltpu.PrefetchScalarGridSpec(
        num_scalar_prefetch=2, grid=(bsz * ntb,),
        in_specs=[pl.BlockSpec(memory_space=pl.ANY),
                  pl.BlockSpec((1, MOE_TILE, LANES), tok), pl.BlockSpec((1, MOE_TILE, LANES), tok),
                  pl.BlockSpec((1, MOE_TILE, d), tok),
                  pl.BlockSpec((1, d), lambda t, o, c: (0, 0)),
                  pl.BlockSpec((1, 1, d), lambda t, o, c: (t // ntb, 0, 0))],
        out_specs=pl.BlockSpec((1, MOE_TILE, d), tok),
        scratch_shapes=[pltpu.VMEM((_MOE_BUF_BLOCKS * MOE_BLOCK, d), BF16), pltpu.VMEM((MOE_TILE, d), F32),
                        pltpu.SemaphoreType.DMA(())])
    return pl.pallas_call(
        _combine_kernel, grid_spec=grid_spec,
        out_shape=jax.ShapeDtypeStruct((bsz, length, d), F32),
        compiler_params=_params("arbitrary"),
        name="moe_combine",
    )(off, cnt, ys, rank, gates, x, gn, gate)


def _route_plan(counts, ntokens):
    ntiles = counts.shape[0]
    rows = (counts + (MOE_ALIGN - 1)) // MOE_ALIGN * MOE_ALIGN
    seg = (rows.sum(0) + MOE_BLOCK + MOE_TM - 1) // MOE_TM * MOE_TM
    seg_end = jnp.cumsum(seg)
    off = (seg_end - seg)[None, :] + jnp.cumsum(rows, axis=0) - rows
    nq = -(-(2 * ntokens + ntiles * N_EXPERTS * (MOE_ALIGN - 1) + N_EXPERTS * (MOE_BLOCK + MOE_TM - 1)) // MOE_TM)
    q = jnp.arange(nq, dtype=jnp.int32) * MOE_TM
    te = jnp.minimum(jnp.sum(q[:, None] >= seg_end[None, :], axis=1), N_EXPERTS - 1).astype(jnp.int32)
    tv = (q < seg_end[-1]).astype(jnp.int32)
    return off.reshape(-1).astype(jnp.int32), te, tv, nq * MOE_TM


def _moe(h, x, gn_pre, sc, sh, router_w, router_b, wg, wu, wd, gn_post, gate, tf):
    bsz, length, d = x.shape
    gates, rank, rankt, cnt = _router(x, gn_pre, sc, sh, router_w, router_b)
    counts = cnt[:, :, 0, :N_EXPERTS].astype(jnp.int32).reshape(-1, N_EXPERTS)
    off, te, tv, nrows = _route_plan(counts, bsz * length)
    cnt_flat = counts.reshape(-1)
    hs = _dispatch(h.reshape(bsz * length, d), rankt, off, cnt_flat, nrows)
    ys = _gffn(hs, te, tv, wg, wu, wd, tf)
    return _combine(ys, rank, gates, x, gn_post, gate, off, cnt_flat)


def _sincos_2d(rows, cols, dim):
    quarter = dim // 4
    omega = 1.0 / (POS_BASE ** (jnp.arange(quarter, dtype=F32) / quarter))

    def axis_emb(n):
        ang = jnp.arange(n, dtype=F32)[:, None] * omega[None, :]
        return jnp.concatenate([jnp.sin(ang), jnp.cos(ang)], axis=-1)

    er, ec = axis_emb(rows), axis_emb(cols)
    emb = jnp.concatenate([jnp.broadcast_to(er[:, None, :], (rows, cols, dim // 2)),
                           jnp.broadcast_to(ec[None, :, :], (rows, cols, dim // 2))], axis=-1)
    return emb.reshape(rows * cols, dim)


def _head_block_ones(width):
    h = jnp.arange(width, dtype=jnp.int32) // HEAD_DIM
    return (h[:, None] == h[None, :]).astype(BF16)


def _pad_lora(w2, d):
    z = jnp.zeros_like(w2[0])
    return jnp.concatenate([w2[0], z], axis=0) if d == 0 else jnp.concatenate([z, w2[1]], axis=0)


def _tile(length, pref):
    return pref if length % pref == 0 else length


def kernel(x, c, ctx, c_ctx, ada_w, ada_b, norm_g, w_in, w_out, rwkv_mu, rwkv_mu_w, rwkv_mu_a, rwkv_w0, rwkv_w2,
           rwkv_a0, rwkv_a2, rwkv_g2, rwkv_k_k, rwkv_k_a, rwkv_r_k, rwkv_gn_w, rwkv_gn_b, conv_w, ffn_w_gate,
           ffn_w_up, ffn_w_down, router_w, router_b, moe_w_gate, moe_w_up, moe_w_down):
    bsz, length, dim = x.shape
    ctx_len = ctx.shape[1]
    depth = ada_w.shape[0]
    d_rwkv = rwkv_w0.shape[-1]

    pos = _sincos_2d(length // GRID_W, GRID_W, dim).astype(x.dtype)
    bd = _head_block_ones(d_rwkv)
    chan_w = _chan_dft_weights(N_FOUR_COLS // HEAD_DIM)
    dft_lat = _dft_tables(length)
    dft_ctx = _dft_tables(ctx_len)

    cvec = jnp.zeros((16, dim), F32).at[:bsz].set(c).at[bsz].set(c_ctx)
    mod = _adaln(cvec, ada_w, ada_b)
    zero_state = jnp.zeros((bsz, d_rwkv // LANES, LANES, LANES), F32)
    xc = ctx

    def mods(l, lo, hi):
        return [mod[l, lo:hi, i * dim:(i + 1) * dim][:, None, :] for i in range(6)]

    def mixers(l, zr, zc, zf, states0, period, dft, need_y=True):
        tm = _tile(zr.shape[1], 512)
        ys, states = [], []
        for d, reverse in enumerate((False, True)):
            p = {"mu": rwkv_mu[l, d], "mu_w": rwkv_mu_w[l].reshape(1, -1), "mu_a": rwkv_mu_a[l].reshape(1, -1),
                 "w0": rwkv_w0[l, d][None], "w2": _pad_lora(rwkv_w2[l], d), "a0": rwkv_a0[l, d][None],
                 "a2": _pad_lora(rwkv_a2[l], d), "k_k": rwkv_k_k[l][None], "k_a": rwkv_k_a[l][None],
                 "r_k": rwkv_r_k[l].reshape(1, -1), "gn_w": rwkv_gn_w[l][None], "gn_b": rwkv_gn_b[l][None],
                 "bd": bd}
            seq = _prep(zr, reverse, p, tm)
            y, st = _scan(seq, states0[d], reverse, p)
            ys.append(y)
            states.append(st)
        if not need_y:
            return None, None, states
        yrc = _mix(ys[0], ys[1], zr, zc, rwkv_g2[l], conv_w[l], period, tm)
        yf = _fourier(zf, chan_w, dft, _tile(zf.shape[1], 512), 512)
        return yrc, yf, states

    def channel_mixer(l, h2, xn, gn3, gate2, sc2, sh2, gn2):
        tm = _tile(xn.shape[1], 512)
        i = l // 2
        if l % 2 == 0:
            return _ffn(h2, ffn_w_gate[i].astype(BF16), ffn_w_up[i].astype(BF16), ffn_w_down[i].astype(BF16),
                        xn, gn3, gate2, tm, 1408)
        return _moe(h2, xn, gn2, sc2, sh2, router_w[i], router_b[i][None], moe_w_gate[i].astype(BF16),
                    moe_w_up[i].astype(BF16), moe_w_down[i].astype(BF16), gn3, gate2, 896)

    for l in range(depth):
        last = l == depth - 1
        w_in_l = w_in[l].astype(BF16)
        w_out_l = w_out[l].astype(BF16)
        gn = [norm_g[l, i][None] for i in range(4)]
        sh1, sc1, g1, sh2, sc2, g2 = mods(l, 0, bsz)
        csh1, csc1, cg1, csh2, csc2, cg2 = mods(l, bsz, bsz + 1)

        xc_flat = xc.reshape(1, bsz * ctx_len, dim)
        tmc = _tile(bsz * ctx_len, 512)
        zr, zc, zf = _inproj(xc_flat, None, gn[0], csc1, csh1, w_in_l, tmc)
        unflat = lambda t: t.reshape(bsz, ctx_len, t.shape[-1])
        yrc, yf, ctx_states = mixers(l, unflat(zr), unflat(zc), unflat(zf), (zero_state, zero_state), ctx_len,
                                     dft_ctx, need_y=not last)
        if not last:
            flat = lambda t: t.reshape(1, bsz * ctx_len, t.shape[-1])
            xcn, hc2 = _outproj(flat(yrc), flat(yf), w_out_l, xc_flat, gn[1], cg1, gn[2], csc2, csh2, tmc)
            xc = channel_mixer(l, hc2, xcn, gn[3], cg2, csc2, csh2, gn[2]).reshape(bsz, ctx_len, dim)

        tm = _tile(length, 256)
        if l == 0:
            zr, zc, zf, x = _inproj(x, pos, gn[0], sc1, sh1, w_in_l, tm)
        else:
            zr, zc, zf = _inproj(x, None, gn[0], sc1, sh1, w_in_l, tm)
        yrc, yf, _ = mixers(l, zr, zc, zf, ctx_states, GRID_W, dft_lat)
        xn, h2 = _outproj(yrc, yf, w_out_l, x, gn[1], g1, gn[2], sc2, sh2, _tile(length, 512))
        x = channel_mixer(l, h2, xn, gn[3], g2, sc2, sh2, gn[2])
    return x
```

```python
import functools

import jax
import jax.numpy as jnp
from jax import lax
from jax.experimental import pallas as pl
from jax.experimental.pallas import tpu as pltpu

F32 = jnp.float32
BF16 = jnp.bfloat16

HEAD_DIM = 64
GRID_W = 64
NORM_EPS = 1e-6
GN_EPS = 64e-5
POS_BASE = 10000.0
N_EXPERTS = 8
CHUNK = 64
LANES = 128
VMEM_LIMIT = 56 * 1024 * 1024


def _params(*sem):
    return pltpu.CompilerParams(dimension_semantics=sem, vmem_limit_bytes=VMEM_LIMIT)


def _dot(a, b):
    return jnp.dot(a.astype(BF16), b.astype(BF16), preferred_element_type=F32)


def _dot_nt(a, b):
    return lax.dot_general(a.astype(BF16), b.astype(BF16), (((1,), (1,)), ((), ())),
                           preferred_element_type=F32)


def _split(x):
    hi = x.astype(BF16)
    lo = (x - hi.astype(F32)).astype(BF16)
    return hi, lo


def _head_sums(x, ones_pair):
    rows = x.shape[0]
    tiles = x.shape[1] // LANES
    hi, lo = _split(x)
    stacked = jnp.concatenate([t[:, g * LANES:(g + 1) * LANES] for t in (hi, lo) for g in range(tiles)], axis=0)
    s = jnp.dot(stacked, ones_pair, preferred_element_type=F32)
    return jnp.concatenate([s[g * rows:(g + 1) * rows] + s[(tiles + g) * rows:(tiles + g + 1) * rows]
                            for g in range(tiles)], axis=1)


def _dot_split_rhs(w, x):
    hi, lo = _split(x)
    return (jnp.dot(w, hi, preferred_element_type=F32) + jnp.dot(w, lo, preferred_element_type=F32))


def _rms(x, g):
    ms = jnp.mean(x * x, axis=-1, keepdims=True)
    return x * lax.rsqrt(ms + NORM_EPS) * g


def _silu(x):
    return x * jax.nn.sigmoid(x)


def _softplus(x):
    return jnp.maximum(x, 0.0) + jnp.log(1.0 + jnp.exp(-jnp.abs(x)))


def _adaln_kernel(c_ref, w_ref, b_ref, o_ref):
    s = _silu(c_ref[...])
    o_ref[0] = _dot(s, w_ref[0]) + b_ref[0]


def _adaln(cvec, ada_w, ada_b):
    nl, d, n = ada_w.shape
    rows = cvec.shape[0]
    tn = 1536
    return pl.pallas_call(
        _adaln_kernel,
        grid=(nl, n // tn),
        in_specs=[pl.BlockSpec((rows, d), lambda l, j: (0, 0)),
                  pl.BlockSpec((1, d, tn), lambda l, j: (l, 0, j)),
                  pl.BlockSpec((1, 1, tn), lambda l, j: (l, 0, j))],
        out_specs=pl.BlockSpec((1, rows, tn), lambda l, j: (l, 0, j)),
        out_shape=jax.ShapeDtypeStruct((nl, rows, n), F32),
        compiler_params=_params("arbitrary", "arbitrary"),
        name="adaln",
    )(cvec, ada_w, ada_b.reshape(nl, 1, n))


N_RWKV_COLS = 1920
N_CONV_COLS = 768
N_FOUR_COLS = 256


def _inproj_kernel(add_pos, *refs):
    if add_pos:
        x_ref, pos_ref, g_ref, sc_ref, sh_ref, w_ref, zr_ref, zc_ref, zf_ref, xp_ref = refs
        x = x_ref[0] + pos_ref[...]
        xp_ref[0] = x
    else:
        x_ref, g_ref, sc_ref, sh_ref, w_ref, zr_ref, zc_ref, zf_ref = refs
        x = x_ref[0]
    h = (_rms(x, g_ref[...]) * (1.0 + sc_ref[0]) + sh_ref[0]).astype(BF16)
    a, b = N_RWKV_COLS, N_RWKV_COLS + N_CONV_COLS
    zr_ref[0] = jnp.dot(h, w_ref[:, :a], preferred_element_type=F32)
    zc_ref[0] = jnp.dot(h, w_ref[:, a:b], preferred_element_type=F32)
    zf_ref[0] = jnp.dot(h, w_ref[:, b:], preferred_element_type=F32)


def _inproj(x, pos, g, sc, sh, w, tm):
    bsz, length, d = x.shape
    n = w.shape[1]
    add_pos = pos is not None
    row = lambda b, i: (b, i, 0)
    in_specs = [pl.BlockSpec((1, tm, d), row)]
    args = [x]
    if add_pos:
        in_specs.append(pl.BlockSpec((tm, d), lambda b, i: (i, 0)))
        args.append(pos)
    in_specs += [pl.BlockSpec((1, d), lambda b, i: (0, 0)),
                 pl.BlockSpec((1, 1, d), lambda b, i: (b, 0, 0)),
                 pl.BlockSpec((1, 1, d), lambda b, i: (b, 0, 0)),
                 pl.BlockSpec((d, n), lambda b, i: (0, 0))]
    args += [g, sc, sh, w]
    out_shape = [jax.ShapeDtypeStruct((bsz, length, N_RWKV_COLS), F32),
                 jax.ShapeDtypeStruct((bsz, length, N_CONV_COLS), F32),
                 jax.ShapeDtypeStruct((bsz, length, N_FOUR_COLS), F32)]
    out_specs = [pl.BlockSpec((1, tm, N_RWKV_COLS), row),
                 pl.BlockSpec((1, tm, N_CONV_COLS), row),
                 pl.BlockSpec((1, tm, N_FOUR_COLS), row)]
    if add_pos:
        out_shape.append(jax.ShapeDtypeStruct((bsz, length, d), F32))
        out_specs.append(pl.BlockSpec((1, tm, d), row))
    return pl.pallas_call(
        functools.partial(_inproj_kernel, add_pos),
        grid=(bsz, length // tm),
        in_specs=in_specs, out_specs=out_specs, out_shape=out_shape,
        compiler_params=_params("parallel", "parallel"),
        name="inproj",
    )(*args)


HALO = 8


def _prep_kernel(tm, z_ref, prev_ref, next_ref, mu_ref, muw_ref, mua_ref, w0_ref, w2_ref, a0_ref, a2_ref,
                 kk_ref, ka_ref, bd_ref, *out_refs):
    i = pl.program_id(1)
    last = pl.num_programs(1) - 1
    z = z_ref[0]
    rows = lax.broadcasted_iota(jnp.int32, (tm, 1), 0)
    for d, reverse in enumerate((False, True)):
        if reverse:
            hrow = next_ref[0, 0:1, :] * jnp.where(i == last, 0.0, 1.0)
            zs = jnp.where(rows == tm - 1, hrow, pltpu.roll(z, tm - 1, 0))
        else:
            hrow = prev_ref[0, HALO - 1:HALO, :] * jnp.where(i == 0, 0.0, 1.0)
            zs = jnp.where(rows == 0, hrow, pltpu.roll(z, 1, 0))
        dz = zs - z
        k = z[:, 0:512] + dz[:, 0:512] * mu_ref[d, 0:1, :]
        v = z[:, 512:1024] + dz[:, 512:1024] * mu_ref[d, 1:2, :]
        r = z[:, 1280:1792] + dz[:, 1280:1792] * mu_ref[d, 2:3, :]
        wl = z[:, 1024:1152] + dz[:, 1024:1152] * muw_ref[...]
        al = z[:, 1152:1280] + dz[:, 1152:1280] * mua_ref[...]
        w_log = -_softplus(-(w0_ref[d:d + 1, :] + _dot(jnp.tanh(wl), w2_ref[d]))) - 0.5
        lw = -jnp.exp(w_log)
        a = jax.nn.sigmoid(a0_ref[d:d + 1, :] + _dot(al, a2_ref[d]))
        kk = k * kk_ref[...]
        ss = _head_sums(kk * kk, bd_ref[...])
        kk = kk / jnp.maximum(jnp.sqrt(ss), 1e-12)
        r_ref, lw_ref, k_ref, v_ref, a_ref, b_ref = out_refs[6 * d:6 * d + 6]
        r_ref[0] = r
        lw_ref[0] = lw
        k_ref[0] = k * (1.0 + (a - 1.0) * ka_ref[...])
        v_ref[0] = v
        a_ref[0] = -kk
        b_ref[0] = kk * a


def _prep(zr, p, tm):
    bsz, length, ncol = zr.shape
    nt = length // tm
    hb = tm // HALO
    nhb = length // HALO
    prev_map = lambda b, i: (b, jnp.maximum(i * hb - 1, 0), 0)
    next_map = lambda b, i: (b, jnp.minimum((i + 1) * hb, nhb - 1), 0)
    full = lambda shape: pl.BlockSpec(shape, lambda b, i: (0,) * len(shape))
    row = lambda b, i: (b, i, 0)
    out = jax.ShapeDtypeStruct((bsz, length, 512), F32)
    outs = pl.pallas_call(
        functools.partial(_prep_kernel, tm),
        grid=(bsz, nt),
        in_specs=[pl.BlockSpec((1, tm, ncol), row), pl.BlockSpec((1, HALO, ncol), prev_map),
                  pl.BlockSpec((1, HALO, ncol), next_map),
                  full((2, 3, 512)), full((1, 128)), full((1, 128)), full((2, 512)), full((2, 128, 512)),
                  full((2, 512)), full((2, 128, 512)), full((1, 512)), full((1, 512)), full((LANES, LANES))],
        out_specs=[pl.BlockSpec((1, tm, 512), row)] * 12,
        out_shape=[out] * 12,
        compiler_params=_params("parallel", "parallel"),
        name="rwkv_prep",
    )(zr, zr, zr, p["mu"], p["mu_w"], p["mu_a"], p["w0"], p["w2"], p["a0"], p["a2"], p["k_k"], p["k_a"], p["bd"])
    return outs[:6], outs[6:]


SOLVE_BLOCK = 8


def _unit_lower_solve(a_list, x_list, diag_blk, eye):
    n = eye.shape[0]
    a_d = [jnp.where(diag_blk, a, 0.0) for a in a_list]
    a_o = [a - d for a, d in zip(a_list, a_d)]
    dinv = [eye + d for d in a_d]
    apow = a_d
    for _ in range(SOLVE_BLOCK.bit_length() - 2):
        apow = [_dot(p, p) for p in apow]
        dinv = [d + _dot(p, d) for p, d in zip(apow, dinv)]
    b = [_dot(d, o) for d, o in zip(dinv, a_o)]
    x = [_dot(d, x) for d, x in zip(dinv, x_list)]
    levels = (n // SOLVE_BLOCK).bit_length() - 1
    for level in range(levels):
        if level:
            b = [_dot(m, m) for m in b]
        x = [v + _dot(m, v) for m, v in zip(b, x)]
    return x


def _scan_kernel(*refs):
    seq_refs = (refs[0:6], refs[6:12])
    s0_refs = refs[12:14]
    rk_ref, gw_ref, gb_ref, bd_ref = refs[14:18]
    y_refs = refs[18:20]
    st_refs = refs[20:22]
    c = pl.program_id(1)

    @pl.when(c == 0)
    def _():
        for st_ref, s0_ref in zip(st_refs, s0_refs):
            st_ref[...] = s0_ref[...]

    n = CHUNK
    t_i = lax.broadcasted_iota(jnp.int32, (n, n), 0)
    s_i = lax.broadcasted_iota(jnp.int32, (n, n), 1)
    blk_shift = SOLVE_BLOCK.bit_length() - 1
    diag_blk = (s_i >> blk_shift) == (t_i >> blk_shift)
    eye = jnp.where(s_i == t_i, 1.0, 0.0)
    lane2 = lax.broadcasted_iota(jnp.int32, (n, LANES), 1)
    t_2 = lax.broadcasted_iota(jnp.int32, (n, LANES), 0)
    s_2 = lane2 & (n - 1)
    first = lane2 < HEAD_DIM
    r_bd = lax.broadcasted_iota(jnp.int32, (LANES, LANES), 0) < HEAD_DIM
    c_bd = lax.broadcasted_iota(jnp.int32, (LANES, LANES), 1) < HEAD_DIM
    blockdiag = r_bd == c_bd
    incl2 = (s_2 <= t_2, s_2 >= t_2)
    strict2 = (s_2 < t_2, s_2 > t_2)

    npairs = seq_refs[0][0].shape[2] // LANES
    sls = [slice(g * LANES, (g + 1) * LANES) for g in range(npairs)]
    items = [(d, g) for d in range(2) for g in range(npairs)]
    heads = [(i, j) for i in range(len(items)) for j in range(2)]
    raw, scaled, w_tot = [], [], []
    for d in range(2):
        r, lw, k, v, a, b = (ref[0] for ref in seq_refs[d])
        incl = (s_i >= t_i) if d else (s_i <= t_i)
        cum = _dot_split_rhs(jnp.where(incl, 1.0, 0.0).astype(BF16), lw)
        tot = cum[0:1, :] if d else cum[n - 1:n, :]
        w_inv = jnp.exp(-cum)
        w_rem = jnp.exp(tot - cum)
        raw.append((r, k, v))
        scaled.append((r * jnp.exp(cum), a * jnp.exp(cum - lw), b * w_inv, k * w_inv, b * w_rem, k * w_rem))
        w_tot.append(jnp.exp(tot))
    states = [st_refs[d][0, g] for d, g in items]

    v_g = [raw[d][2][:, sls[g]] for d, g in items]
    at_h = [(jnp.where(first, scaled[d][1][:, sls[g]], 0.0), jnp.where(first, 0.0, scaled[d][1][:, sls[g]]))
            for d, g in items]
    rt_h = [(jnp.where(first, scaled[d][0][:, sls[g]], 0.0), jnp.where(first, 0.0, scaled[d][0][:, sls[g]]))
            for d, g in items]
    s = [_dot_nt(jnp.concatenate([at_h[i][0], rt_h[i][0], at_h[i][1], rt_h[i][1]], axis=0),
                 jnp.concatenate([scaled[d][2][:, sls[g]], scaled[d][3][:, sls[g]]], axis=0))
         for i, (d, g) in enumerate(items)]
    sa = [(jnp.where(strict2[d], s[i][0:n], 0.0), jnp.where(strict2[d], s[i][2 * n:3 * n], 0.0))
          for i, (d, g) in enumerate(items)]
    sr = [(jnp.where(incl2[d], s[i][n:2 * n], 0.0), jnp.where(incl2[d], s[i][3 * n:4 * n], 0.0))
          for i, (d, g) in enumerate(items)]
    vv = [jnp.concatenate([x, x], axis=0) for x in v_g]
    akv = [_dot(jnp.where(first, 0.0, sa[i][j]), vv[i]) for i, j in heads]
    pq = _unit_lower_solve([sa[i][j][:, 0:n] for i, j in heads],
                           [jnp.concatenate([at_h[i][j], akv[2 * i + j]], axis=1) for i, j in heads],
                           diag_blk, eye)
    idx = range(len(items))
    pt = [_dot_split_rhs(jnp.concatenate([pq[2 * i][:, :LANES], pq[2 * i + 1][:, :LANES]], axis=0).astype(BF16),
                         states[i]) for i in idx]
    rtt = [_dot_split_rhs(jnp.concatenate(rt_h[i], axis=0).astype(BF16), states[i]) for i in idx]
    u = [jnp.where(first, pt[i][0:n] + pq[2 * i][:, LANES:], pt[i][n:] + pq[2 * i + 1][:, LANES:]) for i in idx]
    uv = [jnp.concatenate([u[i], v_g[i]], axis=0) for i in idx]
    yh = [_dot(sr[i][j], uv[i]) for i, j in heads]
    ys = [jnp.where(first, rtt[i][0:n] + yh[2 * i], rtt[i][n:] + yh[2 * i + 1]) for i in idx]
    t_new = [_dot(jnp.concatenate([scaled[d][4][:, sls[g]], scaled[d][5][:, sls[g]]], axis=0).T, uv[i])
             for i, (d, g) in enumerate(items)]
    for i, (d, g) in enumerate(items):
        w_col = jnp.broadcast_to(w_tot[d][:, sls[g]], (LANES, LANES)).T
        st_refs[d][0, g] = jnp.where(blockdiag, w_col * states[i] + t_new[i], 0.0)
    bd = bd_ref[...]
    inv = 1.0 / HEAD_DIM
    y = [jnp.concatenate(ys[d * npairs:(d + 1) * npairs], axis=1) for d in range(2)]
    mean = [_head_sums(y[d], bd) * inv for d in range(2)]
    dev = [y[d] - mean[d] for d in range(2)]
    var = [_head_sums(dev[d] * dev[d], bd) * inv for d in range(2)]
    bonus = [_head_sums(raw[d][0] * raw[d][1] * rk_ref[...], bd) * raw[d][2] for d in range(2)]
    for d in range(2):
        y_refs[d][0] = dev[d] * lax.rsqrt(var[d] + GN_EPS) * gw_ref[...] + gb_ref[...] + bonus[d]


def _scan(seq_fwd, seq_bwd, state_fwd, state_bwd, p):
    bsz, length, width = seq_fwd[0].shape
    nc = length // CHUNK
    npairs = width // LANES
    fmap = lambda b, c: (b, c, 0)
    bmap = lambda b, c: (b, nc - 1 - c, 0)
    full = lambda shape: pl.BlockSpec(shape, lambda b, c: (0,) * len(shape))
    st_spec = pl.BlockSpec((1, npairs, LANES, LANES), lambda b, c: (b, 0, 0, 0))
    y_shape = jax.ShapeDtypeStruct((bsz, length, width), F32)
    st_shape = jax.ShapeDtypeStruct((bsz, npairs, LANES, LANES), F32)
    y_f, y_b, st_f, st_b = pl.pallas_call(
        _scan_kernel,
        grid=(bsz, nc),
        in_specs=[pl.BlockSpec((1, CHUNK, width), fmap)] * 6 + [pl.BlockSpec((1, CHUNK, width), bmap)] * 6 +
                 [st_spec] * 2 + [full((1, width))] * 3 + [full((LANES, LANES))],
        out_specs=[pl.BlockSpec((1, CHUNK, width), fmap), pl.BlockSpec((1, CHUNK, width), bmap), st_spec, st_spec],
        out_shape=[y_shape, y_shape, st_shape, st_shape],
        compiler_params=_params("parallel", "arbitrary"),
        name="wkv_scan",
    )(*seq_fwd, *seq_bwd, state_fwd, state_bwd, p["r_k"], p["gn_w"], p["gn_b"], p["bd"])
    return (y_f, y_b), (st_f, st_b)


def _mix_kernel(period, tm, yf_ref, yb_ref, glo_ref, zc_ref, g2_ref, cw_ref, o_ref):
    gate = _dot(jax.nn.sigmoid(glo_ref[0]), g2_ref[...])
    yr = (yf_ref[0] + yb_ref[0]) * gate
    zc = zc_ref[0]
    u, gate_b, gate_c = zc[:, 0:256], zc[:, 256:512], zc[:, 512:768]
    hc = gate_c * u
    pos = lax.broadcasted_iota(jnp.int32, (tm, 1), 0) % period
    prev = jnp.where(pos == 0, 0.0, pltpu.roll(hc, 1, 0))
    nxt = jnp.where(pos == period - 1, 0.0, pltpu.roll(hc, tm - 1, 0))
    conv = prev * cw_ref[0:1, :] + hc * cw_ref[1:2, :] + nxt * cw_ref[2:3, :]
    o_ref[0, :, 0:512] = yr.astype(BF16)
    o_ref[0, :, 512:768] = (gate_b * conv).astype(BF16)


def _mix(ys_f, ys_b, zr, zc, g2, conv_w, period, tm):
    bsz, length, _ = zc.shape
    row = lambda b, i: (b, i, 0)
    glo_block = (N_RWKV_COLS - 128) // 128
    return pl.pallas_call(
        functools.partial(_mix_kernel, period, tm),
        grid=(bsz, length // tm),
        in_specs=[pl.BlockSpec((1, tm, 512), row), pl.BlockSpec((1, tm, 512), row),
                  pl.BlockSpec((1, tm, 128), lambda b, i: (b, i, glo_block)),
                  pl.BlockSpec((1, tm, N_CONV_COLS), row),
                  pl.BlockSpec((128, 512), lambda b, i: (0, 0)),
                  pl.BlockSpec((3, 256), lambda b, i: (0, 0))],
        out_specs=pl.BlockSpec((1, tm, 768), row),
        out_shape=jax.ShapeDtypeStruct((bsz, length, 768), BF16),
        compiler_params=_params("parallel", "parallel"),
        name="gate_conv",
    )(ys_f, ys_b, zr, zc, g2, conv_w)


def _chan_dft_kernel(f_ref, w_ref, o_ref):
    xcs = jnp.dot(f_ref[0].astype(BF16), w_ref[...], preferred_element_type=F32)
    o_ref[0, 0] = xcs[:, 0:256].astype(BF16)
    o_ref[0, 1] = xcs[:, 256:512].astype(BF16)


def _pos_dft_kernel(nb, cs_ref, x_ref, o_ref, acc_ref):
    kstep = pl.program_id(1)

    @pl.when(kstep == 0)
    def _():
        acc_ref[...] = jnp.zeros_like(acc_ref)

    cs = cs_ref[...]
    for b in range(nb):
        acc_ref[b] += jnp.dot(cs, x_ref[b], preferred_element_type=F32)

    @pl.when(kstep == pl.num_programs(1) - 1)
    def _():
        o_ref[...] = acc_ref[...].astype(BF16)


def _fourier(zf, chan_w, pos_cs, tm, tk):
    bsz, length, ch = zf.shape
    tr = min(512, length)
    xcs = pl.pallas_call(
        _chan_dft_kernel,
        grid=(bsz, length // tr),
        in_specs=[pl.BlockSpec((1, tr, ch), lambda b, i: (b, i, 0)),
                  pl.BlockSpec((ch, 2 * ch), lambda b, i: (0, 0))],
        out_specs=pl.BlockSpec((1, 2, tr, ch), lambda b, i: (b, 0, i, 0)),
        out_shape=jax.ShapeDtypeStruct((bsz, 2, length, ch), BF16),
        compiler_params=_params("parallel", "parallel"),
        name="chan_dft",
    )(zf, chan_w)
    xcs = xcs.reshape(bsz, 2 * length, ch)
    return pl.pallas_call(
        functools.partial(_pos_dft_kernel, bsz),
        grid=(length // tm, 2 * length // tk),
        in_specs=[pl.BlockSpec((tm, tk), lambda i, kk: (i, kk)),
                  pl.BlockSpec((bsz, tk, ch), lambda i, kk: (0, kk, 0))],
        out_specs=pl.BlockSpec((bsz, tm, ch), lambda i, kk: (0, i, 0)),
        out_shape=jax.ShapeDtypeStruct((bsz, length, ch), BF16),
        scratch_shapes=[pltpu.VMEM((bsz, tm, ch), F32)],
        compiler_params=_params("parallel", "arbitrary"),
        name="pos_dft",
    )(pos_cs, xcs)


def _dft_tables(length):
    m = jnp.arange(length, dtype=jnp.int32)[None, :]
    l1 = jnp.arange(length // 64, dtype=jnp.int32)[:, None] * 64
    l2 = jnp.arange(64, dtype=jnp.int32)[:, None]
    ang = lambda l: ((l * m) % length).astype(F32) * (2.0 * jnp.pi / length)
    c1, s1, c2, s2 = jnp.cos(ang(l1)), jnp.sin(ang(l1)), jnp.cos(ang(l2)), jnp.sin(ang(l2))
    scale = 1.0 / jnp.sqrt(jnp.float32(length))
    cos = (c1[:, None, :] * c2[None] - s1[:, None, :] * s2[None]).reshape(length, length) * scale
    sin = (s1[:, None, :] * c2[None] + c1[:, None, :] * s2[None]).reshape(length, length) * scale
    return jnp.concatenate([cos, sin], axis=1).astype(BF16)


def _chan_dft_weights(ngroups):
    j = jnp.arange(HEAD_DIM, dtype=jnp.int32)
    ang = ((j[:, None] * j[None, :]) % HEAD_DIM).astype(F32) * (2.0 * jnp.pi / HEAD_DIM)
    eye = jnp.eye(ngroups, dtype=F32)
    scale = 1.0 / jnp.sqrt(jnp.float32(HEAD_DIM))
    cc = jnp.kron(eye, jnp.cos(ang) * scale)
    sc = jnp.kron(eye, jnp.sin(ang) * scale)
    return jnp.concatenate([cc, -sc], axis=1).astype(BF16)


def _outproj_kernel(yrc_ref, yf_ref, w_ref, x_ref, gn1_ref, gate_ref, gn2_ref, sc_ref, sh_ref, xo_ref, h_ref):
    nrc = yrc_ref.shape[2]
    o = (jnp.dot(yrc_ref[0], w_ref[0:nrc, :], preferred_element_type=F32) +
         jnp.dot(yf_ref[0], w_ref[nrc:, :], preferred_element_type=F32))
    xn = x_ref[0] + gate_ref[0] * _rms(o, gn1_ref[...])
    xo_ref[0] = xn
    h_ref[0] = (_rms(xn, gn2_ref[...]) * (1.0 + sc_ref[0]) + sh_ref[0]).astype(BF16)


def _outproj(yrc, yf, w, x, gn1, gate, gn2, sc, sh, tm):
    bsz, length, d = x.shape
    row = lambda b, i: (b, i, 0)
    vec = pl.BlockSpec((1, d), lambda b, i: (0, 0))
    mod = pl.BlockSpec((1, 1, d), lambda b, i: (b, 0, 0))
    return pl.pallas_call(
        _outproj_kernel,
        grid=(bsz, length // tm),
        in_specs=[pl.BlockSpec((1, tm, yrc.shape[2]), row), pl.BlockSpec((1, tm, yf.shape[2]), row),
                  pl.BlockSpec(w.shape, lambda b, i: (0, 0)), pl.BlockSpec((1, tm, d), row),
                  vec, mod, vec, mod, mod],
        out_specs=[pl.BlockSpec((1, tm, d), row), pl.BlockSpec((1, tm, d), row)],
        out_shape=[jax.ShapeDtypeStruct((bsz, length, d), F32), jax.ShapeDtypeStruct((bsz, length, d), BF16)],
        compiler_params=_params("parallel", "parallel"),
        name="outproj",
    )(yrc, yf, w, x, gn1, gate, gn2, sc, sh)


FF_SUB = 512


def _swiglu_partial(h, wg, wu, wd, tf):
    def up(c, n):
        return (jnp.dot(h, wg(c, n), preferred_element_type=F32), jnp.dot(h, wu(c, n), preferred_element_type=F32))

    def down(gu, c, n):
        return jnp.dot((_silu(gu[0]) * gu[1]).astype(BF16), wd(c, n), preferred_element_type=F32)

    out = None
    prev = None
    for cut in [(c, min(FF_SUB, tf - c)) for c in range(0, tf, FF_SUB)]:
        gu = up(*cut)
        if prev is not None:
            part = down(*prev)
            out = part if out is None else out + part
        prev = (gu,) + cut
    part = down(*prev)
    return part if out is None else out + part


def _ffn_kernel(h_ref, wg_ref, wu_ref, wd_ref, x_ref, gn_ref, gate_ref, o_ref, acc_ref):
    j = pl.program_id(2)

    @pl.when(j == 0)
    def _():
        acc_ref[...] = jnp.zeros_like(acc_ref)

    acc_ref[...] += _swiglu_partial(h_ref[0], lambda c, n: wg_ref[:, c:c + n], lambda c, n: wu_ref[:, c:c + n],
                                    lambda c, n: wd_ref[c:c + n, :], wg_ref.shape[1])

    @pl.when(j == pl.num_programs(2) - 1)
    def _():
        o_ref[0] = x_ref[0] + gate_ref[0] * _rms(acc_ref[...], gn_ref[...])


def _ffn(h, wg, wu, wd, x, gn, gate, tm, tf):
    bsz, length, d = x.shape
    ff = wg.shape[1]
    row = lambda b, i, j: (b, i, 0)
    return pl.pallas_call(
        _ffn_kernel,
        grid=(bsz, length // tm, ff // tf),
        in_specs=[pl.BlockSpec((1, tm, d), row),
                  pl.BlockSpec((d, tf), lambda b, i, j: (0, j)),
                  pl.BlockSpec((d, tf), lambda b, i, j: (0, j)),
                  pl.BlockSpec((tf, d), lambda b, i, j: (j, 0)),
                  pl.BlockSpec((1, tm, d), row),
                  pl.BlockSpec((1, d), lambda b, i, j: (0, 0)),
                  pl.BlockSpec((1, 1, d), lambda b, i, j: (b, 0, 0))],
        out_specs=pl.BlockSpec((1, tm, d), row),
        out_shape=jax.ShapeDtypeStruct((bsz, length, d), F32),
        scratch_shapes=[pltpu.VMEM((tm, d), F32)],
        compiler_params=_params("parallel", "parallel", "arbitrary"),
        name="ffn",
    )(h, wg, wu, wd, x, gn, gate)


MOE_TILE = 512
MOE_BLOCK = 128
MOE_ALIGN = 16
MOE_TM = 1024


def _router_kernel(x_ref, gn_ref, sc_ref, sh_ref, w_ref, b_ref, tri_ref, o_ref, rank_ref, rankt_ref, cnt_ref):
    h = _rms(x_ref[0], gn_ref[...]) * (1.0 + sc_ref[0]) + sh_ref[0]
    logits = jnp.dot(h, w_ref[...], preferred_element_type=F32, precision=lax.Precision.HIGHEST) + b_ref[...]
    ne = float(logits.shape[1])
    idx = lax.broadcasted_iota(jnp.int32, logits.shape, 1).astype(F32)
    m1 = jnp.max(logits, axis=1, keepdims=True)
    i1 = jnp.min(jnp.where(logits == m1, idx, ne), axis=1, keepdims=True)
    rest = jnp.where(idx == i1, -jnp.inf, logits)
    m2 = jnp.max(rest, axis=1, keepdims=True)
    i2 = jnp.min(jnp.where(rest == m2, idx, ne), axis=1, keepdims=True)
    e2 = jnp.exp(m2 - m1)
    p1 = 1.0 / (1.0 + e2)
    p2 = e2 / (1.0 + e2)
    o_ref[0] = jnp.where(idx == i1, p1, 0.0) + jnp.where(idx == i2, p2, 0.0)
    sel = jnp.where(idx == i1, 1.0, jnp.where(idx == i2, 1.0, 0.0))
    before = jnp.dot(tri_ref[...], sel.astype(BF16), preferred_element_type=F32)
    rank = jnp.where(sel > 0.0, before, -1.0)
    rank_ref[0] = rank
    rankt_ref[0] = jnp.transpose(rank)[0:N_EXPERTS, :]
    cnt_ref[0, 0] = jnp.sum(sel, axis=0, keepdims=True)


def _router(x, gn, sc, sh, w, b):
    bsz, length, d = x.shape
    tm = MOE_TILE
    pad = LANES - w.shape[1]
    w = jnp.pad(w, ((0, 0), (0, pad)))
    b = jnp.pad(b, ((0, 0), (0, pad)), constant_values=-1e30)
    ne = LANES
    t = jnp.arange(tm, dtype=jnp.int32)
    tri = (t[None, :] < t[:, None]).astype(BF16)
    row = lambda bb, i: (bb, i, 0)
    vec = pl.BlockSpec((1, d), lambda bb, i: (0, 0))
    mod = pl.BlockSpec((1, 1, d), lambda bb, i: (bb, 0, 0))
    nt = length // tm
    return pl.pallas_call(
        _router_kernel,
        grid=(bsz, nt),
        in_specs=[pl.BlockSpec((1, tm, d), row), vec, mod, mod,
                  pl.BlockSpec((d, ne), lambda bb, i: (0, 0)), pl.BlockSpec((1, ne), lambda bb, i: (0, 0)),
                  pl.BlockSpec((tm, tm), lambda bb, i: (0, 0))],
        out_specs=[pl.BlockSpec((1, tm, ne), row), pl.BlockSpec((1, tm, ne), row),
                   pl.BlockSpec((1, N_EXPERTS, tm), lambda bb, i: (bb, 0, i)),
                   pl.BlockSpec((1, 1, 1, ne), lambda bb, i: (bb, i, 0, 0))],
        out_shape=[jax.ShapeDtypeStruct((bsz, length, ne), F32), jax.ShapeDtypeStruct((bsz, length, ne), F32),
                   jax.ShapeDtypeStruct((bsz, N_EXPERTS, length), F32),
                   jax.ShapeDtypeStruct((bsz, nt, 1, ne), F32)],
        compiler_params=_params("parallel", "parallel"),
        name="router",
    )(x, gn, sc, sh, w, b, tri)


_MOE_SHIFT = MOE_BLOCK.bit_length() - 1
_MOE_BUF_BLOCKS = (2 * MOE_TILE + N_EXPERTS * (MOE_ALIGN - 1 + MOE_BLOCK - 1)) // MOE_BLOCK + 1


def _align_rows(c):
    return ((c + (MOE_ALIGN - 1)) >> (MOE_ALIGN.bit_length() - 1)) << (MOE_ALIGN.bit_length() - 1)


def _blocks(c):
    return (c + (MOE_BLOCK - 1)) >> _MOE_SHIFT


def _dispatch_copies(start, tile, off_ref, cnt_ref, buf_ref, hs_ref, sem):
    def run(src_row, dst_row, nrows):
        cp = pltpu.make_async_copy(buf_ref.at[pl.ds(pl.multiple_of(src_row, MOE_ALIGN), nrows)],
                                   hs_ref.at[pl.ds(pl.multiple_of(dst_row, MOE_ALIGN), nrows)], sem)
        if start:
            cp.start()
        else:
            cp.wait()

    slot = jnp.int32(0)
    for e in range(N_EXPERTS):
        rows = _align_rows(cnt_ref[tile * N_EXPERTS + e])
        dst0 = off_ref[tile * N_EXPERTS + e]
        src0 = slot * MOE_BLOCK
        nfull = rows >> _MOE_SHIFT

        def full(rb, carry, src0=src0, dst0=dst0):
            run(src0 + rb * MOE_BLOCK, dst0 + rb * MOE_BLOCK, MOE_BLOCK)
            return carry

        lax.fori_loop(0, nfull, full, 0)
        done = nfull * MOE_BLOCK
        piece = MOE_BLOCK // 2
        while piece >= MOE_ALIGN:
            has = (rows & piece) != 0

            @pl.when(has)
            def _(done=done, piece=piece, src0=src0, dst0=dst0):
                run(src0 + done, dst0 + done, piece)

            done = done + jnp.where(has, piece, 0)
            piece //= 2
        slot = slot + _blocks(rows)


def _dispatch_kernel(off_ref, cnt_ref, h_ref, rankt_ref, hs_init_ref, hs_ref, buf_ref, sem):
    del hs_init_ref
    tile = pl.program_id(0)
    h = h_ref[...]
    rows = lax.broadcasted_iota(jnp.int32, (MOE_BLOCK, 1), 0)
    slot = jnp.int32(0)
    for e in range(N_EXPERTS):
        nb = _blocks(_align_rows(cnt_ref[tile * N_EXPERTS + e]))
        rank_row = rankt_ref[0, e:e + 1, :]

        def body(rb, carry, slot=slot, rank_row=rank_row):
            want = (rows + rb * MOE_BLOCK).astype(F32)
            onehot = jnp.where(rank_row == want, 1.0, 0.0).astype(BF16)
            dst = pl.multiple_of((slot + rb) * MOE_BLOCK, MOE_BLOCK)
            buf_ref[pl.ds(dst, MOE_BLOCK), :] = jnp.dot(onehot, h, preferred_element_type=F32).astype(BF16)
            return carry

        lax.fori_loop(0, nb, body, 0)
        slot = slot + nb
    _dispatch_copies(True, tile, off_ref, cnt_ref, buf_ref, hs_ref, sem)
    _dispatch_copies(False, tile, off_ref, cnt_ref, buf_ref, hs_ref, sem)


def _dispatch(h, rankt, off, cnt, nrows):
    n, d = h.shape
    ntb = rankt.shape[2] // MOE_TILE
    grid_spec = pltpu.PrefetchScalarGridSpec(
        num_scalar_prefetch=2, grid=(n // MOE_TILE,),
        in_specs=[pl.BlockSpec((MOE_TILE, d), lambda t, o, c: (t, 0)),
                  pl.BlockSpec((1, N_EXPERTS, MOE_TILE), lambda t, o, c: (t // ntb, 0, t % ntb)),
                  pl.BlockSpec(memory_space=pl.ANY)],
        out_specs=pl.BlockSpec(memory_space=pl.ANY),
        scratch_shapes=[pltpu.VMEM((_MOE_BUF_BLOCKS * MOE_BLOCK, d), BF16), pltpu.SemaphoreType.DMA(())])
    return pl.pallas_call(
        _dispatch_kernel, grid_spec=grid_spec,
        out_shape=jax.ShapeDtypeStruct((nrows, d), BF16),
        input_output_aliases={4: 0},
        compiler_params=_params("arbitrary"),
        name="moe_dispatch",
    )(off, cnt, h, rankt, jnp.zeros((nrows, d), BF16))


def _gffn_kernel(te_ref, tv_ref, h_ref, wg_ref, wu_ref, wd_ref, o_ref, acc_ref):
    q = pl.program_id(0)
    j = pl.program_id(1)

    @pl.when(j == 0)
    def _():
        acc_ref[...] = jnp.zeros_like(acc_ref)

    @pl.when(tv_ref[q] > 0)
    def _():
        acc_ref[...] += _swiglu_partial(h_ref[...], lambda c, n: wg_ref[0, :, c:c + n],
                                        lambda c, n: wu_ref[0, :, c:c + n], lambda c, n: wd_ref[0, c:c + n, :],
                                        wg_ref.shape[2])

    @pl.when(j == pl.num_programs(1) - 1)
    def _():
        o_ref[...] = acc_ref[...].astype(BF16)


def _gffn(hs, te, tv, wg, wu, wd, tf):
    nrows, d = hs.shape
    ff = wg.shape[2]
    grid_spec = pltpu.PrefetchScalarGridSpec(
        num_scalar_prefetch=2, grid=(nrows // MOE_TM, ff // tf),
        in_specs=[pl.BlockSpec((MOE_TM, d), lambda q, j, te, tv: (q, 0)),
                  pl.BlockSpec((1, d, tf), lambda q, j, te, tv: (te[q], 0, j)),
                  pl.BlockSpec((1, d, tf), lambda q, j, te, tv: (te[q], 0, j)),
                  pl.BlockSpec((1, tf, d), lambda q, j, te, tv: (te[q], j, 0))],
        out_specs=pl.BlockSpec((MOE_TM, d), lambda q, j, te, tv: (q, 0)),
        scratch_shapes=[pltpu.VMEM((MOE_TM, d), F32)])
    return pl.pallas_call(
        _gffn_kernel, grid_spec=grid_spec,
        out_shape=jax.ShapeDtypeStruct((nrows, d), BF16),
        compiler_params=_params("arbitrary", "arbitrary"),
        name="moe_experts",
    )(te, tv, hs, wg, wu, wd)


def _combine_copies(start, tile, off_ref, cnt_ref, ys_ref, buf_ref, sem):
    slot = jnp.int32(0)
    for e in range(N_EXPERTS):
        nb = _blocks(cnt_ref[tile * N_EXPERTS + e])
        src0 = off_ref[tile * N_EXPERTS + e]

        def body(rb, carry, slot=slot, src0=src0):
            cp = pltpu.make_async_copy(
                ys_ref.at[pl.ds(pl.multiple_of(src0 + rb * MOE_BLOCK, MOE_ALIGN), MOE_BLOCK)],
                buf_ref.at[pl.ds(pl.multiple_of((slot + rb) * MOE_BLOCK, MOE_BLOCK), MOE_BLOCK)], sem)
            if start:
                cp.start()
            else:
                cp.wait()
            return carry

        lax.fori_loop(0, nb, body, 0)
        slot = slot + nb


def _combine_kernel(off_ref, cnt_ref, ys_ref, rank_ref, gates_ref, x_ref, gn_ref, gate_ref, o_ref, buf_ref, acc_ref,
                    sem):
    tile = pl.program_id(0)
    _combine_copies(True, tile, off_ref, cnt_ref, ys_ref, buf_ref, sem)
    acc_ref[...] = jnp.zeros_like(acc_ref)
    _combine_copies(False, tile, off_ref, cnt_ref, ys_ref, buf_ref, sem)
    lanes = lax.broadcasted_iota(jnp.int32, (1, MOE_BLOCK), 1)
    rank = rank_ref[0]
    gates = gates_ref[0]
    slot = jnp.int32(0)
    for e in range(N_EXPERTS):
        nb = _blocks(cnt_ref[tile * N_EXPERTS + e])
        rank_col = rank[:, e:e + 1]
        gate_col = gates[:, e:e + 1]

        def body(rb, carry, slot=slot, rank_col=rank_col, gate_col=gate_col):
            want = (lanes + rb * MOE_BLOCK).astype(F32)
            scatter = jnp.where(rank_col == want, gate_col, 0.0).astype(BF16)
            src = pl.multiple_of((slot + rb) * MOE_BLOCK, MOE_BLOCK)
            acc_ref[...] += jnp.dot(scatter, buf_ref[pl.ds(src, MOE_BLOCK), :], preferred_element_type=F32)
            return carry

        lax.fori_loop(0, nb, body, 0)
        slot = slot + nb
    o_ref[0] = x_ref[0] + gate_ref[0] * _rms(acc_ref[...], gn_ref[...])


def _combine(ys, rank, gates, x, gn, gate, off, cnt):
    bsz, length, d = x.shape
    ntb = length // MOE_TILE
    tok = lambda t, o, c: (t // ntb, t % ntb, 0)
    grid_spec = pltpu.PrefetchScalarGridSpec(
        num_scalar_prefetch=2, grid=(bsz * ntb,),
        in_specs=[pl.BlockSpec(memory_space=pl.ANY),
                  pl.BlockSpec((1, MOE_TILE, LANES), tok), pl.BlockSpec((1, MOE_TILE, LANES), tok),
                  pl.BlockSpec((1, MOE_TILE, d), tok),
                  pl.BlockSpec((1, d), lambda t, o, c: (0, 0)),
                  pl.BlockSpec((1, 1, d), lambda t, o, c: (t // ntb, 0, 0))],
        out_specs=pl.BlockSpec((1, MOE_TILE, d), tok),
        scratch_shapes=[pltpu.VMEM((_MOE_BUF_BLOCKS * MOE_BLOCK, d), BF16), pltpu.VMEM((MOE_TILE, d), F32),
                        pltpu.SemaphoreType.DMA(())])
    return pl.pallas_call(
        _combine_kernel, grid_spec=grid_spec,
        out_shape=jax.ShapeDtypeStruct((bsz, length, d), F32),
        compiler_params=_params("arbitrary"),
        name="moe_combine",
    )(off, cnt, ys, rank, gates, x, gn, gate)


def _route_plan(counts, ntokens):
    ntiles = counts.shape[0]
    rows = (counts + (MOE_ALIGN - 1)) // MOE_ALIGN * MOE_ALIGN
    seg = (rows.sum(0) + MOE_BLOCK + MOE_TM - 1) // MOE_TM * MOE_TM
    seg_end = jnp.cumsum(seg)
    off = (seg_end - seg)[None, :] + jnp.cumsum(rows, axis=0) - rows
    nq = -(-(2 * ntokens + ntiles * N_EXPERTS * (MOE_ALIGN - 1) + N_EXPERTS * (MOE_BLOCK + MOE_TM - 1)) // MOE_TM)
    q = jnp.arange(nq, dtype=jnp.int32) * MOE_TM
    te = jnp.minimum(jnp.sum(q[:, None] >= seg_end[None, :], axis=1), N_EXPERTS - 1).astype(jnp.int32)
    tv = (q < seg_end[-1]).astype(jnp.int32)
    return off.reshape(-1).astype(jnp.int32), te, tv, nq * MOE_TM


def _moe(h, x, gn_pre, sc, sh, router_w, router_b, wg, wu, wd, gn_post, gate, tf):
    bsz, length, d = x.shape
    gates, rank, rankt, cnt = _router(x, gn_pre, sc, sh, router_w, router_b)
    counts = cnt[:, :, 0, :N_EXPERTS].astype(jnp.int32).reshape(-1, N_EXPERTS)
    off, te, tv, nrows = _route_plan(counts, bsz * length)
    cnt_flat = counts.reshape(-1)
    hs = _dispatch(h.reshape(bsz * length, d), rankt, off, cnt_flat, nrows)
    ys = _gffn(hs, te, tv, wg, wu, wd, tf)
    return _combine(ys, rank, gates, x, gn_post, gate, off, cnt_flat)


def _sincos_2d(rows, cols, dim):
    quarter = dim // 4
    omega = 1.0 / (POS_BASE ** (jnp.arange(quarter, dtype=F32) / quarter))

    def axis_emb(n):
        ang = jnp.arange(n, dtype=F32)[:, None] * omega[None, :]
        return jnp.concatenate([jnp.sin(ang), jnp.cos(ang)], axis=-1)

    er, ec = axis_emb(rows), axis_emb(cols)
    emb = jnp.concatenate([jnp.broadcast_to(er[:, None, :], (rows, cols, dim // 2)),
                           jnp.broadcast_to(ec[None, :, :], (rows, cols, dim // 2))], axis=-1)
    return emb.reshape(rows * cols, dim)


def _head_block_ones(width):
    h = jnp.arange(width, dtype=jnp.int32) // HEAD_DIM
    return (h[:, None] == h[None, :]).astype(BF16)


def _pad_lora(w2, d):
    z = jnp.zeros_like(w2[0])
    return jnp.concatenate([w2[0], z], axis=0) if d == 0 else jnp.concatenate([z, w2[1]], axis=0)


def _tile(length, pref):
    return pref if length % pref == 0 else length


def kernel(x, c, ctx, c_ctx, ada_w, ada_b, norm_g, w_in, w_out, rwkv_mu, rwkv_mu_w, rwkv_mu_a, rwkv_w0, rwkv_w2,
           rwkv_a0, rwkv_a2, rwkv_g2, rwkv_k_k, rwkv_k_a, rwkv_r_k, rwkv_gn_w, rwkv_gn_b, conv_w, ffn_w_gate,
           ffn_w_up, ffn_w_down, router_w, router_b, moe_w_gate, moe_w_up, moe_w_down):
    bsz, length, dim = x.shape
    ctx_len = ctx.shape[1]
    depth = ada_w.shape[0]
    d_rwkv = rwkv_w0.shape[-1]

    pos = _sincos_2d(length // GRID_W, GRID_W, dim).astype(x.dtype)
    bd = _head_block_ones(LANES)
    chan_w = _chan_dft_weights(N_FOUR_COLS // HEAD_DIM)
    dft_lat = _dft_tables(length)
    dft_ctx = _dft_tables(ctx_len)

    cvec = jnp.zeros((16, dim), F32).at[:bsz].set(c).at[bsz].set(c_ctx)
    mod = _adaln(cvec, ada_w, ada_b)
    zero_state = jnp.zeros((bsz, d_rwkv // LANES, LANES, LANES), F32)
    xc = ctx

    def mods(l, lo, hi):
        return [mod[l, lo:hi, i * dim:(i + 1) * dim][:, None, :] for i in range(6)]

    def mixers(l, zr, zc, zf, states0, period, dft, need_y=True):
        tm = _tile(zr.shape[1], 512)
        p = {"mu": rwkv_mu[l], "mu_w": rwkv_mu_w[l].reshape(1, -1), "mu_a": rwkv_mu_a[l].reshape(1, -1),
             "w0": rwkv_w0[l], "w2": jnp.stack([_pad_lora(rwkv_w2[l], d) for d in range(2)]), "a0": rwkv_a0[l],
             "a2": jnp.stack([_pad_lora(rwkv_a2[l], d) for d in range(2)]), "k_k": rwkv_k_k[l][None],
             "k_a": rwkv_k_a[l][None], "r_k": rwkv_r_k[l].reshape(1, -1), "gn_w": rwkv_gn_w[l][None],
             "gn_b": rwkv_gn_b[l][None], "bd": bd}
        seq_fwd, seq_bwd = _prep(zr, p, _tile(zr.shape[1], 256))
        ys, states = _scan(seq_fwd, seq_bwd, states0[0], states0[1], p)
        if not need_y:
            return None, None, states
        yrc = _mix(ys[0], ys[1], zr, zc, rwkv_g2[l], conv_w[l], period, tm)
        yf = _fourier(zf, chan_w, dft, _tile(zf.shape[1], 512), 512)
        return yrc, yf, states

    def channel_mixer(l, h2, xn, gn3, gate2, sc2, sh2, gn2):
        tm = _tile(xn.shape[1], 512)
        i = l // 2
        if l % 2 == 0:
            return _ffn(h2, ffn_w_gate[i].astype(BF16), ffn_w_up[i].astype(BF16), ffn_w_down[i].astype(BF16),
                        xn, gn3, gate2, tm, 1408)
        return _moe(h2, xn, gn2, sc2, sh2, router_w[i], router_b[i][None], moe_w_gate[i].astype(BF16),
                    moe_w_up[i].astype(BF16), moe_w_down[i].astype(BF16), gn3, gate2, 896)

    for l in range(depth):
        last = l == depth - 1
        w_in_l = w_in[l].astype(BF16)
        w_out_l = w_out[l].astype(BF16)
        gn = [norm_g[l, i][None] for i in range(4)]
        sh1, sc1, g1, sh2, sc2, g2 = mods(l, 0, bsz)
        csh1, csc1, cg1, csh2, csc2, cg2 = mods(l, bsz, bsz + 1)

        xc_flat = xc.reshape(1, bsz * ctx_len, dim)
        tmc = _tile(bsz * ctx_len, 512)
        zr, zc, zf = _inproj(xc_flat, None, gn[0], csc1, csh1, w_in_l, tmc)
        unflat = lambda t: t.reshape(bsz, ctx_len, t.shape[-1])
        yrc, yf, ctx_states = mixers(l, unflat(zr), unflat(zc), unflat(zf), (zero_state, zero_state), ctx_len,
                                     dft_ctx, need_y=not last)
        if not last:
            flat = lambda t: t.reshape(1, bsz * ctx_len, t.shape[-1])
            xcn, hc2 = _outproj(flat(yrc), flat(yf), w_out_l, xc_flat, gn[1], cg1, gn[2], csc2, csh2, tmc)
            xc = channel_mixer(l, hc2, xcn, gn[3], cg2, csc2, csh2, gn[2]).reshape(bsz, ctx_len, dim)

        tm = _tile(length, 256)
        if l == 0:
            zr, zc, zf, x = _inproj(x, pos, gn[0], sc1, sh1, w_in_l, tm)
        else:
            zr, zc, zf = _inproj(x, None, gn[0], sc1, sh1, w_in_l, tm)
        yrc, yf, _ = mixers(l, zr, zc, zf, ctx_states, GRID_W, dft_lat)
        xn, h2 = _outproj(yrc, yf, w_out_l, x, gn[1], g1, gn[2], sc2, sh2, _tile(length, 512))
        x = channel_mixer(l, h2, xn, gn[3], g2, sc2, sh2, gn[2])
    return x
```

```python
import functools

import jax
import jax.numpy as jnp
from jax import lax
from jax.experimental import pallas as pl
from jax.experimental.pallas import tpu as pltpu

F32 = jnp.float32
BF16 = jnp.bfloat16

HEAD_DIM = 64
GRID_W = 64
NORM_EPS = 1e-6
GN_EPS = 64e-5
POS_BASE = 10000.0
N_EXPERTS = 8
CHUNK = 64
LANES = 128
VMEM_LIMIT = 56 * 1024 * 1024


def _params(*sem):
    return pltpu.CompilerParams(dimension_semantics=sem, vmem_limit_bytes=VMEM_LIMIT)


def _dot(a, b):
    return jnp.dot(a.astype(BF16), b.astype(BF16), preferred_element_type=F32)


def _dot_nt(a, b):
    return lax.dot_general(a.astype(BF16), b.astype(BF16), (((1,), (1,)), ((), ())),
                           preferred_element_type=F32)


def _split(x):
    hi = x.astype(BF16)
    lo = (x - hi.astype(F32)).astype(BF16)
    return hi, lo


def _head_sums(x, ones_pair):
    rows = x.shape[0]
    tiles = x.shape[1] // LANES
    hi, lo = _split(x)
    stacked = jnp.concatenate([t[:, g * LANES:(g + 1) * LANES] for t in (hi, lo) for g in range(tiles)], axis=0)
    s = jnp.dot(stacked, ones_pair, preferred_element_type=F32)
    return jnp.concatenate([s[g * rows:(g + 1) * rows] + s[(tiles + g) * rows:(tiles + g + 1) * rows]
                            for g in range(tiles)], axis=1)


def _dot_split_rhs(w, x):
    hi, lo = _split(x)
    return (jnp.dot(w, hi, preferred_element_type=F32) + jnp.dot(w, lo, preferred_element_type=F32))


def _rms(x, g):
    ms = jnp.mean(x * x, axis=-1, keepdims=True)
    return x * lax.rsqrt(ms + NORM_EPS) * g


def _silu(x):
    return x * jax.nn.sigmoid(x)


def _softplus(x):
    return jnp.maximum(x, 0.0) + jnp.log(1.0 + jnp.exp(-jnp.abs(x)))


def _adaln_kernel(c_ref, w_ref, b_ref, o_ref):
    s = _silu(c_ref[...])
    o_ref[0] = _dot(s, w_ref[0]) + b_ref[0]


def _adaln(cvec, ada_w, ada_b):
    nl, d, n = ada_w.shape
    rows = cvec.shape[0]
    tn = 1536
    return pl.pallas_call(
        _adaln_kernel,
        grid=(nl, n // tn),
        in_specs=[pl.BlockSpec((rows, d), lambda l, j: (0, 0)),
                  pl.BlockSpec((1, d, tn), lambda l, j: (l, 0, j)),
                  pl.BlockSpec((1, 1, tn), lambda l, j: (l, 0, j))],
        out_specs=pl.BlockSpec((1, rows, tn), lambda l, j: (l, 0, j)),
        out_shape=jax.ShapeDtypeStruct((nl, rows, n), F32),
        compiler_params=_params("arbitrary", "arbitrary"),
        name="adaln",
    )(cvec, ada_w, ada_b.reshape(nl, 1, n))


N_RWKV_COLS = 1920
N_CONV_COLS = 768
N_FOUR_COLS = 256


def _inproj_kernel(add_pos, *refs):
    if add_pos:
        x_ref, pos_ref, g_ref, sc_ref, sh_ref, w_ref, zr_ref, zc_ref, zf_ref, xp_ref = refs
        x = x_ref[0] + pos_ref[...]
        xp_ref[0] = x
    else:
        x_ref, g_ref, sc_ref, sh_ref, w_ref, zr_ref, zc_ref, zf_ref = refs
        x = x_ref[0]
    h = (_rms(x, g_ref[...]) * (1.0 + sc_ref[0]) + sh_ref[0]).astype(BF16)
    a, b = N_RWKV_COLS, N_RWKV_COLS + N_CONV_COLS
    zr_ref[0] = jnp.dot(h, w_ref[:, :a], preferred_element_type=F32)
    zc_ref[0] = jnp.dot(h, w_ref[:, a:b], preferred_element_type=F32)
    zf_ref[0] = jnp.dot(h, w_ref[:, b:], preferred_element_type=F32)


def _inproj(x, pos, g, sc, sh, w, tm):
    bsz, length, d = x.shape
    n = w.shape[1]
    add_pos = pos is not None
    row = lambda b, i: (b, i, 0)
    in_specs = [pl.BlockSpec((1, tm, d), row)]
    args = [x]
    if add_pos:
        in_specs.append(pl.BlockSpec((tm, d), lambda b, i: (i, 0)))
        args.append(pos)
    in_specs += [pl.BlockSpec((1, d), lambda b, i: (0, 0)),
                 pl.BlockSpec((1, 1, d), lambda b, i: (b, 0, 0)),
                 pl.BlockSpec((1, 1, d), lambda b, i: (b, 0, 0)),
                 pl.BlockSpec((d, n), lambda b, i: (0, 0))]
    args += [g, sc, sh, w]
    out_shape = [jax.ShapeDtypeStruct((bsz, length, N_RWKV_COLS), F32),
                 jax.ShapeDtypeStruct((bsz, length, N_CONV_COLS), F32),
                 jax.ShapeDtypeStruct((bsz, length, N_FOUR_COLS), F32)]
    out_specs = [pl.BlockSpec((1, tm, N_RWKV_COLS), row),
                 pl.BlockSpec((1, tm, N_CONV_COLS), row),
                 pl.BlockSpec((1, tm, N_FOUR_COLS), row)]
    if add_pos:
        out_shape.append(jax.ShapeDtypeStruct((bsz, length, d), F32))
        out_specs.append(pl.BlockSpec((1, tm, d), row))
    return pl.pallas_call(
        functools.partial(_inproj_kernel, add_pos),
        grid=(bsz, length // tm),
        in_specs=in_specs, out_specs=out_specs, out_shape=out_shape,
        compiler_params=_params("parallel", "parallel"),
        name="inproj",
    )(*args)


HALO = 8


def _prep_kernel(tm, z_ref, prev_ref, next_ref, mu_ref, muw_ref, mua_ref, w0_ref, w2_ref, a0_ref, a2_ref,
                 kk_ref, ka_ref, bd_ref, *out_refs):
    i = pl.program_id(1)
    last = pl.num_programs(1) - 1
    z = z_ref[0]
    rows = lax.broadcasted_iota(jnp.int32, (tm, 1), 0)
    for d, reverse in enumerate((False, True)):
        if reverse:
            hrow = next_ref[0, 0:1, :] * jnp.where(i == last, 0.0, 1.0)
            zs = jnp.where(rows == tm - 1, hrow, pltpu.roll(z, tm - 1, 0))
        else:
            hrow = prev_ref[0, HALO - 1:HALO, :] * jnp.where(i == 0, 0.0, 1.0)
            zs = jnp.where(rows == 0, hrow, pltpu.roll(z, 1, 0))
        dz = zs - z
        k = z[:, 0:512] + dz[:, 0:512] * mu_ref[d, 0:1, :]
        v = z[:, 512:1024] + dz[:, 512:1024] * mu_ref[d, 1:2, :]
        r = z[:, 1280:1792] + dz[:, 1280:1792] * mu_ref[d, 2:3, :]
        wl = z[:, 1024:1152] + dz[:, 1024:1152] * muw_ref[...]
        al = z[:, 1152:1280] + dz[:, 1152:1280] * mua_ref[...]
        w_log = -_softplus(-(w0_ref[d:d + 1, :] + _dot(jnp.tanh(wl), w2_ref[d]))) - 0.5
        lw = -jnp.exp(w_log)
        a = jax.nn.sigmoid(a0_ref[d:d + 1, :] + _dot(al, a2_ref[d]))
        kk = k * kk_ref[...]
        ss = _head_sums(kk * kk, bd_ref[...])
        kk = kk / jnp.maximum(jnp.sqrt(ss), 1e-12)
        r_ref, lw_ref, k_ref, v_ref, a_ref, b_ref = out_refs[6 * d:6 * d + 6]
        r_ref[0] = r
        lw_ref[0] = lw
        k_ref[0] = k * (1.0 + (a - 1.0) * ka_ref[...])
        v_ref[0] = v
        a_ref[0] = -kk
        b_ref[0] = kk * a


def _prep(zr, p, tm):
    bsz, length, ncol = zr.shape
    nt = length // tm
    hb = tm // HALO
    nhb = length // HALO
    prev_map = lambda b, i: (b, jnp.maximum(i * hb - 1, 0), 0)
    next_map = lambda b, i: (b, jnp.minimum((i + 1) * hb, nhb - 1), 0)
    full = lambda shape: pl.BlockSpec(shape, lambda b, i: (0,) * len(shape))
    row = lambda b, i: (b, i, 0)
    out = jax.ShapeDtypeStruct((bsz, length, 512), F32)
    outs = pl.pallas_call(
        functools.partial(_prep_kernel, tm),
        grid=(bsz, nt),
        in_specs=[pl.BlockSpec((1, tm, ncol), row), pl.BlockSpec((1, HALO, ncol), prev_map),
                  pl.BlockSpec((1, HALO, ncol), next_map),
                  full((2, 3, 512)), full((1, 128)), full((1, 128)), full((2, 512)), full((2, 128, 512)),
                  full((2, 512)), full((2, 128, 512)), full((1, 512)), full((1, 512)), full((LANES, LANES))],
        out_specs=[pl.BlockSpec((1, tm, 512), row)] * 12,
        out_shape=[out] * 12,
        compiler_params=_params("parallel", "parallel"),
        name="rwkv_prep",
    )(zr, zr, zr, p["mu"], p["mu_w"], p["mu_a"], p["w0"], p["w2"], p["a0"], p["a2"], p["k_k"], p["k_a"], p["bd"])
    return outs[:6], outs[6:]


SOLVE_BLOCK = 8


def _unit_lower_solve(a_list, x_list, diag_blk, eye):
    n = eye.shape[0]
    a_d = [jnp.where(diag_blk, a, 0.0) for a in a_list]
    a_o = [a - d for a, d in zip(a_list, a_d)]
    dinv = [eye + d for d in a_d]
    apow = a_d
    for _ in range(SOLVE_BLOCK.bit_length() - 2):
        apow = [_dot(p, p) for p in apow]
        dinv = [d + _dot(p, d) for p, d in zip(apow, dinv)]
    b = [_dot(d, o) for d, o in zip(dinv, a_o)]
    x = [_dot(d, x) for d, x in zip(dinv, x_list)]
    levels = (n // SOLVE_BLOCK).bit_length() - 1
    for level in range(levels):
        if level:
            b = [_dot(m, m) for m in b]
        x = [v + _dot(m, v) for m, v in zip(b, x)]
    return x


def _scan_kernel(*refs):
    seq_refs = (refs[0:6], refs[6:12])
    s0_refs = refs[12:14]
    rk_ref, gw_ref, gb_ref, bd_ref = refs[14:18]
    y_refs = refs[18:20]
    st_refs = refs[20:22]
    c = pl.program_id(1)

    @pl.when(c == 0)
    def _():
        for st_ref, s0_ref in zip(st_refs, s0_refs):
            st_ref[...] = s0_ref[...]

    n = CHUNK
    t_i = lax.broadcasted_iota(jnp.int32, (n, n), 0)
    s_i = lax.broadcasted_iota(jnp.int32, (n, n), 1)
    blk_shift = SOLVE_BLOCK.bit_length() - 1
    diag_blk = (s_i >> blk_shift) == (t_i >> blk_shift)
    eye = jnp.where(s_i == t_i, 1.0, 0.0)
    lane2 = lax.broadcasted_iota(jnp.int32, (n, LANES), 1)
    t_2 = lax.broadcasted_iota(jnp.int32, (n, LANES), 0)
    s_2 = lane2 & (n - 1)
    first = lane2 < HEAD_DIM
    r_bd = lax.broadcasted_iota(jnp.int32, (LANES, LANES), 0) < HEAD_DIM
    c_bd = lax.broadcasted_iota(jnp.int32, (LANES, LANES), 1) < HEAD_DIM
    blockdiag = r_bd == c_bd
    incl2 = (s_2 <= t_2, s_2 >= t_2)
    strict2 = (s_2 < t_2, s_2 > t_2)

    npairs = seq_refs[0][0].shape[2] // LANES
    sls = [slice(g * LANES, (g + 1) * LANES) for g in range(npairs)]
    items = [(d, g) for d in range(2) for g in range(npairs)]
    heads = [(i, j) for i in range(len(items)) for j in range(2)]
    raw, scaled, w_tot = [], [], []
    for d in range(2):
        r, lw, k, v, a, b = (ref[0] for ref in seq_refs[d])
        incl = (s_i >= t_i) if d else (s_i <= t_i)
        cum = _dot_split_rhs(jnp.where(incl, 1.0, 0.0).astype(BF16), lw)
        tot = cum[0:1, :] if d else cum[n - 1:n, :]
        w_inv = jnp.exp(-cum)
        w_rem = jnp.exp(tot - cum)
        raw.append((r, k, v))
        scaled.append((r * jnp.exp(cum), a * jnp.exp(cum - lw), b * w_inv, k * w_inv, b * w_rem, k * w_rem))
        w_tot.append(jnp.exp(tot))
    states = [st_refs[d][0, g] for d, g in items]

    v_g = [raw[d][2][:, sls[g]] for d, g in items]
    at_h = [(jnp.where(first, scaled[d][1][:, sls[g]], 0.0), jnp.where(first, 0.0, scaled[d][1][:, sls[g]]))
            for d, g in items]
    rt_h = [(jnp.where(first, scaled[d][0][:, sls[g]], 0.0), jnp.where(first, 0.0, scaled[d][0][:, sls[g]]))
            for d, g in items]
    s = [_dot_nt(jnp.concatenate([at_h[i][0], rt_h[i][0], at_h[i][1], rt_h[i][1]], axis=0),
                 jnp.concatenate([scaled[d][2][:, sls[g]], scaled[d][3][:, sls[g]]], axis=0))
         for i, (d, g) in enumerate(items)]
    sa = [(jnp.where(strict2[d], s[i][0:n], 0.0), jnp.where(strict2[d], s[i][2 * n:3 * n], 0.0))
          for i, (d, g) in enumerate(items)]
    sr = [(jnp.where(incl2[d], s[i][n:2 * n], 0.0), jnp.where(incl2[d], s[i][3 * n:4 * n], 0.0))
          for i, (d, g) in enumerate(items)]
    vv = [jnp.concatenate([x, x], axis=0) for x in v_g]
    akv = [_dot(jnp.where(first, 0.0, sa[i][j]), vv[i]) for i, j in heads]
    pq = _unit_lower_solve([sa[i][j][:, 0:n] for i, j in heads],
                           [jnp.concatenate([at_h[i][j], akv[2 * i + j]], axis=1) for i, j in heads],
                           diag_blk, eye)
    idx = range(len(items))
    pt = [_dot_split_rhs(jnp.concatenate([pq[2 * i][:, :LANES], pq[2 * i + 1][:, :LANES]], axis=0).astype(BF16),
                         states[i]) for i in idx]
    rtt = [_dot_split_rhs(jnp.concatenate(rt_h[i], axis=0).astype(BF16), states[i]) for i in idx]
    u = [jnp.where(first, pt[i][0:n] + pq[2 * i][:, LANES:], pt[i][n:] + pq[2 * i + 1][:, LANES:]) for i in idx]
    uv = [jnp.concatenate([u[i], v_g[i]], axis=0) for i in idx]
    yh = [_dot(sr[i][j], uv[i]) for i, j in heads]
    ys = [jnp.where(first, rtt[i][0:n] + yh[2 * i], rtt[i][n:] + yh[2 * i + 1]) for i in idx]
    t_new = [_dot(jnp.concatenate([scaled[d][4][:, sls[g]], scaled[d][5][:, sls[g]]], axis=0).T, uv[i])
             for i, (d, g) in enumerate(items)]
    for i, (d, g) in enumerate(items):
        w_col = jnp.broadcast_to(w_tot[d][:, sls[g]], (LANES, LANES)).T
        st_refs[d][0, g] = jnp.where(blockdiag, w_col * states[i] + t_new[i], 0.0)
    bd = bd_ref[...]
    inv = 1.0 / HEAD_DIM
    y = [jnp.concatenate(ys[d * npairs:(d + 1) * npairs], axis=1) for d in range(2)]
    mean = [_head_sums(y[d], bd) * inv for d in range(2)]
    dev = [y[d] - mean[d] for d in range(2)]
    var = [_head_sums(dev[d] * dev[d], bd) * inv for d in range(2)]
    bonus = [_head_sums(raw[d][0] * raw[d][1] * rk_ref[...], bd) * raw[d][2] for d in range(2)]
    for d in range(2):
        y_refs[d][0] = dev[d] * lax.rsqrt(var[d] + GN_EPS) * gw_ref[...] + gb_ref[...] + bonus[d]


def _scan(seq_fwd, seq_bwd, state_fwd, state_bwd, p):
    bsz, length, width = seq_fwd[0].shape
    nc = length // CHUNK
    npairs = width // LANES
    fmap = lambda b, c: (b, c, 0)
    bmap = lambda b, c: (b, nc - 1 - c, 0)
    full = lambda shape: pl.BlockSpec(shape, lambda b, c: (0,) * len(shape))
    st_spec = pl.BlockSpec((1, npairs, LANES, LANES), lambda b, c: (b, 0, 0, 0))
    y_shape = jax.ShapeDtypeStruct((bsz, length, width), F32)
    st_shape = jax.ShapeDtypeStruct((bsz, npairs, LANES, LANES), F32)
    y_f, y_b, st_f, st_b = pl.pallas_call(
        _scan_kernel,
        grid=(bsz, nc),
        in_specs=[pl.BlockSpec((1, CHUNK, width), fmap)] * 6 + [pl.BlockSpec((1, CHUNK, width), bmap)] * 6 +
                 [st_spec] * 2 + [full((1, width))] * 3 + [full((LANES, LANES))],
        out_specs=[pl.BlockSpec((1, CHUNK, width), fmap), pl.BlockSpec((1, CHUNK, width), bmap), st_spec, st_spec],
        out_shape=[y_shape, y_shape, st_shape, st_shape],
        compiler_params=_params("parallel", "arbitrary"),
        name="wkv_scan",
    )(*seq_fwd, *seq_bwd, state_fwd, state_bwd, p["r_k"], p["gn_w"], p["gn_b"], p["bd"])
    return (y_f, y_b), (st_f, st_b)


def _mix_kernel(period, tm, yf_ref, yb_ref, glo_ref, zc_ref, g2_ref, cw_ref, o_ref):
    gate = _dot(jax.nn.sigmoid(glo_ref[0]), g2_ref[...])
    yr = (yf_ref[0] + yb_ref[0]) * gate
    zc = zc_ref[0]
    u, gate_b, gate_c = zc[:, 0:256], zc[:, 256:512], zc[:, 512:768]
    hc = gate_c * u
    pos = lax.broadcasted_iota(jnp.int32, (tm, 1), 0) % period
    prev = jnp.where(pos == 0, 0.0, pltpu.roll(hc, 1, 0))
    nxt = jnp.where(pos == period - 1, 0.0, pltpu.roll(hc, tm - 1, 0))
    conv = prev * cw_ref[0:1, :] + hc * cw_ref[1:2, :] + nxt * cw_ref[2:3, :]
    o_ref[0, :, 0:512] = yr.astype(BF16)
    o_ref[0, :, 512:768] = (gate_b * conv).astype(BF16)


def _mix(ys_f, ys_b, zr, zc, g2, conv_w, period, tm):
    bsz, length, _ = zc.shape
    row = lambda b, i: (b, i, 0)
    glo_block = (N_RWKV_COLS - 128) // 128
    return pl.pallas_call(
        functools.partial(_mix_kernel, period, tm),
        grid=(bsz, length // tm),
        in_specs=[pl.BlockSpec((1, tm, 512), row), pl.BlockSpec((1, tm, 512), row),
                  pl.BlockSpec((1, tm, 128), lambda b, i: (b, i, glo_block)),
                  pl.BlockSpec((1, tm, N_CONV_COLS), row),
                  pl.BlockSpec((128, 512), lambda b, i: (0, 0)),
                  pl.BlockSpec((3, 256), lambda b, i: (0, 0))],
        out_specs=pl.BlockSpec((1, tm, 768), row),
        out_shape=jax.ShapeDtypeStruct((bsz, length, 768), BF16),
        compiler_params=_params("parallel", "parallel"),
        name="gate_conv",
    )(ys_f, ys_b, zr, zc, g2, conv_w)


def _chan_dft_kernel(f_ref, w_ref, o_ref):
    xcs = jnp.dot(f_ref[0].astype(BF16), w_ref[...], preferred_element_type=F32)
    o_ref[0, 0] = xcs[:, 0:256].astype(BF16)
    o_ref[0, 1] = xcs[:, 256:512].astype(BF16)


def _pos_dft_kernel(nb, cs_ref, x_ref, o_ref, acc_ref):
    kstep = pl.program_id(1)

    @pl.when(kstep == 0)
    def _():
        acc_ref[...] = jnp.zeros_like(acc_ref)

    cs = cs_ref[...]
    for b in range(nb):
        acc_ref[b] += jnp.dot(cs, x_ref[b], preferred_element_type=F32)

    @pl.when(kstep == pl.num_programs(1) - 1)
    def _():
        o_ref[...] = acc_ref[...].astype(BF16)


def _fourier(zf, chan_w, pos_cs, tm, tk):
    bsz, length, ch = zf.shape
    tr = min(512, length)
    xcs = pl.pallas_call(
        _chan_dft_kernel,
        grid=(bsz, length // tr),
        in_specs=[pl.BlockSpec((1, tr, ch), lambda b, i: (b, i, 0)),
                  pl.BlockSpec((ch, 2 * ch), lambda b, i: (0, 0))],
        out_specs=pl.BlockSpec((1, 2, tr, ch), lambda b, i: (b, 0, i, 0)),
        out_shape=jax.ShapeDtypeStruct((bsz, 2, length, ch), BF16),
        compiler_params=_params("parallel", "parallel"),
        name="chan_dft",
    )(zf, chan_w)
    xcs = xcs.reshape(bsz, 2 * length, ch)
    return pl.pallas_call(
        functools.partial(_pos_dft_kernel, bsz),
        grid=(length // tm, 2 * length // tk),
        in_specs=[pl.BlockSpec((tm, tk), lambda i, kk: (i, kk)),
                  pl.BlockSpec((bsz, tk, ch), lambda i, kk: (0, kk, 0))],
        out_specs=pl.BlockSpec((bsz, tm, ch), lambda i, kk: (0, i, 0)),
        out_shape=jax.ShapeDtypeStruct((bsz, length, ch), BF16),
        scratch_shapes=[pltpu.VMEM((bsz, tm, ch), F32)],
        compiler_params=_params("parallel", "arbitrary"),
        name="pos_dft",
    )(pos_cs, xcs)


def _dft_tables(length):
    m = jnp.arange(length, dtype=jnp.int32)[None, :]
    l1 = jnp.arange(length // 64, dtype=jnp.int32)[:, None] * 64
    l2 = jnp.arange(64, dtype=jnp.int32)[:, None]
    ang = lambda l: ((l * m) % length).astype(F32) * (2.0 * jnp.pi / length)
    c1, s1, c2, s2 = jnp.cos(ang(l1)), jnp.sin(ang(l1)), jnp.cos(ang(l2)), jnp.sin(ang(l2))
    scale = 1.0 / jnp.sqrt(jnp.float32(length))
    cos = (c1[:, None, :] * c2[None] - s1[:, None, :] * s2[None]).reshape(length, length) * scale
    sin = (s1[:, None, :] * c2[None] + c1[:, None, :] * s2[None]).reshape(length, length) * scale
    return jnp.concatenate([cos, sin], axis=1).astype(BF16)


def _chan_dft_weights(ngroups):
    j = jnp.arange(HEAD_DIM, dtype=jnp.int32)
    ang = ((j[:, None] * j[None, :]) % HEAD_DIM).astype(F32) * (2.0 * jnp.pi / HEAD_DIM)
    eye = jnp.eye(ngroups, dtype=F32)
    scale = 1.0 / jnp.sqrt(jnp.float32(HEAD_DIM))
    cc = jnp.kron(eye, jnp.cos(ang) * scale)
    sc = jnp.kron(eye, jnp.sin(ang) * scale)
    return jnp.concatenate([cc, -sc], axis=1).astype(BF16)


def _outproj_kernel(yrc_ref, yf_ref, w_ref, x_ref, gn1_ref, gate_ref, gn2_ref, sc_ref, sh_ref, xo_ref, h_ref):
    nrc = yrc_ref.shape[2]
    o = (jnp.dot(yrc_ref[0], w_ref[0:nrc, :], preferred_element_type=F32) +
         jnp.dot(yf_ref[0], w_ref[nrc:, :], preferred_element_type=F32))
    xn = x_ref[0] + gate_ref[0] * _rms(o, gn1_ref[...])
    xo_ref[0] = xn
    h_ref[0] = (_rms(xn, gn2_ref[...]) * (1.0 + sc_ref[0]) + sh_ref[0]).astype(BF16)


def _outproj(yrc, yf, w, x, gn1, gate, gn2, sc, sh, tm):
    bsz, length, d = x.shape
    row = lambda b, i: (b, i, 0)
    vec = pl.BlockSpec((1, d), lambda b, i: (0, 0))
    mod = pl.BlockSpec((1, 1, d), lambda b, i: (b, 0, 0))
    return pl.pallas_call(
        _outproj_kernel,
        grid=(bsz, length // tm),
        in_specs=[pl.BlockSpec((1, tm, yrc.shape[2]), row), pl.BlockSpec((1, tm, yf.shape[2]), row),
                  pl.BlockSpec(w.shape, lambda b, i: (0, 0)), pl.BlockSpec((1, tm, d), row),
                  vec, mod, vec, mod, mod],
        out_specs=[pl.BlockSpec((1, tm, d), row), pl.BlockSpec((1, tm, d), row)],
        out_shape=[jax.ShapeDtypeStruct((bsz, length, d), F32), jax.ShapeDtypeStruct((bsz, length, d), BF16)],
        compiler_params=_params("parallel", "parallel"),
        name="outproj",
    )(yrc, yf, w, x, gn1, gate, gn2, sc, sh)


FF_SUB = 512


def _swiglu_partial(h, wg, wu, wd, tf):
    def up(c, n):
        return (jnp.dot(h, wg(c, n), preferred_element_type=F32), jnp.dot(h, wu(c, n), preferred_element_type=F32))

    def down(gu, c, n):
        return jnp.dot((_silu(gu[0]) * gu[1]).astype(BF16), wd(c, n), preferred_element_type=F32)

    out = None
    prev = None
    for cut in [(c, min(FF_SUB, tf - c)) for c in range(0, tf, FF_SUB)]:
        gu = up(*cut)
        if prev is not None:
            part = down(*prev)
            out = part if out is None else out + part
        prev = (gu,) + cut
    part = down(*prev)
    return part if out is None else out + part


def _ffn_kernel(h_ref, wg_ref, wu_ref, wd_ref, x_ref, gn_ref, gate_ref, o_ref, acc_ref):
    j = pl.program_id(2)

    @pl.when(j == 0)
    def _():
        acc_ref[...] = jnp.zeros_like(acc_ref)

    acc_ref[...] += _swiglu_partial(h_ref[0], lambda c, n: wg_ref[:, c:c + n], lambda c, n: wu_ref[:, c:c + n],
                                    lambda c, n: wd_ref[c:c + n, :], wg_ref.shape[1])

    @pl.when(j == pl.num_programs(2) - 1)
    def _():
        o_ref[0] = x_ref[0] + gate_ref[0] * _rms(acc_ref[...], gn_ref[...])


def _ffn(h, wg, wu, wd, x, gn, gate, tm, tf):
    bsz, length, d = x.shape
    ff = wg.shape[1]
    row = lambda b, i, j: (b, i, 0)
    return pl.pallas_call(
        _ffn_kernel,
        grid=(bsz, length // tm, ff // tf),
        in_specs=[pl.BlockSpec((1, tm, d), row),
                  pl.BlockSpec((d, tf), lambda b, i, j: (0, j)),
                  pl.BlockSpec((d, tf), lambda b, i, j: (0, j)),
                  pl.BlockSpec((tf, d), lambda b, i, j: (j, 0)),
                  pl.BlockSpec((1, tm, d), row),
                  pl.BlockSpec((1, d), lambda b, i, j: (0, 0)),
                  pl.BlockSpec((1, 1, d), lambda b, i, j: (b, 0, 0))],
        out_specs=pl.BlockSpec((1, tm, d), row),
        out_shape=jax.ShapeDtypeStruct((bsz, length, d), F32),
        scratch_shapes=[pltpu.VMEM((tm, d), F32)],
        compiler_params=_params("parallel", "parallel", "arbitrary"),
        name="ffn",
    )(h, wg, wu, wd, x, gn, gate)


MOE_TILE = 512
MOE_BLOCK = 128
MOE_ALIGN = 16
MOE_TM = 1024


def _router_kernel(x_ref, gn_ref, sc_ref, sh_ref, w_ref, b_ref, tri_ref, routes_ref, routest_ref, cnt_ref):
    h = _rms(x_ref[0], gn_ref[...]) * (1.0 + sc_ref[0]) + sh_ref[0]
    logits = jnp.dot(h, w_ref[...], preferred_element_type=F32, precision=lax.Precision.HIGHEST) + b_ref[...]
    ne = float(logits.shape[1])
    idx = lax.broadcasted_iota(jnp.int32, logits.shape, 1).astype(F32)
    m1 = jnp.max(logits, axis=1, keepdims=True)
    i1 = jnp.min(jnp.where(logits == m1, idx, ne), axis=1, keepdims=True)
    rest = jnp.where(idx == i1, -jnp.inf, logits)
    m2 = jnp.max(rest, axis=1, keepdims=True)
    i2 = jnp.min(jnp.where(rest == m2, idx, ne), axis=1, keepdims=True)
    e2 = jnp.exp(m2 - m1)
    p1 = 1.0 / (1.0 + e2)
    p2 = e2 / (1.0 + e2)
    sel = jnp.where(idx == i1, 1.0, jnp.where(idx == i2, 1.0, 0.0))
    before = jnp.dot(tri_ref[...], sel.astype(BF16), preferred_element_type=F32)
    rank1 = jnp.sum(jnp.where(idx == i1, before, 0.0), axis=1, keepdims=True)
    rank2 = jnp.sum(jnp.where(idx == i2, before, 0.0), axis=1, keepdims=True)
    routes = jnp.zeros_like(logits)
    for lane, val in enumerate((i1, i2, rank1, rank2, p1, p2)):
        routes = jnp.where(idx == float(lane), val, routes)
    routes_ref[0] = routes
    routest_ref[0] = jnp.transpose(routes)[0:N_EXPERTS, :]
    cnt_ref[0, 0] = jnp.sum(sel, axis=0, keepdims=True)


def _router(x, gn, sc, sh, w, b):
    bsz, length, d = x.shape
    tm = MOE_TILE
    pad = LANES - w.shape[1]
    w = jnp.pad(w, ((0, 0), (0, pad)))
    b = jnp.pad(b, ((0, 0), (0, pad)), constant_values=-1e30)
    ne = LANES
    t = jnp.arange(tm, dtype=jnp.int32)
    tri = (t[None, :] < t[:, None]).astype(BF16)
    row = lambda bb, i: (bb, i, 0)
    vec = pl.BlockSpec((1, d), lambda bb, i: (0, 0))
    mod = pl.BlockSpec((1, 1, d), lambda bb, i: (bb, 0, 0))
    nt = length // tm
    return pl.pallas_call(
        _router_kernel,
        grid=(bsz, nt),
        in_specs=[pl.BlockSpec((1, tm, d), row), vec, mod, mod,
                  pl.BlockSpec((d, ne), lambda bb, i: (0, 0)), pl.BlockSpec((1, ne), lambda bb, i: (0, 0)),
                  pl.BlockSpec((tm, tm), lambda bb, i: (0, 0))],
        out_specs=[pl.BlockSpec((1, tm, ne), row),
                   pl.BlockSpec((1, N_EXPERTS, tm), lambda bb, i: (bb, 0, i)),
                   pl.BlockSpec((1, 1, 1, ne), lambda bb, i: (bb, i, 0, 0))],
        out_shape=[jax.ShapeDtypeStruct((bsz, length, ne), F32),
                   jax.ShapeDtypeStruct((bsz, N_EXPERTS, length), F32),
                   jax.ShapeDtypeStruct((bsz, nt, 1, ne), F32)],
        compiler_params=_params("parallel", "parallel"),
        name="router",
    )(x, gn, sc, sh, w, b, tri)


_MOE_SHIFT = MOE_BLOCK.bit_length() - 1
_MOE_BUF_BLOCKS = (2 * MOE_TILE + N_EXPERTS * (MOE_ALIGN - 1 + MOE_BLOCK - 1)) // MOE_BLOCK + 1


def _align_rows(c):
    return ((c + (MOE_ALIGN - 1)) >> (MOE_ALIGN.bit_length() - 1)) << (MOE_ALIGN.bit_length() - 1)


def _blocks(c):
    return (c + (MOE_BLOCK - 1)) >> _MOE_SHIFT


def _dispatch_copies(start, tile, off_ref, cnt_ref, buf_ref, hs_ref, sem):
    def run(src_row, dst_row, nrows):
        cp = pltpu.make_async_copy(buf_ref.at[pl.ds(pl.multiple_of(src_row, MOE_ALIGN), nrows)],
                                   hs_ref.at[pl.ds(pl.multiple_of(dst_row, MOE_ALIGN), nrows)], sem)
        if start:
            cp.start()
        else:
            cp.wait()

    slot = jnp.int32(0)
    for e in range(N_EXPERTS):
        rows = _align_rows(cnt_ref[tile * N_EXPERTS + e])
        dst0 = off_ref[tile * N_EXPERTS + e]
        src0 = slot * MOE_BLOCK
        nfull = rows >> _MOE_SHIFT

        def full(rb, carry, src0=src0, dst0=dst0):
            run(src0 + rb * MOE_BLOCK, dst0 + rb * MOE_BLOCK, MOE_BLOCK)
            return carry

        lax.fori_loop(0, nfull, full, 0)
        done = nfull * MOE_BLOCK
        piece = MOE_BLOCK // 2
        while piece >= MOE_ALIGN:
            has = (rows & piece) != 0

            @pl.when(has)
            def _(done=done, piece=piece, src0=src0, dst0=dst0):
                run(src0 + done, dst0 + done, piece)

            done = done + jnp.where(has, piece, 0)
            piece //= 2
        slot = slot + _blocks(rows)


def _staging_rows(tile, cnt_ref, i1, i2, rank1, rank2):
    row1 = jnp.zeros_like(i1)
    row2 = jnp.zeros_like(i2)
    slot = jnp.int32(0)
    for e in range(N_EXPERTS):
        first_row = (slot * MOE_BLOCK).astype(F32)
        row1 = jnp.where(i1 == float(e), first_row, row1)
        row2 = jnp.where(i2 == float(e), first_row, row2)
        slot = slot + _blocks(cnt_ref[tile * N_EXPERTS + e])
    return row1 + rank1, row2 + rank2


def _dispatch_kernel(off_ref, cnt_ref, h_ref, routest_ref, hs_init_ref, hs_ref, buf_ref, sem):
    del hs_init_ref
    tile = pl.program_id(0)
    rt = routest_ref[0]
    row1, row2 = _staging_rows(tile, cnt_ref, rt[0:1], rt[1:2], rt[2:3], rt[3:4])
    rows = lax.broadcasted_iota(jnp.int32, (MOE_BLOCK, 1), 0)
    onehot = jnp.concatenate(
        [jnp.where(row1 == (rows + blk * MOE_BLOCK).astype(F32), 1.0,
                   jnp.where(row2 == (rows + blk * MOE_BLOCK).astype(F32), 1.0, 0.0)).astype(BF16)
         for blk in range(_MOE_BUF_BLOCKS)], axis=0)
    buf_ref[...] = jnp.dot(onehot, h_ref[...], preferred_element_type=F32).astype(BF16)
    _dispatch_copies(True, tile, off_ref, cnt_ref, buf_ref, hs_ref, sem)
    _dispatch_copies(False, tile, off_ref, cnt_ref, buf_ref, hs_ref, sem)


def _dispatch(h, rankt, off, cnt, nrows):
    n, d = h.shape
    ntb = rankt.shape[2] // MOE_TILE
    grid_spec = pltpu.PrefetchScalarGridSpec(
        num_scalar_prefetch=2, grid=(n // MOE_TILE,),
        in_specs=[pl.BlockSpec((MOE_TILE, d), lambda t, o, c: (t, 0)),
                  pl.BlockSpec((1, N_EXPERTS, MOE_TILE), lambda t, o, c: (t // ntb, 0, t % ntb)),
                  pl.BlockSpec(memory_space=pl.ANY)],
        out_specs=pl.BlockSpec(memory_space=pl.ANY),
        scratch_shapes=[pltpu.VMEM((_MOE_BUF_BLOCKS * MOE_BLOCK, d), BF16), pltpu.SemaphoreType.DMA(())])
    return pl.pallas_call(
        _dispatch_kernel, grid_spec=grid_spec,
        out_shape=jax.ShapeDtypeStruct((nrows, d), BF16),
        input_output_aliases={4: 0},
        compiler_params=_params("arbitrary"),
        name="moe_dispatch",
    )(off, cnt, h, rankt, jnp.zeros((nrows, d), BF16))


def _gffn_kernel(te_ref, tv_ref, h_ref, wg_ref, wu_ref, wd_ref, o_ref, acc_ref):
    q = pl.program_id(0)
    j = pl.program_id(1)

    @pl.when(j == 0)
    def _():
        acc_ref[...] = jnp.zeros_like(acc_ref)

    @pl.when(tv_ref[q] > 0)
    def _():
        acc_ref[...] += _swiglu_partial(h_ref[...], lambda c, n: wg_ref[0, :, c:c + n],
                                        lambda c, n: wu_ref[0, :, c:c + n], lambda c, n: wd_ref[0, c:c + n, :],
                                        wg_ref.shape[2])

    @pl.when(j == pl.num_programs(1) - 1)
    def _():
        o_ref[...] = acc_ref[...].astype(BF16)


def _gffn(hs, te, tv, wg, wu, wd, tf):
    nrows, d = hs.shape
    ff = wg.shape[2]
    grid_spec = pltpu.PrefetchScalarGridSpec(
        num_scalar_prefetch=2, grid=(nrows // MOE_TM, ff // tf),
        in_specs=[pl.BlockSpec((MOE_TM, d), lambda q, j, te, tv: (q, 0)),
                  pl.BlockSpec((1, d, tf), lambda q, j, te, tv: (te[q], 0, j)),
                  pl.BlockSpec((1, d, tf), lambda q, j, te, tv: (te[q], 0, j)),
                  pl.BlockSpec((1, tf, d), lambda q, j, te, tv: (te[q], j, 0))],
        out_specs=pl.BlockSpec((MOE_TM, d), lambda q, j, te, tv: (q, 0)),
        scratch_shapes=[pltpu.VMEM((MOE_TM, d), F32)])
    return pl.pallas_call(
        _gffn_kernel, grid_spec=grid_spec,
        out_shape=jax.ShapeDtypeStruct((nrows, d), BF16),
        compiler_params=_params("arbitrary", "arbitrary"),
        name="moe_experts",
    )(te, tv, hs, wg, wu, wd)


def _combine_copies(start, tile, off_ref, cnt_ref, ys_ref, buf_ref, sem):
    slot = jnp.int32(0)
    for e in range(N_EXPERTS):
        nb = _blocks(cnt_ref[tile * N_EXPERTS + e])
        src0 = off_ref[tile * N_EXPERTS + e]

        def body(rb, carry, slot=slot, src0=src0):
            cp = pltpu.make_async_copy(
                ys_ref.at[pl.ds(pl.multiple_of(src0 + rb * MOE_BLOCK, MOE_ALIGN), MOE_BLOCK)],
                buf_ref.at[pl.ds(pl.multiple_of((slot + rb) * MOE_BLOCK, MOE_BLOCK), MOE_BLOCK)], sem)
            if start:
                cp.start()
            else:
                cp.wait()
            return carry

        lax.fori_loop(0, nb, body, 0)
        slot = slot + nb


def _combine_kernel(off_ref, cnt_ref, ys_ref, routes_ref, x_ref, gn_ref, gate_ref, o_ref, buf_ref, sem):
    tile = pl.program_id(0)

    @pl.when(tile == 0)
    def _():
        buf_ref[...] = jnp.zeros_like(buf_ref)

    _combine_copies(True, tile, off_ref, cnt_ref, ys_ref, buf_ref, sem)
    routes = routes_ref[0]
    col = lambda j: routes[:, j:j + 1]
    row1, row2 = _staging_rows(tile, cnt_ref, col(0), col(1), col(2), col(3))
    lanes = lax.broadcasted_iota(jnp.int32, (1, MOE_BLOCK), 1)
    scatter = jnp.concatenate(
        [(jnp.where(row1 == (lanes + blk * MOE_BLOCK).astype(F32), col(4), 0.0) +
          jnp.where(row2 == (lanes + blk * MOE_BLOCK).astype(F32), col(5), 0.0)).astype(BF16)
         for blk in range(_MOE_BUF_BLOCKS)], axis=1)
    _combine_copies(False, tile, off_ref, cnt_ref, ys_ref, buf_ref, sem)
    mixed = jnp.dot(scatter, buf_ref[...], preferred_element_type=F32)
    o_ref[0] = x_ref[0] + gate_ref[0] * _rms(mixed, gn_ref[...])


def _combine(ys, routes, x, gn, gate, off, cnt):
    bsz, length, d = x.shape
    ntb = length // MOE_TILE
    tok = lambda t, o, c: (t // ntb, t % ntb, 0)
    grid_spec = pltpu.PrefetchScalarGridSpec(
        num_scalar_prefetch=2, grid=(bsz * ntb,),
        in_specs=[pl.BlockSpec(memory_space=pl.ANY),
                  pl.BlockSpec((1, MOE_TILE, LANES), tok),
                  pl.BlockSpec((1, MOE_TILE, d), tok),
                  pl.BlockSpec((1, d), lambda t, o, c: (0, 0)),
                  pl.BlockSpec((1, 1, d), lambda t, o, c: (t // ntb, 0, 0))],
        out_specs=pl.BlockSpec((1, MOE_TILE, d), tok),
        scratch_shapes=[pltpu.VMEM((_MOE_BUF_BLOCKS * MOE_BLOCK, d), BF16), pltpu.SemaphoreType.DMA(())])
    return pl.pallas_call(
        _combine_kernel, grid_spec=grid_spec,
        out_shape=jax.ShapeDtypeStruct((bsz, length, d), F32),
        compiler_params=_params("arbitrary"),
        name="moe_combine",
    )(off, cnt, ys, routes, x, gn, gate)


def _route_plan(counts, ntokens):
    ntiles = counts.shape[0]
    rows = (counts + (MOE_ALIGN - 1)) // MOE_ALIGN * MOE_ALIGN
    seg = (rows.sum(0) + MOE_BLOCK + MOE_TM - 1) // MOE_TM * MOE_TM
    seg_end = jnp.cumsum(seg)
    off = (seg_end - seg)[None, :] + jnp.cumsum(rows, axis=0) - rows
    nq = -(-(2 * ntokens + ntiles * N_EXPERTS * (MOE_ALIGN - 1) + N_EXPERTS * (MOE_BLOCK + MOE_TM - 1)) // MOE_TM)
    q = jnp.arange(nq, dtype=jnp.int32) * MOE_TM
    te = jnp.minimum(jnp.sum(q[:, None] >= seg_end[None, :], axis=1), N_EXPERTS - 1).astype(jnp.int32)
    tv = (q < seg_end[-1]).astype(jnp.int32)
    return off.reshape(-1).astype(jnp.int32), te, tv, nq * MOE_TM


def _moe(h, x, gn_pre, sc, sh, router_w, router_b, wg, wu, wd, gn_post, gate, tf):
    bsz, length, d = x.shape
    routes, routest, cnt = _router(x, gn_pre, sc, sh, router_w, router_b)
    counts = cnt[:, :, 0, :N_EXPERTS].astype(jnp.int32).reshape(-1, N_EXPERTS)
    off, te, tv, nrows = _route_plan(counts, bsz * length)
    cnt_flat = counts.reshape(-1)
    hs = _dispatch(h.reshape(bsz * length, d), routest, off, cnt_flat, nrows)
    ys = _gffn(hs, te, tv, wg, wu, wd, tf)
    return _combine(ys, routes, x, gn_post, gate, off, cnt_flat)


def _sincos_2d(rows, cols, dim):
    quarter = dim // 4
    omega = 1.0 / (POS_BASE ** (jnp.arange(quarter, dtype=F32) / quarter))

    def axis_emb(n):
        ang = jnp.arange(n, dtype=F32)[:, None] * omega[None, :]
        return jnp.concatenate([jnp.sin(ang), jnp.cos(ang)], axis=-1)

    er, ec = axis_emb(rows), axis_emb(cols)
    emb = jnp.concatenate([jnp.broadcast_to(er[:, None, :], (rows, cols, dim // 2)),
                           jnp.broadcast_to(ec[None, :, :], (rows, cols, dim // 2))], axis=-1)
    return emb.reshape(rows * cols, dim)


def _head_block_ones(width):
    h = jnp.arange(width, dtype=jnp.int32) // HEAD_DIM
    return (h[:, None] == h[None, :]).astype(BF16)


def _pad_lora(w2, d):
    z = jnp.zeros_like(w2[0])
    return jnp.concatenate([w2[0], z], axis=0) if d == 0 else jnp.concatenate([z, w2[1]], axis=0)


def _tile(length, pref):
    return pref if length % pref == 0 else length


def kernel(x, c, ctx, c_ctx, ada_w, ada_b, norm_g, w_in, w_out, rwkv_mu, rwkv_mu_w, rwkv_mu_a, rwkv_w0, rwkv_w2,
           rwkv_a0, rwkv_a2, rwkv_g2, rwkv_k_k, rwkv_k_a, rwkv_r_k, rwkv_gn_w, rwkv_gn_b, conv_w, ffn_w_gate,
           ffn_w_up, ffn_w_down, router_w, router_b, moe_w_gate, moe_w_up, moe_w_down):
    bsz, length, dim = x.shape
    ctx_len = ctx.shape[1]
    depth = ada_w.shape[0]
    d_rwkv = rwkv_w0.shape[-1]

    pos = _sincos_2d(length // GRID_W, GRID_W, dim).astype(x.dtype)
    bd = _head_block_ones(LANES)
    chan_w = _chan_dft_weights(N_FOUR_COLS // HEAD_DIM)
    dft_lat = _dft_tables(length)
    dft_ctx = _dft_tables(ctx_len)

    cvec = jnp.zeros((16, dim), F32).at[:bsz].set(c).at[bsz].set(c_ctx)
    mod = _adaln(cvec, ada_w, ada_b)
    zero_state = jnp.zeros((bsz, d_rwkv // LANES, LANES, LANES), F32)
    xc = ctx

    def mods(l, lo, hi):
        return [mod[l, lo:hi, i * dim:(i + 1) * dim][:, None, :] for i in range(6)]

    def mixers(l, zr, zc, zf, states0, period, dft, need_y=True):
        tm = _tile(zr.shape[1], 512)
        p = {"mu": rwkv_mu[l], "mu_w": rwkv_mu_w[l].reshape(1, -1), "mu_a": rwkv_mu_a[l].reshape(1, -1),
             "w0": rwkv_w0[l], "w2": jnp.stack([_pad_lora(rwkv_w2[l], d) for d in range(2)]), "a0": rwkv_a0[l],
             "a2": jnp.stack([_pad_lora(rwkv_a2[l], d) for d in range(2)]), "k_k": rwkv_k_k[l][None],
             "k_a": rwkv_k_a[l][None], "r_k": rwkv_r_k[l].reshape(1, -1), "gn_w": rwkv_gn_w[l][None],
             "gn_b": rwkv_gn_b[l][None], "bd": bd}
        seq_fwd, seq_bwd = _prep(zr, p, _tile(zr.shape[1], 256))
        ys, states = _scan(seq_fwd, seq_bwd, states0[0], states0[1], p)
        if not need_y:
            return None, None, states
        yrc = _mix(ys[0], ys[1], zr, zc, rwkv_g2[l], conv_w[l], period, tm)
        yf = _fourier(zf, chan_w, dft, _tile(zf.shape[1], 512), 512)
        return yrc, yf, states

    def channel_mixer(l, h2, xn, gn3, gate2, sc2, sh2, gn2):
        tm = _tile(xn.shape[1], 512)
        i = l // 2
        if l % 2 == 0:
            return _ffn(h2, ffn_w_gate[i].astype(BF16), ffn_w_up[i].astype(BF16), ffn_w_down[i].astype(BF16),
                        xn, gn3, gate2, tm, 1408)
        return _moe(h2, xn, gn2, sc2, sh2, router_w[i], router_b[i][None], moe_w_gate[i].astype(BF16),
                    moe_w_up[i].astype(BF16), moe_w_down[i].astype(BF16), gn3, gate2, 896)

    for l in range(depth):
        last = l == depth - 1
        w_in_l = w_in[l].astype(BF16)
        w_out_l = w_out[l].astype(BF16)
        gn = [norm_g[l, i][None] for i in range(4)]
        sh1, sc1, g1, sh2, sc2, g2 = mods(l, 0, bsz)
        csh1, csc1, cg1, csh2, csc2, cg2 = mods(l, bsz, bsz + 1)

        xc_flat = xc.reshape(1, bsz * ctx_len, dim)
        tmc = _tile(bsz * ctx_len, 512)
        zr, zc, zf = _inproj(xc_flat, None, gn[0], csc1, csh1, w_in_l, tmc)
        unflat = lambda t: t.reshape(bsz, ctx_len, t.shape[-1])
        yrc, yf, ctx_states = mixers(l, unflat(zr), unflat(zc), unflat(zf), (zero_state, zero_state), ctx_len,
                                     dft_ctx, need_y=not last)
        if not last:
            flat = lambda t: t.reshape(1, bsz * ctx_len, t.shape[-1])
            xcn, hc2 = _outproj(flat(yrc), flat(yf), w_out_l, xc_flat, gn[1], cg1, gn[2], csc2, csh2, tmc)
            xc = channel_mixer(l, hc2, xcn, gn[3], cg2, csc2, csh2, gn[2]).reshape(bsz, ctx_len, dim)

        tm = _tile(length, 512)
        if l == 0:
            zr, zc, zf, x = _inproj(x, pos, gn[0], sc1, sh1, w_in_l, tm)
        else:
            zr, zc, zf = _inproj(x, None, gn[0], sc1, sh1, w_in_l, tm)
        yrc, yf, _ = mixers(l, zr, zc, zf, ctx_states, GRID_W, dft_lat)
        xn, h2 = _outproj(yrc, yf, w_out_l, x, gn[1], g1, gn[2], sc2, sh2, _tile(length, 512))
        x = channel_mixer(l, h2, xn, gn[3], g2, sc2, sh2, gn[2])
    return x
```

```python
import functools

import jax
import jax.numpy as jnp
from jax import lax
from jax.experimental import pallas as pl
from jax.experimental.pallas import tpu as pltpu

F32 = jnp.float32
BF16 = jnp.bfloat16

HEAD_DIM = 64
GRID_W = 64
NORM_EPS = 1e-6
GN_EPS = 64e-5
POS_BASE = 10000.0
N_EXPERTS = 8
CHUNK = 64
LANES = 128
VMEM_LIMIT = 56 * 1024 * 1024


def _params(*sem):
    return pltpu.CompilerParams(dimension_semantics=sem, vmem_limit_bytes=VMEM_LIMIT)


def _dot(a, b):
    return jnp.dot(a.astype(BF16), b.astype(BF16), preferred_element_type=F32)


def _dot_nt(a, b):
    return lax.dot_general(a.astype(BF16), b.astype(BF16), (((1,), (1,)), ((), ())),
                           preferred_element_type=F32)


def _split(x):
    hi = x.astype(BF16)
    lo = (x - hi.astype(F32)).astype(BF16)
    return hi, lo


def _head_sums(x, ones_pair):
    rows = x.shape[0]
    tiles = x.shape[1] // LANES
    hi, lo = _split(x)
    stacked = jnp.concatenate([t[:, g * LANES:(g + 1) * LANES] for t in (hi, lo) for g in range(tiles)], axis=0)
    s = jnp.dot(stacked, ones_pair, preferred_element_type=F32)
    return jnp.concatenate([s[g * rows:(g + 1) * rows] + s[(tiles + g) * rows:(tiles + g + 1) * rows]
                            for g in range(tiles)], axis=1)


def _dot_split_rhs(w, x):
    hi, lo = _split(x)
    return (jnp.dot(w, hi, preferred_element_type=F32) + jnp.dot(w, lo, preferred_element_type=F32))


def _rms(x, g):
    ms = jnp.mean(x * x, axis=-1, keepdims=True)
    return x * lax.rsqrt(ms + NORM_EPS) * g


def _silu(x):
    return x * jax.nn.sigmoid(x)


def _softplus(x):
    return jnp.maximum(x, 0.0) + jnp.log(1.0 + jnp.exp(-jnp.abs(x)))


def _adaln_kernel(c_ref, w_ref, b_ref, o_ref):
    s = _silu(c_ref[...])
    o_ref[0] = _dot(s, w_ref[0]) + b_ref[0]


def _adaln(cvec, ada_w, ada_b):
    nl, d, n = ada_w.shape
    rows = cvec.shape[0]
    tn = 1536
    return pl.pallas_call(
        _adaln_kernel,
        grid=(nl, n // tn),
        in_specs=[pl.BlockSpec((rows, d), lambda l, j: (0, 0)),
                  pl.BlockSpec((1, d, tn), lambda l, j: (l, 0, j)),
                  pl.BlockSpec((1, 1, tn), lambda l, j: (l, 0, j))],
        out_specs=pl.BlockSpec((1, rows, tn), lambda l, j: (l, 0, j)),
        out_shape=jax.ShapeDtypeStruct((nl, rows, n), F32),
        compiler_params=_params("arbitrary", "arbitrary"),
        name="adaln",
    )(cvec, ada_w, ada_b.reshape(nl, 1, n))


N_RWKV_COLS = 1920
N_CONV_COLS = 768
N_FOUR_COLS = 256


def _inproj_kernel(add_pos, *refs):
    if add_pos:
        x_ref, pos_ref, g_ref, sc_ref, sh_ref, w_ref, zr_ref, zc_ref, zf_ref, xp_ref = refs
        x = x_ref[0] + pos_ref[...]
        xp_ref[0] = x
    else:
        x_ref, g_ref, sc_ref, sh_ref, w_ref, zr_ref, zc_ref, zf_ref = refs
        x = x_ref[0]
    h = (_rms(x, g_ref[...]) * (1.0 + sc_ref[0]) + sh_ref[0]).astype(BF16)
    a, b = N_RWKV_COLS, N_RWKV_COLS + N_CONV_COLS
    zr_ref[0] = jnp.dot(h, w_ref[:, :a], preferred_element_type=F32)
    zc_ref[0] = jnp.dot(h, w_ref[:, a:b], preferred_element_type=F32)
    zf_ref[0] = jnp.dot(h, w_ref[:, b:], preferred_element_type=F32)


def _inproj(x, pos, g, sc, sh, w, tm):
    bsz, length, d = x.shape
    n = w.shape[1]
    add_pos = pos is not None
    row = lambda b, i: (b, i, 0)
    in_specs = [pl.BlockSpec((1, tm, d), row)]
    args = [x]
    if add_pos:
        in_specs.append(pl.BlockSpec((tm, d), lambda b, i: (i, 0)))
        args.append(pos)
    in_specs += [pl.BlockSpec((1, d), lambda b, i: (0, 0)),
                 pl.BlockSpec((1, 1, d), lambda b, i: (b, 0, 0)),
                 pl.BlockSpec((1, 1, d), lambda b, i: (b, 0, 0)),
                 pl.BlockSpec((d, n), lambda b, i: (0, 0))]
    args += [g, sc, sh, w]
    out_shape = [jax.ShapeDtypeStruct((bsz, length, N_RWKV_COLS), F32),
                 jax.ShapeDtypeStruct((bsz, length, N_CONV_COLS), F32),
                 jax.ShapeDtypeStruct((bsz, length, N_FOUR_COLS), F32)]
    out_specs = [pl.BlockSpec((1, tm, N_RWKV_COLS), row),
                 pl.BlockSpec((1, tm, N_CONV_COLS), row),
                 pl.BlockSpec((1, tm, N_FOUR_COLS), row)]
    if add_pos:
        out_shape.append(jax.ShapeDtypeStruct((bsz, length, d), F32))
        out_specs.append(pl.BlockSpec((1, tm, d), row))
    return pl.pallas_call(
        functools.partial(_inproj_kernel, add_pos),
        grid=(bsz, length // tm),
        in_specs=in_specs, out_specs=out_specs, out_shape=out_shape,
        compiler_params=_params("parallel", "parallel"),
        name="inproj",
    )(*args)


HALO = 8


def _rwkv_features(z, edge_row, d, mu_ref, muw_ref, mua_ref, w0_ref, w2_ref, a0_ref, a2_ref, kk_ref, ka_ref, bd):
    n = z.shape[0]
    rows = lax.broadcasted_iota(jnp.int32, (n, 1), 0)
    if d:
        zs = jnp.where(rows == n - 1, edge_row, pltpu.roll(z, n - 1, 0))
    else:
        zs = jnp.where(rows == 0, edge_row, pltpu.roll(z, 1, 0))
    dz = zs - z
    k = z[:, 0:512] + dz[:, 0:512] * mu_ref[d, 0:1, :]
    v = z[:, 512:1024] + dz[:, 512:1024] * mu_ref[d, 1:2, :]
    r = z[:, 1280:1792] + dz[:, 1280:1792] * mu_ref[d, 2:3, :]
    wl = z[:, 1024:1152] + dz[:, 1024:1152] * muw_ref[...]
    al = z[:, 1152:1280] + dz[:, 1152:1280] * mua_ref[...]
    w_log = -_softplus(-(w0_ref[d:d + 1, :] + _dot(jnp.tanh(wl), w2_ref[d]))) - 0.5
    lw = -jnp.exp(w_log)
    a = jax.nn.sigmoid(a0_ref[d:d + 1, :] + _dot(al, a2_ref[d]))
    kk = k * kk_ref[...]
    ss = _head_sums(kk * kk, bd)
    kk = kk / jnp.maximum(jnp.sqrt(ss), 1e-12)
    return r, lw, k * (1.0 + (a - 1.0) * ka_ref[...]), v, -kk, kk * a


SOLVE_BLOCK = 8


def _unit_lower_solve(a_list, x_list, diag_blk, eye, side_work=()):
    side_work = list(side_work)

    def boundary():
        if side_work:
            side_work.pop(0)()

    n = eye.shape[0]
    a_d = [jnp.where(diag_blk, a, 0.0) for a in a_list]
    a_o = [a - d for a, d in zip(a_list, a_d)]
    dinv = [eye + d for d in a_d]
    apow = a_d
    for _ in range(SOLVE_BLOCK.bit_length() - 2):
        apow = [_dot(p, p) for p in apow]
        boundary()
        dinv = [d + _dot(p, d) for p, d in zip(apow, dinv)]
        boundary()
    b = [_dot(d, o) for d, o in zip(dinv, a_o)]
    boundary()
    x = [_dot(d, x) for d, x in zip(dinv, x_list)]
    levels = (n // SOLVE_BLOCK).bit_length() - 1
    for level in range(levels):
        boundary()
        if level:
            b = [_dot(m, m) for m in b]
        x = [v + _dot(m, v) for m, v in zip(b, x)]
    while side_work:
        boundary()
    return x


def _scan_kernel(*refs):
    z_refs = (refs[0], refs[2])
    halo_refs = (refs[1], refs[3])
    zfirst_refs = refs[4:6]
    s0_refs = refs[6:8]
    feat_refs = refs[8:17]
    rk_ref, gw_ref, gb_ref, bd_ref = refs[17:21]
    y_refs = refs[21:23]
    st_refs = refs[23:25]
    feat_ref = refs[25]
    c = pl.program_id(1)

    @pl.when(c == 0)
    def _():
        for st_ref, s0_ref in zip(st_refs, s0_refs):
            st_ref[...] = s0_ref[...]
        no_token = jnp.zeros((1, zfirst_refs[0].shape[2]), F32)
        for d in range(2):
            for j, f in enumerate(_rwkv_features(zfirst_refs[d][0], no_token, d, *feat_refs, bd_ref[...])):
                feat_ref[d, j] = f

    n = CHUNK
    t_i = lax.broadcasted_iota(jnp.int32, (n, n), 0)
    s_i = lax.broadcasted_iota(jnp.int32, (n, n), 1)
    blk_shift = SOLVE_BLOCK.bit_length() - 1
    diag_blk = (s_i >> blk_shift) == (t_i >> blk_shift)
    eye = jnp.where(s_i == t_i, 1.0, 0.0)
    lane2 = lax.broadcasted_iota(jnp.int32, (n, LANES), 1)
    t_2 = lax.broadcasted_iota(jnp.int32, (n, LANES), 0)
    s_2 = lane2 & (n - 1)
    first = lane2 < HEAD_DIM
    r_bd = lax.broadcasted_iota(jnp.int32, (LANES, LANES), 0) < HEAD_DIM
    c_bd = lax.broadcasted_iota(jnp.int32, (LANES, LANES), 1) < HEAD_DIM
    blockdiag = r_bd == c_bd
    incl2 = (s_2 <= t_2, s_2 >= t_2)
    strict2 = (s_2 < t_2, s_2 > t_2)

    npairs = st_refs[0].shape[1]
    sls = [slice(g * LANES, (g + 1) * LANES) for g in range(npairs)]
    items = [(d, g) for d in range(2) for g in range(npairs)]
    heads = [(i, j) for i in range(len(items)) for j in range(2)]
    feats = [[feat_ref[d, j] for j in range(6)] for d in range(2)]
    raw, scaled, w_tot = [], [], []
    for d in range(2):
        r, lw, k, v, a, b = feats[d]
        incl = (s_i >= t_i) if d else (s_i <= t_i)
        cum = _dot_split_rhs(jnp.where(incl, 1.0, 0.0).astype(BF16), lw)
        tot = cum[0:1, :] if d else cum[n - 1:n, :]
        w_inv = jnp.exp(-cum)
        w_rem = jnp.exp(tot - cum)
        raw.append((r, k, v))
        scaled.append((r * jnp.exp(cum), a * jnp.exp(cum - lw), b * w_inv, k * w_inv, b * w_rem, k * w_rem))
        w_tot.append(jnp.exp(tot))
    states = [st_refs[d][0, g] for d, g in items]

    v_g = [raw[d][2][:, sls[g]] for d, g in items]
    at_h = [(jnp.where(first, scaled[d][1][:, sls[g]], 0.0), jnp.where(first, 0.0, scaled[d][1][:, sls[g]]))
            for d, g in items]
    rt_h = [(jnp.where(first, scaled[d][0][:, sls[g]], 0.0), jnp.where(first, 0.0, scaled[d][0][:, sls[g]]))
            for d, g in items]
    s = [_dot_nt(jnp.concatenate([at_h[i][0], rt_h[i][0], at_h[i][1], rt_h[i][1]], axis=0),
                 jnp.concatenate([scaled[d][2][:, sls[g]], scaled[d][3][:, sls[g]]], axis=0))
         for i, (d, g) in enumerate(items)]
    sa = [(jnp.where(strict2[d], s[i][0:n], 0.0), jnp.where(strict2[d], s[i][2 * n:3 * n], 0.0))
          for i, (d, g) in enumerate(items)]
    sr = [(jnp.where(incl2[d], s[i][n:2 * n], 0.0), jnp.where(incl2[d], s[i][3 * n:4 * n], 0.0))
          for i, (d, g) in enumerate(items)]
    vv = [jnp.concatenate([x, x], axis=0) for x in v_g]
    akv = [_dot(jnp.where(first, 0.0, sa[i][j]), vv[i]) for i, j in heads]
    edge_rows = (halo_refs[0][0, HALO - 1:HALO, :], halo_refs[1][0, 0:1, :])

    def next_features(d):
        def work():
            for j, f in enumerate(_rwkv_features(z_refs[d][0], edge_rows[d], d, *feat_refs, bd_ref[...])):
                feat_ref[d, j] = f
        return work

    pq = _unit_lower_solve([sa[i][j][:, 0:n] for i, j in heads],
                           [jnp.concatenate([at_h[i][j], akv[2 * i + j]], axis=1) for i, j in heads],
                           diag_blk, eye, side_work=[next_features(0), lambda: None, lambda: None,
                                                     next_features(1)])
    idx = range(len(items))
    pt = [_dot_split_rhs(jnp.concatenate([pq[2 * i][:, :LANES], pq[2 * i + 1][:, :LANES]], axis=0).astype(BF16),
                         states[i]) for i in idx]
    rtt = [_dot_split_rhs(jnp.concatenate(rt_h[i], axis=0).astype(BF16), states[i]) for i in idx]
    u = [jnp.where(first, pt[i][0:n] + pq[2 * i][:, LANES:], pt[i][n:] + pq[2 * i + 1][:, LANES:]) for i in idx]
    uv = [jnp.concatenate([u[i], v_g[i]], axis=0) for i in idx]
    yh = [_dot(sr[i][j], uv[i]) for i, j in heads]
    ys = [jnp.where(first, rtt[i][0:n] + yh[2 * i], rtt[i][n:] + yh[2 * i + 1]) for i in idx]
    t_new = [_dot(jnp.concatenate([scaled[d][4][:, sls[g]], scaled[d][5][:, sls[g]]], axis=0).T, uv[i])
             for i, (d, g) in enumerate(items)]
    for i, (d, g) in enumerate(items):
        w_col = jnp.broadcast_to(w_tot[d][:, sls[g]], (LANES, LANES)).T
        st_refs[d][0, g] = jnp.where(blockdiag, w_col * states[i] + t_new[i], 0.0)
    bd = bd_ref[...]
    inv = 1.0 / HEAD_DIM
    y = [jnp.concatenate(ys[d * npairs:(d + 1) * npairs], axis=1) for d in range(2)]
    mean = [_head_sums(y[d], bd) * inv for d in range(2)]
    dev = [y[d] - mean[d] for d in range(2)]
    var = [_head_sums(dev[d] * dev[d], bd) * inv for d in range(2)]
    bonus = [_head_sums(raw[d][0] * raw[d][1] * rk_ref[...], bd) * raw[d][2] for d in range(2)]
    for d in range(2):
        y_refs[d][0] = dev[d] * lax.rsqrt(var[d] + GN_EPS) * gw_ref[...] + gb_ref[...] + bonus[d]


def _scan(zr, state_fwd, state_bwd, p):
    bsz, length, ncol = zr.shape
    width = p["w0"].shape[1]
    nc = length // CHUNK
    npairs = width // LANES
    per_chunk = CHUNK // HALO
    last_halo = length // HALO - 1
    fmap = lambda b, c: (b, c, 0)
    bmap = lambda b, c: (b, nc - 1 - c, 0)
    fnext = lambda c: jnp.minimum(c + 1, nc - 1)
    bnext = lambda c: jnp.maximum(nc - 2 - c, 0)
    fnext_map = lambda b, c: (b, fnext(c), 0)
    bnext_map = lambda b, c: (b, bnext(c), 0)
    prev_map = lambda b, c: (b, jnp.maximum(fnext(c) * per_chunk - 1, 0), 0)
    next_map = lambda b, c: (b, jnp.minimum((bnext(c) + 1) * per_chunk, last_halo), 0)
    full = lambda shape: pl.BlockSpec(shape, lambda b, c: (0,) * len(shape))
    st_spec = pl.BlockSpec((1, npairs, LANES, LANES), lambda b, c: (b, 0, 0, 0))
    y_shape = jax.ShapeDtypeStruct((bsz, length, width), F32)
    st_shape = jax.ShapeDtypeStruct((bsz, npairs, LANES, LANES), F32)
    y_f, y_b, st_f, st_b = pl.pallas_call(
        _scan_kernel,
        grid=(bsz, nc),
        in_specs=[pl.BlockSpec((1, CHUNK, ncol), fnext_map), pl.BlockSpec((1, HALO, ncol), prev_map),
                  pl.BlockSpec((1, CHUNK, ncol), bnext_map), pl.BlockSpec((1, HALO, ncol), next_map),
                  pl.BlockSpec((1, CHUNK, ncol), lambda b, c: (b, 0, 0)),
                  pl.BlockSpec((1, CHUNK, ncol), lambda b, c: (b, nc - 1, 0)),
                  st_spec, st_spec,
                  full((2, 3, width)), full((1, LANES)), full((1, LANES)), full((2, width)),
                  full((2, LANES, width)), full((2, width)), full((2, LANES, width)), full((1, width)),
                  full((1, width)), full((1, width)), full((1, width)), full((1, width)), full((LANES, LANES))],
        out_specs=[pl.BlockSpec((1, CHUNK, width), fmap), pl.BlockSpec((1, CHUNK, width), bmap), st_spec, st_spec],
        out_shape=[y_shape, y_shape, st_shape, st_shape],
        scratch_shapes=[pltpu.VMEM((2, 6, CHUNK, width), F32)],
        compiler_params=_params("parallel", "arbitrary"),
        name="wkv_scan",
    )(zr, zr, zr, zr, zr, zr, state_fwd, state_bwd, p["mu"], p["mu_w"], p["mu_a"], p["w0"], p["w2"], p["a0"], p["a2"],
      p["k_k"], p["k_a"], p["r_k"], p["gn_w"], p["gn_b"], p["bd"])
    return (y_f, y_b), (st_f, st_b)


def _mix_kernel(period, tm, yf_ref, yb_ref, glo_ref, zc_ref, g2_ref, cw_ref, o_ref):
    gate = _dot(jax.nn.sigmoid(glo_ref[0]), g2_ref[...])
    yr = (yf_ref[0] + yb_ref[0]) * gate
    zc = zc_ref[0]
    u, gate_b, gate_c = zc[:, 0:256], zc[:, 256:512], zc[:, 512:768]
    hc = gate_c * u
    pos = lax.broadcasted_iota(jnp.int32, (tm, 1), 0) % period
    prev = jnp.where(pos == 0, 0.0, pltpu.roll(hc, 1, 0))
    nxt = jnp.where(pos == period - 1, 0.0, pltpu.roll(hc, tm - 1, 0))
    conv = prev * cw_ref[0:1, :] + hc * cw_ref[1:2, :] + nxt * cw_ref[2:3, :]
    o_ref[0, :, 0:512] = yr.astype(BF16)
    o_ref[0, :, 512:768] = (gate_b * conv).astype(BF16)


def _mix(ys_f, ys_b, zr, zc, g2, conv_w, period, tm):
    bsz, length, _ = zc.shape
    row = lambda b, i: (b, i, 0)
    glo_block = (N_RWKV_COLS - 128) // 128
    return pl.pallas_call(
        functools.partial(_mix_kernel, period, tm),
        grid=(bsz, length // tm),
        in_specs=[pl.BlockSpec((1, tm, 512), row), pl.BlockSpec((1, tm, 512), row),
                  pl.BlockSpec((1, tm, 128), lambda b, i: (b, i, glo_block)),
                  pl.BlockSpec((1, tm, N_CONV_COLS), row),
                  pl.BlockSpec((128, 512), lambda b, i: (0, 0)),
                  pl.BlockSpec((3, 256), lambda b, i: (0, 0))],
        out_specs=pl.BlockSpec((1, tm, 768), row),
        out_shape=jax.ShapeDtypeStruct((bsz, length, 768), BF16),
        compiler_params=_params("parallel", "parallel"),
        name="gate_conv",
    )(ys_f, ys_b, zr, zc, g2, conv_w)


def _chan_dft_kernel(f_ref, w_ref, o_ref):
    xcs = jnp.dot(f_ref[0].astype(BF16), w_ref[...], preferred_element_type=F32)
    o_ref[0, 0] = xcs[:, 0:256].astype(BF16)
    o_ref[0, 1] = xcs[:, 256:512].astype(BF16)


def _pos_dft_kernel(nb, cs_ref, x_ref, o_ref, acc_ref):
    kstep = pl.program_id(1)

    @pl.when(kstep == 0)
    def _():
        acc_ref[...] = jnp.zeros_like(acc_ref)

    cs = cs_ref[...]
    for b in range(nb):
        acc_ref[b] += jnp.dot(cs, x_ref[b], preferred_element_type=F32)

    @pl.when(kstep == pl.num_programs(1) - 1)
    def _():
        o_ref[...] = acc_ref[...].astype(BF16)


def _fourier(zf, chan_w, pos_cs, tm, tk):
    bsz, length, ch = zf.shape
    tr = min(512, length)
    xcs = pl.pallas_call(
        _chan_dft_kernel,
        grid=(bsz, length // tr),
        in_specs=[pl.BlockSpec((1, tr, ch), lambda b, i: (b, i, 0)),
                  pl.BlockSpec((ch, 2 * ch), lambda b, i: (0, 0))],
        out_specs=pl.BlockSpec((1, 2, tr, ch), lambda b, i: (b, 0, i, 0)),
        out_shape=jax.ShapeDtypeStruct((bsz, 2, length, ch), BF16),
        compiler_params=_params("parallel", "parallel"),
        name="chan_dft",
    )(zf, chan_w)
    xcs = xcs.reshape(bsz, 2 * length, ch)
    return pl.pallas_call(
        functools.partial(_pos_dft_kernel, bsz),
        grid=(length // tm, 2 * length // tk),
        in_specs=[pl.BlockSpec((tm, tk), lambda i, kk: (i, kk)),
                  pl.BlockSpec((bsz, tk, ch), lambda i, kk: (0, kk, 0))],
        out_specs=pl.BlockSpec((bsz, tm, ch), lambda i, kk: (0, i, 0)),
        out_shape=jax.ShapeDtypeStruct((bsz, length, ch), BF16),
        scratch_shapes=[pltpu.VMEM((bsz, tm, ch), F32)],
        compiler_params=_params("parallel", "arbitrary"),
        name="pos_dft",
    )(pos_cs, xcs)


def _dft_tables(length):
    m = jnp.arange(length, dtype=jnp.int32)[None, :]
    l1 = jnp.arange(length // 64, dtype=jnp.int32)[:, None] * 64
    l2 = jnp.arange(64, dtype=jnp.int32)[:, None]
    ang = lambda l: ((l * m) % length).astype(F32) * (2.0 * jnp.pi / length)
    c1, s1, c2, s2 = jnp.cos(ang(l1)), jnp.sin(ang(l1)), jnp.cos(ang(l2)), jnp.sin(ang(l2))
    scale = 1.0 / jnp.sqrt(jnp.float32(length))
    cos = (c1[:, None, :] * c2[None] - s1[:, None, :] * s2[None]).reshape(length, length) * scale
    sin = (s1[:, None, :] * c2[None] + c1[:, None, :] * s2[None]).reshape(length, length) * scale
    return jnp.concatenate([cos, sin], axis=1).astype(BF16)


def _chan_dft_weights(ngroups):
    j = jnp.arange(HEAD_DIM, dtype=jnp.int32)
    ang = ((j[:, None] * j[None, :]) % HEAD_DIM).astype(F32) * (2.0 * jnp.pi / HEAD_DIM)
    eye = jnp.eye(ngroups, dtype=F32)
    scale = 1.0 / jnp.sqrt(jnp.float32(HEAD_DIM))
    cc = jnp.kron(eye, jnp.cos(ang) * scale)
    sc = jnp.kron(eye, jnp.sin(ang) * scale)
    return jnp.concatenate([cc, -sc], axis=1).astype(BF16)


def _outproj_kernel(yrc_ref, yf_ref, w_ref, x_ref, gn1_ref, gate_ref, gn2_ref, sc_ref, sh_ref, xo_ref, h_ref):
    nrc = yrc_ref.shape[2]
    o = (jnp.dot(yrc_ref[0], w_ref[0:nrc, :], preferred_element_type=F32) +
         jnp.dot(yf_ref[0], w_ref[nrc:, :], preferred_element_type=F32))
    xn = x_ref[0] + gate_ref[0] * _rms(o, gn1_ref[...])
    xo_ref[0] = xn
    h_ref[0] = (_rms(xn, gn2_ref[...]) * (1.0 + sc_ref[0]) + sh_ref[0]).astype(BF16)


def _outproj(yrc, yf, w, x, gn1, gate, gn2, sc, sh, tm):
    bsz, length, d = x.shape
    row = lambda b, i: (b, i, 0)
    vec = pl.BlockSpec((1, d), lambda b, i: (0, 0))
    mod = pl.BlockSpec((1, 1, d), lambda b, i: (b, 0, 0))
    return pl.pallas_call(
        _outproj_kernel,
        grid=(bsz, length // tm),
        in_specs=[pl.BlockSpec((1, tm, yrc.shape[2]), row), pl.BlockSpec((1, tm, yf.shape[2]), row),
                  pl.BlockSpec(w.shape, lambda b, i: (0, 0)), pl.BlockSpec((1, tm, d), row),
                  vec, mod, vec, mod, mod],
        out_specs=[pl.BlockSpec((1, tm, d), row), pl.BlockSpec((1, tm, d), row)],
        out_shape=[jax.ShapeDtypeStruct((bsz, length, d), F32), jax.ShapeDtypeStruct((bsz, length, d), BF16)],
        compiler_params=_params("parallel", "parallel"),
        name="outproj",
    )(yrc, yf, w, x, gn1, gate, gn2, sc, sh)


FF_SUB = 512


def _swiglu_partial(h, wg, wu, wd, tf):
    def up(c, n):
        return (jnp.dot(h, wg(c, n), preferred_element_type=F32), jnp.dot(h, wu(c, n), preferred_element_type=F32))

    def down(gu, c, n):
        return jnp.dot((_silu(gu[0]) * gu[1]).astype(BF16), wd(c, n), preferred_element_type=F32)

    out = None
    prev = None
    for cut in [(c, min(FF_SUB, tf - c)) for c in range(0, tf, FF_SUB)]:
        gu = up(*cut)
        if prev is not None:
            part = down(*prev)
            out = part if out is None else out + part
        prev = (gu,) + cut
    part = down(*prev)
    return part if out is None else out + part


def _ffn_kernel(h_ref, wg_ref, wu_ref, wd_ref, x_ref, gn_ref, gate_ref, o_ref, acc_ref):
    j = pl.program_id(2)

    @pl.when(j == 0)
    def _():
        acc_ref[...] = jnp.zeros_like(acc_ref)

    acc_ref[...] += _swiglu_partial(h_ref[0], lambda c, n: wg_ref[:, c:c + n], lambda c, n: wu_ref[:, c:c + n],
                                    lambda c, n: wd_ref[c:c + n, :], wg_ref.shape[1])

    @pl.when(j == pl.num_programs(2) - 1)
    def _():
        o_ref[0] = x_ref[0] + gate_ref[0] * _rms(acc_ref[...], gn_ref[...])


def _ffn(h, wg, wu, wd, x, gn, gate, tm, tf):
    bsz, length, d = x.shape
    ff = wg.shape[1]
    row = lambda b, i, j: (b, i, 0)
    return pl.pallas_call(
        _ffn_kernel,
        grid=(bsz, length // tm, ff // tf),
        in_specs=[pl.BlockSpec((1, tm, d), row),
                  pl.BlockSpec((d, tf), lambda b, i, j: (0, j)),
                  pl.BlockSpec((d, tf), lambda b, i, j: (0, j)),
                  pl.BlockSpec((tf, d), lambda b, i, j: (j, 0)),
                  pl.BlockSpec((1, tm, d), row),
                  pl.BlockSpec((1, d), lambda b, i, j: (0, 0)),
                  pl.BlockSpec((1, 1, d), lambda b, i, j: (b, 0, 0))],
        out_specs=pl.BlockSpec((1, tm, d), row),
        out_shape=jax.ShapeDtypeStruct((bsz, length, d), F32),
        scratch_shapes=[pltpu.VMEM((tm, d), F32)],
        compiler_params=_params("parallel", "parallel", "arbitrary"),
        name="ffn",
    )(h, wg, wu, wd, x, gn, gate)


MOE_TILE = 512
MOE_BLOCK = 128
MOE_ALIGN = 16
MOE_TM = 1024


def _router_kernel(x_ref, gn_ref, sc_ref, sh_ref, w_ref, b_ref, tri_ref, routes_ref, routest_ref, cnt_ref):
    h = _rms(x_ref[0], gn_ref[...]) * (1.0 + sc_ref[0]) + sh_ref[0]
    logits = jnp.dot(h, w_ref[...], preferred_element_type=F32, precision=lax.Precision.HIGHEST) + b_ref[...]
    ne = float(logits.shape[1])
    idx = lax.broadcasted_iota(jnp.int32, logits.shape, 1).astype(F32)
    m1 = jnp.max(logits, axis=1, keepdims=True)
    i1 = jnp.min(jnp.where(logits == m1, idx, ne), axis=1, keepdims=True)
    rest = jnp.where(idx == i1, -jnp.inf, logits)
    m2 = jnp.max(rest, axis=1, keepdims=True)
    i2 = jnp.min(jnp.where(rest == m2, idx, ne), axis=1, keepdims=True)
    e2 = jnp.exp(m2 - m1)
    p1 = 1.0 / (1.0 + e2)
    p2 = e2 / (1.0 + e2)
    sel = jnp.where(idx == i1, 1.0, jnp.where(idx == i2, 1.0, 0.0))
    before = jnp.dot(tri_ref[...], sel.astype(BF16), preferred_element_type=F32)
    rank1 = jnp.sum(jnp.where(idx == i1, before, 0.0), axis=1, keepdims=True)
    rank2 = jnp.sum(jnp.where(idx == i2, before, 0.0), axis=1, keepdims=True)
    routes = jnp.zeros_like(logits)
    for lane, val in enumerate((i1, i2, rank1, rank2, p1, p2)):
        routes = jnp.where(idx == float(lane), val, routes)
    routes_ref[0] = routes
    routest_ref[0] = jnp.transpose(routes)[0:N_EXPERTS, :]
    cnt_ref[0, 0] = jnp.sum(sel, axis=0, keepdims=True)


def _router(x, gn, sc, sh, w, b):
    bsz, length, d = x.shape
    tm = MOE_TILE
    pad = LANES - w.shape[1]
    w = jnp.pad(w, ((0, 0), (0, pad)))
    b = jnp.pad(b, ((0, 0), (0, pad)), constant_values=-1e30)
    ne = LANES
    t = jnp.arange(tm, dtype=jnp.int32)
    tri = (t[None, :] < t[:, None]).astype(BF16)
    row = lambda bb, i: (bb, i, 0)
    vec = pl.BlockSpec((1, d), lambda bb, i: (0, 0))
    mod = pl.BlockSpec((1, 1, d), lambda bb, i: (bb, 0, 0))
    nt = length // tm
    return pl.pallas_call(
        _router_kernel,
        grid=(bsz, nt),
        in_specs=[pl.BlockSpec((1, tm, d), row), vec, mod, mod,
                  pl.BlockSpec((d, ne), lambda bb, i: (0, 0)), pl.BlockSpec((1, ne), lambda bb, i: (0, 0)),
                  pl.BlockSpec((tm, tm), lambda bb, i: (0, 0))],
        out_specs=[pl.BlockSpec((1, tm, ne), row),
                   pl.BlockSpec((1, N_EXPERTS, tm), lambda bb, i: (bb, 0, i)),
                   pl.BlockSpec((1, 1, 1, ne), lambda bb, i: (bb, i, 0, 0))],
        out_shape=[jax.ShapeDtypeStruct((bsz, length, ne), F32),
                   jax.ShapeDtypeStruct((bsz, N_EXPERTS, length), F32),
                   jax.ShapeDtypeStruct((bsz, nt, 1, ne), F32)],
        compiler_params=_params("parallel", "parallel"),
        name="router",
    )(x, gn, sc, sh, w, b, tri)


_MOE_SHIFT = MOE_BLOCK.bit_length() - 1
_MOE_BUF_BLOCKS = (2 * MOE_TILE + N_EXPERTS * (MOE_ALIGN - 1 + MOE_BLOCK - 1)) // MOE_BLOCK + 1


def _align_rows(c):
    return ((c + (MOE_ALIGN - 1)) >> (MOE_ALIGN.bit_length() - 1)) << (MOE_ALIGN.bit_length() - 1)


def _blocks(c):
    return (c + (MOE_BLOCK - 1)) >> _MOE_SHIFT


def _dispatch_copies(start, tile, off_ref, cnt_ref, buf_ref, hs_ref, sem):
    def run(src_row, dst_row, nrows):
        cp = pltpu.make_async_copy(buf_ref.at[pl.ds(pl.multiple_of(src_row, MOE_ALIGN), nrows)],
                                   hs_ref.at[pl.ds(pl.multiple_of(dst_row, MOE_ALIGN), nrows)], sem)
        if start:
            cp.start()
        else:
            cp.wait()

    slot = jnp.int32(0)
    for e in range(N_EXPERTS):
        rows = _align_rows(cnt_ref[tile * N_EXPERTS + e])
        dst0 = off_ref[tile * N_EXPERTS + e]
        src0 = slot * MOE_BLOCK
        nfull = rows >> _MOE_SHIFT

        def full(rb, carry, src0=src0, dst0=dst0):
            run(src0 + rb * MOE_BLOCK, dst0 + rb * MOE_BLOCK, MOE_BLOCK)
            return carry

        lax.fori_loop(0, nfull, full, 0)
        done = nfull * MOE_BLOCK
        piece = MOE_BLOCK // 2
        while piece >= MOE_ALIGN:
            has = (rows & piece) != 0

            @pl.when(has)
            def _(done=done, piece=piece, src0=src0, dst0=dst0):
                run(src0 + done, dst0 + done, piece)

            done = done + jnp.where(has, piece, 0)
            piece //= 2
        slot = slot + _blocks(rows)


def _staging_rows(tile, cnt_ref, i1, i2, rank1, rank2):
    row1 = jnp.zeros_like(i1)
    row2 = jnp.zeros_like(i2)
    slot = jnp.int32(0)
    for e in range(N_EXPERTS):
        first_row = (slot * MOE_BLOCK).astype(F32)
        row1 = jnp.where(i1 == float(e), first_row, row1)
        row2 = jnp.where(i2 == float(e), first_row, row2)
        slot = slot + _blocks(cnt_ref[tile * N_EXPERTS + e])
    return row1 + rank1, row2 + rank2


def _dispatch_kernel(off_ref, cnt_ref, h_ref, routest_ref, hs_init_ref, hs_ref, buf_ref, sem):
    del hs_init_ref
    tile = pl.program_id(0)
    rt = routest_ref[0]
    row1, row2 = _staging_rows(tile, cnt_ref, rt[0:1], rt[1:2], rt[2:3], rt[3:4])
    rows = lax.broadcasted_iota(jnp.int32, (MOE_BLOCK, 1), 0)
    onehot = jnp.concatenate(
        [jnp.where(row1 == (rows + blk * MOE_BLOCK).astype(F32), 1.0,
                   jnp.where(row2 == (rows + blk * MOE_BLOCK).astype(F32), 1.0, 0.0)).astype(BF16)
         for blk in range(_MOE_BUF_BLOCKS)], axis=0)
    buf_ref[...] = jnp.dot(onehot, h_ref[...], preferred_element_type=F32).astype(BF16)
    _dispatch_copies(True, tile, off_ref, cnt_ref, buf_ref, hs_ref, sem)
    _dispatch_copies(False, tile, off_ref, cnt_ref, buf_ref, hs_ref, sem)


def _dispatch(h, rankt, off, cnt, nrows):
    n, d = h.shape
    ntb = rankt.shape[2] // MOE_TILE
    grid_spec = pltpu.PrefetchScalarGridSpec(
        num_scalar_prefetch=2, grid=(n // MOE_TILE,),
        in_specs=[pl.BlockSpec((MOE_TILE, d), lambda t, o, c: (t, 0)),
                  pl.BlockSpec((1, N_EXPERTS, MOE_TILE), lambda t, o, c: (t // ntb, 0, t % ntb)),
                  pl.BlockSpec(memory_space=pl.ANY)],
        out_specs=pl.BlockSpec(memory_space=pl.ANY),
        scratch_shapes=[pltpu.VMEM((_MOE_BUF_BLOCKS * MOE_BLOCK, d), BF16), pltpu.SemaphoreType.DMA(())])
    return pl.pallas_call(
        _dispatch_kernel, grid_spec=grid_spec,
        out_shape=jax.ShapeDtypeStruct((nrows, d), BF16),
        input_output_aliases={4: 0},
        compiler_params=_params("arbitrary"),
        name="moe_dispatch",
    )(off, cnt, h, rankt, jnp.zeros((nrows, d), BF16))


def _gffn_kernel(te_ref, tv_ref, h_ref, wg_ref, wu_ref, wd_ref, o_ref, acc_ref):
    q = pl.program_id(0)
    j = pl.program_id(1)

    @pl.when(j == 0)
    def _():
        acc_ref[...] = jnp.zeros_like(acc_ref)

    @pl.when(tv_ref[q] > 0)
    def _():
        acc_ref[...] += _swiglu_partial(h_ref[...], lambda c, n: wg_ref[0, :, c:c + n],
                                        lambda c, n: wu_ref[0, :, c:c + n], lambda c, n: wd_ref[0, c:c + n, :],
                                        wg_ref.shape[2])

    @pl.when(j == pl.num_programs(1) - 1)
    def _():
        o_ref[...] = acc_ref[...].astype(BF16)


def _gffn(hs, te, tv, wg, wu, wd, tf):
    nrows, d = hs.shape
    ff = wg.shape[2]
    grid_spec = pltpu.PrefetchScalarGridSpec(
        num_scalar_prefetch=2, grid=(nrows // MOE_TM, ff // tf),
        in_specs=[pl.BlockSpec((MOE_TM, d), lambda q, j, te, tv: (q, 0)),
                  pl.BlockSpec((1, d, tf), lambda q, j, te, tv: (te[q], 0, j)),
                  pl.BlockSpec((1, d, tf), lambda q, j, te, tv: (te[q], 0, j)),
                  pl.BlockSpec((1, tf, d), lambda q, j, te, tv: (te[q], j, 0))],
        out_specs=pl.BlockSpec((MOE_TM, d), lambda q, j, te, tv: (q, 0)),
        scratch_shapes=[pltpu.VMEM((MOE_TM, d), F32)])
    return pl.pallas_call(
        _gffn_kernel, grid_spec=grid_spec,
        out_shape=jax.ShapeDtypeStruct((nrows, d), BF16),
        compiler_params=_params("arbitrary", "arbitrary"),
        name="moe_experts",
    )(te, tv, hs, wg, wu, wd)


def _combine_copies(start, tile, off_ref, cnt_ref, ys_ref, buf_ref, sem):
    slot = jnp.int32(0)
    for e in range(N_EXPERTS):
        nb = _blocks(cnt_ref[tile * N_EXPERTS + e])
        src0 = off_ref[tile * N_EXPERTS + e]

        def body(rb, carry, slot=slot, src0=src0):
            cp = pltpu.make_async_copy(
                ys_ref.at[pl.ds(pl.multiple_of(src0 + rb * MOE_BLOCK, MOE_ALIGN), MOE_BLOCK)],
                buf_ref.at[pl.ds(pl.multiple_of((slot + rb) * MOE_BLOCK, MOE_BLOCK), MOE_BLOCK)], sem)
            if start:
                cp.start()
            else:
                cp.wait()
            return carry

        lax.fori_loop(0, nb, body, 0)
        slot = slot + nb


def _combine_kernel(off_ref, cnt_ref, ys_ref, routes_ref, x_ref, gn_ref, gate_ref, o_ref, buf_ref, sem):
    tile = pl.program_id(0)

    @pl.when(tile == 0)
    def _():
        buf_ref[...] = jnp.zeros_like(buf_ref)

    _combine_copies(True, tile, off_ref, cnt_ref, ys_ref, buf_ref, sem)
    routes = routes_ref[0]
    col = lambda j: routes[:, j:j + 1]
    row1, row2 = _staging_rows(tile, cnt_ref, col(0), col(1), col(2), col(3))
    lanes = lax.broadcasted_iota(jnp.int32, (1, MOE_BLOCK), 1)
    scatter = jnp.concatenate(
        [(jnp.where(row1 == (lanes + blk * MOE_BLOCK).astype(F32), col(4), 0.0) +
          jnp.where(row2 == (lanes + blk * MOE_BLOCK).astype(F32), col(5), 0.0)).astype(BF16)
         for blk in range(_MOE_BUF_BLOCKS)], axis=1)
    _combine_copies(False, tile, off_ref, cnt_ref, ys_ref, buf_ref, sem)
    mixed = jnp.dot(scatter, buf_ref[...], preferred_element_type=F32)
    o_ref[0] = x_ref[0] + gate_ref[0] * _rms(mixed, gn_ref[...])


def _combine(ys, routes, x, gn, gate, off, cnt):
    bsz, length, d = x.shape
    ntb = length // MOE_TILE
    tok = lambda t, o, c: (t // ntb, t % ntb, 0)
    grid_spec = pltpu.PrefetchScalarGridSpec(
        num_scalar_prefetch=2, grid=(bsz * ntb,),
        in_specs=[pl.BlockSpec(memory_space=pl.ANY),
                  pl.BlockSpec((1, MOE_TILE, LANES), tok),
                  pl.BlockSpec((1, MOE_TILE, d), tok),
                  pl.BlockSpec((1, d), lambda t, o, c: (0, 0)),
                  pl.BlockSpec((1, 1, d), lambda t, o, c: (t // ntb, 0, 0))],
        out_specs=pl.BlockSpec((1, MOE_TILE, d), tok),
        scratch_shapes=[pltpu.VMEM((_MOE_BUF_BLOCKS * MOE_BLOCK, d), BF16), pltpu.SemaphoreType.DMA(())])
    return pl.pallas_call(
        _combine_kernel, grid_spec=grid_spec,
        out_shape=jax.ShapeDtypeStruct((bsz, length, d), F32),
        compiler_params=_params("arbitrary"),
        name="moe_combine",
    )(off, cnt, ys, routes, x, gn, gate)


def _route_plan(counts, ntokens):
    ntiles = counts.shape[0]
    rows = (counts + (MOE_ALIGN - 1)) // MOE_ALIGN * MOE_ALIGN
    seg = (rows.sum(0) + MOE_BLOCK + MOE_TM - 1) // MOE_TM * MOE_TM
    seg_end = jnp.cumsum(seg)
    off = (seg_end - seg)[None, :] + jnp.cumsum(rows, axis=0) - rows
    nq = -(-(2 * ntokens + ntiles * N_EXPERTS * (MOE_ALIGN - 1) + N_EXPERTS * (MOE_BLOCK + MOE_TM - 1)) // MOE_TM)
    q = jnp.arange(nq, dtype=jnp.int32) * MOE_TM
    te = jnp.minimum(jnp.sum(q[:, None] >= seg_end[None, :], axis=1), N_EXPERTS - 1).astype(jnp.int32)
    tv = (q < seg_end[-1]).astype(jnp.int32)
    return off.reshape(-1).astype(jnp.int32), te, tv, nq * MOE_TM


def _moe(h, x, gn_pre, sc, sh, router_w, router_b, wg, wu, wd, gn_post, gate, tf):
    bsz, length, d = x.shape
    routes, routest, cnt = _router(x, gn_pre, sc, sh, router_w, router_b)
    counts = cnt[:, :, 0, :N_EXPERTS].astype(jnp.int32).reshape(-1, N_EXPERTS)
    off, te, tv, nrows = _route_plan(counts, bsz * length)
    cnt_flat = counts.reshape(-1)
    hs = _dispatch(h.reshape(bsz * length, d), routest, off, cnt_flat, nrows)
    ys = _gffn(hs, te, tv, wg, wu, wd, tf)
    return _combine(ys, routes, x, gn_post, gate, off, cnt_flat)


def _sincos_2d(rows, cols, dim):
    quarter = dim // 4
    omega = 1.0 / (POS_BASE ** (jnp.arange(quarter, dtype=F32) / quarter))

    def axis_emb(n):
        ang = jnp.arange(n, dtype=F32)[:, None] * omega[None, :]
        return jnp.concatenate([jnp.sin(ang), jnp.cos(ang)], axis=-1)

    er, ec = axis_emb(rows), axis_emb(cols)
    emb = jnp.concatenate([jnp.broadcast_to(er[:, None, :], (rows, cols, dim // 2)),
                           jnp.broadcast_to(ec[None, :, :], (rows, cols, dim // 2))], axis=-1)
    return emb.reshape(rows * cols, dim)


def _head_block_ones(width):
    h = jnp.arange(width, dtype=jnp.int32) // HEAD_DIM
    return (h[:, None] == h[None, :]).astype(BF16)


def _pad_lora(w2, d):
    z = jnp.zeros_like(w2[0])
    return jnp.concatenate([w2[0], z], axis=0) if d == 0 else jnp.concatenate([z, w2[1]], axis=0)


def _tile(length, pref):
    return pref if length % pref == 0 else length


def kernel(x, c, ctx, c_ctx, ada_w, ada_b, norm_g, w_in, w_out, rwkv_mu, rwkv_mu_w, rwkv_mu_a, rwkv_w0, rwkv_w2,
           rwkv_a0, rwkv_a2, rwkv_g2, rwkv_k_k, rwkv_k_a, rwkv_r_k, rwkv_gn_w, rwkv_gn_b, conv_w, ffn_w_gate,
           ffn_w_up, ffn_w_down, router_w, router_b, moe_w_gate, moe_w_up, moe_w_down):
    bsz, length, dim = x.shape
    ctx_len = ctx.shape[1]
    depth = ada_w.shape[0]
    d_rwkv = rwkv_w0.shape[-1]

    pos = _sincos_2d(length // GRID_W, GRID_W, dim).astype(x.dtype)
    bd = _head_block_ones(LANES)
    chan_w = _chan_dft_weights(N_FOUR_COLS // HEAD_DIM)
    dft_lat = _dft_tables(length)
    dft_ctx = _dft_tables(ctx_len)

    cvec = jnp.zeros((16, dim), F32).at[:bsz].set(c).at[bsz].set(c_ctx)
    mod = _adaln(cvec, ada_w, ada_b)
    zero_state = jnp.zeros((bsz, d_rwkv // LANES, LANES, LANES), F32)
    xc = ctx

    def mods(l, lo, hi):
        return [mod[l, lo:hi, i * dim:(i + 1) * dim][:, None, :] for i in range(6)]

    def mixers(l, zr, zc, zf, states0, period, dft, need_y=True):
        tm = _tile(zr.shape[1], 512)
        p = {"mu": rwkv_mu[l], "mu_w": rwkv_mu_w[l].reshape(1, -1), "mu_a": rwkv_mu_a[l].reshape(1, -1),
             "w0": rwkv_w0[l], "w2": jnp.stack([_pad_lora(rwkv_w2[l], d) for d in range(2)]), "a0": rwkv_a0[l],
             "a2": jnp.stack([_pad_lora(rwkv_a2[l], d) for d in range(2)]), "k_k": rwkv_k_k[l][None],
             "k_a": rwkv_k_a[l][None], "r_k": rwkv_r_k[l].reshape(1, -1), "gn_w": rwkv_gn_w[l][None],
             "gn_b": rwkv_gn_b[l][None], "bd": bd}
        ys, states = _scan(zr, states0[0], states0[1], p)
        if not need_y:
            return None, None, states
        yrc = _mix(ys[0], ys[1], zr, zc, rwkv_g2[l], conv_w[l], period, tm)
        yf = _fourier(zf, chan_w, dft, _tile(zf.shape[1], 512), 512)
        return yrc, yf, states

    def channel_mixer(l, h2, xn, gn3, gate2, sc2, sh2, gn2):
        tm = _tile(xn.shape[1], 512)
        i = l // 2
        if l % 2 == 0:
            return _ffn(h2, ffn_w_gate[i].astype(BF16), ffn_w_up[i].astype(BF16), ffn_w_down[i].astype(BF16),
                        xn, gn3, gate2, tm, 1408)
        return _moe(h2, xn, gn2, sc2, sh2, router_w[i], router_b[i][None], moe_w_gate[i].astype(BF16),
                    moe_w_up[i].astype(BF16), moe_w_down[i].astype(BF16), gn3, gate2, 896)

    for l in range(depth):
        last = l == depth - 1
        w_in_l = w_in[l].astype(BF16)
        w_out_l = w_out[l].astype(BF16)
        gn = [norm_g[l, i][None] for i in range(4)]
        sh1, sc1, g1, sh2, sc2, g2 = mods(l, 0, bsz)
        csh1, csc1, cg1, csh2, csc2, cg2 = mods(l, bsz, bsz + 1)

        xc_flat = xc.reshape(1, bsz * ctx_len, dim)
        tmc = _tile(bsz * ctx_len, 512)
        zr, zc, zf = _inproj(xc_flat, None, gn[0], csc1, csh1, w_in_l, tmc)
        unflat = lambda t: t.reshape(bsz, ctx_len, t.shape[-1])
        yrc, yf, ctx_states = mixers(l, unflat(zr), unflat(zc), unflat(zf), (zero_state, zero_state), ctx_len,
                                     dft_ctx, need_y=not last)
        if not last:
            flat = lambda t: t.reshape(1, bsz * ctx_len, t.shape[-1])
            xcn, hc2 = _outproj(flat(yrc), flat(yf), w_out_l, xc_flat, gn[1], cg1, gn[2], csc2, csh2, tmc)
            xc = channel_mixer(l, hc2, xcn, gn[3], cg2, csc2, csh2, gn[2]).reshape(bsz, ctx_len, dim)

        tm = _tile(length, 512)
        if l == 0:
            zr, zc, zf, x = _inproj(x, pos, gn[0], sc1, sh1, w_in_l, tm)
        else:
            zr, zc, zf = _inproj(x, None, gn[0], sc1, sh1, w_in_l, tm)
        yrc, yf, _ = mixers(l, zr, zc, zf, ctx_states, GRID_W, dft_lat)
        xn, h2 = _outproj(yrc, yf, w_out_l, x, gn[1], g1, gn[2], sc2, sh2, _tile(length, 512))
        x = channel_mixer(l, h2, xn, gn[3], g2, sc2, sh2, gn[2])
    return x
```

```python
import functools

import jax
import jax.numpy as jnp
from jax import lax
from jax.experimental import pallas as pl
from jax.experimental.pallas import tpu as pltpu

F32 = jnp.float32
BF16 = jnp.bfloat16

HEAD_DIM = 64
GRID_W = 64
NORM_EPS = 1e-6
GN_EPS = 64e-5
POS_BASE = 10000.0
N_EXPERTS = 8
CHUNK = 64
LANES = 128
VMEM_LIMIT = 56 * 1024 * 1024


def _params(*sem):
    return pltpu.CompilerParams(dimension_semantics=sem, vmem_limit_bytes=VMEM_LIMIT)


def _dot(a, b):
    return jnp.dot(a.astype(BF16), b.astype(BF16), preferred_element_type=F32)


def _dot_nt(a, b):
    return lax.dot_general(a.astype(BF16), b.astype(BF16), (((1,), (1,)), ((), ())),
                           preferred_element_type=F32)


def _split(x):
    hi = x.astype(BF16)
    lo = (x - hi.astype(F32)).astype(BF16)
    return hi, lo


def _head_sums(x, ones_pair):
    rows = x.shape[0]
    tiles = x.shape[1] // LANES
    hi, lo = _split(x)
    stacked = jnp.concatenate([t[:, g * LANES:(g + 1) * LANES] for t in (hi, lo) for g in range(tiles)], axis=0)
    s = jnp.dot(stacked, ones_pair, preferred_element_type=F32)
    return jnp.concatenate([s[g * rows:(g + 1) * rows] + s[(tiles + g) * rows:(tiles + g + 1) * rows]
                            for g in range(tiles)], axis=1)


def _dot_split_rhs(w, x):
    hi, lo = _split(x)
    return (jnp.dot(w, hi, preferred_element_type=F32) + jnp.dot(w, lo, preferred_element_type=F32))


def _rms(x, g):
    ms = jnp.mean(x * x, axis=-1, keepdims=True)
    return x * lax.rsqrt(ms + NORM_EPS) * g


def _silu(x):
    return x * jax.nn.sigmoid(x)


def _softplus(x):
    return jnp.maximum(x, 0.0) + jnp.log(1.0 + jnp.exp(-jnp.abs(x)))


def _adaln_kernel(c_ref, w_ref, b_ref, o_ref):
    s = _silu(c_ref[...])
    o_ref[0] = _dot(s, w_ref[0]) + b_ref[0]


def _adaln(cvec, ada_w, ada_b):
    nl, d, n = ada_w.shape
    rows = cvec.shape[0]
    tn = 1536
    return pl.pallas_call(
        _adaln_kernel,
        grid=(nl, n // tn),
        in_specs=[pl.BlockSpec((rows, d), lambda l, j: (0, 0)),
                  pl.BlockSpec((1, d, tn), lambda l, j: (l, 0, j)),
                  pl.BlockSpec((1, 1, tn), lambda l, j: (l, 0, j))],
        out_specs=pl.BlockSpec((1, rows, tn), lambda l, j: (l, 0, j)),
        out_shape=jax.ShapeDtypeStruct((nl, rows, n), F32),
        compiler_params=_params("arbitrary", "arbitrary"),
        name="adaln",
    )(cvec, ada_w, ada_b.reshape(nl, 1, n))


N_RWKV_COLS = 1920
N_CONV_COLS = 768
N_FOUR_COLS = 256


def _inproj_kernel(add_pos, *refs):
    if add_pos:
        x_ref, pos_ref, g_ref, sc_ref, sh_ref, w_ref, zr_ref, zc_ref, zf_ref, xp_ref = refs
        x = x_ref[0] + pos_ref[...]
        xp_ref[0] = x
    else:
        x_ref, g_ref, sc_ref, sh_ref, w_ref, zr_ref, zc_ref, zf_ref = refs
        x = x_ref[0]
    h = (_rms(x, g_ref[...]) * (1.0 + sc_ref[0]) + sh_ref[0]).astype(BF16)
    a, b = N_RWKV_COLS, N_RWKV_COLS + N_CONV_COLS
    zr_ref[0] = jnp.dot(h, w_ref[:, :a], preferred_element_type=F32)
    zc_ref[0] = jnp.dot(h, w_ref[:, a:b], preferred_element_type=F32).astype(BF16)
    zf_ref[0] = jnp.dot(h, w_ref[:, b:], preferred_element_type=F32).astype(BF16)


def _inproj(x, pos, g, sc, sh, w, tm):
    bsz, length, d = x.shape
    n = w.shape[1]
    add_pos = pos is not None
    row = lambda b, i: (b, i, 0)
    in_specs = [pl.BlockSpec((1, tm, d), row)]
    args = [x]
    if add_pos:
        in_specs.append(pl.BlockSpec((tm, d), lambda b, i: (i, 0)))
        args.append(pos)
    in_specs += [pl.BlockSpec((1, d), lambda b, i: (0, 0)),
                 pl.BlockSpec((1, 1, d), lambda b, i: (b, 0, 0)),
                 pl.BlockSpec((1, 1, d), lambda b, i: (b, 0, 0)),
                 pl.BlockSpec((d, n), lambda b, i: (0, 0))]
    args += [g, sc, sh, w]
    out_shape = [jax.ShapeDtypeStruct((bsz, length, N_RWKV_COLS), F32),
                 jax.ShapeDtypeStruct((bsz, length, N_CONV_COLS), BF16),
                 jax.ShapeDtypeStruct((bsz, length, N_FOUR_COLS), BF16)]
    out_specs = [pl.BlockSpec((1, tm, N_RWKV_COLS), row),
                 pl.BlockSpec((1, tm, N_CONV_COLS), row),
                 pl.BlockSpec((1, tm, N_FOUR_COLS), row)]
    if add_pos:
        out_shape.append(jax.ShapeDtypeStruct((bsz, length, d), F32))
        out_specs.append(pl.BlockSpec((1, tm, d), row))
    return pl.pallas_call(
        functools.partial(_inproj_kernel, add_pos),
        grid=(bsz, length // tm),
        in_specs=in_specs, out_specs=out_specs, out_shape=out_shape,
        compiler_params=_params("parallel", "parallel"),
        name="inproj",
    )(*args)


HALO = 8


def _rwkv_features(z, edge_row, d, mu_ref, muw_ref, mua_ref, w0_ref, w2_ref, a0_ref, a2_ref, kk_ref, ka_ref, bd):
    n = z.shape[0]
    rows = lax.broadcasted_iota(jnp.int32, (n, 1), 0)
    if d:
        zs = jnp.where(rows == n - 1, edge_row, pltpu.roll(z, n - 1, 0))
    else:
        zs = jnp.where(rows == 0, edge_row, pltpu.roll(z, 1, 0))
    dz = zs - z
    k = z[:, 0:512] + dz[:, 0:512] * mu_ref[d, 0:1, :]
    v = z[:, 512:1024] + dz[:, 512:1024] * mu_ref[d, 1:2, :]
    r = z[:, 1280:1792] + dz[:, 1280:1792] * mu_ref[d, 2:3, :]
    wl = z[:, 1024:1152] + dz[:, 1024:1152] * muw_ref[...]
    al = z[:, 1152:1280] + dz[:, 1152:1280] * mua_ref[...]
    w_log = -_softplus(-(w0_ref[d:d + 1, :] + _dot(jnp.tanh(wl), w2_ref[d]))) - 0.5
    lw = -jnp.exp(w_log)
    a = jax.nn.sigmoid(a0_ref[d:d + 1, :] + _dot(al, a2_ref[d]))
    kk = k * kk_ref[...]
    ss = _head_sums(kk * kk, bd)
    kk = kk / jnp.maximum(jnp.sqrt(ss), 1e-12)
    return r, lw, k * (1.0 + (a - 1.0) * ka_ref[...]), v, -kk, kk * a


SOLVE_BLOCK = 8


def _unit_lower_solve(a_list, x_list, diag_blk, eye, side_work=()):
    side_work = list(side_work)

    def boundary():
        if side_work:
            side_work.pop(0)()

    n = eye.shape[0]
    a_d = [jnp.where(diag_blk, a, 0.0) for a in a_list]
    a_o = [a - d for a, d in zip(a_list, a_d)]
    dinv = [eye + d for d in a_d]
    apow = a_d
    for _ in range(SOLVE_BLOCK.bit_length() - 2):
        apow = [_dot(p, p) for p in apow]
        boundary()
        dinv = [d + _dot(p, d) for p, d in zip(apow, dinv)]
        boundary()
    b = [_dot(d, o) for d, o in zip(dinv, a_o)]
    boundary()
    x = [_dot(d, x) for d, x in zip(dinv, x_list)]
    levels = (n // SOLVE_BLOCK).bit_length() - 1
    for level in range(levels):
        boundary()
        if level:
            b = [_dot(m, m) for m in b]
        x = [v + _dot(m, v) for m, v in zip(b, x)]
    while side_work:
        boundary()
    return x


def _scan_kernel(*refs):
    z_refs = (refs[0], refs[2])
    halo_refs = (refs[1], refs[3])
    zfirst_refs = refs[4:6]
    s0_refs = refs[6:8]
    feat_refs = refs[8:17]
    rk_ref, gw_ref, gb_ref, bd_ref = refs[17:21]
    y_refs = refs[21:23]
    st_refs = refs[23:25]
    feat_ref = refs[25]
    c = pl.program_id(1)

    @pl.when(c == 0)
    def _():
        for st_ref, s0_ref in zip(st_refs, s0_refs):
            st_ref[...] = s0_ref[...]
        no_token = jnp.zeros((1, zfirst_refs[0].shape[2]), F32)
        for d in range(2):
            for j, f in enumerate(_rwkv_features(zfirst_refs[d][0], no_token, d, *feat_refs, bd_ref[...])):
                feat_ref[d, j] = f

    n = CHUNK
    t_i = lax.broadcasted_iota(jnp.int32, (n, n), 0)
    s_i = lax.broadcasted_iota(jnp.int32, (n, n), 1)
    blk_shift = SOLVE_BLOCK.bit_length() - 1
    diag_blk = (s_i >> blk_shift) == (t_i >> blk_shift)
    eye = jnp.where(s_i == t_i, 1.0, 0.0)
    lane2 = lax.broadcasted_iota(jnp.int32, (n, LANES), 1)
    t_2 = lax.broadcasted_iota(jnp.int32, (n, LANES), 0)
    s_2 = lane2 & (n - 1)
    first = lane2 < HEAD_DIM
    r_bd = lax.broadcasted_iota(jnp.int32, (LANES, LANES), 0) < HEAD_DIM
    c_bd = lax.broadcasted_iota(jnp.int32, (LANES, LANES), 1) < HEAD_DIM
    blockdiag = r_bd == c_bd
    incl2 = (s_2 <= t_2, s_2 >= t_2)
    strict2 = (s_2 < t_2, s_2 > t_2)

    npairs = st_refs[0].shape[1]
    sls = [slice(g * LANES, (g + 1) * LANES) for g in range(npairs)]
    items = [(d, g) for d in range(2) for g in range(npairs)]
    heads = [(i, j) for i in range(len(items)) for j in range(2)]
    feats = [[feat_ref[d, j] for j in range(6)] for d in range(2)]
    raw, scaled, w_tot = [], [], []
    for d in range(2):
        r, lw, k, v, a, b = feats[d]
        incl = (s_i >= t_i) if d else (s_i <= t_i)
        cum = _dot_split_rhs(jnp.where(incl, 1.0, 0.0).astype(BF16), lw)
        tot = cum[0:1, :] if d else cum[n - 1:n, :]
        w_inv = jnp.exp(-cum)
        w_rem = jnp.exp(tot - cum)
        raw.append((r, k, v))
        scaled.append((r * jnp.exp(cum), a * jnp.exp(cum - lw), b * w_inv, k * w_inv, b * w_rem, k * w_rem))
        w_tot.append(jnp.exp(tot))
    states = [st_refs[d][0, g] for d, g in items]

    v_g = [raw[d][2][:, sls[g]] for d, g in items]
    at_h = [(jnp.where(first, scaled[d][1][:, sls[g]], 0.0), jnp.where(first, 0.0, scaled[d][1][:, sls[g]]))
            for d, g in items]
    rt_h = [(jnp.where(first, scaled[d][0][:, sls[g]], 0.0), jnp.where(first, 0.0, scaled[d][0][:, sls[g]]))
            for d, g in items]
    s = [_dot_nt(jnp.concatenate([at_h[i][0], rt_h[i][0], at_h[i][1], rt_h[i][1]], axis=0),
                 jnp.concatenate([scaled[d][2][:, sls[g]], scaled[d][3][:, sls[g]]], axis=0))
         for i, (d, g) in enumerate(items)]
    sa = [(jnp.where(strict2[d], s[i][0:n], 0.0), jnp.where(strict2[d], s[i][2 * n:3 * n], 0.0))
          for i, (d, g) in enumerate(items)]
    sr = [(jnp.where(incl2[d], s[i][n:2 * n], 0.0), jnp.where(incl2[d], s[i][3 * n:4 * n], 0.0))
          for i, (d, g) in enumerate(items)]
    vv = [jnp.concatenate([x, x], axis=0) for x in (pltpu.roll(x, HEAD_DIM, 1) for x in v_g)]
    akv = [_dot(jnp.where(first, 0.0, sa[i][j]), vv[i]) for i, j in heads]
    rhs = [jnp.where(first, at_h[i][0], akv[2 * i]) if j == 0 else jnp.where(first, akv[2 * i + 1], at_h[i][1])
           for i, j in heads]
    edge_rows = (halo_refs[0][0, HALO - 1:HALO, :], halo_refs[1][0, 0:1, :])

    def next_features(d):
        def work():
            for j, f in enumerate(_rwkv_features(z_refs[d][0], edge_rows[d], d, *feat_refs, bd_ref[...])):
                feat_ref[d, j] = f
        return work

    pq = _unit_lower_solve([sa[i][j][:, 0:n] for i, j in heads], rhs, diag_blk, eye,
                           side_work=[next_features(0), lambda: None, lambda: None, next_features(1)])
    idx = range(len(items))
    p_pair = [jnp.concatenate([jnp.where(first, pq[2 * i], 0.0), jnp.where(first, 0.0, pq[2 * i + 1])], axis=0)
              for i in idx]
    q_pair = [pltpu.roll(jnp.where(first, pq[2 * i + 1], pq[2 * i]), HEAD_DIM, 1) for i in idx]
    pt = [_dot_split_rhs(p_pair[i].astype(BF16), states[i]) for i in idx]
    rtt = [_dot_split_rhs(jnp.concatenate(rt_h[i], axis=0).astype(BF16), states[i]) for i in idx]
    u = [jnp.where(first, pt[i][0:n], pt[i][n:]) + q_pair[i] for i in idx]
    uv = [jnp.concatenate([u[i], v_g[i]], axis=0) for i in idx]
    yh = [_dot(sr[i][j], uv[i]) for i, j in heads]
    ys = [jnp.where(first, rtt[i][0:n] + yh[2 * i], rtt[i][n:] + yh[2 * i + 1]) for i in idx]
    t_new = [_dot(jnp.concatenate([scaled[d][4][:, sls[g]], scaled[d][5][:, sls[g]]], axis=0).T, uv[i])
             for i, (d, g) in enumerate(items)]
    for i, (d, g) in enumerate(items):
        w_col = jnp.broadcast_to(w_tot[d][:, sls[g]], (LANES, LANES)).T
        st_refs[d][0, g] = jnp.where(blockdiag, w_col * states[i] + t_new[i], 0.0)
    bd = bd_ref[...]
    inv = 1.0 / HEAD_DIM
    y = [jnp.concatenate(ys[d * npairs:(d + 1) * npairs], axis=1) for d in range(2)]
    mean = [_head_sums(y[d], bd) * inv for d in range(2)]
    dev = [y[d] - mean[d] for d in range(2)]
    var = [_head_sums(dev[d] * dev[d], bd) * inv for d in range(2)]
    bonus = [_head_sums(raw[d][0] * raw[d][1] * rk_ref[...], bd) * raw[d][2] for d in range(2)]
    for d in range(2):
        y_refs[d][0] = (dev[d] * lax.rsqrt(var[d] + GN_EPS) * gw_ref[...] + gb_ref[...] + bonus[d]).astype(BF16)


def _scan(zr, state_fwd, state_bwd, p):
    bsz, length, ncol = zr.shape
    width = p["w0"].shape[1]
    nc = length // CHUNK
    npairs = width // LANES
    per_chunk = CHUNK // HALO
    last_halo = length // HALO - 1
    fmap = lambda b, c: (b, c, 0)
    bmap = lambda b, c: (b, nc - 1 - c, 0)
    fnext = lambda c: jnp.minimum(c + 1, nc - 1)
    bnext = lambda c: jnp.maximum(nc - 2 - c, 0)
    fnext_map = lambda b, c: (b, fnext(c), 0)
    bnext_map = lambda b, c: (b, bnext(c), 0)
    prev_map = lambda b, c: (b, jnp.maximum(fnext(c) * per_chunk - 1, 0), 0)
    next_map = lambda b, c: (b, jnp.minimum((bnext(c) + 1) * per_chunk, last_halo), 0)
    full = lambda shape: pl.BlockSpec(shape, lambda b, c: (0,) * len(shape))
    st_spec = pl.BlockSpec((1, npairs, LANES, LANES), lambda b, c: (b, 0, 0, 0))
    y_shape = jax.ShapeDtypeStruct((bsz, length, width), BF16)
    st_shape = jax.ShapeDtypeStruct((bsz, npairs, LANES, LANES), F32)
    y_f, y_b, st_f, st_b = pl.pallas_call(
        _scan_kernel,
        grid=(bsz, nc),
        in_specs=[pl.BlockSpec((1, CHUNK, ncol), fnext_map), pl.BlockSpec((1, HALO, ncol), prev_map),
                  pl.BlockSpec((1, CHUNK, ncol), bnext_map), pl.BlockSpec((1, HALO, ncol), next_map),
                  pl.BlockSpec((1, CHUNK, ncol), lambda b, c: (b, 0, 0)),
                  pl.BlockSpec((1, CHUNK, ncol), lambda b, c: (b, nc - 1, 0)),
                  st_spec, st_spec,
                  full((2, 3, width)), full((1, LANES)), full((1, LANES)), full((2, width)),
                  full((2, LANES, width)), full((2, width)), full((2, LANES, width)), full((1, width)),
                  full((1, width)), full((1, width)), full((1, width)), full((1, width)), full((LANES, LANES))],
        out_specs=[pl.BlockSpec((1, CHUNK, width), fmap), pl.BlockSpec((1, CHUNK, width), bmap), st_spec, st_spec],
        out_shape=[y_shape, y_shape, st_shape, st_shape],
        scratch_shapes=[pltpu.VMEM((2, 6, CHUNK, width), F32)],
        compiler_params=_params("parallel", "arbitrary"),
        name="wkv_scan",
    )(zr, zr, zr, zr, zr, zr, state_fwd, state_bwd, p["mu"], p["mu_w"], p["mu_a"], p["w0"], p["w2"], p["a0"], p["a2"],
      p["k_k"], p["k_a"], p["r_k"], p["gn_w"], p["gn_b"], p["bd"])
    return (y_f, y_b), (st_f, st_b)


def _mix_kernel(period, tm, yf_ref, yb_ref, glo_ref, zc_ref, g2_ref, cw_ref, o_ref):
    gate = _dot(jax.nn.sigmoid(glo_ref[0]), g2_ref[...])
    yr = (yf_ref[0].astype(F32) + yb_ref[0].astype(F32)) * gate
    zc = zc_ref[0].astype(F32)
    u, gate_b, gate_c = zc[:, 0:256], zc[:, 256:512], zc[:, 512:768]
    hc = gate_c * u
    pos = lax.broadcasted_iota(jnp.int32, (tm, 1), 0) % period
    prev = jnp.where(pos == 0, 0.0, pltpu.roll(hc, 1, 0))
    nxt = jnp.where(pos == period - 1, 0.0, pltpu.roll(hc, tm - 1, 0))
    conv = prev * cw_ref[0:1, :] + hc * cw_ref[1:2, :] + nxt * cw_ref[2:3, :]
    o_ref[0, :, 0:512] = yr.astype(BF16)
    o_ref[0, :, 512:768] = (gate_b * conv).astype(BF16)


def _mix(ys_f, ys_b, zr, zc, g2, conv_w, period, tm):
    bsz, length, _ = zc.shape
    row = lambda b, i: (b, i, 0)
    glo_block = (N_RWKV_COLS - 128) // 128
    return pl.pallas_call(
        functools.partial(_mix_kernel, period, tm),
        grid=(bsz, length // tm),
        in_specs=[pl.BlockSpec((1, tm, 512), row), pl.BlockSpec((1, tm, 512), row),
                  pl.BlockSpec((1, tm, 128), lambda b, i: (b, i, glo_block)),
                  pl.BlockSpec((1, tm, N_CONV_COLS), row),
                  pl.BlockSpec((128, 512), lambda b, i: (0, 0)),
                  pl.BlockSpec((3, 256), lambda b, i: (0, 0))],
        out_specs=pl.BlockSpec((1, tm, 768), row),
        out_shape=jax.ShapeDtypeStruct((bsz, length, 768), BF16),
        compiler_params=_params("parallel", "parallel"),
        name="gate_conv",
    )(ys_f, ys_b, zr, zc, g2, conv_w)


def _chan_dft_kernel(f_ref, w_ref, o_ref):
    xcs = jnp.dot(f_ref[0].astype(BF16), w_ref[...], preferred_element_type=F32)
    o_ref[0, 0] = xcs[:, 0:256].astype(BF16)
    o_ref[0, 1] = xcs[:, 256:512].astype(BF16)


def _pos_dft_kernel(nb, cs_ref, x_ref, o_ref, acc_ref):
    kstep = pl.program_id(1)

    @pl.when(kstep == 0)
    def _():
        acc_ref[...] = jnp.zeros_like(acc_ref)

    cs = cs_ref[...]
    for b in range(nb):
        acc_ref[b] += jnp.dot(cs, x_ref[b], preferred_element_type=F32)

    @pl.when(kstep == pl.num_programs(1) - 1)
    def _():
        o_ref[...] = acc_ref[...].astype(BF16)


def _fourier(zf, chan_w, pos_cs, tm, tk):
    bsz, length, ch = zf.shape
    tr = min(512, length)
    xcs = pl.pallas_call(
        _chan_dft_kernel,
        grid=(bsz, length // tr),
        in_specs=[pl.BlockSpec((1, tr, ch), lambda b, i: (b, i, 0)),
                  pl.BlockSpec((ch, 2 * ch), lambda b, i: (0, 0))],
        out_specs=pl.BlockSpec((1, 2, tr, ch), lambda b, i: (b, 0, i, 0)),
        out_shape=jax.ShapeDtypeStruct((bsz, 2, length, ch), BF16),
        compiler_params=_params("parallel", "parallel"),
        name="chan_dft",
    )(zf, chan_w)
    xcs = xcs.reshape(bsz, 2 * length, ch)
    return pl.pallas_call(
        functools.partial(_pos_dft_kernel, bsz),
        grid=(length // tm, 2 * length // tk),
        in_specs=[pl.BlockSpec((tm, tk), lambda i, kk: (i, kk)),
                  pl.BlockSpec((bsz, tk, ch), lambda i, kk: (0, kk, 0))],
        out_specs=pl.BlockSpec((bsz, tm, ch), lambda i, kk: (0, i, 0)),
        out_shape=jax.ShapeDtypeStruct((bsz, length, ch), BF16),
        scratch_shapes=[pltpu.VMEM((bsz, tm, ch), F32)],
        compiler_params=_params("parallel", "arbitrary"),
        name="pos_dft",
    )(pos_cs, xcs)


def _dft_tables(length):
    m = jnp.arange(length, dtype=jnp.int32)[None, :]
    l1 = jnp.arange(length // 64, dtype=jnp.int32)[:, None] * 64
    l2 = jnp.arange(64, dtype=jnp.int32)[:, None]
    ang = lambda l: ((l * m) % length).astype(F32) * (2.0 * jnp.pi / length)
    c1, s1, c2, s2 = jnp.cos(ang(l1)), jnp.sin(ang(l1)), jnp.cos(ang(l2)), jnp.sin(ang(l2))
    scale = 1.0 / jnp.sqrt(jnp.float32(length))
    cos = (c1[:, None, :] * c2[None] - s1[:, None, :] * s2[None]).reshape(length, length) * scale
    sin = (s1[:, None, :] * c2[None] + c1[:, None, :] * s2[None]).reshape(length, length) * scale
    return jnp.concatenate([cos, sin], axis=1).astype(BF16)


def _chan_dft_weights(ngroups):
    j = jnp.arange(HEAD_DIM, dtype=jnp.int32)
    ang = ((j[:, None] * j[None, :]) % HEAD_DIM).astype(F32) * (2.0 * jnp.pi / HEAD_DIM)
    eye = jnp.eye(ngroups, dtype=F32)
    scale = 1.0 / jnp.sqrt(jnp.float32(HEAD_DIM))
    cc = jnp.kron(eye, jnp.cos(ang) * scale)
    sc = jnp.kron(eye, jnp.sin(ang) * scale)
    return jnp.concatenate([cc, -sc], axis=1).astype(BF16)


def _outproj_kernel(yrc_ref, yf_ref, w_ref, x_ref, gn1_ref, gate_ref, gn2_ref, sc_ref, sh_ref, xo_ref, h_ref):
    nrc = yrc_ref.shape[2]
    o = (jnp.dot(yrc_ref[0], w_ref[0:nrc, :], preferred_element_type=F32) +
         jnp.dot(yf_ref[0], w_ref[nrc:, :], preferred_element_type=F32))
    xn = x_ref[0] + gate_ref[0] * _rms(o, gn1_ref[...])
    xo_ref[0] = xn
    h_ref[0] = (_rms(xn, gn2_ref[...]) * (1.0 + sc_ref[0]) + sh_ref[0]).astype(BF16)


def _outproj(yrc, yf, w, x, gn1, gate, gn2, sc, sh, tm):
    bsz, length, d = x.shape
    row = lambda b, i: (b, i, 0)
    vec = pl.BlockSpec((1, d), lambda b, i: (0, 0))
    mod = pl.BlockSpec((1, 1, d), lambda b, i: (b, 0, 0))
    return pl.pallas_call(
        _outproj_kernel,
        grid=(bsz, length // tm),
        in_specs=[pl.BlockSpec((1, tm, yrc.shape[2]), row), pl.BlockSpec((1, tm, yf.shape[2]), row),
                  pl.BlockSpec(w.shape, lambda b, i: (0, 0)), pl.BlockSpec((1, tm, d), row),
                  vec, mod, vec, mod, mod],
        out_specs=[pl.BlockSpec((1, tm, d), row), pl.BlockSpec((1, tm, d), row)],
        out_shape=[jax.ShapeDtypeStruct((bsz, length, d), F32), jax.ShapeDtypeStruct((bsz, length, d), BF16)],
        compiler_params=_params("parallel", "parallel"),
        name="outproj",
    )(yrc, yf, w, x, gn1, gate, gn2, sc, sh)


FF_SUB = 512


def _swiglu_partial(h, wg, wu, wd, tf):
    def up(c, n):
        return (jnp.dot(h, wg(c, n), preferred_element_type=F32), jnp.dot(h, wu(c, n), preferred_element_type=F32))

    def down(gu, c, n):
        return jnp.dot((_silu(gu[0]) * gu[1]).astype(BF16), wd(c, n), preferred_element_type=F32)

    out = None
    prev = None
    for cut in [(c, min(FF_SUB, tf - c)) for c in range(0, tf, FF_SUB)]:
        gu = up(*cut)
        if prev is not None:
            part = down(*prev)
            out = part if out is None else out + part
        prev = (gu,) + cut
    part = down(*prev)
    return part if out is None else out + part


def _ffn_kernel(h_ref, wg_ref, wu_ref, wd_ref, x_ref, gn_ref, gate_ref, o_ref, acc_ref):
    j = pl.program_id(2)

    @pl.when(j == 0)
    def _():
        acc_ref[...] = jnp.zeros_like(acc_ref)

    acc_ref[...] += _swiglu_partial(h_ref[0], lambda c, n: wg_ref[:, c:c + n], lambda c, n: wu_ref[:, c:c + n],
                                    lambda c, n: wd_ref[c:c + n, :], wg_ref.shape[1])

    @pl.when(j == pl.num_programs(2) - 1)
    def _():
        o_ref[0] = x_ref[0] + gate_ref[0] * _rms(acc_ref[...], gn_ref[...])


def _ffn(h, wg, wu, wd, x, gn, gate, tm, tf):
    bsz, length, d = x.shape
    ff = wg.shape[1]
    row = lambda b, i, j: (b, i, 0)
    return pl.pallas_call(
        _ffn_kernel,
        grid=(bsz, length // tm, ff // tf),
        in_specs=[pl.BlockSpec((1, tm, d), row),
                  pl.BlockSpec((d, tf), lambda b, i, j: (0, j)),
                  pl.BlockSpec((d, tf), lambda b, i, j: (0, j)),
                  pl.BlockSpec((tf, d), lambda b, i, j: (j, 0)),
                  pl.BlockSpec((1, tm, d), row),
                  pl.BlockSpec((1, d), lambda b, i, j: (0, 0)),
                  pl.BlockSpec((1, 1, d), lambda b, i, j: (b, 0, 0))],
        out_specs=pl.BlockSpec((1, tm, d), row),
        out_shape=jax.ShapeDtypeStruct((bsz, length, d), F32),
        scratch_shapes=[pltpu.VMEM((tm, d), F32)],
        compiler_params=_params("parallel", "parallel", "arbitrary"),
        name="ffn",
    )(h, wg, wu, wd, x, gn, gate)


MOE_TILE = 512
MOE_BLOCK = 128
MOE_ALIGN = 16
MOE_TM = 1024


def _router_kernel(x_ref, gn_ref, sc_ref, sh_ref, w_ref, b_ref, tri_ref, routes_ref, routest_ref, cnt_ref):
    h = _rms(x_ref[0], gn_ref[...]) * (1.0 + sc_ref[0]) + sh_ref[0]
    logits = jnp.dot(h, w_ref[...], preferred_element_type=F32, precision=lax.Precision.HIGHEST) + b_ref[...]
    ne = float(logits.shape[1])
    idx = lax.broadcasted_iota(jnp.int32, logits.shape, 1).astype(F32)
    m1 = jnp.max(logits, axis=1, keepdims=True)
    i1 = jnp.min(jnp.where(logits == m1, idx, ne), axis=1, keepdims=True)
    rest = jnp.where(idx == i1, -jnp.inf, logits)
    m2 = jnp.max(rest, axis=1, keepdims=True)
    i2 = jnp.min(jnp.where(rest == m2, idx, ne), axis=1, keepdims=True)
    e2 = jnp.exp(m2 - m1)
    p1 = 1.0 / (1.0 + e2)
    p2 = e2 / (1.0 + e2)
    sel = jnp.where(idx == i1, 1.0, jnp.where(idx == i2, 1.0, 0.0))
    before = jnp.dot(tri_ref[...], sel.astype(BF16), preferred_element_type=F32)
    rank1 = jnp.sum(jnp.where(idx == i1, before, 0.0), axis=1, keepdims=True)
    rank2 = jnp.sum(jnp.where(idx == i2, before, 0.0), axis=1, keepdims=True)
    routes = jnp.zeros_like(logits)
    for lane, val in enumerate((i1, i2, rank1, rank2, p1, p2)):
        routes = jnp.where(idx == float(lane), val, routes)
    routes_ref[0] = routes
    routest_ref[0] = jnp.transpose(routes)[0:N_EXPERTS, :]
    cnt_ref[0, 0] = jnp.sum(sel, axis=0, keepdims=True)


def _router(x, gn, sc, sh, w, b):
    bsz, length, d = x.shape
    tm = MOE_TILE
    pad = LANES - w.shape[1]
    w = jnp.pad(w, ((0, 0), (0, pad)))
    b = jnp.pad(b, ((0, 0), (0, pad)), constant_values=-1e30)
    ne = LANES
    t = jnp.arange(tm, dtype=jnp.int32)
    tri = (t[None, :] < t[:, None]).astype(BF16)
    row = lambda bb, i: (bb, i, 0)
    vec = pl.BlockSpec((1, d), lambda bb, i: (0, 0))
    mod = pl.BlockSpec((1, 1, d), lambda bb, i: (bb, 0, 0))
    nt = length // tm
    return pl.pallas_call(
        _router_kernel,
        grid=(bsz, nt),
        in_specs=[pl.BlockSpec((1, tm, d), row), vec, mod, mod,
                  pl.BlockSpec((d, ne), lambda bb, i: (0, 0)), pl.BlockSpec((1, ne), lambda bb, i: (0, 0)),
                  pl.BlockSpec((tm, tm), lambda bb, i: (0, 0))],
        out_specs=[pl.BlockSpec((1, tm, ne), row),
                   pl.BlockSpec((1, N_EXPERTS, tm), lambda bb, i: (bb, 0, i)),
                   pl.BlockSpec((1, 1, 1, ne), lambda bb, i: (bb, i, 0, 0))],
        out_shape=[jax.ShapeDtypeStruct((bsz, length, ne), F32),
                   jax.ShapeDtypeStruct((bsz, N_EXPERTS, length), F32),
                   jax.ShapeDtypeStruct((bsz, nt, 1, ne), F32)],
        compiler_params=_params("parallel", "parallel"),
        name="router",
    )(x, gn, sc, sh, w, b, tri)


_MOE_SHIFT = MOE_BLOCK.bit_length() - 1
_MOE_BUF_BLOCKS = (2 * MOE_TILE + N_EXPERTS * (MOE_ALIGN - 1 + MOE_BLOCK - 1)) // MOE_BLOCK + 1


def _align_rows(c):
    return ((c + (MOE_ALIGN - 1)) >> (MOE_ALIGN.bit_length() - 1)) << (MOE_ALIGN.bit_length() - 1)


def _blocks(c):
    return (c + (MOE_BLOCK - 1)) >> _MOE_SHIFT


def _dispatch_copies(start, tile, off_ref, cnt_ref, buf_ref, hs_ref, sem):
    def run(src_row, dst_row, nrows):
        cp = pltpu.make_async_copy(buf_ref.at[pl.ds(pl.multiple_of(src_row, MOE_ALIGN), nrows)],
                                   hs_ref.at[pl.ds(pl.multiple_of(dst_row, MOE_ALIGN), nrows)], sem)
        if start:
            cp.start()
        else:
            cp.wait()

    slot = jnp.int32(0)
    for e in range(N_EXPERTS):
        rows = _align_rows(cnt_ref[tile * N_EXPERTS + e])
        dst0 = off_ref[tile * N_EXPERTS + e]
        src0 = slot * MOE_BLOCK
        nfull = rows >> _MOE_SHIFT

        def full(rb, carry, src0=src0, dst0=dst0):
            run(src0 + rb * MOE_BLOCK, dst0 + rb * MOE_BLOCK, MOE_BLOCK)
            return carry

        lax.fori_loop(0, nfull, full, 0)
        done = nfull * MOE_BLOCK
        piece = MOE_BLOCK // 2
        while piece >= MOE_ALIGN:
            has = (rows & piece) != 0

            @pl.when(has)
            def _(done=done, piece=piece, src0=src0, dst0=dst0):
                run(src0 + done, dst0 + done, piece)

            done = done + jnp.where(has, piece, 0)
            piece //= 2
        slot = slot + _blocks(rows)


def _staging_rows(tile, cnt_ref, i1, i2, rank1, rank2):
    row1 = jnp.zeros_like(i1)
    row2 = jnp.zeros_like(i2)
    slot = jnp.int32(0)
    for e in range(N_EXPERTS):
        first_row = (slot * MOE_BLOCK).astype(F32)
        row1 = jnp.where(i1 == float(e), first_row, row1)
        row2 = jnp.where(i2 == float(e), first_row, row2)
        slot = slot + _blocks(cnt_ref[tile * N_EXPERTS + e])
    return row1 + rank1, row2 + rank2


def _dispatch_kernel(off_ref, cnt_ref, h_ref, routest_ref, hs_init_ref, hs_ref, buf_ref, sem):
    del hs_init_ref
    tile = pl.program_id(0)
    rt = routest_ref[0]
    row1, row2 = _staging_rows(tile, cnt_ref, rt[0:1], rt[1:2], rt[2:3], rt[3:4])
    rows = lax.broadcasted_iota(jnp.int32, (MOE_BLOCK, 1), 0)
    onehot = jnp.concatenate(
        [jnp.where(row1 == (rows + blk * MOE_BLOCK).astype(F32), 1.0,
                   jnp.where(row2 == (rows + blk * MOE_BLOCK).astype(F32), 1.0, 0.0)).astype(BF16)
         for blk in range(_MOE_BUF_BLOCKS)], axis=0)
    buf_ref[...] = jnp.dot(onehot, h_ref[...], preferred_element_type=F32).astype(BF16)
    _dispatch_copies(True, tile, off_ref, cnt_ref, buf_ref, hs_ref, sem)
    _dispatch_copies(False, tile, off_ref, cnt_ref, buf_ref, hs_ref, sem)


def _dispatch(h, rankt, off, cnt, nrows):
    n, d = h.shape
    ntb = rankt.shape[2] // MOE_TILE
    grid_spec = pltpu.PrefetchScalarGridSpec(
        num_scalar_prefetch=2, grid=(n // MOE_TILE,),
        in_specs=[pl.BlockSpec((MOE_TILE, d), lambda t, o, c: (t, 0)),
                  pl.BlockSpec((1, N_EXPERTS, MOE_TILE), lambda t, o, c: (t // ntb, 0, t % ntb)),
                  pl.BlockSpec(memory_space=pl.ANY)],
        out_specs=pl.BlockSpec(memory_space=pl.ANY),
        scratch_shapes=[pltpu.VMEM((_MOE_BUF_BLOCKS * MOE_BLOCK, d), BF16), pltpu.SemaphoreType.DMA(())])
    return pl.pallas_call(
        _dispatch_kernel, grid_spec=grid_spec,
        out_shape=jax.ShapeDtypeStruct((nrows, d), BF16),
        input_output_aliases={4: 0},
        compiler_params=_params("arbitrary"),
        name="moe_dispatch",
    )(off, cnt, h, rankt, jnp.zeros((nrows, d), BF16))


def _gffn_kernel(te_ref, tv_ref, h_ref, wg_ref, wu_ref, wd_ref, o_ref, acc_ref):
    q = pl.program_id(0)
    j = pl.program_id(1)

    @pl.when(j == 0)
    def _():
        acc_ref[...] = jnp.zeros_like(acc_ref)

    @pl.when(tv_ref[q] > 0)
    def _():
        acc_ref[...] += _swiglu_partial(h_ref[...], lambda c, n: wg_ref[0, :, c:c + n],
                                        lambda c, n: wu_ref[0, :, c:c + n], lambda c, n: wd_ref[0, c:c + n, :],
                                        wg_ref.shape[2])

    @pl.when(j == pl.num_programs(1) - 1)
    def _():
        o_ref[...] = acc_ref[...].astype(BF16)


def _gffn(hs, te, tv, wg, wu, wd, tf):
    nrows, d = hs.shape
    ff = wg.shape[2]
    grid_spec = pltpu.PrefetchScalarGridSpec(
        num_scalar_prefetch=2, grid=(nrows // MOE_TM, ff // tf),
        in_specs=[pl.BlockSpec((MOE_TM, d), lambda q, j, te, tv: (q, 0)),
                  pl.BlockSpec((1, d, tf), lambda q, j, te, tv: (te[q], 0, j)),
                  pl.BlockSpec((1, d, tf), lambda q, j, te, tv: (te[q], 0, j)),
                  pl.BlockSpec((1, tf, d), lambda q, j, te, tv: (te[q], j, 0))],
        out_specs=pl.BlockSpec((MOE_TM, d), lambda q, j, te, tv: (q, 0)),
        scratch_shapes=[pltpu.VMEM((MOE_TM, d), F32)])
    return pl.pallas_call(
        _gffn_kernel, grid_spec=grid_spec,
        out_shape=jax.ShapeDtypeStruct((nrows, d), BF16),
        compiler_params=_params("arbitrary", "arbitrary"),
        name="moe_experts",
    )(te, tv, hs, wg, wu, wd)


def _combine_copies(start, tile, off_ref, cnt_ref, ys_ref, buf_ref, sem):
    slot = jnp.int32(0)
    for e in range(N_EXPERTS):
        nb = _blocks(cnt_ref[tile * N_EXPERTS + e])
        src0 = off_ref[tile * N_EXPERTS + e]

        def body(rb, carry, slot=slot, src0=src0):
            cp = pltpu.make_async_copy(
                ys_ref.at[pl.ds(pl.multiple_of(src0 + rb * MOE_BLOCK, MOE_ALIGN), MOE_BLOCK)],
                buf_ref.at[pl.ds(pl.multiple_of((slot + rb) * MOE_BLOCK, MOE_BLOCK), MOE_BLOCK)], sem)
            if start:
                cp.start()
            else:
                cp.wait()
            return carry

        lax.fori_loop(0, nb, body, 0)
        slot = slot + nb


def _combine_kernel(off_ref, cnt_ref, ys_ref, routes_ref, x_ref, gn_ref, gate_ref, o_ref, buf_ref, sem):
    tile = pl.program_id(0)

    @pl.when(tile == 0)
    def _():
        buf_ref[...] = jnp.zeros_like(buf_ref)

    _combine_copies(True, tile, off_ref, cnt_ref, ys_ref, buf_ref, sem)
    routes = routes_ref[0]
    col = lambda j: routes[:, j:j + 1]
    row1, row2 = _staging_rows(tile, cnt_ref, col(0), col(1), col(2), col(3))
    lanes = lax.broadcasted_iota(jnp.int32, (1, MOE_BLOCK), 1)
    scatter = jnp.concatenate(
        [(jnp.where(row1 == (lanes + blk * MOE_BLOCK).astype(F32), col(4), 0.0) +
          jnp.where(row2 == (lanes + blk * MOE_BLOCK).astype(F32), col(5), 0.0)).astype(BF16)
         for blk in range(_MOE_BUF_BLOCKS)], axis=1)
    _combine_copies(False, tile, off_ref, cnt_ref, ys_ref, buf_ref, sem)
    mixed = jnp.dot(scatter, buf_ref[...], preferred_element_type=F32)
    o_ref[0] = x_ref[0] + gate_ref[0] * _rms(mixed, gn_ref[...])


def _combine(ys, routes, x, gn, gate, off, cnt):
    bsz, length, d = x.shape
    ntb = length // MOE_TILE
    tok = lambda t, o, c: (t // ntb, t % ntb, 0)
    grid_spec = pltpu.PrefetchScalarGridSpec(
        num_scalar_prefetch=2, grid=(bsz * ntb,),
        in_specs=[pl.BlockSpec(memory_space=pl.ANY),
                  pl.BlockSpec((1, MOE_TILE, LANES), tok),
                  pl.BlockSpec((1, MOE_TILE, d), tok),
                  pl.BlockSpec((1, d), lambda t, o, c: (0, 0)),
                  pl.BlockSpec((1, 1, d), lambda t, o, c: (t // ntb, 0, 0))],
        out_specs=pl.BlockSpec((1, MOE_TILE, d), tok),
        scratch_shapes=[pltpu.VMEM((_MOE_BUF_BLOCKS * MOE_BLOCK, d), BF16), pltpu.SemaphoreType.DMA(())])
    return pl.pallas_call(
        _combine_kernel, grid_spec=grid_spec,
        out_shape=jax.ShapeDtypeStruct((bsz, length, d), F32),
        compiler_params=_params("arbitrary"),
        name="moe_combine",
    )(off, cnt, ys, routes, x, gn, gate)


def _route_plan(counts, ntokens):
    ntiles = counts.shape[0]
    rows = (counts + (MOE_ALIGN - 1)) // MOE_ALIGN * MOE_ALIGN
    seg = (rows.sum(0) + MOE_BLOCK + MOE_TM - 1) // MOE_TM * MOE_TM
    seg_end = jnp.cumsum(seg)
    off = (seg_end - seg)[None, :] + jnp.cumsum(rows, axis=0) - rows
    nq = -(-(2 * ntokens + ntiles * N_EXPERTS * (MOE_ALIGN - 1) + N_EXPERTS * (MOE_BLOCK + MOE_TM - 1)) // MOE_TM)
    q = jnp.arange(nq, dtype=jnp.int32) * MOE_TM
    te = jnp.minimum(jnp.sum(q[:, None] >= seg_end[None, :], axis=1), N_EXPERTS - 1).astype(jnp.int32)
    tv = (q < seg_end[-1]).astype(jnp.int32)
    return off.reshape(-1).astype(jnp.int32), te, tv, nq * MOE_TM


def _moe(h, x, gn_pre, sc, sh, router_w, router_b, wg, wu, wd, gn_post, gate, tf):
    bsz, length, d = x.shape
    routes, routest, cnt = _router(x, gn_pre, sc, sh, router_w, router_b)
    counts = cnt[:, :, 0, :N_EXPERTS].astype(jnp.int32).reshape(-1, N_EXPERTS)
    off, te, tv, nrows = _route_plan(counts, bsz * length)
    cnt_flat = counts.reshape(-1)
    hs = _dispatch(h.reshape(bsz * length, d), routest, off, cnt_flat, nrows)
    ys = _gffn(hs, te, tv, wg, wu, wd, tf)
    return _combine(ys, routes, x, gn_post, gate, off, cnt_flat)


def _sincos_2d(rows, cols, dim):
    quarter = dim // 4
    omega = 1.0 / (POS_BASE ** (jnp.arange(quarter, dtype=F32) / quarter))

    def axis_emb(n):
        ang = jnp.arange(n, dtype=F32)[:, None] * omega[None, :]
        return jnp.concatenate([jnp.sin(ang), jnp.cos(ang)], axis=-1)

    er, ec = axis_emb(rows), axis_emb(cols)
    emb = jnp.concatenate([jnp.broadcast_to(er[:, None, :], (rows, cols, dim // 2)),
                           jnp.broadcast_to(ec[None, :, :], (rows, cols, dim // 2))], axis=-1)
    return emb.reshape(rows * cols, dim)


def _head_block_ones(width):
    h = jnp.arange(width, dtype=jnp.int32) // HEAD_DIM
    return (h[:, None] == h[None, :]).astype(BF16)


def _pad_lora(w2, d):
    z = jnp.zeros_like(w2[0])
    return jnp.concatenate([w2[0], z], axis=0) if d == 0 else jnp.concatenate([z, w2[1]], axis=0)


def _tile(length, pref):
    return pref if length % pref == 0 else length


def kernel(x, c, ctx, c_ctx, ada_w, ada_b, norm_g, w_in, w_out, rwkv_mu, rwkv_mu_w, rwkv_mu_a, rwkv_w0, rwkv_w2,
           rwkv_a0, rwkv_a2, rwkv_g2, rwkv_k_k, rwkv_k_a, rwkv_r_k, rwkv_gn_w, rwkv_gn_b, conv_w, ffn_w_gate,
           ffn_w_up, ffn_w_down, router_w, router_b, moe_w_gate, moe_w_up, moe_w_down):
    bsz, length, dim = x.shape
    ctx_len = ctx.shape[1]
    depth = ada_w.shape[0]
    d_rwkv = rwkv_w0.shape[-1]

    pos = _sincos_2d(length // GRID_W, GRID_W, dim).astype(x.dtype)
    bd = _head_block_ones(LANES)
    chan_w = _chan_dft_weights(N_FOUR_COLS // HEAD_DIM)
    dft_lat = _dft_tables(length)
    dft_ctx = _dft_tables(ctx_len)

    cvec = jnp.zeros((16, dim), F32).at[:bsz].set(c).at[bsz].set(c_ctx)
    mod = _adaln(cvec, ada_w, ada_b)
    zero_state = jnp.zeros((bsz, d_rwkv // LANES, LANES, LANES), F32)
    xc = ctx

    def mods(l, lo, hi):
        return [mod[l, lo:hi, i * dim:(i + 1) * dim][:, None, :] for i in range(6)]

    def mixers(l, zr, zc, zf, states0, period, dft, need_y=True):
        tm = _tile(zr.shape[1], 512)
        p = {"mu": rwkv_mu[l], "mu_w": rwkv_mu_w[l].reshape(1, -1), "mu_a": rwkv_mu_a[l].reshape(1, -1),
             "w0": rwkv_w0[l], "w2": jnp.stack([_pad_lora(rwkv_w2[l], d) for d in range(2)]), "a0": rwkv_a0[l],
             "a2": jnp.stack([_pad_lora(rwkv_a2[l], d) for d in range(2)]), "k_k": rwkv_k_k[l][None],
             "k_a": rwkv_k_a[l][None], "r_k": rwkv_r_k[l].reshape(1, -1), "gn_w": rwkv_gn_w[l][None],
             "gn_b": rwkv_gn_b[l][None], "bd": bd}
        ys, states = _scan(zr, states0[0], states0[1], p)
        if not need_y:
            return None, None, states
        yrc = _mix(ys[0], ys[1], zr, zc, rwkv_g2[l], conv_w[l], period, tm)
        yf = _fourier(zf, chan_w, dft, _tile(zf.shape[1], 512), 512)
        return yrc, yf, states

    def channel_mixer(l, h2, xn, gn3, gate2, sc2, sh2, gn2):
        tm = _tile(xn.shape[1], 512)
        i = l // 2
        if l % 2 == 0:
            return _ffn(h2, ffn_w_gate[i].astype(BF16), ffn_w_up[i].astype(BF16), ffn_w_down[i].astype(BF16),
                        xn, gn3, gate2, tm, 1408)
        return _moe(h2, xn, gn2, sc2, sh2, router_w[i], router_b[i][None], moe_w_gate[i].astype(BF16),
                    moe_w_up[i].astype(BF16), moe_w_down[i].astype(BF16), gn3, gate2, 1792)

    for l in range(depth):
        last = l == depth - 1
        w_in_l = w_in[l].astype(BF16)
        w_out_l = w_out[l].astype(BF16)
        gn = [norm_g[l, i][None] for i in range(4)]
        sh1, sc1, g1, sh2, sc2, g2 = mods(l, 0, bsz)
        csh1, csc1, cg1, csh2, csc2, cg2 = mods(l, bsz, bsz + 1)

        xc_flat = xc.reshape(1, bsz * ctx_len, dim)
        tmc = _tile(bsz * ctx_len, 512)
        zr, zc, zf = _inproj(xc_flat, None, gn[0], csc1, csh1, w_in_l, tmc)
        unflat = lambda t: t.reshape(bsz, ctx_len, t.shape[-1])
        yrc, yf, ctx_states = mixers(l, unflat(zr), unflat(zc), unflat(zf), (zero_state, zero_state), ctx_len,
                                     dft_ctx, need_y=not last)
        if not last:
            flat = lambda t: t.reshape(1, bsz * ctx_len, t.shape[-1])
            xcn, hc2 = _outproj(flat(yrc), flat(yf), w_out_l, xc_flat, gn[1], cg1, gn[2], csc2, csh2, tmc)
            xc = channel_mixer(l, hc2, xcn, gn[3], cg2, csc2, csh2, gn[2]).reshape(bsz, ctx_len, dim)

        tm = _tile(length, 512)
        if l == 0:
            zr, zc, zf, x = _inproj(x, pos, gn[0], sc1, sh1, w_in_l, tm)
        else:
            zr, zc, zf = _inproj(x, None, gn[0], sc1, sh1, w_in_l, tm)
        yrc, yf, _ = mixers(l, zr, zc, zf, ctx_states, GRID_W, dft_lat)
        xn, h2 = _outproj(yrc, yf, w_out_l, x, gn[1], g1, gn[2], sc2, sh2, _tile(length, 512))
        x = channel_mixer(l, h2, xn, gn[3], g2, sc2, sh2, gn[2])
    return x
```

```python
import functools

import jax
import jax.numpy as jnp
from jax import lax
from jax.experimental import pallas as pl
from jax.experimental.pallas import tpu as pltpu

F32 = jnp.float32
BF16 = jnp.bfloat16

HEAD_DIM = 64
GRID_W = 64
NORM_EPS = 1e-6
GN_EPS = 64e-5
POS_BASE = 10000.0
N_EXPERTS = 8
CHUNK = 64
LANES = 128
VMEM_LIMIT = 56 * 1024 * 1024


def _params(*sem):
    return pltpu.CompilerParams(dimension_semantics=sem, vmem_limit_bytes=VMEM_LIMIT)


def _dot(a, b):
    return jnp.dot(a.astype(BF16), b.astype(BF16), preferred_element_type=F32)


def _dot_nt(a, b):
    return lax.dot_general(a.astype(BF16), b.astype(BF16), (((1,), (1,)), ((), ())),
                           preferred_element_type=F32)


def _split(x):
    hi = x.astype(BF16)
    lo = (x - hi.astype(F32)).astype(BF16)
    return hi, lo


def _head_sums(x, ones_pair):
    rows = x.shape[0]
    tiles = x.shape[1] // LANES
    hi, lo = _split(x)
    stacked = jnp.concatenate([t[:, g * LANES:(g + 1) * LANES] for t in (hi, lo) for g in range(tiles)], axis=0)
    s = jnp.dot(stacked, ones_pair, preferred_element_type=F32)
    return jnp.concatenate([s[g * rows:(g + 1) * rows] + s[(tiles + g) * rows:(tiles + g + 1) * rows]
                            for g in range(tiles)], axis=1)


def _dot_split_rhs(w, x):
    hi, lo = _split(x)
    return (jnp.dot(w, hi, preferred_element_type=F32) + jnp.dot(w, lo, preferred_element_type=F32))


def _rms(x, g):
    ms = jnp.mean(x * x, axis=-1, keepdims=True)
    return x * lax.rsqrt(ms + NORM_EPS) * g


def _silu(x):
    return x * jax.nn.sigmoid(x)


def _softplus(x):
    return jnp.maximum(x, 0.0) + jnp.log(1.0 + jnp.exp(-jnp.abs(x)))


def _adaln_kernel(c_ref, w_ref, b_ref, o_ref):
    s = _silu(c_ref[...])
    o_ref[0] = _dot(s, w_ref[0]) + b_ref[0]


def _adaln(cvec, ada_w, ada_b):
    nl, d, n = ada_w.shape
    rows = cvec.shape[0]
    tn = 1536
    return pl.pallas_call(
        _adaln_kernel,
        grid=(nl, n // tn),
        in_specs=[pl.BlockSpec((rows, d), lambda l, j: (0, 0)),
                  pl.BlockSpec((1, d, tn), lambda l, j: (l, 0, j)),
                  pl.BlockSpec((1, 1, tn), lambda l, j: (l, 0, j))],
        out_specs=pl.BlockSpec((1, rows, tn), lambda l, j: (l, 0, j)),
        out_shape=jax.ShapeDtypeStruct((nl, rows, n), F32),
        compiler_params=_params("arbitrary", "arbitrary"),
        name="adaln",
    )(cvec, ada_w, ada_b.reshape(nl, 1, n))


N_RWKV_COLS = 1920
N_CONV_COLS = 768
N_FOUR_COLS = 256


def _inproj_kernel(add_pos, *refs):
    if add_pos:
        x_ref, pos_ref, g_ref, sc_ref, sh_ref, w_ref, zr_ref, zc_ref, zf_ref, xp_ref = refs
        x = x_ref[0] + pos_ref[...]
        xp_ref[0] = x
    else:
        x_ref, g_ref, sc_ref, sh_ref, w_ref, zr_ref, zc_ref, zf_ref = refs
        x = x_ref[0]
    h = (_rms(x, g_ref[...]) * (1.0 + sc_ref[0]) + sh_ref[0]).astype(BF16)
    a, b = N_RWKV_COLS, N_RWKV_COLS + N_CONV_COLS
    zr_ref[0] = jnp.dot(h, w_ref[:, :a], preferred_element_type=F32)
    zc_ref[0] = jnp.dot(h, w_ref[:, a:b], preferred_element_type=F32).astype(BF16)
    zf_ref[0] = jnp.dot(h, w_ref[:, b:], preferred_element_type=F32).astype(BF16)


def _inproj(x, pos, g, sc, sh, w, tm):
    bsz, length, d = x.shape
    n = w.shape[1]
    add_pos = pos is not None
    row = lambda b, i: (b, i, 0)
    in_specs = [pl.BlockSpec((1, tm, d), row)]
    args = [x]
    if add_pos:
        in_specs.append(pl.BlockSpec((tm, d), lambda b, i: (i, 0)))
        args.append(pos)
    in_specs += [pl.BlockSpec((1, d), lambda b, i: (0, 0)),
                 pl.BlockSpec((1, 1, d), lambda b, i: (b, 0, 0)),
                 pl.BlockSpec((1, 1, d), lambda b, i: (b, 0, 0)),
                 pl.BlockSpec((d, n), lambda b, i: (0, 0))]
    args += [g, sc, sh, w]
    out_shape = [jax.ShapeDtypeStruct((bsz, length, N_RWKV_COLS), F32),
                 jax.ShapeDtypeStruct((bsz, length, N_CONV_COLS), BF16),
                 jax.ShapeDtypeStruct((bsz, length, N_FOUR_COLS), BF16)]
    out_specs = [pl.BlockSpec((1, tm, N_RWKV_COLS), row),
                 pl.BlockSpec((1, tm, N_CONV_COLS), row),
                 pl.BlockSpec((1, tm, N_FOUR_COLS), row)]
    if add_pos:
        out_shape.append(jax.ShapeDtypeStruct((bsz, length, d), F32))
        out_specs.append(pl.BlockSpec((1, tm, d), row))
    return pl.pallas_call(
        functools.partial(_inproj_kernel, add_pos),
        grid=(bsz, length // tm),
        in_specs=in_specs, out_specs=out_specs, out_shape=out_shape,
        compiler_params=_params("parallel", "parallel"),
        name="inproj",
    )(*args)


HALO = 8


def _rwkv_features(z, edge_row, d, mu_ref, muw_ref, mua_ref, w0_ref, w2_ref, a0_ref, a2_ref, kk_ref, ka_ref, bd):
    n = z.shape[0]
    rows = lax.broadcasted_iota(jnp.int32, (n, 1), 0)
    if d:
        zs = jnp.where(rows == n - 1, edge_row, pltpu.roll(z, n - 1, 0))
    else:
        zs = jnp.where(rows == 0, edge_row, pltpu.roll(z, 1, 0))
    dz = zs - z
    k = z[:, 0:512] + dz[:, 0:512] * mu_ref[d, 0:1, :]
    v = z[:, 512:1024] + dz[:, 512:1024] * mu_ref[d, 1:2, :]
    r = z[:, 1280:1792] + dz[:, 1280:1792] * mu_ref[d, 2:3, :]
    wl = z[:, 1024:1152] + dz[:, 1024:1152] * muw_ref[...]
    al = z[:, 1152:1280] + dz[:, 1152:1280] * mua_ref[...]
    w_log = -_softplus(-(w0_ref[d:d + 1, :] + _dot(jnp.tanh(wl), w2_ref[d]))) - 0.5
    lw = -jnp.exp(w_log)
    a = jax.nn.sigmoid(a0_ref[d:d + 1, :] + _dot(al, a2_ref[d]))
    kk = k * kk_ref[...]
    ss = _head_sums(kk * kk, bd)
    kk = kk / jnp.maximum(jnp.sqrt(ss), 1e-12)
    return r, lw, k * (1.0 + (a - 1.0) * ka_ref[...]), v, -kk, kk * a


SOLVE_BLOCK = 8


def _unit_lower_solve(a_list, x_list, diag_blk, eye, first, side_work=()):
    side_work = list(side_work)

    def boundary():
        if side_work:
            side_work.pop(0)()

    def pair(m, w):
        return _dot(m, jnp.concatenate([jnp.where(first, w, 0.0), jnp.where(first, 0.0, w)], axis=0))

    def apply(m, x):
        zero = jnp.zeros((x.shape[0], LANES), x.dtype)
        return _dot(m, jnp.concatenate([jnp.concatenate([x[:, :LANES], zero], axis=1),
                                        jnp.concatenate([zero, x[:, LANES:]], axis=1)], axis=0))

    n = eye.shape[0]
    a_d = [jnp.where(diag_blk, a, 0.0) for a in a_list]
    a_o = [a - d for a, d in zip(a_list, a_d)]
    dinv = [eye + d for d in a_d]
    apow = a_d
    for _ in range(SOLVE_BLOCK.bit_length() - 2):
        apow = [pair(p, p) for p in apow]
        boundary()
        dinv = [d + pair(p, d) for p, d in zip(apow, dinv)]
        boundary()
    b = [pair(d, o) for d, o in zip(dinv, a_o)]
    boundary()
    x = [apply(d, x) for d, x in zip(dinv, x_list)]
    levels = (n // SOLVE_BLOCK).bit_length() - 1
    for level in range(levels):
        boundary()
        if level:
            b = [pair(m, m) for m in b]
        x = [v + apply(m, v) for m, v in zip(b, x)]
    while side_work:
        boundary()
    return x


def _scan_kernel(*refs):
    z_refs = (refs[0], refs[2])
    halo_refs = (refs[1], refs[3])
    zfirst_refs = refs[4:6]
    s0_refs = refs[6:8]
    feat_refs = refs[8:17]
    rk_ref, gw_ref, gb_ref, bd_ref = refs[17:21]
    y_refs = refs[21:23]
    st_refs = refs[23:25]
    feat_ref = refs[25]
    c = pl.program_id(1)

    @pl.when(c == 0)
    def _():
        for st_ref, s0_ref in zip(st_refs, s0_refs):
            st_ref[...] = s0_ref[...]
        no_token = jnp.zeros((1, zfirst_refs[0].shape[2]), F32)
        for d in range(2):
            for j, f in enumerate(_rwkv_features(zfirst_refs[d][0], no_token, d, *feat_refs, bd_ref[...])):
                feat_ref[d, j] = f

    n = CHUNK
    t_i = lax.broadcasted_iota(jnp.int32, (n, n), 0)
    s_i = lax.broadcasted_iota(jnp.int32, (n, n), 1)
    blk_shift = SOLVE_BLOCK.bit_length() - 1
    lane2 = lax.broadcasted_iota(jnp.int32, (n, LANES), 1)
    t_2 = lax.broadcasted_iota(jnp.int32, (n, LANES), 0)
    s_2 = lane2 & (n - 1)
    diag_blk2 = (s_2 >> blk_shift) == (t_2 >> blk_shift)
    eye2 = jnp.where(s_2 == t_2, 1.0, 0.0)
    first = lane2 < HEAD_DIM
    r_bd = lax.broadcasted_iota(jnp.int32, (LANES, LANES), 0) < HEAD_DIM
    c_bd = lax.broadcasted_iota(jnp.int32, (LANES, LANES), 1) < HEAD_DIM
    blockdiag = r_bd == c_bd
    incl2 = (s_2 <= t_2, s_2 >= t_2)
    strict2 = (s_2 < t_2, s_2 > t_2)

    npairs = st_refs[0].shape[1]
    sls = [slice(g * LANES, (g + 1) * LANES) for g in range(npairs)]
    items = [(d, g) for d in range(2) for g in range(npairs)]
    heads = [(i, j) for i in range(len(items)) for j in range(2)]
    feats = [[feat_ref[d, j] for j in range(6)] for d in range(2)]
    raw, scaled, w_tot = [], [], []
    for d in range(2):
        r, lw, k, v, a, b = feats[d]
        incl = (s_i >= t_i) if d else (s_i <= t_i)
        cum = _dot_split_rhs(jnp.where(incl, 1.0, 0.0).astype(BF16), lw)
        tot = cum[0:1, :] if d else cum[n - 1:n, :]
        w_inv = jnp.exp(-cum)
        w_rem = jnp.exp(tot - cum)
        raw.append((r, k, v))
        scaled.append((r * jnp.exp(cum), a * jnp.exp(cum - lw), b * w_inv, k * w_inv, b * w_rem, k * w_rem))
        w_tot.append(jnp.exp(tot))
    states = [st_refs[d][0, g] for d, g in items]

    v_g = [raw[d][2][:, sls[g]] for d, g in items]
    at_h = [(jnp.where(first, scaled[d][1][:, sls[g]], 0.0), jnp.where(first, 0.0, scaled[d][1][:, sls[g]]))
            for d, g in items]
    rt_h = [(jnp.where(first, scaled[d][0][:, sls[g]], 0.0), jnp.where(first, 0.0, scaled[d][0][:, sls[g]]))
            for d, g in items]
    s = [_dot_nt(jnp.concatenate([at_h[i][0], rt_h[i][0], at_h[i][1], rt_h[i][1]], axis=0),
                 jnp.concatenate([scaled[d][2][:, sls[g]], scaled[d][3][:, sls[g]]], axis=0))
         for i, (d, g) in enumerate(items)]
    sa = [(jnp.where(strict2[d], s[i][0:n], 0.0), jnp.where(strict2[d], s[i][2 * n:3 * n], 0.0))
          for i, (d, g) in enumerate(items)]
    sr = [(jnp.where(incl2[d], s[i][n:2 * n], 0.0), jnp.where(incl2[d], s[i][3 * n:4 * n], 0.0))
          for i, (d, g) in enumerate(items)]
    vv = [jnp.concatenate([x, x], axis=0) for x in (pltpu.roll(x, HEAD_DIM, 1) for x in v_g)]
    akv = [_dot(jnp.where(first, 0.0, sa[i][j]), vv[i]) for i, j in heads]
    rhs = [jnp.where(first, at_h[i][0], akv[2 * i]) if j == 0 else jnp.where(first, akv[2 * i + 1], at_h[i][1])
           for i, j in heads]
    edge_rows = (halo_refs[0][0, HALO - 1:HALO, :], halo_refs[1][0, 0:1, :])

    def next_features(d):
        def work():
            for j, f in enumerate(_rwkv_features(z_refs[d][0], edge_rows[d], d, *feat_refs, bd_ref[...])):
                feat_ref[d, j] = f
        return work

    idx = range(len(items))
    a_pair = [jnp.where(first, sa[i][0], pltpu.roll(sa[i][1], HEAD_DIM, 1)) for i in idx]
    sol = _unit_lower_solve(a_pair, [jnp.concatenate([rhs[2 * i], rhs[2 * i + 1]], axis=1) for i in idx],
                            diag_blk2, eye2, first,
                            side_work=[next_features(0), lambda: None, lambda: None, next_features(1)])
    pq = [sol[i // 2][:, (i % 2) * LANES:(i % 2 + 1) * LANES] for i in range(2 * len(items))]
    p_pair = [jnp.concatenate([jnp.where(first, pq[2 * i], 0.0), jnp.where(first, 0.0, pq[2 * i + 1])], axis=0)
              for i in idx]
    q_pair = [pltpu.roll(jnp.where(first, pq[2 * i + 1], pq[2 * i]), HEAD_DIM, 1) for i in idx]
    t_split = [jnp.concatenate(_split(states[i]), axis=0) for i in idx]
    pr = [jnp.concatenate([p_pair[i], rt_h[i][0], rt_h[i][1]], axis=0).astype(BF16) for i in idx]
    prt = [jnp.dot(jnp.concatenate([pr[i], pr[i]], axis=1), t_split[i], preferred_element_type=F32) for i in idx]
    pt = [m[0:2 * n] for m in prt]
    rtt = [m[2 * n:] for m in prt]
    u = [jnp.where(first, pt[i][0:n], pt[i][n:]) + q_pair[i] for i in idx]
    uv = [jnp.concatenate([u[i], v_g[i]], axis=0) for i in idx]
    yh = [_dot(sr[i][j], uv[i]) for i, j in heads]
    ys = [jnp.where(first, rtt[i][0:n] + yh[2 * i], rtt[i][n:] + yh[2 * i + 1]) for i in idx]
    t_new = [_dot(jnp.concatenate([scaled[d][4][:, sls[g]], scaled[d][5][:, sls[g]]], axis=0).T, uv[i])
             for i, (d, g) in enumerate(items)]
    for i, (d, g) in enumerate(items):
        w_col = jnp.broadcast_to(w_tot[d][:, sls[g]], (LANES, LANES)).T
        st_refs[d][0, g] = jnp.where(blockdiag, w_col * states[i] + t_new[i], 0.0)
    bd = bd_ref[...]
    inv = 1.0 / HEAD_DIM
    y = [jnp.concatenate(ys[d * npairs:(d + 1) * npairs], axis=1) for d in range(2)]
    mean = [_head_sums(y[d], bd) * inv for d in range(2)]
    dev = [y[d] - mean[d] for d in range(2)]
    var = [_head_sums(dev[d] * dev[d], bd) * inv for d in range(2)]
    bonus = [_head_sums(raw[d][0] * raw[d][1] * rk_ref[...], bd) * raw[d][2] for d in range(2)]
    for d in range(2):
        y_refs[d][0] = (dev[d] * lax.rsqrt(var[d] + GN_EPS) * gw_ref[...] + gb_ref[...] + bonus[d]).astype(BF16)


def _scan(zr, state_fwd, state_bwd, p):
    bsz, length, ncol = zr.shape
    width = p["w0"].shape[1]
    nc = length // CHUNK
    npairs = width // LANES
    per_chunk = CHUNK // HALO
    last_halo = length // HALO - 1
    fmap = lambda b, c: (b, c, 0)
    bmap = lambda b, c: (b, nc - 1 - c, 0)
    fnext = lambda c: jnp.minimum(c + 1, nc - 1)
    bnext = lambda c: jnp.maximum(nc - 2 - c, 0)
    fnext_map = lambda b, c: (b, fnext(c), 0)
    bnext_map = lambda b, c: (b, bnext(c), 0)
    prev_map = lambda b, c: (b, jnp.maximum(fnext(c) * per_chunk - 1, 0), 0)
    next_map = lambda b, c: (b, jnp.minimum((bnext(c) + 1) * per_chunk, last_halo), 0)
    full = lambda shape: pl.BlockSpec(shape, lambda b, c: (0,) * len(shape))
    st_spec = pl.BlockSpec((1, npairs, LANES, LANES), lambda b, c: (b, 0, 0, 0))
    y_shape = jax.ShapeDtypeStruct((bsz, length, width), BF16)
    st_shape = jax.ShapeDtypeStruct((bsz, npairs, LANES, LANES), F32)
    y_f, y_b, st_f, st_b = pl.pallas_call(
        _scan_kernel,
        grid=(bsz, nc),
        in_specs=[pl.BlockSpec((1, CHUNK, ncol), fnext_map), pl.BlockSpec((1, HALO, ncol), prev_map),
                  pl.BlockSpec((1, CHUNK, ncol), bnext_map), pl.BlockSpec((1, HALO, ncol), next_map),
                  pl.BlockSpec((1, CHUNK, ncol), lambda b, c: (b, 0, 0)),
                  pl.BlockSpec((1, CHUNK, ncol), lambda b, c: (b, nc - 1, 0)),
                  st_spec, st_spec,
                  full((2, 3, width)), full((1, LANES)), full((1, LANES)), full((2, width)),
                  full((2, LANES, width)), full((2, width)), full((2, LANES, width)), full((1, width)),
                  full((1, width)), full((1, width)), full((1, width)), full((1, width)), full((LANES, LANES))],
        out_specs=[pl.BlockSpec((1, CHUNK, width), fmap), pl.BlockSpec((1, CHUNK, width), bmap), st_spec, st_spec],
        out_shape=[y_shape, y_shape, st_shape, st_shape],
        scratch_shapes=[pltpu.VMEM((2, 6, CHUNK, width), F32)],
        compiler_params=_params("parallel", "arbitrary"),
        name="wkv_scan",
    )(zr, zr, zr, zr, zr, zr, state_fwd, state_bwd, p["mu"], p["mu_w"], p["mu_a"], p["w0"], p["w2"], p["a0"], p["a2"],
      p["k_k"], p["k_a"], p["r_k"], p["gn_w"], p["gn_b"], p["bd"])
    return (y_f, y_b), (st_f, st_b)


def _mix_kernel(period, tm, yf_ref, yb_ref, glo_ref, zc_ref, g2_ref, cw_ref, o_ref):
    gate = _dot(jax.nn.sigmoid(glo_ref[0]), g2_ref[...])
    yr = (yf_ref[0].astype(F32) + yb_ref[0].astype(F32)) * gate
    zc = zc_ref[0].astype(F32)
    u, gate_b, gate_c = zc[:, 0:256], zc[:, 256:512], zc[:, 512:768]
    hc = gate_c * u
    pos = lax.broadcasted_iota(jnp.int32, (tm, 1), 0) % period
    prev = jnp.where(pos == 0, 0.0, pltpu.roll(hc, 1, 0))
    nxt = jnp.where(pos == period - 1, 0.0, pltpu.roll(hc, tm - 1, 0))
    conv = prev * cw_ref[0:1, :] + hc * cw_ref[1:2, :] + nxt * cw_ref[2:3, :]
    o_ref[0, :, 0:512] = yr.astype(BF16)
    o_ref[0, :, 512:768] = (gate_b * conv).astype(BF16)


def _mix(ys_f, ys_b, zr, zc, g2, conv_w, period, tm):
    bsz, length, _ = zc.shape
    row = lambda b, i: (b, i, 0)
    glo_block = (N_RWKV_COLS - 128) // 128
    return pl.pallas_call(
        functools.partial(_mix_kernel, period, tm),
        grid=(bsz, length // tm),
        in_specs=[pl.BlockSpec((1, tm, 512), row), pl.BlockSpec((1, tm, 512), row),
                  pl.BlockSpec((1, tm, 128), lambda b, i: (b, i, glo_block)),
                  pl.BlockSpec((1, tm, N_CONV_COLS), row),
                  pl.BlockSpec((128, 512), lambda b, i: (0, 0)),
                  pl.BlockSpec((3, 256), lambda b, i: (0, 0))],
        out_specs=pl.BlockSpec((1, tm, 768), row),
        out_shape=jax.ShapeDtypeStruct((bsz, length, 768), BF16),
        compiler_params=_params("parallel", "parallel"),
        name="gate_conv",
    )(ys_f, ys_b, zr, zc, g2, conv_w)


def _chan_dft_kernel(f_ref, w_ref, o_ref):
    xcs = jnp.dot(f_ref[0].astype(BF16), w_ref[...], preferred_element_type=F32)
    o_ref[0, 0] = xcs[:, 0:256].astype(BF16)
    o_ref[0, 1] = xcs[:, 256:512].astype(BF16)


def _pos_dft_kernel(nb, cs_ref, x_ref, o_ref, acc_ref):
    kstep = pl.program_id(1)

    @pl.when(kstep == 0)
    def _():
        acc_ref[...] = jnp.zeros_like(acc_ref)

    cs = cs_ref[...]
    for b in range(nb):
        acc_ref[b] += jnp.dot(cs, x_ref[b], preferred_element_type=F32)

    @pl.when(kstep == pl.num_programs(1) - 1)
    def _():
        o_ref[...] = acc_ref[...].astype(BF16)


def _fourier(zf, chan_w, pos_cs, tm, tk):
    bsz, length, ch = zf.shape
    tr = min(512, length)
    xcs = pl.pallas_call(
        _chan_dft_kernel,
        grid=(bsz, length // tr),
        in_specs=[pl.BlockSpec((1, tr, ch), lambda b, i: (b, i, 0)),
                  pl.BlockSpec((ch, 2 * ch), lambda b, i: (0, 0))],
        out_specs=pl.BlockSpec((1, 2, tr, ch), lambda b, i: (b, 0, i, 0)),
        out_shape=jax.ShapeDtypeStruct((bsz, 2, length, ch), BF16),
        compiler_params=_params("parallel", "parallel"),
        name="chan_dft",
    )(zf, chan_w)
    xcs = xcs.reshape(bsz, 2 * length, ch)
    return pl.pallas_call(
        functools.partial(_pos_dft_kernel, bsz),
        grid=(length // tm, 2 * length // tk),
        in_specs=[pl.BlockSpec((tm, tk), lambda i, kk: (i, kk)),
                  pl.BlockSpec((bsz, tk, ch), lambda i, kk: (0, kk, 0))],
        out_specs=pl.BlockSpec((bsz, tm, ch), lambda i, kk: (0, i, 0)),
        out_shape=jax.ShapeDtypeStruct((bsz, length, ch), BF16),
        scratch_shapes=[pltpu.VMEM((bsz, tm, ch), F32)],
        compiler_params=_params("parallel", "arbitrary"),
        name="pos_dft",
    )(pos_cs, xcs)


def _dft_tables(length):
    m = jnp.arange(length, dtype=jnp.int32)[None, :]
    l1 = jnp.arange(length // 64, dtype=jnp.int32)[:, None] * 64
    l2 = jnp.arange(64, dtype=jnp.int32)[:, None]
    ang = lambda l: ((l * m) % length).astype(F32) * (2.0 * jnp.pi / length)
    c1, s1, c2, s2 = jnp.cos(ang(l1)), jnp.sin(ang(l1)), jnp.cos(ang(l2)), jnp.sin(ang(l2))
    scale = 1.0 / jnp.sqrt(jnp.float32(length))
    cos = (c1[:, None, :] * c2[None] - s1[:, None, :] * s2[None]).reshape(length, length) * scale
    sin = (s1[:, None, :] * c2[None] + c1[:, None, :] * s2[None]).reshape(length, length) * scale
    return jnp.concatenate([cos, sin], axis=1).astype(BF16)


def _chan_dft_weights(ngroups):
    j = jnp.arange(HEAD_DIM, dtype=jnp.int32)
    ang = ((j[:, None] * j[None, :]) % HEAD_DIM).astype(F32) * (2.0 * jnp.pi / HEAD_DIM)
    eye = jnp.eye(ngroups, dtype=F32)
    scale = 1.0 / jnp.sqrt(jnp.float32(HEAD_DIM))
    cc = jnp.kron(eye, jnp.cos(ang) * scale)
    sc = jnp.kron(eye, jnp.sin(ang) * scale)
    return jnp.concatenate([cc, -sc], axis=1).astype(BF16)


def _outproj_kernel(yrc_ref, yf_ref, w_ref, x_ref, gn1_ref, gate_ref, gn2_ref, sc_ref, sh_ref, xo_ref, h_ref):
    nrc = yrc_ref.shape[2]
    o = (jnp.dot(yrc_ref[0], w_ref[0:nrc, :], preferred_element_type=F32) +
         jnp.dot(yf_ref[0], w_ref[nrc:, :], preferred_element_type=F32))
    xn = x_ref[0] + gate_ref[0] * _rms(o, gn1_ref[...])
    xo_ref[0] = xn
    h_ref[0] = (_rms(xn, gn2_ref[...]) * (1.0 + sc_ref[0]) + sh_ref[0]).astype(BF16)


def _outproj(yrc, yf, w, x, gn1, gate, gn2, sc, sh, tm):
    bsz, length, d = x.shape
    row = lambda b, i: (b, i, 0)
    vec = pl.BlockSpec((1, d), lambda b, i: (0, 0))
    mod = pl.BlockSpec((1, 1, d), lambda b, i: (b, 0, 0))
    return pl.pallas_call(
        _outproj_kernel,
        grid=(bsz, length // tm),
        in_specs=[pl.BlockSpec((1, tm, yrc.shape[2]), row), pl.BlockSpec((1, tm, yf.shape[2]), row),
                  pl.BlockSpec(w.shape, lambda b, i: (0, 0)), pl.BlockSpec((1, tm, d), row),
                  vec, mod, vec, mod, mod],
        out_specs=[pl.BlockSpec((1, tm, d), row), pl.BlockSpec((1, tm, d), row)],
        out_shape=[jax.ShapeDtypeStruct((bsz, length, d), F32), jax.ShapeDtypeStruct((bsz, length, d), BF16)],
        compiler_params=_params("parallel", "parallel"),
        name="outproj",
    )(yrc, yf, w, x, gn1, gate, gn2, sc, sh)


FF_SUB = 256


def _swiglu_partial(h, wg, wu, wd, tf):
    def up(c, n):
        return (jnp.dot(h, wg(c, n), preferred_element_type=F32), jnp.dot(h, wu(c, n), preferred_element_type=F32))

    def down(gu, c, n):
        return jnp.dot((_silu(gu[0]) * gu[1]).astype(BF16), wd(c, n), preferred_element_type=F32)

    out = None
    prev = None
    for cut in [(c, min(FF_SUB, tf - c)) for c in range(0, tf, FF_SUB)]:
        gu = up(*cut)
        if prev is not None:
            part = down(*prev)
            out = part if out is None else out + part
        prev = (gu,) + cut
    part = down(*prev)
    return part if out is None else out + part


def _ffn_kernel(h_ref, wg_ref, wu_ref, wd_ref, x_ref, gn_ref, gate_ref, o_ref, acc_ref):
    j = pl.program_id(2)

    @pl.when(j == 0)
    def _():
        acc_ref[...] = jnp.zeros_like(acc_ref)

    acc_ref[...] += _swiglu_partial(h_ref[0], lambda c, n: wg_ref[:, c:c + n], lambda c, n: wu_ref[:, c:c + n],
                                    lambda c, n: wd_ref[c:c + n, :], wg_ref.shape[1])

    @pl.when(j == pl.num_programs(2) - 1)
    def _():
        o_ref[0] = x_ref[0] + gate_ref[0] * _rms(acc_ref[...], gn_ref[...])


def _ffn(h, wg, wu, wd, x, gn, gate, tm, tf):
    bsz, length, d = x.shape
    ff = wg.shape[1]
    row = lambda b, i, j: (b, i, 0)
    return pl.pallas_call(
        _ffn_kernel,
        grid=(bsz, length // tm, ff // tf),
        in_specs=[pl.BlockSpec((1, tm, d), row),
                  pl.BlockSpec((d, tf), lambda b, i, j: (0, j)),
                  pl.BlockSpec((d, tf), lambda b, i, j: (0, j)),
                  pl.BlockSpec((tf, d), lambda b, i, j: (j, 0)),
                  pl.BlockSpec((1, tm, d), row),
                  pl.BlockSpec((1, d), lambda b, i, j: (0, 0)),
                  pl.BlockSpec((1, 1, d), lambda b, i, j: (b, 0, 0))],
        out_specs=pl.BlockSpec((1, tm, d), row),
        out_shape=jax.ShapeDtypeStruct((bsz, length, d), F32),
        scratch_shapes=[pltpu.VMEM((tm, d), F32)],
        compiler_params=_params("parallel", "parallel", "arbitrary"),
        name="ffn",
    )(h, wg, wu, wd, x, gn, gate)


MOE_TILE = 512
MOE_BLOCK = 128
MOE_ALIGN = 16
MOE_TM = 1024


def _router_kernel(x_ref, gn_ref, sc_ref, sh_ref, w_ref, b_ref, tri_ref, routes_ref, routest_ref, cnt_ref):
    h = _rms(x_ref[0], gn_ref[...]) * (1.0 + sc_ref[0]) + sh_ref[0]
    logits = jnp.dot(h, w_ref[...], preferred_element_type=F32, precision=lax.Precision.HIGHEST) + b_ref[...]
    ne = float(logits.shape[1])
    idx = lax.broadcasted_iota(jnp.int32, logits.shape, 1).astype(F32)
    m1 = jnp.max(logits, axis=1, keepdims=True)
    i1 = jnp.min(jnp.where(logits == m1, idx, ne), axis=1, keepdims=True)
    rest = jnp.where(idx == i1, -jnp.inf, logits)
    m2 = jnp.max(rest, axis=1, keepdims=True)
    i2 = jnp.min(jnp.where(rest == m2, idx, ne), axis=1, keepdims=True)
    e2 = jnp.exp(m2 - m1)
    p1 = 1.0 / (1.0 + e2)
    p2 = e2 / (1.0 + e2)
    sel = jnp.where(idx == i1, 1.0, jnp.where(idx == i2, 1.0, 0.0))
    before = jnp.dot(tri_ref[...], sel.astype(BF16), preferred_element_type=F32)
    rank1 = jnp.sum(jnp.where(idx == i1, before, 0.0), axis=1, keepdims=True)
    rank2 = jnp.sum(jnp.where(idx == i2, before, 0.0), axis=1, keepdims=True)
    routes = jnp.zeros_like(logits)
    for lane, val in enumerate((i1, i2, rank1, rank2, p1, p2)):
        routes = jnp.where(idx == float(lane), val, routes)
    routes_ref[0] = routes
    routest_ref[0] = jnp.transpose(routes)[0:N_EXPERTS, :]
    cnt_ref[0, 0] = jnp.sum(sel, axis=0, keepdims=True)


def _router(x, gn, sc, sh, w, b):
    bsz, length, d = x.shape
    tm = MOE_TILE
    pad = LANES - w.shape[1]
    w = jnp.pad(w, ((0, 0), (0, pad)))
    b = jnp.pad(b, ((0, 0), (0, pad)), constant_values=-1e30)
    ne = LANES
    t = jnp.arange(tm, dtype=jnp.int32)
    tri = (t[None, :] < t[:, None]).astype(BF16)
    row = lambda bb, i: (bb, i, 0)
    vec = pl.BlockSpec((1, d), lambda bb, i: (0, 0))
    mod = pl.BlockSpec((1, 1, d), lambda bb, i: (bb, 0, 0))
    nt = length // tm
    return pl.pallas_call(
        _router_kernel,
        grid=(bsz, nt),
        in_specs=[pl.BlockSpec((1, tm, d), row), vec, mod, mod,
                  pl.BlockSpec((d, ne), lambda bb, i: (0, 0)), pl.BlockSpec((1, ne), lambda bb, i: (0, 0)),
                  pl.BlockSpec((tm, tm), lambda bb, i: (0, 0))],
        out_specs=[pl.BlockSpec((1, tm, ne), row),
                   pl.BlockSpec((1, N_EXPERTS, tm), lambda bb, i: (bb, 0, i)),
                   pl.BlockSpec((1, 1, 1, ne), lambda bb, i: (bb, i, 0, 0))],
        out_shape=[jax.ShapeDtypeStruct((bsz, length, ne), F32),
                   jax.ShapeDtypeStruct((bsz, N_EXPERTS, length), F32),
                   jax.ShapeDtypeStruct((bsz, nt, 1, ne), F32)],
        compiler_params=_params("parallel", "parallel"),
        name="router",
    )(x, gn, sc, sh, w, b, tri)


_MOE_SHIFT = MOE_BLOCK.bit_length() - 1
_MOE_BUF_BLOCKS = (2 * MOE_TILE + N_EXPERTS * (MOE_ALIGN - 1 + MOE_BLOCK - 1)) // MOE_BLOCK + 1


def _align_rows(c):
    return ((c + (MOE_ALIGN - 1)) >> (MOE_ALIGN.bit_length() - 1)) << (MOE_ALIGN.bit_length() - 1)


def _blocks(c):
    return (c + (MOE_BLOCK - 1)) >> _MOE_SHIFT


def _dispatch_copies(start, tile, off_ref, cnt_ref, buf_ref, hs_ref, sem):
    def run(src_row, dst_row, nrows):
        cp = pltpu.make_async_copy(buf_ref.at[pl.ds(pl.multiple_of(src_row, MOE_ALIGN), nrows)],
                                   hs_ref.at[pl.ds(pl.multiple_of(dst_row, MOE_ALIGN), nrows)], sem)
        if start:
            cp.start()
        else:
            cp.wait()

    slot = jnp.int32(0)
    for e in range(N_EXPERTS):
        rows = _align_rows(cnt_ref[tile * N_EXPERTS + e])
        dst0 = off_ref[tile * N_EXPERTS + e]
        src0 = slot * MOE_BLOCK
        nfull = rows >> _MOE_SHIFT

        def full(rb, carry, src0=src0, dst0=dst0):
            run(src0 + rb * MOE_BLOCK, dst0 + rb * MOE_BLOCK, MOE_BLOCK)
            return carry

        lax.fori_loop(0, nfull, full, 0)
        done = nfull * MOE_BLOCK
        piece = MOE_BLOCK // 2
        while piece >= MOE_ALIGN:
            has = (rows & piece) != 0

            @pl.when(has)
            def _(done=done, piece=piece, src0=src0, dst0=dst0):
                run(src0 + done, dst0 + done, piece)

            done = done + jnp.where(has, piece, 0)
            piece //= 2
        slot = slot + _blocks(rows)


def _staging_rows(tile, cnt_ref, i1, i2, rank1, rank2):
    row1 = jnp.zeros_like(i1)
    row2 = jnp.zeros_like(i2)
    slot = jnp.int32(0)
    for e in range(N_EXPERTS):
        first_row = (slot * MOE_BLOCK).astype(F32)
        row1 = jnp.where(i1 == float(e), first_row, row1)
        row2 = jnp.where(i2 == float(e), first_row, row2)
        slot = slot + _blocks(cnt_ref[tile * N_EXPERTS + e])
    return row1 + rank1, row2 + rank2


def _dispatch_kernel(off_ref, cnt_ref, h_ref, routest_ref, hs_init_ref, hs_ref, buf_ref, sem):
    del hs_init_ref
    tile = pl.program_id(0)
    rt = routest_ref[0]
    row1, row2 = _staging_rows(tile, cnt_ref, rt[0:1], rt[1:2], rt[2:3], rt[3:4])
    rows = lax.broadcasted_iota(jnp.int32, (MOE_BLOCK, 1), 0)
    onehot = jnp.concatenate(
        [jnp.where(row1 == (rows + blk * MOE_BLOCK).astype(F32), 1.0,
                   jnp.where(row2 == (rows + blk * MOE_BLOCK).astype(F32), 1.0, 0.0)).astype(BF16)
         for blk in range(_MOE_BUF_BLOCKS)], axis=0)
    buf_ref[...] = jnp.dot(onehot, h_ref[...], preferred_element_type=F32).astype(BF16)
    _dispatch_copies(True, tile, off_ref, cnt_ref, buf_ref, hs_ref, sem)
    _dispatch_copies(False, tile, off_ref, cnt_ref, buf_ref, hs_ref, sem)


def _dispatch(h, rankt, off, cnt, nrows):
    n, d = h.shape
    ntb = rankt.shape[2] // MOE_TILE
    grid_spec = pltpu.PrefetchScalarGridSpec(
        num_scalar_prefetch=2, grid=(n // MOE_TILE,),
        in_specs=[pl.BlockSpec((MOE_TILE, d), lambda t, o, c: (t, 0)),
                  pl.BlockSpec((1, N_EXPERTS, MOE_TILE), lambda t, o, c: (t // ntb, 0, t % ntb)),
                  pl.BlockSpec(memory_space=pl.ANY)],
        out_specs=pl.BlockSpec(memory_space=pl.ANY),
        scratch_shapes=[pltpu.VMEM((_MOE_BUF_BLOCKS * MOE_BLOCK, d), BF16), pltpu.SemaphoreType.DMA(())])
    return pl.pallas_call(
        _dispatch_kernel, grid_spec=grid_spec,
        out_shape=jax.ShapeDtypeStruct((nrows, d), BF16),
        input_output_aliases={4: 0},
        compiler_params=_params("arbitrary"),
        name="moe_dispatch",
    )(off, cnt, h, rankt, jnp.zeros((nrows, d), BF16))


def _gffn_kernel(te_ref, tv_ref, h_ref, wg_ref, wu_ref, wd_ref, o_ref, acc_ref):
    q = pl.program_id(0)
    j = pl.program_id(1)

    @pl.when(j == 0)
    def _():
        acc_ref[...] = jnp.zeros_like(acc_ref)

    @pl.when(tv_ref[q] > 0)
    def _():
        acc_ref[...] += _swiglu_partial(h_ref[...], lambda c, n: wg_ref[0, :, c:c + n],
                                        lambda c, n: wu_ref[0, :, c:c + n], lambda c, n: wd_ref[0, c:c + n, :],
                                        wg_ref.shape[2])

    @pl.when(j == pl.num_programs(1) - 1)
    def _():
        o_ref[...] = acc_ref[...].astype(BF16)


def _gffn(hs, te, tv, wg, wu, wd, tf):
    nrows, d = hs.shape
    ff = wg.shape[2]
    grid_spec = pltpu.PrefetchScalarGridSpec(
        num_scalar_prefetch=2, grid=(nrows // MOE_TM, ff // tf),
        in_specs=[pl.BlockSpec((MOE_TM, d), lambda q, j, te, tv: (q, 0)),
                  pl.BlockSpec((1, d, tf), lambda q, j, te, tv: (te[q], 0, j)),
                  pl.BlockSpec((1, d, tf), lambda q, j, te, tv: (te[q], 0, j)),
                  pl.BlockSpec((1, tf, d), lambda q, j, te, tv: (te[q], j, 0))],
        out_specs=pl.BlockSpec((MOE_TM, d), lambda q, j, te, tv: (q, 0)),
        scratch_shapes=[pltpu.VMEM((MOE_TM, d), F32)])
    return pl.pallas_call(
        _gffn_kernel, grid_spec=grid_spec,
        out_shape=jax.ShapeDtypeStruct((nrows, d), BF16),
        compiler_params=_params("arbitrary", "arbitrary"),
        name="moe_experts",
    )(te, tv, hs, wg, wu, wd)


def _combine_copies(start, tile, off_ref, cnt_ref, ys_ref, buf_ref, sem):
    slot = jnp.int32(0)
    for e in range(N_EXPERTS):
        nb = _blocks(cnt_ref[tile * N_EXPERTS + e])
        src0 = off_ref[tile * N_EXPERTS + e]

        def body(rb, carry, slot=slot, src0=src0):
            cp = pltpu.make_async_copy(
                ys_ref.at[pl.ds(pl.multiple_of(src0 + rb * MOE_BLOCK, MOE_ALIGN), MOE_BLOCK)],
                buf_ref.at[pl.ds(pl.multiple_of((slot + rb) * MOE_BLOCK, MOE_BLOCK), MOE_BLOCK)], sem)
            if start:
                cp.start()
            else:
                cp.wait()
            return carry

        lax.fori_loop(0, nb, body, 0)
        slot = slot + nb


def _combine_kernel(off_ref, cnt_ref, ys_ref, routes_ref, x_ref, gn_ref, gate_ref, o_ref, buf_ref, sem):
    tile = pl.program_id(0)

    @pl.when(tile == 0)
    def _():
        buf_ref[...] = jnp.zeros_like(buf_ref)

    _combine_copies(True, tile, off_ref, cnt_ref, ys_ref, buf_ref, sem)
    routes = routes_ref[0]
    col = lambda j: routes[:, j:j + 1]
    row1, row2 = _staging_rows(tile, cnt_ref, col(0), col(1), col(2), col(3))
    lanes = lax.broadcasted_iota(jnp.int32, (1, MOE_BLOCK), 1)
    scatter = jnp.concatenate(
        [(jnp.where(row1 == (lanes + blk * MOE_BLOCK).astype(F32), col(4), 0.0) +
          jnp.where(row2 == (lanes + blk * MOE_BLOCK).astype(F32), col(5), 0.0)).astype(BF16)
         for blk in range(_MOE_BUF_BLOCKS)], axis=1)
    _combine_copies(False, tile, off_ref, cnt_ref, ys_ref, buf_ref, sem)
    mixed = jnp.dot(scatter, buf_ref[...], preferred_element_type=F32)
    o_ref[0] = x_ref[0] + gate_ref[0] * _rms(mixed, gn_ref[...])


def _combine(ys, routes, x, gn, gate, off, cnt):
    bsz, length, d = x.shape
    ntb = length // MOE_TILE
    tok = lambda t, o, c: (t // ntb, t % ntb, 0)
    grid_spec = pltpu.PrefetchScalarGridSpec(
        num_scalar_prefetch=2, grid=(bsz * ntb,),
        in_specs=[pl.BlockSpec(memory_space=pl.ANY),
                  pl.BlockSpec((1, MOE_TILE, LANES), tok),
                  pl.BlockSpec((1, MOE_TILE, d), tok),
                  pl.BlockSpec((1, d), lambda t, o, c: (0, 0)),
                  pl.BlockSpec((1, 1, d), lambda t, o, c: (t // ntb, 0, 0))],
        out_specs=pl.BlockSpec((1, MOE_TILE, d), tok),
        scratch_shapes=[pltpu.VMEM((_MOE_BUF_BLOCKS * MOE_BLOCK, d), BF16), pltpu.SemaphoreType.DMA(())])
    return pl.pallas_call(
        _combine_kernel, grid_spec=grid_spec,
        out_shape=jax.ShapeDtypeStruct((bsz, length, d), F32),
        compiler_params=_params("arbitrary"),
        name="moe_combine",
    )(off, cnt, ys, routes, x, gn, gate)


def _route_plan(counts, ntokens):
    ntiles = counts.shape[0]
    rows = (counts + (MOE_ALIGN - 1)) // MOE_ALIGN * MOE_ALIGN
    seg = (rows.sum(0) + MOE_BLOCK + MOE_TM - 1) // MOE_TM * MOE_TM
    seg_end = jnp.cumsum(seg)
    off = (seg_end - seg)[None, :] + jnp.cumsum(rows, axis=0) - rows
    nq = -(-(2 * ntokens + ntiles * N_EXPERTS * (MOE_ALIGN - 1) + N_EXPERTS * (MOE_BLOCK + MOE_TM - 1)) // MOE_TM)
    q = jnp.arange(nq, dtype=jnp.int32) * MOE_TM
    te = jnp.minimum(jnp.sum(q[:, None] >= seg_end[None, :], axis=1), N_EXPERTS - 1).astype(jnp.int32)
    tv = (q < seg_end[-1]).astype(jnp.int32)
    return off.reshape(-1).astype(jnp.int32), te, tv, nq * MOE_TM


def _moe(h, x, gn_pre, sc, sh, router_w, router_b, wg, wu, wd, gn_post, gate, tf):
    bsz, length, d = x.shape
    routes, routest, cnt = _router(x, gn_pre, sc, sh, router_w, router_b)
    counts = cnt[:, :, 0, :N_EXPERTS].astype(jnp.int32).reshape(-1, N_EXPERTS)
    off, te, tv, nrows = _route_plan(counts, bsz * length)
    cnt_flat = counts.reshape(-1)
    hs = _dispatch(h.reshape(bsz * length, d), routest, off, cnt_flat, nrows)
    ys = _gffn(hs, te, tv, wg, wu, wd, tf)
    return _combine(ys, routes, x, gn_post, gate, off, cnt_flat)


def _sincos_2d(rows, cols, dim):
    quarter = dim // 4
    omega = 1.0 / (POS_BASE ** (jnp.arange(quarter, dtype=F32) / quarter))

    def axis_emb(n):
        ang = jnp.arange(n, dtype=F32)[:, None] * omega[None, :]
        return jnp.concatenate([jnp.sin(ang), jnp.cos(ang)], axis=-1)

    er, ec = axis_emb(rows), axis_emb(cols)
    emb = jnp.concatenate([jnp.broadcast_to(er[:, None, :], (rows, cols, dim // 2)),
                           jnp.broadcast_to(ec[None, :, :], (rows, cols, dim // 2))], axis=-1)
    return emb.reshape(rows * cols, dim)


def _head_block_ones(width):
    h = jnp.arange(width, dtype=jnp.int32) // HEAD_DIM
    return (h[:, None] == h[None, :]).astype(BF16)


def _pad_lora(w2, d):
    z = jnp.zeros_like(w2[0])
    return jnp.concatenate([w2[0], z], axis=0) if d == 0 else jnp.concatenate([z, w2[1]], axis=0)


def _tile(length, pref):
    return pref if length % pref == 0 else length


def kernel(x, c, ctx, c_ctx, ada_w, ada_b, norm_g, w_in, w_out, rwkv_mu, rwkv_mu_w, rwkv_mu_a, rwkv_w0, rwkv_w2,
           rwkv_a0, rwkv_a2, rwkv_g2, rwkv_k_k, rwkv_k_a, rwkv_r_k, rwkv_gn_w, rwkv_gn_b, conv_w, ffn_w_gate,
           ffn_w_up, ffn_w_down, router_w, router_b, moe_w_gate, moe_w_up, moe_w_down):
    bsz, length, dim = x.shape
    ctx_len = ctx.shape[1]
    depth = ada_w.shape[0]
    d_rwkv = rwkv_w0.shape[-1]

    pos = _sincos_2d(length // GRID_W, GRID_W, dim).astype(x.dtype)
    bd = _head_block_ones(LANES)
    chan_w = _chan_dft_weights(N_FOUR_COLS // HEAD_DIM)
    dft_lat = _dft_tables(length)
    dft_ctx = _dft_tables(ctx_len)

    cvec = jnp.zeros((16, dim), F32).at[:bsz].set(c).at[bsz].set(c_ctx)
    mod = _adaln(cvec, ada_w, ada_b)
    zero_state = jnp.zeros((bsz, d_rwkv // LANES, LANES, LANES), F32)
    xc = ctx

    def mods(l, lo, hi):
        return [mod[l, lo:hi, i * dim:(i + 1) * dim][:, None, :] for i in range(6)]

    def mixers(l, zr, zc, zf, states0, period, dft, need_y=True):
        tm = _tile(zr.shape[1], 512)
        p = {"mu": rwkv_mu[l], "mu_w": rwkv_mu_w[l].reshape(1, -1), "mu_a": rwkv_mu_a[l].reshape(1, -1),
             "w0": rwkv_w0[l], "w2": jnp.stack([_pad_lora(rwkv_w2[l], d) for d in range(2)]), "a0": rwkv_a0[l],
             "a2": jnp.stack([_pad_lora(rwkv_a2[l], d) for d in range(2)]), "k_k": rwkv_k_k[l][None],
             "k_a": rwkv_k_a[l][None], "r_k": rwkv_r_k[l].reshape(1, -1), "gn_w": rwkv_gn_w[l][None],
             "gn_b": rwkv_gn_b[l][None], "bd": bd}
        ys, states = _scan(zr, states0[0], states0[1], p)
        if not need_y:
            return None, None, states
        yrc = _mix(ys[0], ys[1], zr, zc, rwkv_g2[l], conv_w[l], period, tm)
        yf = _fourier(zf, chan_w, dft, _tile(zf.shape[1], 512), 512)
        return yrc, yf, states

    def channel_mixer(l, h2, xn, gn3, gate2, sc2, sh2, gn2):
        tm = _tile(xn.shape[1], 512)
        i = l // 2
        if l % 2 == 0:
            return _ffn(h2, ffn_w_gate[i].astype(BF16), ffn_w_up[i].astype(BF16), ffn_w_down[i].astype(BF16),
                        xn, gn3, gate2, tm, 1408)
        return _moe(h2, xn, gn2, sc2, sh2, router_w[i], router_b[i][None], moe_w_gate[i].astype(BF16),
                    moe_w_up[i].astype(BF16), moe_w_down[i].astype(BF16), gn3, gate2, 1792)

    for l in range(depth):
        last = l == depth - 1
        w_in_l = w_in[l].astype(BF16)
        w_out_l = w_out[l].astype(BF16)
        gn = [norm_g[l, i][None] for i in range(4)]
        sh1, sc1, g1, sh2, sc2, g2 = mods(l, 0, bsz)
        csh1, csc1, cg1, csh2, csc2, cg2 = mods(l, bsz, bsz + 1)

        xc_flat = xc.reshape(1, bsz * ctx_len, dim)
        tmc = _tile(bsz * ctx_len, 512)
        zr, zc, zf = _inproj(xc_flat, None, gn[0], csc1, csh1, w_in_l, tmc)
        unflat = lambda t: t.reshape(bsz, ctx_len, t.shape[-1])
        yrc, yf, ctx_states = mixers(l, unflat(zr), unflat(zc), unflat(zf), (zero_state, zero_state), ctx_len,
                                     dft_ctx, need_y=not last)
        if not last:
            flat = lambda t: t.reshape(1, bsz * ctx_len, t.shape[-1])
            xcn, hc2 = _outproj(flat(yrc), flat(yf), w_out_l, xc_flat, gn[1], cg1, gn[2], csc2, csh2, tmc)
            xc = channel_mixer(l, hc2, xcn, gn[3], cg2, csc2, csh2, gn[2]).reshape(bsz, ctx_len, dim)

        tm = _tile(length, 512)
        if l == 0:
            zr, zc, zf, x = _inproj(x, pos, gn[0], sc1, sh1, w_in_l, tm)
        else:
            zr, zc, zf = _inproj(x, None, gn[0], sc1, sh1, w_in_l, tm)
        yrc, yf, _ = mixers(l, zr, zc, zf, ctx_states, GRID_W, dft_lat)
        xn, h2 = _outproj(yrc, yf, w_out_l, x, gn[1], g1, gn[2], sc2, sh2, _tile(length, 512))
        x = channel_mixer(l, h2, xn, gn[3], g2, sc2, sh2, gn[2])
    return x
```

```python
import functools

import jax
import jax.numpy as jnp
from jax import lax
from jax.experimental import pallas as pl
from jax.experimental.pallas import tpu as pltpu

F32 = jnp.float32
BF16 = jnp.bfloat16

HEAD_DIM = 64
GRID_W = 64
NORM_EPS = 1e-6
GN_EPS = 64e-5
POS_BASE = 10000.0
DECAY_SCALE = 0.6065306597126334
NORM_FLOOR = 1e-12
N_EXPERTS = 8
CHUNK = 64
LANES = 128
VMEM_LIMIT = 56 * 1024 * 1024


def _params(*sem):
    return pltpu.CompilerParams(dimension_semantics=sem, vmem_limit_bytes=VMEM_LIMIT)


def _dot(a, b):
    return jnp.dot(a.astype(BF16), b.astype(BF16), preferred_element_type=F32)


def _dot_nt(a, b):
    return lax.dot_general(a.astype(BF16), b.astype(BF16), (((1,), (1,)), ((), ())),
                           preferred_element_type=F32)


def _split(x):
    hi = x.astype(BF16)
    lo = (x - hi.astype(F32)).astype(BF16)
    return hi, lo


def _head_sums(x, ones_pair):
    rows = x.shape[0]
    tiles = x.shape[1] // LANES
    hi, lo = _split(x)
    stacked = jnp.concatenate([t[:, g * LANES:(g + 1) * LANES] for t in (hi, lo) for g in range(tiles)], axis=0)
    s = jnp.dot(stacked, ones_pair, preferred_element_type=F32)
    return jnp.concatenate([s[g * rows:(g + 1) * rows] + s[(tiles + g) * rows:(tiles + g + 1) * rows]
                            for g in range(tiles)], axis=1)


def _dot_split_rhs(w, x):
    hi, lo = _split(x)
    return (jnp.dot(w, hi, preferred_element_type=F32) + jnp.dot(w, lo, preferred_element_type=F32))


def _rms(x, g):
    ms = jnp.mean(x * x, axis=-1, keepdims=True)
    return x * lax.rsqrt(ms + NORM_EPS) * g


def _silu(x):
    return x * jax.nn.sigmoid(x)


def _adaln_kernel(c_ref, w_ref, b_ref, o_ref):
    s = _silu(c_ref[...])
    o_ref[0] = _dot(s, w_ref[0]) + b_ref[0]


def _adaln(cvec, ada_w, ada_b):
    nl, d, n = ada_w.shape
    rows = cvec.shape[0]
    tn = 1536
    return pl.pallas_call(
        _adaln_kernel,
        grid=(nl, n // tn),
        in_specs=[pl.BlockSpec((rows, d), lambda l, j: (0, 0)),
                  pl.BlockSpec((1, d, tn), lambda l, j: (l, 0, j)),
                  pl.BlockSpec((1, 1, tn), lambda l, j: (l, 0, j))],
        out_specs=pl.BlockSpec((1, rows, tn), lambda l, j: (l, 0, j)),
        out_shape=jax.ShapeDtypeStruct((nl, rows, n), F32),
        compiler_params=_params("arbitrary", "arbitrary"),
        name="adaln",
    )(cvec, ada_w, ada_b.reshape(nl, 1, n))


N_RWKV_COLS = 1920
N_CONV_COLS = 768
N_FOUR_COLS = 256


def _inproj_kernel(add_pos, *refs):
    if add_pos:
        x_ref, pos_ref, g_ref, sc_ref, sh_ref, w_ref, zr_ref, zc_ref, zf_ref, xp_ref = refs
        x = x_ref[0] + pos_ref[...]
        xp_ref[0] = x
    else:
        x_ref, g_ref, sc_ref, sh_ref, w_ref, zr_ref, zc_ref, zf_ref = refs
        x = x_ref[0]
    h = (_rms(x, g_ref[...]) * (1.0 + sc_ref[0]) + sh_ref[0]).astype(BF16)
    a, b = N_RWKV_COLS, N_RWKV_COLS + N_CONV_COLS
    zr_ref[0] = jnp.dot(h, w_ref[:, :a], preferred_element_type=F32)
    zc_ref[0] = jnp.dot(h, w_ref[:, a:b], preferred_element_type=F32).astype(BF16)
    zf_ref[0] = jnp.dot(h, w_ref[:, b:], preferred_element_type=F32).astype(BF16)


def _inproj(x, pos, g, sc, sh, w, tm):
    bsz, length, d = x.shape
    n = w.shape[1]
    add_pos = pos is not None
    row = lambda b, i: (b, i, 0)
    in_specs = [pl.BlockSpec((1, tm, d), row)]
    args = [x]
    if add_pos:
        in_specs.append(pl.BlockSpec((tm, d), lambda b, i: (i, 0)))
        args.append(pos)
    in_specs += [pl.BlockSpec((1, d), lambda b, i: (0, 0)),
                 pl.BlockSpec((1, 1, d), lambda b, i: (b, 0, 0)),
                 pl.BlockSpec((1, 1, d), lambda b, i: (b, 0, 0)),
                 pl.BlockSpec((d, n), lambda b, i: (0, 0))]
    args += [g, sc, sh, w]
    out_shape = [jax.ShapeDtypeStruct((bsz, length, N_RWKV_COLS), F32),
                 jax.ShapeDtypeStruct((bsz, length, N_CONV_COLS), BF16),
                 jax.ShapeDtypeStruct((bsz, length, N_FOUR_COLS), BF16)]
    out_specs = [pl.BlockSpec((1, tm, N_RWKV_COLS), row),
                 pl.BlockSpec((1, tm, N_CONV_COLS), row),
                 pl.BlockSpec((1, tm, N_FOUR_COLS), row)]
    if add_pos:
        out_shape.append(jax.ShapeDtypeStruct((bsz, length, d), F32))
        out_specs.append(pl.BlockSpec((1, tm, d), row))
    return pl.pallas_call(
        functools.partial(_inproj_kernel, add_pos),
        grid=(bsz, length // tm),
        in_specs=in_specs, out_specs=out_specs, out_shape=out_shape,
        compiler_params=_params("parallel", "parallel"),
        name="inproj",
    )(*args)


HALO = 8


def _rwkv_features(z, edge_row, d, mu_ref, muw_ref, mua_ref, w0_ref, w2_ref, a0_ref, a2_ref, kk_ref, ka_ref, bd):
    n = z.shape[0]
    rows = lax.broadcasted_iota(jnp.int32, (n, 1), 0)
    if d:
        zs = jnp.where(rows == n - 1, edge_row, pltpu.roll(z, n - 1, 0))
    else:
        zs = jnp.where(rows == 0, edge_row, pltpu.roll(z, 1, 0))
    dz = zs - z
    k = z[:, 0:512] + dz[:, 0:512] * mu_ref[d, 0:1, :]
    v = z[:, 512:1024] + dz[:, 512:1024] * mu_ref[d, 1:2, :]
    r = z[:, 1280:1792] + dz[:, 1280:1792] * mu_ref[d, 2:3, :]
    wl = z[:, 1024:1152] + dz[:, 1024:1152] * muw_ref[...]
    al = z[:, 1152:1280] + dz[:, 1152:1280] * mua_ref[...]
    lw = -DECAY_SCALE * jax.nn.sigmoid(w0_ref[d:d + 1, :] + _dot(jnp.tanh(wl), w2_ref[d]))
    a = jax.nn.sigmoid(a0_ref[d:d + 1, :] + _dot(al, a2_ref[d]))
    kk = k * kk_ref[...]
    ss = _head_sums(kk * kk, bd)
    kk = kk * lax.rsqrt(jnp.maximum(ss, NORM_FLOOR * NORM_FLOOR))
    return r, lw, k * (1.0 + (a - 1.0) * ka_ref[...]), v, -kk, kk * a


SOLVE_BLOCK = 8


def _unit_lower_solve(a_list, x_list, diag_blk, eye, first, side_work=()):
    side_work = list(side_work)

    def boundary():
        if side_work:
            side_work.pop(0)()

    def pair(m, w):
        return _dot(m, jnp.concatenate([jnp.where(first, w, 0.0), jnp.where(first, 0.0, w)], axis=0))

    def apply(m, x):
        zero = jnp.zeros((x.shape[0], LANES), x.dtype)
        return _dot(m, jnp.concatenate([jnp.concatenate([x[:, :LANES], zero], axis=1),
                                        jnp.concatenate([zero, x[:, LANES:]], axis=1)], axis=0))

    n = eye.shape[0]
    a_d = [jnp.where(diag_blk, a, 0.0) for a in a_list]
    a_o = [a - d for a, d in zip(a_list, a_d)]
    dinv = [eye + d for d in a_d]
    apow = a_d
    for _ in range(SOLVE_BLOCK.bit_length() - 2):
        apow = [pair(p, p) for p in apow]
        boundary()
        dinv = [d + pair(p, d) for p, d in zip(apow, dinv)]
        boundary()
    b = [pair(d, o) for d, o in zip(dinv, a_o)]
    boundary()
    x = [apply(d, x) for d, x in zip(dinv, x_list)]
    levels = (n // SOLVE_BLOCK).bit_length() - 1
    for level in range(levels):
        boundary()
        if level:
            b = [pair(m, m) for m in b]
        x = [v + apply(m, v) for m, v in zip(b, x)]
    while side_work:
        boundary()
    return x


def _scan_kernel(*refs):
    z_refs = (refs[0], refs[2])
    halo_refs = (refs[1], refs[3])
    zfirst_refs = refs[4:6]
    s0_refs = refs[6:8]
    feat_refs = refs[8:17]
    rk_ref, gw_ref, gb_ref, bd_ref = refs[17:21]
    y_refs = refs[21:23]
    st_refs = refs[23:25]
    feat_ref = refs[25]
    c = pl.program_id(1)

    @pl.when(c == 0)
    def _():
        for st_ref, s0_ref in zip(st_refs, s0_refs):
            st_ref[...] = s0_ref[...]
        no_token = jnp.zeros((1, zfirst_refs[0].shape[2]), F32)
        for d in range(2):
            for j, f in enumerate(_rwkv_features(zfirst_refs[d][0], no_token, d, *feat_refs, bd_ref[...])):
                feat_ref[d, j] = f

    n = CHUNK
    t_i = lax.broadcasted_iota(jnp.int32, (n, n), 0)
    s_i = lax.broadcasted_iota(jnp.int32, (n, n), 1)
    blk_shift = SOLVE_BLOCK.bit_length() - 1
    lane2 = lax.broadcasted_iota(jnp.int32, (n, LANES), 1)
    t_2 = lax.broadcasted_iota(jnp.int32, (n, LANES), 0)
    s_2 = lane2 & (n - 1)
    diag_blk2 = (s_2 >> blk_shift) == (t_2 >> blk_shift)
    eye2 = jnp.where(s_2 == t_2, 1.0, 0.0)
    first = lane2 < HEAD_DIM
    r_bd = lax.broadcasted_iota(jnp.int32, (LANES, LANES), 0) < HEAD_DIM
    c_bd = lax.broadcasted_iota(jnp.int32, (LANES, LANES), 1) < HEAD_DIM
    blockdiag = r_bd == c_bd
    incl2 = (s_2 <= t_2, s_2 >= t_2)
    strict2 = (s_2 < t_2, s_2 > t_2)

    npairs = st_refs[0].shape[1]
    sls = [slice(g * LANES, (g + 1) * LANES) for g in range(npairs)]
    items = [(d, g) for d in range(2) for g in range(npairs)]
    heads = [(i, j) for i in range(len(items)) for j in range(2)]
    feats = [[feat_ref[d, j] for j in range(6)] for d in range(2)]
    raw, scaled, w_tot = [], [], []
    for d in range(2):
        r, lw, k, v, a, b = feats[d]
        incl = (s_i >= t_i) if d else (s_i <= t_i)
        cum = _dot_split_rhs(jnp.where(incl, 1.0, 0.0).astype(BF16), lw)
        tot = cum[0:1, :] if d else cum[n - 1:n, :]
        w_inv = jnp.exp(-cum)
        w_all = jnp.exp(tot)
        b_inv = b * w_inv
        k_inv = k * w_inv
        raw.append((r, k, v))
        scaled.append((r * jnp.exp(cum), a * jnp.exp(cum - lw), b_inv, k_inv, b_inv * w_all, k_inv * w_all))
        w_tot.append(w_all)
    states = [st_refs[d][0, g] for d, g in items]

    v_g = [raw[d][2][:, sls[g]] for d, g in items]
    at_h = [(jnp.where(first, scaled[d][1][:, sls[g]], 0.0), jnp.where(first, 0.0, scaled[d][1][:, sls[g]]))
            for d, g in items]
    rt_h = [(jnp.where(first, scaled[d][0][:, sls[g]], 0.0), jnp.where(first, 0.0, scaled[d][0][:, sls[g]]))
            for d, g in items]
    s = [_dot_nt(jnp.concatenate([at_h[i][0], rt_h[i][0], at_h[i][1], rt_h[i][1]], axis=0),
                 jnp.concatenate([scaled[d][2][:, sls[g]], scaled[d][3][:, sls[g]]], axis=0))
         for i, (d, g) in enumerate(items)]
    sa = [(jnp.where(strict2[d], s[i][0:n], 0.0), jnp.where(strict2[d], s[i][2 * n:3 * n], 0.0))
          for i, (d, g) in enumerate(items)]
    sr = [(jnp.where(incl2[d], s[i][n:2 * n], 0.0), jnp.where(incl2[d], s[i][3 * n:4 * n], 0.0))
          for i, (d, g) in enumerate(items)]
    vv = [jnp.concatenate([x, x], axis=0) for x in (pltpu.roll(x, HEAD_DIM, 1) for x in v_g)]
    akv = [_dot(jnp.where(first, 0.0, sa[i][j]), vv[i]) for i, j in heads]
    rhs = [jnp.where(first, at_h[i][0], akv[2 * i]) if j == 0 else jnp.where(first, akv[2 * i + 1], at_h[i][1])
           for i, j in heads]
    edge_rows = (halo_refs[0][0, HALO - 1:HALO, :], halo_refs[1][0, 0:1, :])

    def next_features(d):
        def work():
            for j, f in enumerate(_rwkv_features(z_refs[d][0], edge_rows[d], d, *feat_refs, bd_ref[...])):
                feat_ref[d, j] = f
        return work

    idx = range(len(items))
    a_pair = [jnp.where(first, sa[i][0], pltpu.roll(sa[i][1], HEAD_DIM, 1)) for i in idx]
    sol = _unit_lower_solve(a_pair, [jnp.concatenate([rhs[2 * i], rhs[2 * i + 1]], axis=1) for i in idx],
                            diag_blk2, eye2, first,
                            side_work=[next_features(0), lambda: None, lambda: None, next_features(1)])
    pq = [sol[i // 2][:, (i % 2) * LANES:(i % 2 + 1) * LANES] for i in range(2 * len(items))]
    p_pair = [jnp.concatenate([jnp.where(first, pq[2 * i], 0.0), jnp.where(first, 0.0, pq[2 * i + 1])], axis=0)
              for i in idx]
    q_pair = [pltpu.roll(jnp.where(first, pq[2 * i + 1], pq[2 * i]), HEAD_DIM, 1) for i in idx]
    t_split = [jnp.concatenate(_split(states[i]), axis=0) for i in idx]
    pr = [jnp.concatenate([p_pair[i], rt_h[i][0], rt_h[i][1]], axis=0).astype(BF16) for i in idx]
    prt = [jnp.dot(jnp.concatenate([pr[i], pr[i]], axis=1), t_split[i], preferred_element_type=F32) for i in idx]
    pt = [m[0:2 * n] for m in prt]
    rtt = [m[2 * n:] for m in prt]
    u = [jnp.where(first, pt[i][0:n], pt[i][n:]) + q_pair[i] for i in idx]
    uv = [jnp.concatenate([u[i], v_g[i]], axis=0) for i in idx]
    yh = [_dot(sr[i][j], uv[i]) for i, j in heads]
    ys = [jnp.where(first, rtt[i][0:n] + yh[2 * i], rtt[i][n:] + yh[2 * i + 1]) for i in idx]
    t_new = [_dot(jnp.concatenate([scaled[d][4][:, sls[g]], scaled[d][5][:, sls[g]]], axis=0).T, uv[i])
             for i, (d, g) in enumerate(items)]
    for i, (d, g) in enumerate(items):
        w_col = jnp.broadcast_to(w_tot[d][:, sls[g]], (LANES, LANES)).T
        st_refs[d][0, g] = jnp.where(blockdiag, w_col * states[i] + t_new[i], 0.0)
    bd = bd_ref[...]
    inv = 1.0 / HEAD_DIM
    y = [jnp.concatenate(ys[d * npairs:(d + 1) * npairs], axis=1) for d in range(2)]
    mean = [_head_sums(y[d], bd) * inv for d in range(2)]
    dev = [y[d] - mean[d] for d in range(2)]
    var = [_head_sums(dev[d] * dev[d], bd) * inv for d in range(2)]
    bonus = [_head_sums(raw[d][0] * raw[d][1] * rk_ref[...], bd) * raw[d][2] for d in range(2)]
    for d in range(2):
        y_refs[d][0] = (dev[d] * lax.rsqrt(var[d] + GN_EPS) * gw_ref[...] + gb_ref[...] + bonus[d]).astype(BF16)


def _scan(zr, state_fwd, state_bwd, p):
    bsz, length, ncol = zr.shape
    width = p["w0"].shape[1]
    nc = length // CHUNK
    npairs = width // LANES
    per_chunk = CHUNK // HALO
    last_halo = length // HALO - 1
    fmap = lambda b, c: (b, c, 0)
    bmap = lambda b, c: (b, nc - 1 - c, 0)
    fnext = lambda c: jnp.minimum(c + 1, nc - 1)
    bnext = lambda c: jnp.maximum(nc - 2 - c, 0)
    fnext_map = lambda b, c: (b, fnext(c), 0)
    bnext_map = lambda b, c: (b, bnext(c), 0)
    prev_map = lambda b, c: (b, jnp.maximum(fnext(c) * per_chunk - 1, 0), 0)
    next_map = lambda b, c: (b, jnp.minimum((bnext(c) + 1) * per_chunk, last_halo), 0)
    full = lambda shape: pl.BlockSpec(shape, lambda b, c: (0,) * len(shape))
    st_spec = pl.BlockSpec((1, npairs, LANES, LANES), lambda b, c: (b, 0, 0, 0))
    y_shape = jax.ShapeDtypeStruct((bsz, length, width), BF16)
    st_shape = jax.ShapeDtypeStruct((bsz, npairs, LANES, LANES), F32)
    y_f, y_b, st_f, st_b = pl.pallas_call(
        _scan_kernel,
        grid=(bsz, nc),
        in_specs=[pl.BlockSpec((1, CHUNK, ncol), fnext_map), pl.BlockSpec((1, HALO, ncol), prev_map),
                  pl.BlockSpec((1, CHUNK, ncol), bnext_map), pl.BlockSpec((1, HALO, ncol), next_map),
                  pl.BlockSpec((1, CHUNK, ncol), lambda b, c: (b, 0, 0)),
                  pl.BlockSpec((1, CHUNK, ncol), lambda b, c: (b, nc - 1, 0)),
                  st_spec, st_spec,
                  full((2, 3, width)), full((1, LANES)), full((1, LANES)), full((2, width)),
                  full((2, LANES, width)), full((2, width)), full((2, LANES, width)), full((1, width)),
                  full((1, width)), full((1, width)), full((1, width)), full((1, width)), full((LANES, LANES))],
        out_specs=[pl.BlockSpec((1, CHUNK, width), fmap), pl.BlockSpec((1, CHUNK, width), bmap), st_spec, st_spec],
        out_shape=[y_shape, y_shape, st_shape, st_shape],
        scratch_shapes=[pltpu.VMEM((2, 6, CHUNK, width), F32)],
        compiler_params=_params("parallel", "arbitrary"),
        name="wkv_scan",
    )(zr, zr, zr, zr, zr, zr, state_fwd, state_bwd, p["mu"], p["mu_w"], p["mu_a"], p["w0"], p["w2"], p["a0"], p["a2"],
      p["k_k"], p["k_a"], p["r_k"], p["gn_w"], p["gn_b"], p["bd"])
    return (y_f, y_b), (st_f, st_b)


def _mix_kernel(period, tm, yf_ref, yb_ref, glo_ref, zc_ref, g2_ref, cw_ref, o_ref):
    gate = _dot(jax.nn.sigmoid(glo_ref[0]), g2_ref[...])
    yr = (yf_ref[0].astype(F32) + yb_ref[0].astype(F32)) * gate
    zc = zc_ref[0].astype(F32)
    u, gate_b, gate_c = zc[:, 0:256], zc[:, 256:512], zc[:, 512:768]
    hc = gate_c * u
    pos = lax.broadcasted_iota(jnp.int32, (tm, 1), 0) % period
    prev = jnp.where(pos == 0, 0.0, pltpu.roll(hc, 1, 0))
    nxt = jnp.where(pos == period - 1, 0.0, pltpu.roll(hc, tm - 1, 0))
    conv = prev * cw_ref[0:1, :] + hc * cw_ref[1:2, :] + nxt * cw_ref[2:3, :]
    o_ref[0, :, 0:512] = yr.astype(BF16)
    o_ref[0, :, 512:768] = (gate_b * conv).astype(BF16)


def _mix(ys_f, ys_b, zr, zc, g2, conv_w, period, tm):
    bsz, length, _ = zc.shape
    row = lambda b, i: (b, i, 0)
    glo_block = (N_RWKV_COLS - 128) // 128
    return pl.pallas_call(
        functools.partial(_mix_kernel, period, tm),
        grid=(bsz, length // tm),
        in_specs=[pl.BlockSpec((1, tm, 512), row), pl.BlockSpec((1, tm, 512), row),
                  pl.BlockSpec((1, tm, 128), lambda b, i: (b, i, glo_block)),
                  pl.BlockSpec((1, tm, N_CONV_COLS), row),
                  pl.BlockSpec((128, 512), lambda b, i: (0, 0)),
                  pl.BlockSpec((3, 256), lambda b, i: (0, 0))],
        out_specs=pl.BlockSpec((1, tm, 768), row),
        out_shape=jax.ShapeDtypeStruct((bsz, length, 768), BF16),
        compiler_params=_params("parallel", "parallel"),
        name="gate_conv",
    )(ys_f, ys_b, zr, zc, g2, conv_w)


def _chan_dft_kernel(f_ref, w_ref, o_ref):
    xcs = jnp.dot(f_ref[0].astype(BF16), w_ref[...], preferred_element_type=F32)
    o_ref[0, 0] = xcs[:, 0:256].astype(BF16)
    o_ref[0, 1] = xcs[:, 256:512].astype(BF16)


def _pos_dft_kernel(nb, cs_ref, x_ref, o_ref, acc_ref):
    kstep = pl.program_id(1)

    @pl.when(kstep == 0)
    def _():
        acc_ref[...] = jnp.zeros_like(acc_ref)

    cs = cs_ref[...]
    for b in range(nb):
        acc_ref[b] += jnp.dot(cs, x_ref[b], preferred_element_type=F32)

    @pl.when(kstep == pl.num_programs(1) - 1)
    def _():
        o_ref[...] = acc_ref[...].astype(BF16)


def _fourier(zf, chan_w, pos_cs, tm, tk):
    bsz, length, ch = zf.shape
    tr = min(512, length)
    xcs = pl.pallas_call(
        _chan_dft_kernel,
        grid=(bsz, length // tr),
        in_specs=[pl.BlockSpec((1, tr, ch), lambda b, i: (b, i, 0)),
                  pl.BlockSpec((ch, 2 * ch), lambda b, i: (0, 0))],
        out_specs=pl.BlockSpec((1, 2, tr, ch), lambda b, i: (b, 0, i, 0)),
        out_shape=jax.ShapeDtypeStruct((bsz, 2, length, ch), BF16),
        compiler_params=_params("parallel", "parallel"),
        name="chan_dft",
    )(zf, chan_w)
    xcs = xcs.reshape(bsz, 2 * length, ch)
    return pl.pallas_call(
        functools.partial(_pos_dft_kernel, bsz),
        grid=(length // tm, 2 * length // tk),
        in_specs=[pl.BlockSpec((tm, tk), lambda i, kk: (i, kk)),
                  pl.BlockSpec((bsz, tk, ch), lambda i, kk: (0, kk, 0))],
        out_specs=pl.BlockSpec((bsz, tm, ch), lambda i, kk: (0, i, 0)),
        out_shape=jax.ShapeDtypeStruct((bsz, length, ch), BF16),
        scratch_shapes=[pltpu.VMEM((bsz, tm, ch), F32)],
        compiler_params=_params("parallel", "arbitrary"),
        name="pos_dft",
    )(pos_cs, xcs)


def _dft_tables(length):
    m = jnp.arange(length, dtype=jnp.int32)[None, :]
    l1 = jnp.arange(length // 64, dtype=jnp.int32)[:, None] * 64
    l2 = jnp.arange(64, dtype=jnp.int32)[:, None]
    ang = lambda l: ((l * m) % length).astype(F32) * (2.0 * jnp.pi / length)
    c1, s1, c2, s2 = jnp.cos(ang(l1)), jnp.sin(ang(l1)), jnp.cos(ang(l2)), jnp.sin(ang(l2))
    scale = 1.0 / jnp.sqrt(jnp.float32(length))
    cos = (c1[:, None, :] * c2[None] - s1[:, None, :] * s2[None]).reshape(length, length) * scale
    sin = (s1[:, None, :] * c2[None] + c1[:, None, :] * s2[None]).reshape(length, length) * scale
    return jnp.concatenate([cos, sin], axis=1).astype(BF16)


def _chan_dft_weights(ngroups):
    j = jnp.arange(HEAD_DIM, dtype=jnp.int32)
    ang = ((j[:, None] * j[None, :]) % HEAD_DIM).astype(F32) * (2.0 * jnp.pi / HEAD_DIM)
    eye = jnp.eye(ngroups, dtype=F32)
    scale = 1.0 / jnp.sqrt(jnp.float32(HEAD_DIM))
    cc = jnp.kron(eye, jnp.cos(ang) * scale)
    sc = jnp.kron(eye, jnp.sin(ang) * scale)
    return jnp.concatenate([cc, -sc], axis=1).astype(BF16)


def _outproj_kernel(yrc_ref, yf_ref, w_ref, x_ref, gn1_ref, gate_ref, gn2_ref, sc_ref, sh_ref, xo_ref, h_ref):
    nrc = yrc_ref.shape[2]
    o = (jnp.dot(yrc_ref[0], w_ref[0:nrc, :], preferred_element_type=F32) +
         jnp.dot(yf_ref[0], w_ref[nrc:, :], preferred_element_type=F32))
    xn = x_ref[0] + gate_ref[0] * _rms(o, gn1_ref[...])
    xo_ref[0] = xn
    h_ref[0] = (_rms(xn, gn2_ref[...]) * (1.0 + sc_ref[0]) + sh_ref[0]).astype(BF16)


def _outproj(yrc, yf, w, x, gn1, gate, gn2, sc, sh, tm):
    bsz, length, d = x.shape
    row = lambda b, i: (b, i, 0)
    vec = pl.BlockSpec((1, d), lambda b, i: (0, 0))
    mod = pl.BlockSpec((1, 1, d), lambda b, i: (b, 0, 0))
    return pl.pallas_call(
        _outproj_kernel,
        grid=(bsz, length // tm),
        in_specs=[pl.BlockSpec((1, tm, yrc.shape[2]), row), pl.BlockSpec((1, tm, yf.shape[2]), row),
                  pl.BlockSpec(w.shape, lambda b, i: (0, 0)), pl.BlockSpec((1, tm, d), row),
                  vec, mod, vec, mod, mod],
        out_specs=[pl.BlockSpec((1, tm, d), row), pl.BlockSpec((1, tm, d), row)],
        out_shape=[jax.ShapeDtypeStruct((bsz, length, d), F32), jax.ShapeDtypeStruct((bsz, length, d), BF16)],
        compiler_params=_params("parallel", "parallel"),
        name="outproj",
    )(yrc, yf, w, x, gn1, gate, gn2, sc, sh)


FF_SUB = 256


def _swiglu_partial(h, wg, wu, wd, tf):
    def up(c, n):
        return (jnp.dot(h, wg(c, n), preferred_element_type=F32), jnp.dot(h, wu(c, n), preferred_element_type=F32))

    def down(gu, c, n):
        return jnp.dot((_silu(gu[0]) * gu[1]).astype(BF16), wd(c, n), preferred_element_type=F32)

    out = None
    prev = None
    for cut in [(c, min(FF_SUB, tf - c)) for c in range(0, tf, FF_SUB)]:
        gu = up(*cut)
        if prev is not None:
            part = down(*prev)
            out = part if out is None else out + part
        prev = (gu,) + cut
    part = down(*prev)
    return part if out is None else out + part


def _ffn_kernel(h_ref, wg_ref, wu_ref, wd_ref, x_ref, gn_ref, gate_ref, o_ref, acc_ref):
    j = pl.program_id(2)

    @pl.when(j == 0)
    def _():
        acc_ref[...] = jnp.zeros_like(acc_ref)

    acc_ref[...] += _swiglu_partial(h_ref[0], lambda c, n: wg_ref[:, c:c + n], lambda c, n: wu_ref[:, c:c + n],
                                    lambda c, n: wd_ref[c:c + n, :], wg_ref.shape[1])

    @pl.when(j == pl.num_programs(2) - 1)
    def _():
        o_ref[0] = x_ref[0] + gate_ref[0] * _rms(acc_ref[...], gn_ref[...])


def _ffn(h, wg, wu, wd, x, gn, gate, tm, tf):
    bsz, length, d = x.shape
    ff = wg.shape[1]
    row = lambda b, i, j: (b, i, 0)
    return pl.pallas_call(
        _ffn_kernel,
        grid=(bsz, length // tm, ff // tf),
        in_specs=[pl.BlockSpec((1, tm, d), row),
                  pl.BlockSpec((d, tf), lambda b, i, j: (0, j)),
                  pl.BlockSpec((d, tf), lambda b, i, j: (0, j)),
                  pl.BlockSpec((tf, d), lambda b, i, j: (j, 0)),
                  pl.BlockSpec((1, tm, d), row),
                  pl.BlockSpec((1, d), lambda b, i, j: (0, 0)),
                  pl.BlockSpec((1, 1, d), lambda b, i, j: (b, 0, 0))],
        out_specs=pl.BlockSpec((1, tm, d), row),
        out_shape=jax.ShapeDtypeStruct((bsz, length, d), F32),
        scratch_shapes=[pltpu.VMEM((tm, d), F32)],
        compiler_params=_params("parallel", "parallel", "arbitrary"),
        name="ffn",
    )(h, wg, wu, wd, x, gn, gate)


MOE_TILE = 512
MOE_BLOCK = 128
MOE_ALIGN = 16
MOE_TM = 1024


def _router_kernel(x_ref, gn_ref, sc_ref, sh_ref, w_ref, b_ref, tri_ref, routes_ref, routest_ref, cnt_ref):
    h = _rms(x_ref[0], gn_ref[...]) * (1.0 + sc_ref[0]) + sh_ref[0]
    logits = jnp.dot(h, w_ref[...], preferred_element_type=F32, precision=lax.Precision.HIGHEST) + b_ref[...]
    ne = float(logits.shape[1])
    idx = lax.broadcasted_iota(jnp.int32, logits.shape, 1).astype(F32)
    m1 = jnp.max(logits, axis=1, keepdims=True)
    i1 = jnp.min(jnp.where(logits == m1, idx, ne), axis=1, keepdims=True)
    rest = jnp.where(idx == i1, -jnp.inf, logits)
    m2 = jnp.max(rest, axis=1, keepdims=True)
    i2 = jnp.min(jnp.where(rest == m2, idx, ne), axis=1, keepdims=True)
    e2 = jnp.exp(m2 - m1)
    p1 = 1.0 / (1.0 + e2)
    p2 = e2 / (1.0 + e2)
    sel = jnp.where(idx == i1, 1.0, jnp.where(idx == i2, 1.0, 0.0))
    before = jnp.dot(tri_ref[...], sel.astype(BF16), preferred_element_type=F32)
    rank1 = jnp.sum(jnp.where(idx == i1, before, 0.0), axis=1, keepdims=True)
    rank2 = jnp.sum(jnp.where(idx == i2, before, 0.0), axis=1, keepdims=True)
    routes = jnp.zeros_like(logits)
    for lane, val in enumerate((i1, i2, rank1, rank2, p1, p2)):
        routes = jnp.where(idx == float(lane), val, routes)
    routes_ref[0] = routes
    routest_ref[0] = jnp.transpose(routes)[0:N_EXPERTS, :]
    cnt_ref[0, 0] = jnp.sum(sel, axis=0, keepdims=True)


def _router(x, gn, sc, sh, w, b):
    bsz, length, d = x.shape
    tm = MOE_TILE
    pad = LANES - w.shape[1]
    w = jnp.pad(w, ((0, 0), (0, pad)))
    b = jnp.pad(b, ((0, 0), (0, pad)), constant_values=-1e30)
    ne = LANES
    t = jnp.arange(tm, dtype=jnp.int32)
    tri = (t[None, :] < t[:, None]).astype(BF16)
    row = lambda bb, i: (bb, i, 0)
    vec = pl.BlockSpec((1, d), lambda bb, i: (0, 0))
    mod = pl.BlockSpec((1, 1, d), lambda bb, i: (bb, 0, 0))
    nt = length // tm
    return pl.pallas_call(
        _router_kernel,
        grid=(bsz, nt),
        in_specs=[pl.BlockSpec((1, tm, d), row), vec, mod, mod,
                  pl.BlockSpec((d, ne), lambda bb, i: (0, 0)), pl.BlockSpec((1, ne), lambda bb, i: (0, 0)),
                  pl.BlockSpec((tm, tm), lambda bb, i: (0, 0))],
        out_specs=[pl.BlockSpec((1, tm, ne), row),
                   pl.BlockSpec((1, N_EXPERTS, tm), lambda bb, i: (bb, 0, i)),
                   pl.BlockSpec((1, 1, 1, ne), lambda bb, i: (bb, i, 0, 0))],
        out_shape=[jax.ShapeDtypeStruct((bsz, length, ne), F32),
                   jax.ShapeDtypeStruct((bsz, N_EXPERTS, length), F32),
                   jax.ShapeDtypeStruct((bsz, nt, 1, ne), F32)],
        compiler_params=_params("parallel", "parallel"),
        name="router",
    )(x, gn, sc, sh, w, b, tri)


_MOE_SHIFT = MOE_BLOCK.bit_length() - 1
_MOE_BUF_BLOCKS = (2 * MOE_TILE + N_EXPERTS * (MOE_ALIGN - 1 + MOE_BLOCK - 1)) // MOE_BLOCK + 1


def _align_rows(c):
    return ((c + (MOE_ALIGN - 1)) >> (MOE_ALIGN.bit_length() - 1)) << (MOE_ALIGN.bit_length() - 1)


def _blocks(c):
    return (c + (MOE_BLOCK - 1)) >> _MOE_SHIFT


def _dispatch_copies(start, tile, off_ref, cnt_ref, buf_ref, hs_ref, sem):
    def run(src_row, dst_row, nrows):
        cp = pltpu.make_async_copy(buf_ref.at[pl.ds(pl.multiple_of(src_row, MOE_ALIGN), nrows)],
                                   hs_ref.at[pl.ds(pl.multiple_of(dst_row, MOE_ALIGN), nrows)], sem)
        if start:
            cp.start()
        else:
            cp.wait()

    slot = jnp.int32(0)
    for e in range(N_EXPERTS):
        rows = _align_rows(cnt_ref[tile * N_EXPERTS + e])
        dst0 = off_ref[tile * N_EXPERTS + e]
        src0 = slot * MOE_BLOCK
        nfull = rows >> _MOE_SHIFT

        def full(rb, carry, src0=src0, dst0=dst0):
            run(src0 + rb * MOE_BLOCK, dst0 + rb * MOE_BLOCK, MOE_BLOCK)
            return carry

        lax.fori_loop(0, nfull, full, 0)
        done = nfull * MOE_BLOCK
        piece = MOE_BLOCK // 2
        while piece >= MOE_ALIGN:
            has = (rows & piece) != 0

            @pl.when(has)
            def _(done=done, piece=piece, src0=src0, dst0=dst0):
                run(src0 + done, dst0 + done, piece)

            done = done + jnp.where(has, piece, 0)
            piece //= 2
        slot = slot + _blocks(rows)


def _staging_rows(tile, cnt_ref, i1, i2, rank1, rank2):
    row1 = jnp.zeros_like(i1)
    row2 = jnp.zeros_like(i2)
    slot = jnp.int32(0)
    for e in range(N_EXPERTS):
        first_row = (slot * MOE_BLOCK).astype(F32)
        row1 = jnp.where(i1 == float(e), first_row, row1)
        row2 = jnp.where(i2 == float(e), first_row, row2)
        slot = slot + _blocks(cnt_ref[tile * N_EXPERTS + e])
    return row1 + rank1, row2 + rank2


def _dispatch_kernel(off_ref, cnt_ref, h_ref, routest_ref, hs_init_ref, hs_ref, buf_ref, sem):
    del hs_init_ref
    tile = pl.program_id(0)
    rt = routest_ref[0]
    row1, row2 = _staging_rows(tile, cnt_ref, rt[0:1], rt[1:2], rt[2:3], rt[3:4])
    rows = lax.broadcasted_iota(jnp.int32, (MOE_BLOCK, 1), 0)
    onehot = jnp.concatenate(
        [jnp.where(row1 == (rows + blk * MOE_BLOCK).astype(F32), 1.0,
                   jnp.where(row2 == (rows + blk * MOE_BLOCK).astype(F32), 1.0, 0.0)).astype(BF16)
         for blk in range(_MOE_BUF_BLOCKS)], axis=0)
    buf_ref[...] = jnp.dot(onehot, h_ref[...], preferred_element_type=F32).astype(BF16)
    _dispatch_copies(True, tile, off_ref, cnt_ref, buf_ref, hs_ref, sem)
    _dispatch_copies(False, tile, off_ref, cnt_ref, buf_ref, hs_ref, sem)


def _dispatch(h, rankt, off, cnt, nrows):
    n, d = h.shape
    ntb = rankt.shape[2] // MOE_TILE
    grid_spec = pltpu.PrefetchScalarGridSpec(
        num_scalar_prefetch=2, grid=(n // MOE_TILE,),
        in_specs=[pl.BlockSpec((MOE_TILE, d), lambda t, o, c: (t, 0)),
                  pl.BlockSpec((1, N_EXPERTS, MOE_TILE), lambda t, o, c: (t // ntb, 0, t % ntb)),
                  pl.BlockSpec(memory_space=pl.ANY)],
        out_specs=pl.BlockSpec(memory_space=pl.ANY),
        scratch_shapes=[pltpu.VMEM((_MOE_BUF_BLOCKS * MOE_BLOCK, d), BF16), pltpu.SemaphoreType.DMA(())])
    return pl.pallas_call(
        _dispatch_kernel, grid_spec=grid_spec,
        out_shape=jax.ShapeDtypeStruct((nrows, d), BF16),
        input_output_aliases={4: 0},
        compiler_params=_params("arbitrary"),
        name="moe_dispatch",
    )(off, cnt, h, rankt, jnp.zeros((nrows, d), BF16))


def _gffn_kernel(te_ref, tv_ref, h_ref, wg_ref, wu_ref, wd_ref, o_ref, acc_ref):
    q = pl.program_id(0)
    j = pl.program_id(1)

    @pl.when(j == 0)
    def _():
        acc_ref[...] = jnp.zeros_like(acc_ref)

    @pl.when(tv_ref[q] > 0)
    def _():
        acc_ref[...] += _swiglu_partial(h_ref[...], lambda c, n: wg_ref[0, :, c:c + n],
                                        lambda c, n: wu_ref[0, :, c:c + n], lambda c, n: wd_ref[0, c:c + n, :],
                                        wg_ref.shape[2])

    @pl.when(j == pl.num_programs(1) - 1)
    def _():
        o_ref[...] = acc_ref[...].astype(BF16)


def _gffn(hs, te, tv, wg, wu, wd, tf):
    nrows, d = hs.shape
    ff = wg.shape[2]
    grid_spec = pltpu.PrefetchScalarGridSpec(
        num_scalar_prefetch=2, grid=(nrows // MOE_TM, ff // tf),
        in_specs=[pl.BlockSpec((MOE_TM, d), lambda q, j, te, tv: (q, 0)),
                  pl.BlockSpec((1, d, tf), lambda q, j, te, tv: (te[q], 0, j)),
                  pl.BlockSpec((1, d, tf), lambda q, j, te, tv: (te[q], 0, j)),
                  pl.BlockSpec((1, tf, d), lambda q, j, te, tv: (te[q], j, 0))],
        out_specs=pl.BlockSpec((MOE_TM, d), lambda q, j, te, tv: (q, 0)),
        scratch_shapes=[pltpu.VMEM((MOE_TM, d), F32)])
    return pl.pallas_call(
        _gffn_kernel, grid_spec=grid_spec,
        out_shape=jax.ShapeDtypeStruct((nrows, d), BF16),
        compiler_params=_params("arbitrary", "arbitrary"),
        name="moe_experts",
    )(te, tv, hs, wg, wu, wd)


def _combine_copies(start, tile, off_ref, cnt_ref, ys_ref, buf_ref, sem):
    slot = jnp.int32(0)
    for e in range(N_EXPERTS):
        nb = _blocks(cnt_ref[tile * N_EXPERTS + e])
        src0 = off_ref[tile * N_EXPERTS + e]

        def body(rb, carry, slot=slot, src0=src0):
            cp = pltpu.make_async_copy(
                ys_ref.at[pl.ds(pl.multiple_of(src0 + rb * MOE_BLOCK, MOE_ALIGN), MOE_BLOCK)],
                buf_ref.at[pl.ds(pl.multiple_of((slot + rb) * MOE_BLOCK, MOE_BLOCK), MOE_BLOCK)], sem)
            if start:
                cp.start()
            else:
                cp.wait()
            return carry

        lax.fori_loop(0, nb, body, 0)
        slot = slot + nb


def _combine_kernel(off_ref, cnt_ref, ys_ref, routes_ref, x_ref, gn_ref, gate_ref, o_ref, buf_ref, sem):
    tile = pl.program_id(0)

    @pl.when(tile == 0)
    def _():
        buf_ref[...] = jnp.zeros_like(buf_ref)

    _combine_copies(True, tile, off_ref, cnt_ref, ys_ref, buf_ref, sem)
    routes = routes_ref[0]
    col = lambda j: routes[:, j:j + 1]
    row1, row2 = _staging_rows(tile, cnt_ref, col(0), col(1), col(2), col(3))
    lanes = lax.broadcasted_iota(jnp.int32, (1, MOE_BLOCK), 1)
    scatter = jnp.concatenate(
        [(jnp.where(row1 == (lanes + blk * MOE_BLOCK).astype(F32), col(4), 0.0) +
          jnp.where(row2 == (lanes + blk * MOE_BLOCK).astype(F32), col(5), 0.0)).astype(BF16)
         for blk in range(_MOE_BUF_BLOCKS)], axis=1)
    _combine_copies(False, tile, off_ref, cnt_ref, ys_ref, buf_ref, sem)
    mixed = jnp.dot(scatter, buf_ref[...], preferred_element_type=F32)
    o_ref[0] = x_ref[0] + gate_ref[0] * _rms(mixed, gn_ref[...])


def _combine(ys, routes, x, gn, gate, off, cnt):
    bsz, length, d = x.shape
    ntb = length // MOE_TILE
    tok = lambda t, o, c: (t // ntb, t % ntb, 0)
    grid_spec = pltpu.PrefetchScalarGridSpec(
        num_scalar_prefetch=2, grid=(bsz * ntb,),
        in_specs=[pl.BlockSpec(memory_space=pl.ANY),
                  pl.BlockSpec((1, MOE_TILE, LANES), tok),
                  pl.BlockSpec((1, MOE_TILE, d), tok),
                  pl.BlockSpec((1, d), lambda t, o, c: (0, 0)),
                  pl.BlockSpec((1, 1, d), lambda t, o, c: (t // ntb, 0, 0))],
        out_specs=pl.BlockSpec((1, MOE_TILE, d), tok),
        scratch_shapes=[pltpu.VMEM((_MOE_BUF_BLOCKS * MOE_BLOCK, d), BF16), pltpu.SemaphoreType.DMA(())])
    return pl.pallas_call(
        _combine_kernel, grid_spec=grid_spec,
        out_shape=jax.ShapeDtypeStruct((bsz, length, d), F32),
        compiler_params=_params("arbitrary"),
        name="moe_combine",
    )(off, cnt, ys, routes, x, gn, gate)


def _route_plan(counts, ntokens):
    ntiles = counts.shape[0]
    rows = (counts + (MOE_ALIGN - 1)) // MOE_ALIGN * MOE_ALIGN
    seg = (rows.sum(0) + MOE_BLOCK + MOE_TM - 1) // MOE_TM * MOE_TM
    seg_end = jnp.cumsum(seg)
    off = (seg_end - seg)[None, :] + jnp.cumsum(rows, axis=0) - rows
    nq = -(-(2 * ntokens + ntiles * N_EXPERTS * (MOE_ALIGN - 1) + N_EXPERTS * (MOE_BLOCK + MOE_TM - 1)) // MOE_TM)
    q = jnp.arange(nq, dtype=jnp.int32) * MOE_TM
    te = jnp.minimum(jnp.sum(q[:, None] >= seg_end[None, :], axis=1), N_EXPERTS - 1).astype(jnp.int32)
    tv = (q < seg_end[-1]).astype(jnp.int32)
    return off.reshape(-1).astype(jnp.int32), te, tv, nq * MOE_TM


def _moe(h, x, gn_pre, sc, sh, router_w, router_b, wg, wu, wd, gn_post, gate, tf):
    bsz, length, d = x.shape
    routes, routest, cnt = _router(x, gn_pre, sc, sh, router_w, router_b)
    counts = cnt[:, :, 0, :N_EXPERTS].astype(jnp.int32).reshape(-1, N_EXPERTS)
    off, te, tv, nrows = _route_plan(counts, bsz * length)
    cnt_flat = counts.reshape(-1)
    hs = _dispatch(h.reshape(bsz * length, d), routest, off, cnt_flat, nrows)
    ys = _gffn(hs, te, tv, wg, wu, wd, tf)
    return _combine(ys, routes, x, gn_post, gate, off, cnt_flat)


def _sincos_2d(rows, cols, dim):
    quarter = dim // 4
    omega = 1.0 / (POS_BASE ** (jnp.arange(quarter, dtype=F32) / quarter))

    def axis_emb(n):
        ang = jnp.arange(n, dtype=F32)[:, None] * omega[None, :]
        return jnp.concatenate([jnp.sin(ang), jnp.cos(ang)], axis=-1)

    er, ec = axis_emb(rows), axis_emb(cols)
    emb = jnp.concatenate([jnp.broadcast_to(er[:, None, :], (rows, cols, dim // 2)),
                           jnp.broadcast_to(ec[None, :, :], (rows, cols, dim // 2))], axis=-1)
    return emb.reshape(rows * cols, dim)


def _head_block_ones(width):
    h = jnp.arange(width, dtype=jnp.int32) // HEAD_DIM
    return (h[:, None] == h[None, :]).astype(BF16)


def _pad_lora(w2, d):
    z = jnp.zeros_like(w2[0])
    return jnp.concatenate([w2[0], z], axis=0) if d == 0 else jnp.concatenate([z, w2[1]], axis=0)


def _tile(length, pref):
    return pref if length % pref == 0 else length


def kernel(x, c, ctx, c_ctx, ada_w, ada_b, norm_g, w_in, w_out, rwkv_mu, rwkv_mu_w, rwkv_mu_a, rwkv_w0, rwkv_w2,
           rwkv_a0, rwkv_a2, rwkv_g2, rwkv_k_k, rwkv_k_a, rwkv_r_k, rwkv_gn_w, rwkv_gn_b, conv_w, ffn_w_gate,
           ffn_w_up, ffn_w_down, router_w, router_b, moe_w_gate, moe_w_up, moe_w_down):
    bsz, length, dim = x.shape
    ctx_len = ctx.shape[1]
    depth = ada_w.shape[0]
    d_rwkv = rwkv_w0.shape[-1]

    pos = _sincos_2d(length // GRID_W, GRID_W, dim).astype(x.dtype)
    bd = _head_block_ones(LANES)
    chan_w = _chan_dft_weights(N_FOUR_COLS // HEAD_DIM)
    dft_lat = _dft_tables(length)
    dft_ctx = _dft_tables(ctx_len)

    cvec = jnp.zeros((16, dim), F32).at[:bsz].set(c).at[bsz].set(c_ctx)
    mod = _adaln(cvec, ada_w, ada_b)
    zero_state = jnp.zeros((bsz, d_rwkv // LANES, LANES, LANES), F32)
    xc = ctx

    def mods(l, lo, hi):
        return [mod[l, lo:hi, i * dim:(i + 1) * dim][:, None, :] for i in range(6)]

    def mixers(l, zr, zc, zf, states0, period, dft, need_y=True):
        tm = _tile(zr.shape[1], 512)
        p = {"mu": rwkv_mu[l], "mu_w": rwkv_mu_w[l].reshape(1, -1), "mu_a": rwkv_mu_a[l].reshape(1, -1),
             "w0": rwkv_w0[l], "w2": jnp.stack([_pad_lora(rwkv_w2[l], d) for d in range(2)]), "a0": rwkv_a0[l],
             "a2": jnp.stack([_pad_lora(rwkv_a2[l], d) for d in range(2)]), "k_k": rwkv_k_k[l][None],
             "k_a": rwkv_k_a[l][None], "r_k": rwkv_r_k[l].reshape(1, -1), "gn_w": rwkv_gn_w[l][None],
             "gn_b": rwkv_gn_b[l][None], "bd": bd}
        ys, states = _scan(zr, states0[0], states0[1], p)
        if not need_y:
            return None, None, states
        yrc = _mix(ys[0], ys[1], zr, zc, rwkv_g2[l], conv_w[l], period, tm)
        yf = _fourier(zf, chan_w, dft, _tile(zf.shape[1], 512), 512)
        return yrc, yf, states

    def channel_mixer(l, h2, xn, gn3, gate2, sc2, sh2, gn2):
        tm = _tile(xn.shape[1], 512)
        i = l // 2
        if l % 2 == 0:
            return _ffn(h2, ffn_w_gate[i].astype(BF16), ffn_w_up[i].astype(BF16), ffn_w_down[i].astype(BF16),
                        xn, gn3, gate2, tm, 1408)
        return _moe(h2, xn, gn2, sc2, sh2, router_w[i], router_b[i][None], moe_w_gate[i].astype(BF16),
                    moe_w_up[i].astype(BF16), moe_w_down[i].astype(BF16), gn3, gate2, 1792)

    for l in range(depth):
        last = l == depth - 1
        w_in_l = w_in[l].astype(BF16)
        w_out_l = w_out[l].astype(BF16)
        gn = [norm_g[l, i][None] for i in range(4)]
        sh1, sc1, g1, sh2, sc2, g2 = mods(l, 0, bsz)
        csh1, csc1, cg1, csh2, csc2, cg2 = mods(l, bsz, bsz + 1)

        xc_flat = xc.reshape(1, bsz * ctx_len, dim)
        tmc = _tile(bsz * ctx_len, 512)
        zr, zc, zf = _inproj(xc_flat, None, gn[0], csc1, csh1, w_in_l, tmc)
        unflat = lambda t: t.reshape(bsz, ctx_len, t.shape[-1])
        yrc, yf, ctx_states = mixers(l, unflat(zr), unflat(zc), unflat(zf), (zero_state, zero_state), ctx_len,
                                     dft_ctx, need_y=not last)
        if not last:
            flat = lambda t: t.reshape(1, bsz * ctx_len, t.shape[-1])
            xcn, hc2 = _outproj(flat(yrc), flat(yf), w_out_l, xc_flat, gn[1], cg1, gn[2], csc2, csh2, tmc)
            xc = channel_mixer(l, hc2, xcn, gn[3], cg2, csc2, csh2, gn[2]).reshape(bsz, ctx_len, dim)

        tm = _tile(length, 512)
        if l == 0:
            zr, zc, zf, x = _inproj(x, pos, gn[0], sc1, sh1, w_in_l, tm)
        else:
            zr, zc, zf = _inproj(x, None, gn[0], sc1, sh1, w_in_l, tm)
        yrc, yf, _ = mixers(l, zr, zc, zf, ctx_states, GRID_W, dft_lat)
        xn, h2 = _outproj(yrc, yf, w_out_l, x, gn[1], g1, gn[2], sc2, sh2, _tile(length, 512))
        x = channel_mixer(l, h2, xn, gn[3], g2, sc2, sh2, gn[2])
    return x
```

```python
import functools

import jax
import jax.numpy as jnp
from jax import lax
from jax.experimental import pallas as pl
from jax.experimental.pallas import tpu as pltpu

F32 = jnp.float32
BF16 = jnp.bfloat16

HEAD_DIM = 64
GRID_W = 64
NORM_EPS = 1e-6
GN_EPS = 64e-5
POS_BASE = 10000.0
DECAY_SCALE = 0.6065306597126334
NORM_FLOOR = 1e-12
N_EXPERTS = 8
CHUNK = 64
LANES = 128
VMEM_LIMIT = 56 * 1024 * 1024


def _params(*sem):
    return pltpu.CompilerParams(dimension_semantics=sem, vmem_limit_bytes=VMEM_LIMIT)


def _dot(a, b):
    return jnp.dot(a.astype(BF16), b.astype(BF16), preferred_element_type=F32)


def _dot_nt(a, b):
    return lax.dot_general(a.astype(BF16), b.astype(BF16), (((1,), (1,)), ((), ())),
                           preferred_element_type=F32)


def _split(x):
    hi = x.astype(BF16)
    lo = (x - hi.astype(F32)).astype(BF16)
    return hi, lo


def _head_sums(x, ones_pair):
    rows = x.shape[0]
    tiles = x.shape[1] // LANES
    hi, lo = _split(x)
    stacked = jnp.concatenate([t[:, g * LANES:(g + 1) * LANES] for t in (hi, lo) for g in range(tiles)], axis=0)
    s = jnp.dot(stacked, ones_pair, preferred_element_type=F32)
    return jnp.concatenate([s[g * rows:(g + 1) * rows] + s[(tiles + g) * rows:(tiles + g + 1) * rows]
                            for g in range(tiles)], axis=1)


def _dot_split_rhs(w, x):
    hi, lo = _split(x)
    return (jnp.dot(w, hi, preferred_element_type=F32) + jnp.dot(w, lo, preferred_element_type=F32))


def _rms(x, g):
    ms = jnp.mean(x * x, axis=-1, keepdims=True)
    return x * lax.rsqrt(ms + NORM_EPS) * g


def _silu(x):
    return x * jax.nn.sigmoid(x)


def _adaln_kernel(c_ref, w_ref, b_ref, o_ref):
    s = _silu(c_ref[...])
    o_ref[0] = _dot(s, w_ref[0]) + b_ref[0]


def _adaln(cvec, ada_w, ada_b):
    nl, d, n = ada_w.shape
    rows = cvec.shape[0]
    tn = 1536
    return pl.pallas_call(
        _adaln_kernel,
        grid=(nl, n // tn),
        in_specs=[pl.BlockSpec((rows, d), lambda l, j: (0, 0)),
                  pl.BlockSpec((1, d, tn), lambda l, j: (l, 0, j)),
                  pl.BlockSpec((1, 1, tn), lambda l, j: (l, 0, j))],
        out_specs=pl.BlockSpec((1, rows, tn), lambda l, j: (l, 0, j)),
        out_shape=jax.ShapeDtypeStruct((nl, rows, n), F32),
        compiler_params=_params("arbitrary", "arbitrary"),
        name="adaln",
    )(cvec, ada_w, ada_b.reshape(nl, 1, n))


N_RWKV_COLS = 1920
N_CONV_COLS = 768
N_FOUR_COLS = 256


def _inproj_kernel(add_pos, *refs):
    if add_pos:
        x_ref, pos_ref, g_ref, sc_ref, sh_ref, w_ref, zr_ref, zc_ref, zf_ref, xp_ref = refs
        x = x_ref[0] + pos_ref[...]
        xp_ref[0] = x
    else:
        x_ref, g_ref, sc_ref, sh_ref, w_ref, zr_ref, zc_ref, zf_ref = refs
        x = x_ref[0]
    h = (_rms(x, g_ref[...]) * (1.0 + sc_ref[0]) + sh_ref[0]).astype(BF16)
    a, b = N_RWKV_COLS, N_RWKV_COLS + N_CONV_COLS
    zr_ref[0] = jnp.dot(h, w_ref[:, :a], preferred_element_type=F32)
    zc_ref[0] = jnp.dot(h, w_ref[:, a:b], preferred_element_type=F32).astype(BF16)
    zf_ref[0] = jnp.dot(h, w_ref[:, b:], preferred_element_type=F32).astype(BF16)


def _inproj(x, pos, g, sc, sh, w, tm):
    bsz, length, d = x.shape
    n = w.shape[1]
    add_pos = pos is not None
    row = lambda b, i: (b, i, 0)
    in_specs = [pl.BlockSpec((1, tm, d), row)]
    args = [x]
    if add_pos:
        in_specs.append(pl.BlockSpec((tm, d), lambda b, i: (i, 0)))
        args.append(pos)
    in_specs += [pl.BlockSpec((1, d), lambda b, i: (0, 0)),
                 pl.BlockSpec((1, 1, d), lambda b, i: (b, 0, 0)),
                 pl.BlockSpec((1, 1, d), lambda b, i: (b, 0, 0)),
                 pl.BlockSpec((d, n), lambda b, i: (0, 0))]
    args += [g, sc, sh, w]
    out_shape = [jax.ShapeDtypeStruct((bsz, length, N_RWKV_COLS), F32),
                 jax.ShapeDtypeStruct((bsz, length, N_CONV_COLS), BF16),
                 jax.ShapeDtypeStruct((bsz, length, N_FOUR_COLS), BF16)]
    out_specs = [pl.BlockSpec((1, tm, N_RWKV_COLS), row),
                 pl.BlockSpec((1, tm, N_CONV_COLS), row),
                 pl.BlockSpec((1, tm, N_FOUR_COLS), row)]
    if add_pos:
        out_shape.append(jax.ShapeDtypeStruct((bsz, length, d), F32))
        out_specs.append(pl.BlockSpec((1, tm, d), row))
    return pl.pallas_call(
        functools.partial(_inproj_kernel, add_pos),
        grid=(bsz, length // tm),
        in_specs=in_specs, out_specs=out_specs, out_shape=out_shape,
        compiler_params=_params("parallel", "parallel"),
        name="inproj",
    )(*args)


HALO = 8


def _rwkv_features(z, edge_row, d, mu_ref, muw_ref, mua_ref, w0_ref, w2_ref, a0_ref, a2_ref, kk_ref, ka_ref, bd):
    n = z.shape[0]
    rows = lax.broadcasted_iota(jnp.int32, (n, 1), 0)
    if d:
        zs = jnp.where(rows == n - 1, edge_row, pltpu.roll(z, n - 1, 0))
    else:
        zs = jnp.where(rows == 0, edge_row, pltpu.roll(z, 1, 0))
    dz = zs - z
    k = z[:, 0:512] + dz[:, 0:512] * mu_ref[d, 0:1, :]
    v = z[:, 512:1024] + dz[:, 512:1024] * mu_ref[d, 1:2, :]
    r = z[:, 1280:1792] + dz[:, 1280:1792] * mu_ref[d, 2:3, :]
    wl = z[:, 1024:1152] + dz[:, 1024:1152] * muw_ref[...]
    al = z[:, 1152:1280] + dz[:, 1152:1280] * mua_ref[...]
    lw = -DECAY_SCALE * jax.nn.sigmoid(w0_ref[d:d + 1, :] + _dot(jnp.tanh(wl), w2_ref[d]))
    a = jax.nn.sigmoid(a0_ref[d:d + 1, :] + _dot(al, a2_ref[d]))
    kk = k * kk_ref[...]
    ss = _head_sums(kk * kk, bd)
    kk = kk * lax.rsqrt(jnp.maximum(ss, NORM_FLOOR * NORM_FLOOR))
    return r, lw, k * (1.0 + (a - 1.0) * ka_ref[...]), v, -kk, kk * a


SOLVE_BLOCK = 8


def _unit_lower_solve(a_list, x_list, diag_blk, eye, first, side_work=()):
    side_work = list(side_work)

    def boundary():
        if side_work:
            side_work.pop(0)()

    def pair(m, w):
        return _dot(m, jnp.concatenate([jnp.where(first, w, 0.0), jnp.where(first, 0.0, w)], axis=0))

    def apply(m, x):
        zero = jnp.zeros((x.shape[0], LANES), x.dtype)
        return _dot(m, jnp.concatenate([jnp.concatenate([x[:, :LANES], zero], axis=1),
                                        jnp.concatenate([zero, x[:, LANES:]], axis=1)], axis=0))

    n = eye.shape[0]
    a_d = [jnp.where(diag_blk, a, 0.0) for a in a_list]
    a_o = [a - d for a, d in zip(a_list, a_d)]
    dinv = [eye + d for d in a_d]
    apow = a_d
    for _ in range(SOLVE_BLOCK.bit_length() - 2):
        apow = [pair(p, p) for p in apow]
        boundary()
        dinv = [d + pair(p, d) for p, d in zip(apow, dinv)]
        boundary()
    b = [pair(d, o) for d, o in zip(dinv, a_o)]
    boundary()
    x = [apply(d, x) for d, x in zip(dinv, x_list)]
    levels = (n // SOLVE_BLOCK).bit_length() - 1
    for level in range(levels):
        boundary()
        if level:
            b = [pair(m, m) for m in b]
        x = [v + apply(m, v) for m, v in zip(b, x)]
    while side_work:
        boundary()
    return x


def _scan_kernel(*refs):
    z_refs = (refs[0], refs[2])
    halo_refs = (refs[1], refs[3])
    zfirst_refs = refs[4:6]
    s0_refs = refs[6:8]
    feat_refs = refs[8:17]
    rk_ref, gw_ref, gb_ref, bd_ref = refs[17:21]
    y_refs = refs[21:23]
    st_refs = refs[23:25]
    feat_ref = refs[25]
    c = pl.program_id(1)

    @pl.when(c == 0)
    def _():
        for st_ref, s0_ref in zip(st_refs, s0_refs):
            st_ref[...] = s0_ref[...]
        no_token = jnp.zeros((1, zfirst_refs[0].shape[2]), F32)
        for d in range(2):
            for j, f in enumerate(_rwkv_features(zfirst_refs[d][0], no_token, d, *feat_refs, bd_ref[...])):
                feat_ref[d, j] = f

    n = CHUNK
    t_i = lax.broadcasted_iota(jnp.int32, (n, n), 0)
    s_i = lax.broadcasted_iota(jnp.int32, (n, n), 1)
    blk_shift = SOLVE_BLOCK.bit_length() - 1
    lane2 = lax.broadcasted_iota(jnp.int32, (n, LANES), 1)
    t_2 = lax.broadcasted_iota(jnp.int32, (n, LANES), 0)
    s_2 = lane2 & (n - 1)
    diag_blk2 = (s_2 >> blk_shift) == (t_2 >> blk_shift)
    eye2 = jnp.where(s_2 == t_2, 1.0, 0.0)
    first = lane2 < HEAD_DIM
    r_bd = lax.broadcasted_iota(jnp.int32, (LANES, LANES), 0) < HEAD_DIM
    c_bd = lax.broadcasted_iota(jnp.int32, (LANES, LANES), 1) < HEAD_DIM
    blockdiag = r_bd == c_bd
    incl2 = (s_2 <= t_2, s_2 >= t_2)
    strict2 = (s_2 < t_2, s_2 > t_2)

    npairs = st_refs[0].shape[1]
    sls = [slice(g * LANES, (g + 1) * LANES) for g in range(npairs)]
    items = [(d, g) for d in range(2) for g in range(npairs)]
    heads = [(i, j) for i in range(len(items)) for j in range(2)]
    feats = [[feat_ref[d, j] for j in range(6)] for d in range(2)]
    raw, scaled, w_tot = [], [], []
    for d in range(2):
        r, lw, k, v, a, b = feats[d]
        incl = (s_i >= t_i) if d else (s_i <= t_i)
        cum = _dot_split_rhs(jnp.where(incl, 1.0, 0.0).astype(BF16), lw)
        tot = cum[0:1, :] if d else cum[n - 1:n, :]
        w_inv = jnp.exp(-cum)
        w_all = jnp.exp(tot)
        b_inv = b * w_inv
        k_inv = k * w_inv
        raw.append((r, k, v))
        scaled.append((r * jnp.exp(cum), a * jnp.exp(cum - lw), b_inv, k_inv, b_inv * w_all, k_inv * w_all))
        w_tot.append(w_all)
    states = [st_refs[d][0, g] for d, g in items]

    v_g = [raw[d][2][:, sls[g]] for d, g in items]
    at_h = [(jnp.where(first, scaled[d][1][:, sls[g]], 0.0), jnp.where(first, 0.0, scaled[d][1][:, sls[g]]))
            for d, g in items]
    rt_h = [(jnp.where(first, scaled[d][0][:, sls[g]], 0.0), jnp.where(first, 0.0, scaled[d][0][:, sls[g]]))
            for d, g in items]
    s = [_dot_nt(jnp.concatenate([at_h[i][0], rt_h[i][0], at_h[i][1], rt_h[i][1]], axis=0),
                 jnp.concatenate([scaled[d][2][:, sls[g]], scaled[d][3][:, sls[g]]], axis=0))
         for i, (d, g) in enumerate(items)]
    sa = [(jnp.where(strict2[d], s[i][0:n], 0.0), jnp.where(strict2[d], s[i][2 * n:3 * n], 0.0))
          for i, (d, g) in enumerate(items)]
    sr = [(jnp.where(incl2[d], s[i][n:2 * n], 0.0), jnp.where(incl2[d], s[i][3 * n:4 * n], 0.0))
          for i, (d, g) in enumerate(items)]
    vv = [jnp.concatenate([x, x], axis=0) for x in (pltpu.roll(x, HEAD_DIM, 1) for x in v_g)]
    akv = [_dot(jnp.where(first, 0.0, sa[i][j]), vv[i]) for i, j in heads]
    rhs = [jnp.where(first, at_h[i][0], akv[2 * i]) if j == 0 else jnp.where(first, akv[2 * i + 1], at_h[i][1])
           for i, j in heads]
    edge_rows = (halo_refs[0][0, HALO - 1:HALO, :], halo_refs[1][0, 0:1, :])

    def next_features(d):
        def work():
            for j, f in enumerate(_rwkv_features(z_refs[d][0], edge_rows[d], d, *feat_refs, bd_ref[...])):
                feat_ref[d, j] = f
        return work

    idx = range(len(items))
    a_pair = [jnp.where(first, sa[i][0], pltpu.roll(sa[i][1], HEAD_DIM, 1)) for i in idx]
    sol = _unit_lower_solve(a_pair, [jnp.concatenate([rhs[2 * i], rhs[2 * i + 1]], axis=1) for i in idx],
                            diag_blk2, eye2, first,
                            side_work=[next_features(0), lambda: None, lambda: None, next_features(1)])
    pq = [sol[i // 2][:, (i % 2) * LANES:(i % 2 + 1) * LANES] for i in range(2 * len(items))]
    p_pair = [jnp.concatenate([jnp.where(first, pq[2 * i], 0.0), jnp.where(first, 0.0, pq[2 * i + 1])], axis=0)
              for i in idx]
    q_pair = [pltpu.roll(jnp.where(first, pq[2 * i + 1], pq[2 * i]), HEAD_DIM, 1) for i in idx]
    t_split = [jnp.concatenate(_split(states[i]), axis=0) for i in idx]
    pr = [jnp.concatenate([p_pair[i], rt_h[i][0], rt_h[i][1]], axis=0).astype(BF16) for i in idx]
    prt = [jnp.dot(jnp.concatenate([pr[i], pr[i]], axis=1), t_split[i], preferred_element_type=F32) for i in idx]
    pt = [m[0:2 * n] for m in prt]
    rtt = [m[2 * n:] for m in prt]
    u = [jnp.where(first, pt[i][0:n], pt[i][n:]) + q_pair[i] for i in idx]
    uv = [jnp.concatenate([u[i], v_g[i]], axis=0) for i in idx]
    yh = [_dot(sr[i][j], uv[i]) for i, j in heads]
    ys = [jnp.where(first, rtt[i][0:n] + yh[2 * i], rtt[i][n:] + yh[2 * i + 1]) for i in idx]
    t_new = [_dot(jnp.concatenate([scaled[d][4][:, sls[g]], scaled[d][5][:, sls[g]]], axis=0).T, uv[i])
             for i, (d, g) in enumerate(items)]
    for i, (d, g) in enumerate(items):
        w_col = jnp.broadcast_to(w_tot[d][:, sls[g]], (LANES, LANES)).T
        st_refs[d][0, g] = jnp.where(blockdiag, w_col * states[i] + t_new[i], 0.0)
    bd = bd_ref[...]
    inv = 1.0 / HEAD_DIM
    y = [jnp.concatenate(ys[d * npairs:(d + 1) * npairs], axis=1) for d in range(2)]
    mean = [_head_sums(y[d], bd) * inv for d in range(2)]
    dev = [y[d] - mean[d] for d in range(2)]
    var = [_head_sums(dev[d] * dev[d], bd) * inv for d in range(2)]
    bonus = [_head_sums(raw[d][0] * raw[d][1] * rk_ref[...], bd) * raw[d][2] for d in range(2)]
    for d in range(2):
        y_refs[d][0] = (dev[d] * lax.rsqrt(var[d] + GN_EPS) * gw_ref[...] + gb_ref[...] + bonus[d]).astype(BF16)


def _scan(zr, state_fwd, state_bwd, p):
    bsz, length, ncol = zr.shape
    width = p["w0"].shape[1]
    nc = length // CHUNK
    npairs = width // LANES
    per_chunk = CHUNK // HALO
    last_halo = length // HALO - 1
    fmap = lambda b, c: (b, c, 0)
    bmap = lambda b, c: (b, nc - 1 - c, 0)
    fnext = lambda c: jnp.minimum(c + 1, nc - 1)
    bnext = lambda c: jnp.maximum(nc - 2 - c, 0)
    fnext_map = lambda b, c: (b, fnext(c), 0)
    bnext_map = lambda b, c: (b, bnext(c), 0)
    prev_map = lambda b, c: (b, jnp.maximum(fnext(c) * per_chunk - 1, 0), 0)
    next_map = lambda b, c: (b, jnp.minimum((bnext(c) + 1) * per_chunk, last_halo), 0)
    full = lambda shape: pl.BlockSpec(shape, lambda b, c: (0,) * len(shape))
    st_spec = pl.BlockSpec((1, npairs, LANES, LANES), lambda b, c: (b, 0, 0, 0))
    y_shape = jax.ShapeDtypeStruct((bsz, length, width), BF16)
    st_shape = jax.ShapeDtypeStruct((bsz, npairs, LANES, LANES), F32)
    y_f, y_b, st_f, st_b = pl.pallas_call(
        _scan_kernel,
        grid=(bsz, nc),
        in_specs=[pl.BlockSpec((1, CHUNK, ncol), fnext_map), pl.BlockSpec((1, HALO, ncol), prev_map),
                  pl.BlockSpec((1, CHUNK, ncol), bnext_map), pl.BlockSpec((1, HALO, ncol), next_map),
                  pl.BlockSpec((1, CHUNK, ncol), lambda b, c: (b, 0, 0)),
                  pl.BlockSpec((1, CHUNK, ncol), lambda b, c: (b, nc - 1, 0)),
                  st_spec, st_spec,
                  full((2, 3, width)), full((1, LANES)), full((1, LANES)), full((2, width)),
                  full((2, LANES, width)), full((2, width)), full((2, LANES, width)), full((1, width)),
                  full((1, width)), full((1, width)), full((1, width)), full((1, width)), full((LANES, LANES))],
        out_specs=[pl.BlockSpec((1, CHUNK, width), fmap), pl.BlockSpec((1, CHUNK, width), bmap), st_spec, st_spec],
        out_shape=[y_shape, y_shape, st_shape, st_shape],
        scratch_shapes=[pltpu.VMEM((2, 6, CHUNK, width), F32)],
        compiler_params=_params("parallel", "arbitrary"),
        name="wkv_scan",
    )(zr, zr, zr, zr, zr, zr, state_fwd, state_bwd, p["mu"], p["mu_w"], p["mu_a"], p["w0"], p["w2"], p["a0"], p["a2"],
      p["k_k"], p["k_a"], p["r_k"], p["gn_w"], p["gn_b"], p["bd"])
    return (y_f, y_b), (st_f, st_b)


def _mix_kernel(period, tm, yf_ref, yb_ref, glo_ref, zc_ref, g2_ref, cw_ref, o_ref):
    gate = _dot(jax.nn.sigmoid(glo_ref[0]), g2_ref[...])
    yr = (yf_ref[0].astype(F32) + yb_ref[0].astype(F32)) * gate
    zc = zc_ref[0].astype(F32)
    u, gate_b, gate_c = zc[:, 0:256], zc[:, 256:512], zc[:, 512:768]
    hc = gate_c * u
    pos = lax.broadcasted_iota(jnp.int32, (tm, 1), 0) % period
    prev = jnp.where(pos == 0, 0.0, pltpu.roll(hc, 1, 0))
    nxt = jnp.where(pos == period - 1, 0.0, pltpu.roll(hc, tm - 1, 0))
    conv = prev * cw_ref[0:1, :] + hc * cw_ref[1:2, :] + nxt * cw_ref[2:3, :]
    o_ref[0, :, 0:512] = yr.astype(BF16)
    o_ref[0, :, 512:768] = (gate_b * conv).astype(BF16)


def _mix(ys_f, ys_b, zr, zc, g2, conv_w, period, tm):
    bsz, length, _ = zc.shape
    row = lambda b, i: (b, i, 0)
    glo_block = (N_RWKV_COLS - 128) // 128
    return pl.pallas_call(
        functools.partial(_mix_kernel, period, tm),
        grid=(bsz, length // tm),
        in_specs=[pl.BlockSpec((1, tm, 512), row), pl.BlockSpec((1, tm, 512), row),
                  pl.BlockSpec((1, tm, 128), lambda b, i: (b, i, glo_block)),
                  pl.BlockSpec((1, tm, N_CONV_COLS), row),
                  pl.BlockSpec((128, 512), lambda b, i: (0, 0)),
                  pl.BlockSpec((3, 256), lambda b, i: (0, 0))],
        out_specs=pl.BlockSpec((1, tm, 768), row),
        out_shape=jax.ShapeDtypeStruct((bsz, length, 768), BF16),
        compiler_params=_params("parallel", "parallel"),
        name="gate_conv",
    )(ys_f, ys_b, zr, zc, g2, conv_w)


def _chan_dft_kernel(f_ref, w_ref, o_ref):
    xcs = jnp.dot(f_ref[0].astype(BF16), w_ref[...], preferred_element_type=F32)
    o_ref[0, 0] = xcs[:, 0:256].astype(BF16)
    o_ref[0, 1] = xcs[:, 256:512].astype(BF16)


def _pos_dft_kernel(nb, cs_ref, x_ref, o_ref, acc_ref):
    kstep = pl.program_id(1)

    @pl.when(kstep == 0)
    def _():
        acc_ref[...] = jnp.zeros_like(acc_ref)

    cs = cs_ref[...]
    for b in range(nb):
        acc_ref[b] += jnp.dot(cs, x_ref[b], preferred_element_type=F32)

    @pl.when(kstep == pl.num_programs(1) - 1)
    def _():
        o_ref[...] = acc_ref[...].astype(BF16)


def _fourier(zf, chan_w, pos_cs, tm, tk):
    bsz, length, ch = zf.shape
    tr = min(512, length)
    xcs = pl.pallas_call(
        _chan_dft_kernel,
        grid=(bsz, length // tr),
        in_specs=[pl.BlockSpec((1, tr, ch), lambda b, i: (b, i, 0)),
                  pl.BlockSpec((ch, 2 * ch), lambda b, i: (0, 0))],
        out_specs=pl.BlockSpec((1, 2, tr, ch), lambda b, i: (b, 0, i, 0)),
        out_shape=jax.ShapeDtypeStruct((bsz, 2, length, ch), BF16),
        compiler_params=_params("parallel", "parallel"),
        name="chan_dft",
    )(zf, chan_w)
    xcs = xcs.reshape(bsz, 2 * length, ch)
    return pl.pallas_call(
        functools.partial(_pos_dft_kernel, bsz),
        grid=(length // tm, 2 * length // tk),
        in_specs=[pl.BlockSpec((tm, tk), lambda i, kk: (i, kk)),
                  pl.BlockSpec((bsz, tk, ch), lambda i, kk: (0, kk, 0))],
        out_specs=pl.BlockSpec((bsz, tm, ch), lambda i, kk: (0, i, 0)),
        out_shape=jax.ShapeDtypeStruct((bsz, length, ch), BF16),
        scratch_shapes=[pltpu.VMEM((bsz, tm, ch), F32)],
        compiler_params=_params("parallel", "arbitrary"),
        name="pos_dft",
    )(pos_cs, xcs)


def _dft_tables(length):
    m = jnp.arange(length, dtype=jnp.int32)[None, :]
    l1 = jnp.arange(length // 64, dtype=jnp.int32)[:, None] * 64
    l2 = jnp.arange(64, dtype=jnp.int32)[:, None]
    ang = lambda l: ((l * m) % length).astype(F32) * (2.0 * jnp.pi / length)
    c1, s1, c2, s2 = jnp.cos(ang(l1)), jnp.sin(ang(l1)), jnp.cos(ang(l2)), jnp.sin(ang(l2))
    scale = 1.0 / jnp.sqrt(jnp.float32(length))
    cos = (c1[:, None, :] * c2[None] - s1[:, None, :] * s2[None]).reshape(length, length) * scale
    sin = (s1[:, None, :] * c2[None] + c1[:, None, :] * s2[None]).reshape(length, length) * scale
    return jnp.concatenate([cos, sin], axis=1).astype(BF16)


def _chan_dft_weights(ngroups):
    j = jnp.arange(HEAD_DIM, dtype=jnp.int32)
    ang = ((j[:, None] * j[None, :]) % HEAD_DIM).astype(F32) * (2.0 * jnp.pi / HEAD_DIM)
    eye = jnp.eye(ngroups, dtype=F32)
    scale = 1.0 / jnp.sqrt(jnp.float32(HEAD_DIM))
    cc = jnp.kron(eye, jnp.cos(ang) * scale)
    sc = jnp.kron(eye, jnp.sin(ang) * scale)
    return jnp.concatenate([cc, -sc], axis=1).astype(BF16)


def _outproj_kernel(yrc_ref, yf_ref, w_ref, x_ref, gn1_ref, gate_ref, gn2_ref, sc_ref, sh_ref, xo_ref, h_ref):
    nrc = yrc_ref.shape[2]
    o = (jnp.dot(yrc_ref[0], w_ref[0:nrc, :], preferred_element_type=F32) +
         jnp.dot(yf_ref[0], w_ref[nrc:, :], preferred_element_type=F32))
    xn = x_ref[0] + gate_ref[0] * _rms(o, gn1_ref[...])
    xo_ref[0] = xn
    h_ref[0] = (_rms(xn, gn2_ref[...]) * (1.0 + sc_ref[0]) + sh_ref[0]).astype(BF16)


def _outproj(yrc, yf, w, x, gn1, gate, gn2, sc, sh, tm):
    bsz, length, d = x.shape
    row = lambda b, i: (b, i, 0)
    vec = pl.BlockSpec((1, d), lambda b, i: (0, 0))
    mod = pl.BlockSpec((1, 1, d), lambda b, i: (b, 0, 0))
    return pl.pallas_call(
        _outproj_kernel,
        grid=(bsz, length // tm),
        in_specs=[pl.BlockSpec((1, tm, yrc.shape[2]), row), pl.BlockSpec((1, tm, yf.shape[2]), row),
                  pl.BlockSpec(w.shape, lambda b, i: (0, 0)), pl.BlockSpec((1, tm, d), row),
                  vec, mod, vec, mod, mod],
        out_specs=[pl.BlockSpec((1, tm, d), row), pl.BlockSpec((1, tm, d), row)],
        out_shape=[jax.ShapeDtypeStruct((bsz, length, d), F32), jax.ShapeDtypeStruct((bsz, length, d), BF16)],
        compiler_params=_params("parallel", "parallel"),
        name="outproj",
    )(yrc, yf, w, x, gn1, gate, gn2, sc, sh)


FF_SUB = 256


def _swiglu_partial(h, wg, wu, wd, tf):
    def up(c, n):
        return (jnp.dot(h, wg(c, n), preferred_element_type=F32), jnp.dot(h, wu(c, n), preferred_element_type=F32))

    def down(gu, c, n):
        return jnp.dot((_silu(gu[0]) * gu[1]).astype(BF16), wd(c, n), preferred_element_type=F32)

    out = None
    prev = None
    for cut in [(c, min(FF_SUB, tf - c)) for c in range(0, tf, FF_SUB)]:
        gu = up(*cut)
        if prev is not None:
            part = down(*prev)
            out = part if out is None else out + part
        prev = (gu,) + cut
    part = down(*prev)
    return part if out is None else out + part


def _ffn_kernel(h_ref, wg_ref, wu_ref, wd_ref, x_ref, gn_ref, gate_ref, o_ref, acc_ref):
    j = pl.program_id(2)

    @pl.when(j == 0)
    def _():
        acc_ref[...] = jnp.zeros_like(acc_ref)

    acc_ref[...] += _swiglu_partial(h_ref[0], lambda c, n: wg_ref[:, c:c + n], lambda c, n: wu_ref[:, c:c + n],
                                    lambda c, n: wd_ref[c:c + n, :], wg_ref.shape[1])

    @pl.when(j == pl.num_programs(2) - 1)
    def _():
        o_ref[0] = x_ref[0] + gate_ref[0] * _rms(acc_ref[...], gn_ref[...])


def _ffn(h, wg, wu, wd, x, gn, gate, tm, tf):
    bsz, length, d = x.shape
    ff = wg.shape[1]
    row = lambda b, i, j: (b, i, 0)
    return pl.pallas_call(
        _ffn_kernel,
        grid=(bsz, length // tm, ff // tf),
        in_specs=[pl.BlockSpec((1, tm, d), row),
                  pl.BlockSpec((d, tf), lambda b, i, j: (0, j)),
                  pl.BlockSpec((d, tf), lambda b, i, j: (0, j)),
                  pl.BlockSpec((tf, d), lambda b, i, j: (j, 0)),
                  pl.BlockSpec((1, tm, d), row),
                  pl.BlockSpec((1, d), lambda b, i, j: (0, 0)),
                  pl.BlockSpec((1, 1, d), lambda b, i, j: (b, 0, 0))],
        out_specs=pl.BlockSpec((1, tm, d), row),
        out_shape=jax.ShapeDtypeStruct((bsz, length, d), F32),
        scratch_shapes=[pltpu.VMEM((tm, d), F32)],
        compiler_params=_params("parallel", "parallel", "arbitrary"),
        name="ffn",
    )(h, wg, wu, wd, x, gn, gate)


MOE_TILE = 512
MOE_BLOCK = 128
MOE_ALIGN = 16
MOE_TM = 1024


def _router_kernel(x_ref, gn_ref, sc_ref, sh_ref, w_ref, b_ref, tri_ref, routes_ref, routest_ref, cnt_ref):
    h = _rms(x_ref[0], gn_ref[...]) * (1.0 + sc_ref[0]) + sh_ref[0]
    logits = jnp.dot(h, w_ref[...], preferred_element_type=F32, precision=lax.Precision.HIGHEST) + b_ref[...]
    ne = float(logits.shape[1])
    idx = lax.broadcasted_iota(jnp.int32, logits.shape, 1).astype(F32)
    m1 = jnp.max(logits, axis=1, keepdims=True)
    i1 = jnp.min(jnp.where(logits == m1, idx, ne), axis=1, keepdims=True)
    rest = jnp.where(idx == i1, -jnp.inf, logits)
    m2 = jnp.max(rest, axis=1, keepdims=True)
    i2 = jnp.min(jnp.where(rest == m2, idx, ne), axis=1, keepdims=True)
    e2 = jnp.exp(m2 - m1)
    p1 = 1.0 / (1.0 + e2)
    p2 = e2 / (1.0 + e2)
    sel = jnp.where(idx == i1, 1.0, jnp.where(idx == i2, 1.0, 0.0))
    before = jnp.dot(tri_ref[...], sel.astype(BF16), preferred_element_type=F32)
    rank1 = jnp.sum(jnp.where(idx == i1, before, 0.0), axis=1, keepdims=True)
    rank2 = jnp.sum(jnp.where(idx == i2, before, 0.0), axis=1, keepdims=True)
    routes = jnp.zeros_like(logits)
    for lane, val in enumerate((i1, i2, rank1, rank2, p1, p2)):
        routes = jnp.where(idx == float(lane), val, routes)
    routes_ref[0] = routes
    routest_ref[0] = jnp.transpose(routes)[0:N_EXPERTS, :]
    cnt_ref[0, 0] = jnp.sum(sel, axis=0, keepdims=True)


def _router(x, gn, sc, sh, w, b):
    bsz, length, d = x.shape
    tm = MOE_TILE
    pad = LANES - w.shape[1]
    w = jnp.pad(w, ((0, 0), (0, pad)))
    b = jnp.pad(b, ((0, 0), (0, pad)), constant_values=-1e30)
    ne = LANES
    t = jnp.arange(tm, dtype=jnp.int32)
    tri = (t[None, :] < t[:, None]).astype(BF16)
    row = lambda bb, i: (bb, i, 0)
    vec = pl.BlockSpec((1, d), lambda bb, i: (0, 0))
    mod = pl.BlockSpec((1, 1, d), lambda bb, i: (bb, 0, 0))
    nt = length // tm
    return pl.pallas_call(
        _router_kernel,
        grid=(bsz, nt),
        in_specs=[pl.BlockSpec((1, tm, d), row), vec, mod, mod,
                  pl.BlockSpec((d, ne), lambda bb, i: (0, 0)), pl.BlockSpec((1, ne), lambda bb, i: (0, 0)),
                  pl.BlockSpec((tm, tm), lambda bb, i: (0, 0))],
        out_specs=[pl.BlockSpec((1, tm, ne), row),
                   pl.BlockSpec((1, N_EXPERTS, tm), lambda bb, i: (bb, 0, i)),
                   pl.BlockSpec((1, 1, 1, ne), lambda bb, i: (bb, i, 0, 0))],
        out_shape=[jax.ShapeDtypeStruct((bsz, length, ne), F32),
                   jax.ShapeDtypeStruct((bsz, N_EXPERTS, length), F32),
                   jax.ShapeDtypeStruct((bsz, nt, 1, ne), F32)],
        compiler_params=_params("parallel", "parallel"),
        name="router",
    )(x, gn, sc, sh, w, b, tri)


_MOE_SHIFT = MOE_BLOCK.bit_length() - 1
_MOE_STAGE_ROWS = -(-(2 * MOE_TILE + N_EXPERTS * (MOE_ALIGN - 1)) // LANES) * LANES


def _align_rows(c):
    return ((c + (MOE_ALIGN - 1)) >> (MOE_ALIGN.bit_length() - 1)) << (MOE_ALIGN.bit_length() - 1)


def _run_copies(start, to_sorted, tile, off_ref, cnt_ref, stage_ref, sorted_ref, sem):
    def run(stage_row, sorted_row, nrows):
        stage = stage_ref.at[pl.ds(pl.multiple_of(stage_row, MOE_ALIGN), nrows)]
        sort = sorted_ref.at[pl.ds(pl.multiple_of(sorted_row, MOE_ALIGN), nrows)]
        cp = pltpu.make_async_copy(stage, sort, sem) if to_sorted else pltpu.make_async_copy(sort, stage, sem)
        if start:
            cp.start()
        else:
            cp.wait()

    stage0 = jnp.int32(0)
    for e in range(N_EXPERTS):
        rows = _align_rows(cnt_ref[tile * N_EXPERTS + e])
        sorted0 = off_ref[tile * N_EXPERTS + e]
        nfull = rows >> _MOE_SHIFT

        def full(rb, carry, stage0=stage0, sorted0=sorted0):
            run(stage0 + rb * MOE_BLOCK, sorted0 + rb * MOE_BLOCK, MOE_BLOCK)
            return carry

        lax.fori_loop(0, nfull, full, 0)
        done = nfull * MOE_BLOCK
        piece = MOE_BLOCK // 2
        while piece >= MOE_ALIGN:
            has = (rows & piece) != 0

            @pl.when(has)
            def _(done=done, piece=piece, stage0=stage0, sorted0=sorted0):
                run(stage0 + done, sorted0 + done, piece)

            done = done + jnp.where(has, piece, 0)
            piece //= 2
        stage0 = stage0 + rows


def _staging_rows(tile, cnt_ref, i1, i2, rank1, rank2):
    row1 = jnp.zeros_like(i1)
    row2 = jnp.zeros_like(i2)
    stage0 = jnp.int32(0)
    for e in range(N_EXPERTS):
        first_row = stage0.astype(F32)
        row1 = jnp.where(i1 == float(e), first_row, row1)
        row2 = jnp.where(i2 == float(e), first_row, row2)
        stage0 = stage0 + _align_rows(cnt_ref[tile * N_EXPERTS + e])
    return row1 + rank1, row2 + rank2


def _dispatch_kernel(off_ref, cnt_ref, h_ref, routest_ref, hs_init_ref, hs_ref, buf_ref, sem):
    del hs_init_ref
    tile = pl.program_id(0)
    rt = routest_ref[0]
    row1, row2 = _staging_rows(tile, cnt_ref, rt[0:1], rt[1:2], rt[2:3], rt[3:4])
    rows = lax.broadcasted_iota(jnp.int32, (LANES, 1), 0)
    onehot = jnp.concatenate(
        [jnp.where(row1 == (rows + part * LANES).astype(F32), 1.0,
                   jnp.where(row2 == (rows + part * LANES).astype(F32), 1.0, 0.0)).astype(BF16)
         for part in range(_MOE_STAGE_ROWS // LANES)], axis=0)
    buf_ref[...] = jnp.dot(onehot, h_ref[...], preferred_element_type=F32).astype(BF16)
    _run_copies(True, True, tile, off_ref, cnt_ref, buf_ref, hs_ref, sem)
    _run_copies(False, True, tile, off_ref, cnt_ref, buf_ref, hs_ref, sem)


def _dispatch(h, rankt, off, cnt, nrows):
    n, d = h.shape
    ntb = rankt.shape[2] // MOE_TILE
    grid_spec = pltpu.PrefetchScalarGridSpec(
        num_scalar_prefetch=2, grid=(n // MOE_TILE,),
        in_specs=[pl.BlockSpec((MOE_TILE, d), lambda t, o, c: (t, 0)),
                  pl.BlockSpec((1, N_EXPERTS, MOE_TILE), lambda t, o, c: (t // ntb, 0, t % ntb)),
                  pl.BlockSpec(memory_space=pl.ANY)],
        out_specs=pl.BlockSpec(memory_space=pl.ANY),
        scratch_shapes=[pltpu.VMEM((_MOE_STAGE_ROWS, d), BF16), pltpu.SemaphoreType.DMA(())])
    return pl.pallas_call(
        _dispatch_kernel, grid_spec=grid_spec,
        out_shape=jax.ShapeDtypeStruct((nrows, d), BF16),
        input_output_aliases={4: 0},
        compiler_params=_params("arbitrary"),
        name="moe_dispatch",
    )(off, cnt, h, rankt, jnp.zeros((nrows, d), BF16))


def _gffn_kernel(te_ref, tv_ref, h_ref, wg_ref, wu_ref, wd_ref, o_ref, acc_ref):
    q = pl.program_id(0)
    j = pl.program_id(1)

    @pl.when(j == 0)
    def _():
        acc_ref[...] = jnp.zeros_like(acc_ref)

    @pl.when(tv_ref[q] > 0)
    def _():
        acc_ref[...] += _swiglu_partial(h_ref[...], lambda c, n: wg_ref[0, :, c:c + n],
                                        lambda c, n: wu_ref[0, :, c:c + n], lambda c, n: wd_ref[0, c:c + n, :],
                                        wg_ref.shape[2])

    @pl.when(j == pl.num_programs(1) - 1)
    def _():
        o_ref[...] = acc_ref[...].astype(BF16)


def _gffn(hs, te, tv, wg, wu, wd, tf):
    nrows, d = hs.shape
    ff = wg.shape[2]
    grid_spec = pltpu.PrefetchScalarGridSpec(
        num_scalar_prefetch=2, grid=(nrows // MOE_TM, ff // tf),
        in_specs=[pl.BlockSpec((MOE_TM, d), lambda q, j, te, tv: (q, 0)),
                  pl.BlockSpec((1, d, tf), lambda q, j, te, tv: (te[q], 0, j)),
                  pl.BlockSpec((1, d, tf), lambda q, j, te, tv: (te[q], 0, j)),
                  pl.BlockSpec((1, tf, d), lambda q, j, te, tv: (te[q], j, 0))],
        out_specs=pl.BlockSpec((MOE_TM, d), lambda q, j, te, tv: (q, 0)),
        scratch_shapes=[pltpu.VMEM((MOE_TM, d), F32)])
    return pl.pallas_call(
        _gffn_kernel, grid_spec=grid_spec,
        out_shape=jax.ShapeDtypeStruct((nrows, d), BF16),
        compiler_params=_params("arbitrary", "arbitrary"),
        name="moe_experts",
    )(te, tv, hs, wg, wu, wd)


def _combine_kernel(off_ref, cnt_ref, ys_ref, routes_ref, x_ref, gn_ref, gate_ref, o_ref, buf_ref, sem):
    tile = pl.program_id(0)

    @pl.when(tile == 0)
    def _():
        buf_ref[...] = jnp.zeros_like(buf_ref)

    _run_copies(True, False, tile, off_ref, cnt_ref, buf_ref, ys_ref, sem)
    routes = routes_ref[0]
    col = lambda j: routes[:, j:j + 1]
    row1, row2 = _staging_rows(tile, cnt_ref, col(0), col(1), col(2), col(3))
    lanes = lax.broadcasted_iota(jnp.int32, (1, LANES), 1)
    scatter = jnp.concatenate(
        [(jnp.where(row1 == (lanes + part * LANES).astype(F32), col(4), 0.0) +
          jnp.where(row2 == (lanes + part * LANES).astype(F32), col(5), 0.0)).astype(BF16)
         for part in range(_MOE_STAGE_ROWS // LANES)], axis=1)
    _run_copies(False, False, tile, off_ref, cnt_ref, buf_ref, ys_ref, sem)
    mixed = jnp.dot(scatter, buf_ref[...], preferred_element_type=F32)
    o_ref[0] = x_ref[0] + gate_ref[0] * _rms(mixed, gn_ref[...])


def _combine(ys, routes, x, gn, gate, off, cnt):
    bsz, length, d = x.shape
    ntb = length // MOE_TILE
    tok = lambda t, o, c: (t // ntb, t % ntb, 0)
    grid_spec = pltpu.PrefetchScalarGridSpec(
        num_scalar_prefetch=2, grid=(bsz * ntb,),
        in_specs=[pl.BlockSpec(memory_space=pl.ANY),
                  pl.BlockSpec((1, MOE_TILE, LANES), tok),
                  pl.BlockSpec((1, MOE_TILE, d), tok),
                  pl.BlockSpec((1, d), lambda t, o, c: (0, 0)),
                  pl.BlockSpec((1, 1, d), lambda t, o, c: (t // ntb, 0, 0))],
        out_specs=pl.BlockSpec((1, MOE_TILE, d), tok),
        scratch_shapes=[pltpu.VMEM((_MOE_STAGE_ROWS, d), BF16), pltpu.SemaphoreType.DMA(())])
    return pl.pallas_call(
        _combine_kernel, grid_spec=grid_spec,
        out_shape=jax.ShapeDtypeStruct((bsz, length, d), F32),
        compiler_params=_params("arbitrary"),
        name="moe_combine",
    )(off, cnt, ys, routes, x, gn, gate)


def _route_plan(counts, ntokens):
    ntiles = counts.shape[0]
    rows = (counts + (MOE_ALIGN - 1)) // MOE_ALIGN * MOE_ALIGN
    seg = (rows.sum(0) + MOE_BLOCK + MOE_TM - 1) // MOE_TM * MOE_TM
    seg_end = jnp.cumsum(seg)
    off = (seg_end - seg)[None, :] + jnp.cumsum(rows, axis=0) - rows
    nq = -(-(2 * ntokens + ntiles * N_EXPERTS * (MOE_ALIGN - 1) + N_EXPERTS * (MOE_BLOCK + MOE_TM - 1)) // MOE_TM)
    q = jnp.arange(nq, dtype=jnp.int32) * MOE_TM
    te = jnp.minimum(jnp.sum(q[:, None] >= seg_end[None, :], axis=1), N_EXPERTS - 1).astype(jnp.int32)
    tv = (q < seg_end[-1]).astype(jnp.int32)
    return off.reshape(-1).astype(jnp.int32), te, tv, nq * MOE_TM


def _moe(h, x, gn_pre, sc, sh, router_w, router_b, wg, wu, wd, gn_post, gate, tf):
    bsz, length, d = x.shape
    routes, routest, cnt = _router(x, gn_pre, sc, sh, router_w, router_b)
    counts = cnt[:, :, 0, :N_EXPERTS].astype(jnp.int32).reshape(-1, N_EXPERTS)
    off, te, tv, nrows = _route_plan(counts, bsz * length)
    cnt_flat = counts.reshape(-1)
    hs = _dispatch(h.reshape(bsz * length, d), routest, off, cnt_flat, nrows)
    ys = _gffn(hs, te, tv, wg, wu, wd, tf)
    return _combine(ys, routes, x, gn_post, gate, off, cnt_flat)


def _sincos_2d(rows, cols, dim):
    quarter = dim // 4
    omega = 1.0 / (POS_BASE ** (jnp.arange(quarter, dtype=F32) / quarter))

    def axis_emb(n):
        ang = jnp.arange(n, dtype=F32)[:, None] * omega[None, :]
        return jnp.concatenate([jnp.sin(ang), jnp.cos(ang)], axis=-1)

    er, ec = axis_emb(rows), axis_emb(cols)
    emb = jnp.concatenate([jnp.broadcast_to(er[:, None, :], (rows, cols, dim // 2)),
                           jnp.broadcast_to(ec[None, :, :], (rows, cols, dim // 2))], axis=-1)
    return emb.reshape(rows * cols, dim)


def _head_block_ones(width):
    h = jnp.arange(width, dtype=jnp.int32) // HEAD_DIM
    return (h[:, None] == h[None, :]).astype(BF16)


def _pad_lora(w2, d):
    z = jnp.zeros_like(w2[0])
    return jnp.concatenate([w2[0], z], axis=0) if d == 0 else jnp.concatenate([z, w2[1]], axis=0)


def _tile(length, pref):
    return pref if length % pref == 0 else length


def kernel(x, c, ctx, c_ctx, ada_w, ada_b, norm_g, w_in, w_out, rwkv_mu, rwkv_mu_w, rwkv_mu_a, rwkv_w0, rwkv_w2,
           rwkv_a0, rwkv_a2, rwkv_g2, rwkv_k_k, rwkv_k_a, rwkv_r_k, rwkv_gn_w, rwkv_gn_b, conv_w, ffn_w_gate,
           ffn_w_up, ffn_w_down, router_w, router_b, moe_w_gate, moe_w_up, moe_w_down):
    bsz, length, dim = x.shape
    ctx_len = ctx.shape[1]
    depth = ada_w.shape[0]
    d_rwkv = rwkv_w0.shape[-1]

    pos = _sincos_2d(length // GRID_W, GRID_W, dim).astype(x.dtype)
    bd = _head_block_ones(LANES)
    chan_w = _chan_dft_weights(N_FOUR_COLS // HEAD_DIM)
    dft_lat = _dft_tables(length)
    dft_ctx = _dft_tables(ctx_len)

    cvec = jnp.zeros((16, dim), F32).at[:bsz].set(c).at[bsz].set(c_ctx)
    mod = _adaln(cvec, ada_w, ada_b)
    zero_state = jnp.zeros((bsz, d_rwkv // LANES, LANES, LANES), F32)
    xc = ctx

    def mods(l, lo, hi):
        return [mod[l, lo:hi, i * dim:(i + 1) * dim][:, None, :] for i in range(6)]

    def mixers(l, zr, zc, zf, states0, period, dft, need_y=True):
        tm = _tile(zr.shape[1], 512)
        p = {"mu": rwkv_mu[l], "mu_w": rwkv_mu_w[l].reshape(1, -1), "mu_a": rwkv_mu_a[l].reshape(1, -1),
             "w0": rwkv_w0[l], "w2": jnp.stack([_pad_lora(rwkv_w2[l], d) for d in range(2)]), "a0": rwkv_a0[l],
             "a2": jnp.stack([_pad_lora(rwkv_a2[l], d) for d in range(2)]), "k_k": rwkv_k_k[l][None],
             "k_a": rwkv_k_a[l][None], "r_k": rwkv_r_k[l].reshape(1, -1), "gn_w": rwkv_gn_w[l][None],
             "gn_b": rwkv_gn_b[l][None], "bd": bd}
        ys, states = _scan(zr, states0[0], states0[1], p)
        if not need_y:
            return None, None, states
        yrc = _mix(ys[0], ys[1], zr, zc, rwkv_g2[l], conv_w[l], period, tm)
        yf = _fourier(zf, chan_w, dft, _tile(zf.shape[1], 512), 512)
        return yrc, yf, states

    def channel_mixer(l, h2, xn, gn3, gate2, sc2, sh2, gn2):
        tm = _tile(xn.shape[1], 512)
        i = l // 2
        if l % 2 == 0:
            return _ffn(h2, ffn_w_gate[i].astype(BF16), ffn_w_up[i].astype(BF16), ffn_w_down[i].astype(BF16),
                        xn, gn3, gate2, tm, 1408)
        return _moe(h2, xn, gn2, sc2, sh2, router_w[i], router_b[i][None], moe_w_gate[i].astype(BF16),
                    moe_w_up[i].astype(BF16), moe_w_down[i].astype(BF16), gn3, gate2, 1792)

    for l in range(depth):
        last = l == depth - 1
        w_in_l = w_in[l].astype(BF16)
        w_out_l = w_out[l].astype(BF16)
        gn = [norm_g[l, i][None] for i in range(4)]
        sh1, sc1, g1, sh2, sc2, g2 = mods(l, 0, bsz)
        csh1, csc1, cg1, csh2, csc2, cg2 = mods(l, bsz, bsz + 1)

        xc_flat = xc.reshape(1, bsz * ctx_len, dim)
        tmc = _tile(bsz * ctx_len, 512)
        zr, zc, zf = _inproj(xc_flat, None, gn[0], csc1, csh1, w_in_l, tmc)
        unflat = lambda t: t.reshape(bsz, ctx_len, t.shape[-1])
        yrc, yf, ctx_states = mixers(l, unflat(zr), unflat(zc), unflat(zf), (zero_state, zero_state), ctx_len,
                                     dft_ctx, need_y=not last)
        if not last:
            flat = lambda t: t.reshape(1, bsz * ctx_len, t.shape[-1])
            xcn, hc2 = _outproj(flat(yrc), flat(yf), w_out_l, xc_flat, gn[1], cg1, gn[2], csc2, csh2, tmc)
            xc = channel_mixer(l, hc2, xcn, gn[3], cg2, csc2, csh2, gn[2]).reshape(bsz, ctx_len, dim)

        tm = _tile(length, 512)
        if l == 0:
            zr, zc, zf, x = _inproj(x, pos, gn[0], sc1, sh1, w_in_l, tm)
        else:
            zr, zc, zf = _inproj(x, None, gn[0], sc1, sh1, w_in_l, tm)
        yrc, yf, _ = mixers(l, zr, zc, zf, ctx_states, GRID_W, dft_lat)
        xn, h2 = _outproj(yrc, yf, w_out_l, x, gn[1], g1, gn[2], sc2, sh2, _tile(length, 512))
        x = channel_mixer(l, h2, xn, gn[3], g2, sc2, sh2, gn[2])
    return x
```

```python
import functools

import jax
import jax.numpy as jnp
from jax import lax
from jax.experimental import pallas as pl
from jax.experimental.pallas import tpu as pltpu

F32 = jnp.float32
BF16 = jnp.bfloat16

HEAD_DIM = 64
GRID_W = 64
NORM_EPS = 1e-6
GN_EPS = 64e-5
POS_BASE = 10000.0
DECAY_SCALE = 0.6065306597126334
NORM_FLOOR = 1e-12
N_EXPERTS = 8
CHUNK = 64
LANES = 128
VMEM_LIMIT = 56 * 1024 * 1024


def _params(*sem):
    return pltpu.CompilerParams(dimension_semantics=sem, vmem_limit_bytes=VMEM_LIMIT)


def _dot(a, b):
    return jnp.dot(a.astype(BF16), b.astype(BF16), preferred_element_type=F32)


def _dot_nt(a, b):
    return lax.dot_general(a.astype(BF16), b.astype(BF16), (((1,), (1,)), ((), ())),
                           preferred_element_type=F32)


def _split(x):
    hi = x.astype(BF16)
    lo = (x - hi.astype(F32)).astype(BF16)
    return hi, lo


def _head_sums(x, ones_pair):
    rows = x.shape[0]
    tiles = x.shape[1] // LANES
    hi, lo = _split(x)
    stacked = jnp.concatenate([t[:, g * LANES:(g + 1) * LANES] for t in (hi, lo) for g in range(tiles)], axis=0)
    s = jnp.dot(stacked, ones_pair, preferred_element_type=F32)
    return jnp.concatenate([s[g * rows:(g + 1) * rows] + s[(tiles + g) * rows:(tiles + g + 1) * rows]
                            for g in range(tiles)], axis=1)


def _dot_split_rhs(w, x):
    hi, lo = _split(x)
    return (jnp.dot(w, hi, preferred_element_type=F32) + jnp.dot(w, lo, preferred_element_type=F32))


def _rms(x, g):
    ms = jnp.mean(x * x, axis=-1, keepdims=True)
    return x * lax.rsqrt(ms + NORM_EPS) * g


def _silu(x):
    return x * jax.nn.sigmoid(x)


def _adaln_kernel(c_ref, w_ref, b_ref, o_ref):
    s = _silu(c_ref[...])
    o_ref[0] = _dot(s, w_ref[0]) + b_ref[0]


def _adaln(cvec, ada_w, ada_b):
    nl, d, n = ada_w.shape
    rows = cvec.shape[0]
    tn = 1536
    return pl.pallas_call(
        _adaln_kernel,
        grid=(nl, n // tn),
        in_specs=[pl.BlockSpec((rows, d), lambda l, j: (0, 0)),
                  pl.BlockSpec((1, d, tn), lambda l, j: (l, 0, j)),
                  pl.BlockSpec((1, 1, tn), lambda l, j: (l, 0, j))],
        out_specs=pl.BlockSpec((1, rows, tn), lambda l, j: (l, 0, j)),
        out_shape=jax.ShapeDtypeStruct((nl, rows, n), F32),
        compiler_params=_params("arbitrary", "arbitrary"),
        name="adaln",
    )(cvec, ada_w, ada_b.reshape(nl, 1, n))


N_RWKV_COLS = 1920
N_CONV_COLS = 768
N_FOUR_COLS = 256


def _inproj_kernel(add_pos, *refs):
    if add_pos:
        x_ref, pos_ref, g_ref, sc_ref, sh_ref, w_ref, zr_ref, zc_ref, zf_ref, xp_ref = refs
        x = x_ref[0] + pos_ref[...]
        xp_ref[0] = x
    else:
        x_ref, g_ref, sc_ref, sh_ref, w_ref, zr_ref, zc_ref, zf_ref = refs
        x = x_ref[0]
    h = (_rms(x, g_ref[...]) * (1.0 + sc_ref[0]) + sh_ref[0]).astype(BF16)
    a, b = N_RWKV_COLS, N_RWKV_COLS + N_CONV_COLS
    zr_ref[0] = jnp.dot(h, w_ref[:, :a], preferred_element_type=F32)
    zc_ref[0] = jnp.dot(h, w_ref[:, a:b], preferred_element_type=F32).astype(BF16)
    zf_ref[0] = jnp.dot(h, w_ref[:, b:], preferred_element_type=F32).astype(BF16)


def _inproj(x, pos, g, sc, sh, w, tm):
    bsz, length, d = x.shape
    n = w.shape[1]
    add_pos = pos is not None
    row = lambda b, i: (b, i, 0)
    in_specs = [pl.BlockSpec((1, tm, d), row)]
    args = [x]
    if add_pos:
        in_specs.append(pl.BlockSpec((tm, d), lambda b, i: (i, 0)))
        args.append(pos)
    in_specs += [pl.BlockSpec((1, d), lambda b, i: (0, 0)),
                 pl.BlockSpec((1, 1, d), lambda b, i: (b, 0, 0)),
                 pl.BlockSpec((1, 1, d), lambda b, i: (b, 0, 0)),
                 pl.BlockSpec((d, n), lambda b, i: (0, 0))]
    args += [g, sc, sh, w]
    out_shape = [jax.ShapeDtypeStruct((bsz, length, N_RWKV_COLS), F32),
                 jax.ShapeDtypeStruct((bsz, length, N_CONV_COLS), BF16),
                 jax.ShapeDtypeStruct((bsz, length, N_FOUR_COLS), BF16)]
    out_specs = [pl.BlockSpec((1, tm, N_RWKV_COLS), row),
                 pl.BlockSpec((1, tm, N_CONV_COLS), row),
                 pl.BlockSpec((1, tm, N_FOUR_COLS), row)]
    if add_pos:
        out_shape.append(jax.ShapeDtypeStruct((bsz, length, d), F32))
        out_specs.append(pl.BlockSpec((1, tm, d), row))
    return pl.pallas_call(
        functools.partial(_inproj_kernel, add_pos),
        grid=(bsz, length // tm),
        in_specs=in_specs, out_specs=out_specs, out_shape=out_shape,
        compiler_params=_params("parallel", "parallel"),
        name="inproj",
    )(*args)


HALO = 8


def _rwkv_features(z, edge_row, d, mu_ref, muw_ref, mua_ref, w0_ref, w2_ref, a0_ref, a2_ref, kk_ref, ka_ref, bd):
    n = z.shape[0]
    rows = lax.broadcasted_iota(jnp.int32, (n, 1), 0)
    if d:
        zs = jnp.where(rows == n - 1, edge_row, pltpu.roll(z, n - 1, 0))
    else:
        zs = jnp.where(rows == 0, edge_row, pltpu.roll(z, 1, 0))
    dz = zs - z
    k = z[:, 0:512] + dz[:, 0:512] * mu_ref[d, 0:1, :]
    v = z[:, 512:1024] + dz[:, 512:1024] * mu_ref[d, 1:2, :]
    r = z[:, 1280:1792] + dz[:, 1280:1792] * mu_ref[d, 2:3, :]
    wl = z[:, 1024:1152] + dz[:, 1024:1152] * muw_ref[...]
    al = z[:, 1152:1280] + dz[:, 1152:1280] * mua_ref[...]
    lw = -DECAY_SCALE * jax.nn.sigmoid(w0_ref[d:d + 1, :] + _dot(jnp.tanh(wl), w2_ref[d]))
    a = jax.nn.sigmoid(a0_ref[d:d + 1, :] + _dot(al, a2_ref[d]))
    kk = k * kk_ref[...]
    ss = _head_sums(kk * kk, bd)
    kk = kk * lax.rsqrt(jnp.maximum(ss, NORM_FLOOR * NORM_FLOOR))
    return r, lw, k * (1.0 + (a - 1.0) * ka_ref[...]), v, -kk, kk * a


SOLVE_BLOCK = 8


def _unit_lower_solve(a_list, x_list, diag_blk, eye, first, side_work=()):
    side_work = list(side_work)

    def boundary():
        if side_work:
            side_work.pop(0)()

    def pair(m, w):
        return _dot(m, jnp.concatenate([jnp.where(first, w, 0.0), jnp.where(first, 0.0, w)], axis=0))

    def apply(m, x):
        zero = jnp.zeros((x.shape[0], LANES), x.dtype)
        return _dot(m, jnp.concatenate([jnp.concatenate([x[:, :LANES], zero], axis=1),
                                        jnp.concatenate([zero, x[:, LANES:]], axis=1)], axis=0))

    n = eye.shape[0]
    a_d = [jnp.where(diag_blk, a, 0.0) for a in a_list]
    a_o = [a - d for a, d in zip(a_list, a_d)]
    dinv = [eye + d for d in a_d]
    apow = a_d
    for _ in range(SOLVE_BLOCK.bit_length() - 2):
        apow = [pair(p, p) for p in apow]
        boundary()
        dinv = [d + pair(p, d) for p, d in zip(apow, dinv)]
        boundary()
    b = [pair(d, o) for d, o in zip(dinv, a_o)]
    boundary()
    x = [apply(d, x) for d, x in zip(dinv, x_list)]
    levels = (n // SOLVE_BLOCK).bit_length() - 1
    for level in range(levels):
        boundary()
        if level:
            b = [pair(m, m) for m in b]
        x = [v + apply(m, v) for m, v in zip(b, x)]
    while side_work:
        boundary()
    return x


def _scan_kernel(*refs):
    z_refs = (refs[0], refs[2])
    halo_refs = (refs[1], refs[3])
    zfirst_refs = refs[4:6]
    s0_refs = refs[6:8]
    feat_refs = refs[8:17]
    rk_ref, gw_ref, gb_ref, bd_ref = refs[17:21]
    y_refs = refs[21:23]
    st_refs = refs[23:25]
    feat_ref = refs[25]
    c = pl.program_id(1)

    @pl.when(c == 0)
    def _():
        for st_ref, s0_ref in zip(st_refs, s0_refs):
            st_ref[...] = s0_ref[...]
        no_token = jnp.zeros((1, zfirst_refs[0].shape[2]), F32)
        for d in range(2):
            for j, f in enumerate(_rwkv_features(zfirst_refs[d][0], no_token, d, *feat_refs, bd_ref[...])):
                feat_ref[d, j] = f

    n = CHUNK
    t_i = lax.broadcasted_iota(jnp.int32, (n, n), 0)
    s_i = lax.broadcasted_iota(jnp.int32, (n, n), 1)
    blk_shift = SOLVE_BLOCK.bit_length() - 1
    lane2 = lax.broadcasted_iota(jnp.int32, (n, LANES), 1)
    t_2 = lax.broadcasted_iota(jnp.int32, (n, LANES), 0)
    s_2 = lane2 & (n - 1)
    diag_blk2 = (s_2 >> blk_shift) == (t_2 >> blk_shift)
    eye2 = jnp.where(s_2 == t_2, 1.0, 0.0)
    first = lane2 < HEAD_DIM
    r_bd = lax.broadcasted_iota(jnp.int32, (LANES, LANES), 0) < HEAD_DIM
    c_bd = lax.broadcasted_iota(jnp.int32, (LANES, LANES), 1) < HEAD_DIM
    blockdiag = r_bd == c_bd
    incl2 = (s_2 <= t_2, s_2 >= t_2)
    strict2 = (s_2 < t_2, s_2 > t_2)

    npairs = st_refs[0].shape[1]
    sls = [slice(g * LANES, (g + 1) * LANES) for g in range(npairs)]
    items = [(d, g) for d in range(2) for g in range(npairs)]
    heads = [(i, j) for i in range(len(items)) for j in range(2)]
    feats = [[feat_ref[d, j] for j in range(6)] for d in range(2)]
    raw, scaled, w_tot = [], [], []
    for d in range(2):
        r, lw, k, v, a, b = feats[d]
        incl = (s_i >= t_i) if d else (s_i <= t_i)
        cum = _dot_split_rhs(jnp.where(incl, 1.0, 0.0).astype(BF16), lw)
        tot = cum[0:1, :] if d else cum[n - 1:n, :]
        w_inv = jnp.exp(-cum)
        w_all = jnp.exp(tot)
        b_inv = b * w_inv
        k_inv = k * w_inv
        raw.append((r, k, v))
        scaled.append((r * jnp.exp(cum), a * jnp.exp(cum - lw), b_inv, k_inv, b_inv * w_all, k_inv * w_all))
        w_tot.append(w_all)
    states = [st_refs[d][0, g] for d, g in items]

    v_g = [raw[d][2][:, sls[g]] for d, g in items]
    at_h = [(jnp.where(first, scaled[d][1][:, sls[g]], 0.0), jnp.where(first, 0.0, scaled[d][1][:, sls[g]]))
            for d, g in items]
    rt_h = [(jnp.where(first, scaled[d][0][:, sls[g]], 0.0), jnp.where(first, 0.0, scaled[d][0][:, sls[g]]))
            for d, g in items]
    s = [_dot_nt(jnp.concatenate([at_h[i][0], rt_h[i][0], at_h[i][1], rt_h[i][1]], axis=0),
                 jnp.concatenate([scaled[d][2][:, sls[g]], scaled[d][3][:, sls[g]]], axis=0))
         for i, (d, g) in enumerate(items)]
    sa = [(jnp.where(strict2[d], s[i][0:n], 0.0), jnp.where(strict2[d], s[i][2 * n:3 * n], 0.0))
          for i, (d, g) in enumerate(items)]
    sr = [(jnp.where(incl2[d], s[i][n:2 * n], 0.0), jnp.where(incl2[d], s[i][3 * n:4 * n], 0.0))
          for i, (d, g) in enumerate(items)]
    vv = [jnp.concatenate([x, x], axis=0) for x in (pltpu.roll(x, HEAD_DIM, 1) for x in v_g)]
    akv = [_dot(jnp.where(first, 0.0, sa[i][j]), vv[i]) for i, j in heads]
    rhs = [jnp.where(first, at_h[i][0], akv[2 * i]) if j == 0 else jnp.where(first, akv[2 * i + 1], at_h[i][1])
           for i, j in heads]
    edge_rows = (halo_refs[0][0, HALO - 1:HALO, :], halo_refs[1][0, 0:1, :])

    def next_features(d):
        def work():
            for j, f in enumerate(_rwkv_features(z_refs[d][0], edge_rows[d], d, *feat_refs, bd_ref[...])):
                feat_ref[d, j] = f
        return work

    idx = range(len(items))
    a_pair = [jnp.where(first, sa[i][0], pltpu.roll(sa[i][1], HEAD_DIM, 1)) for i in idx]
    sol = _unit_lower_solve(a_pair, [jnp.concatenate([rhs[2 * i], rhs[2 * i + 1]], axis=1) for i in idx],
                            diag_blk2, eye2, first,
                            side_work=[next_features(0), lambda: None, lambda: None, next_features(1)])
    pq = [sol[i // 2][:, (i % 2) * LANES:(i % 2 + 1) * LANES] for i in range(2 * len(items))]
    p_pair = [jnp.concatenate([jnp.where(first, pq[2 * i], 0.0), jnp.where(first, 0.0, pq[2 * i + 1])], axis=0)
              for i in idx]
    q_pair = [pltpu.roll(jnp.where(first, pq[2 * i + 1], pq[2 * i]), HEAD_DIM, 1) for i in idx]
    t_split = [jnp.concatenate(_split(states[i]), axis=0) for i in idx]
    pr = [jnp.concatenate([p_pair[i], rt_h[i][0], rt_h[i][1]], axis=0).astype(BF16) for i in idx]
    prt = [jnp.dot(jnp.concatenate([pr[i], pr[i]], axis=1), t_split[i], preferred_element_type=F32) for i in idx]
    pt = [m[0:2 * n] for m in prt]
    rtt = [m[2 * n:] for m in prt]
    u = [jnp.where(first, pt[i][0:n], pt[i][n:]) + q_pair[i] for i in idx]
    uv = [jnp.concatenate([u[i], v_g[i]], axis=0) for i in idx]
    yh = [_dot(sr[i][j], uv[i]) for i, j in heads]
    ys = [jnp.where(first, rtt[i][0:n] + yh[2 * i], rtt[i][n:] + yh[2 * i + 1]) for i in idx]
    t_new = [_dot(jnp.concatenate([scaled[d][4][:, sls[g]], scaled[d][5][:, sls[g]]], axis=0).T, uv[i])
             for i, (d, g) in enumerate(items)]
    for i, (d, g) in enumerate(items):
        w_col = jnp.broadcast_to(w_tot[d][:, sls[g]], (LANES, LANES)).T
        st_refs[d][0, g] = jnp.where(blockdiag, w_col * states[i] + t_new[i], 0.0)
    bd = bd_ref[...]
    inv = 1.0 / HEAD_DIM
    y = [jnp.concatenate(ys[d * npairs:(d + 1) * npairs], axis=1) for d in range(2)]
    mean = [_head_sums(y[d], bd) * inv for d in range(2)]
    dev = [y[d] - mean[d] for d in range(2)]
    var = [_head_sums(dev[d] * dev[d], bd) * inv for d in range(2)]
    bonus = [_head_sums(raw[d][0] * raw[d][1] * rk_ref[...], bd) * raw[d][2] for d in range(2)]
    for d in range(2):
        y_refs[d][0] = (dev[d] * lax.rsqrt(var[d] + GN_EPS) * gw_ref[...] + gb_ref[...] + bonus[d]).astype(BF16)


def _scan(zr, state_fwd, state_bwd, p):
    bsz, length, ncol = zr.shape
    width = p["w0"].shape[1]
    nc = length // CHUNK
    npairs = width // LANES
    per_chunk = CHUNK // HALO
    last_halo = length // HALO - 1
    fmap = lambda b, c: (b, c, 0)
    bmap = lambda b, c: (b, nc - 1 - c, 0)
    fnext = lambda c: jnp.minimum(c + 1, nc - 1)
    bnext = lambda c: jnp.maximum(nc - 2 - c, 0)
    fnext_map = lambda b, c: (b, fnext(c), 0)
    bnext_map = lambda b, c: (b, bnext(c), 0)
    prev_map = lambda b, c: (b, jnp.maximum(fnext(c) * per_chunk - 1, 0), 0)
    next_map = lambda b, c: (b, jnp.minimum((bnext(c) + 1) * per_chunk, last_halo), 0)
    full = lambda shape: pl.BlockSpec(shape, lambda b, c: (0,) * len(shape))
    st_spec = pl.BlockSpec((1, npairs, LANES, LANES), lambda b, c: (b, 0, 0, 0))
    y_shape = jax.ShapeDtypeStruct((bsz, length, width), BF16)
    st_shape = jax.ShapeDtypeStruct((bsz, npairs, LANES, LANES), F32)
    y_f, y_b, st_f, st_b = pl.pallas_call(
        _scan_kernel,
        grid=(bsz, nc),
        in_specs=[pl.BlockSpec((1, CHUNK, ncol), fnext_map), pl.BlockSpec((1, HALO, ncol), prev_map),
                  pl.BlockSpec((1, CHUNK, ncol), bnext_map), pl.BlockSpec((1, HALO, ncol), next_map),
                  pl.BlockSpec((1, CHUNK, ncol), lambda b, c: (b, 0, 0)),
                  pl.BlockSpec((1, CHUNK, ncol), lambda b, c: (b, nc - 1, 0)),
                  st_spec, st_spec,
                  full((2, 3, width)), full((1, LANES)), full((1, LANES)), full((2, width)),
                  full((2, LANES, width)), full((2, width)), full((2, LANES, width)), full((1, width)),
                  full((1, width)), full((1, width)), full((1, width)), full((1, width)), full((LANES, LANES))],
        out_specs=[pl.BlockSpec((1, CHUNK, width), fmap), pl.BlockSpec((1, CHUNK, width), bmap), st_spec, st_spec],
        out_shape=[y_shape, y_shape, st_shape, st_shape],
        scratch_shapes=[pltpu.VMEM((2, 6, CHUNK, width), F32)],
        compiler_params=_params("parallel", "arbitrary"),
        name="wkv_scan",
    )(zr, zr, zr, zr, zr, zr, state_fwd, state_bwd, p["mu"], p["mu_w"], p["mu_a"], p["w0"], p["w2"], p["a0"], p["a2"],
      p["k_k"], p["k_a"], p["r_k"], p["gn_w"], p["gn_b"], p["bd"])
    return (y_f, y_b), (st_f, st_b)


def _mix_kernel(period, tm, yf_ref, yb_ref, glo_ref, zc_ref, g2_ref, cw_ref, o_ref):
    gate = _dot(jax.nn.sigmoid(glo_ref[0]), g2_ref[...])
    yr = (yf_ref[0].astype(F32) + yb_ref[0].astype(F32)) * gate
    zc = zc_ref[0].astype(F32)
    u, gate_b, gate_c = zc[:, 0:256], zc[:, 256:512], zc[:, 512:768]
    hc = gate_c * u
    pos = lax.broadcasted_iota(jnp.int32, (tm, 1), 0) % period
    prev = jnp.where(pos == 0, 0.0, pltpu.roll(hc, 1, 0))
    nxt = jnp.where(pos == period - 1, 0.0, pltpu.roll(hc, tm - 1, 0))
    conv = prev * cw_ref[0:1, :] + hc * cw_ref[1:2, :] + nxt * cw_ref[2:3, :]
    o_ref[0, :, 0:512] = yr.astype(BF16)
    o_ref[0, :, 512:768] = (gate_b * conv).astype(BF16)


def _mix(ys_f, ys_b, zr, zc, g2, conv_w, period, tm):
    bsz, length, _ = zc.shape
    row = lambda b, i: (b, i, 0)
    glo_block = (N_RWKV_COLS - 128) // 128
    return pl.pallas_call(
        functools.partial(_mix_kernel, period, tm),
        grid=(bsz, length // tm),
        in_specs=[pl.BlockSpec((1, tm, 512), row), pl.BlockSpec((1, tm, 512), row),
                  pl.BlockSpec((1, tm, 128), lambda b, i: (b, i, glo_block)),
                  pl.BlockSpec((1, tm, N_CONV_COLS), row),
                  pl.BlockSpec((128, 512), lambda b, i: (0, 0)),
                  pl.BlockSpec((3, 256), lambda b, i: (0, 0))],
        out_specs=pl.BlockSpec((1, tm, 768), row),
        out_shape=jax.ShapeDtypeStruct((bsz, length, 768), BF16),
        compiler_params=_params("parallel", "parallel"),
        name="gate_conv",
    )(ys_f, ys_b, zr, zc, g2, conv_w)


def _chan_dft_kernel(f_ref, w_ref, o_ref):
    xcs = jnp.dot(f_ref[0].astype(BF16), w_ref[...], preferred_element_type=F32)
    o_ref[0, 0] = xcs[:, 0:256].astype(BF16)
    o_ref[0, 1] = xcs[:, 256:512].astype(BF16)


def _pos_dft_kernel(nb, cs_ref, x_ref, o_ref, acc_ref):
    kstep = pl.program_id(1)

    @pl.when(kstep == 0)
    def _():
        acc_ref[...] = jnp.zeros_like(acc_ref)

    cs = cs_ref[...]
    for b in range(nb):
        acc_ref[b] += jnp.dot(cs, x_ref[b], preferred_element_type=F32)

    @pl.when(kstep == pl.num_programs(1) - 1)
    def _():
        o_ref[...] = acc_ref[...].astype(BF16)


def _fourier(zf, chan_w, pos_cs, tm, tk):
    bsz, length, ch = zf.shape
    tr = min(512, length)
    xcs = pl.pallas_call(
        _chan_dft_kernel,
        grid=(bsz, length // tr),
        in_specs=[pl.BlockSpec((1, tr, ch), lambda b, i: (b, i, 0)),
                  pl.BlockSpec((ch, 2 * ch), lambda b, i: (0, 0))],
        out_specs=pl.BlockSpec((1, 2, tr, ch), lambda b, i: (b, 0, i, 0)),
        out_shape=jax.ShapeDtypeStruct((bsz, 2, length, ch), BF16),
        compiler_params=_params("parallel", "parallel"),
        name="chan_dft",
    )(zf, chan_w)
    xcs = xcs.reshape(bsz, 2 * length, ch)
    return pl.pallas_call(
        functools.partial(_pos_dft_kernel, bsz),
        grid=(length // tm, 2 * length // tk),
        in_specs=[pl.BlockSpec((tm, tk), lambda i, kk: (i, kk)),
                  pl.BlockSpec((bsz, tk, ch), lambda i, kk: (0, kk, 0))],
        out_specs=pl.BlockSpec((bsz, tm, ch), lambda i, kk: (0, i, 0)),
        out_shape=jax.ShapeDtypeStruct((bsz, length, ch), BF16),
        scratch_shapes=[pltpu.VMEM((bsz, tm, ch), F32)],
        compiler_params=_params("parallel", "arbitrary"),
        name="pos_dft",
    )(pos_cs, xcs)


def _dft_tables(length):
    m = jnp.arange(length, dtype=jnp.int32)[None, :]
    l1 = jnp.arange(length // 64, dtype=jnp.int32)[:, None] * 64
    l2 = jnp.arange(64, dtype=jnp.int32)[:, None]
    ang = lambda l: ((l * m) % length).astype(F32) * (2.0 * jnp.pi / length)
    c1, s1, c2, s2 = jnp.cos(ang(l1)), jnp.sin(ang(l1)), jnp.cos(ang(l2)), jnp.sin(ang(l2))
    scale = 1.0 / jnp.sqrt(jnp.float32(length))
    cos = (c1[:, None, :] * c2[None] - s1[:, None, :] * s2[None]).reshape(length, length) * scale
    sin = (s1[:, None, :] * c2[None] + c1[:, None, :] * s2[None]).reshape(length, length) * scale
    return jnp.concatenate([cos, sin], axis=1).astype(BF16)


def _chan_dft_weights(ngroups):
    j = jnp.arange(HEAD_DIM, dtype=jnp.int32)
    ang = ((j[:, None] * j[None, :]) % HEAD_DIM).astype(F32) * (2.0 * jnp.pi / HEAD_DIM)
    eye = jnp.eye(ngroups, dtype=F32)
    scale = 1.0 / jnp.sqrt(jnp.float32(HEAD_DIM))
    cc = jnp.kron(eye, jnp.cos(ang) * scale)
    sc = jnp.kron(eye, jnp.sin(ang) * scale)
    return jnp.concatenate([cc, -sc], axis=1).astype(BF16)


OUTPROJ_SPLIT = 4


def _outproj_kernel(yrc_ref, yf_ref, w_ref, x_ref, gn1_ref, gate_ref, gn2_ref, sc_ref, sh_ref, xo_ref, h_ref):
    nrc = yrc_ref.shape[2]
    tm = x_ref.shape[1]
    parts = [slice(s, s + tm // OUTPROJ_SPLIT) for s in range(0, tm, tm // OUTPROJ_SPLIT)]
    outs = [jnp.dot(yrc_ref[0, rows, :], w_ref[0:nrc, :], preferred_element_type=F32) +
            jnp.dot(yf_ref[0, rows, :], w_ref[nrc:, :], preferred_element_type=F32) for rows in parts]
    for rows, o in zip(parts, outs):
        xn = x_ref[0, rows, :] + gate_ref[0] * _rms(o, gn1_ref[...])
        xo_ref[0, rows, :] = xn
        h_ref[0, rows, :] = (_rms(xn, gn2_ref[...]) * (1.0 + sc_ref[0]) + sh_ref[0]).astype(BF16)


def _outproj(yrc, yf, w, x, gn1, gate, gn2, sc, sh, tm):
    bsz, length, d = x.shape
    row = lambda b, i: (b, i, 0)
    vec = pl.BlockSpec((1, d), lambda b, i: (0, 0))
    mod = pl.BlockSpec((1, 1, d), lambda b, i: (b, 0, 0))
    return pl.pallas_call(
        _outproj_kernel,
        grid=(bsz, length // tm),
        in_specs=[pl.BlockSpec((1, tm, yrc.shape[2]), row), pl.BlockSpec((1, tm, yf.shape[2]), row),
                  pl.BlockSpec(w.shape, lambda b, i: (0, 0)), pl.BlockSpec((1, tm, d), row),
                  vec, mod, vec, mod, mod],
        out_specs=[pl.BlockSpec((1, tm, d), row), pl.BlockSpec((1, tm, d), row)],
        out_shape=[jax.ShapeDtypeStruct((bsz, length, d), F32), jax.ShapeDtypeStruct((bsz, length, d), BF16)],
        compiler_params=_params("parallel", "parallel"),
        name="outproj",
    )(yrc, yf, w, x, gn1, gate, gn2, sc, sh)


FF_SUB = 256


def _swiglu_partial(h, wg, wu, wd, tf):
    def up(c, n):
        return (jnp.dot(h, wg(c, n), preferred_element_type=F32), jnp.dot(h, wu(c, n), preferred_element_type=F32))

    def down(gu, c, n):
        return jnp.dot((_silu(gu[0]) * gu[1]).astype(BF16), wd(c, n), preferred_element_type=F32)

    out = None
    prev = None
    for cut in [(c, min(FF_SUB, tf - c)) for c in range(0, tf, FF_SUB)]:
        gu = up(*cut)
        if prev is not None:
            part = down(*prev)
            out = part if out is None else out + part
        prev = (gu,) + cut
    part = down(*prev)
    return part if out is None else out + part


def _ffn_kernel(h_ref, wg_ref, wu_ref, wd_ref, x_ref, gn_ref, gate_ref, o_ref, acc_ref):
    j = pl.program_id(2)

    @pl.when(j == 0)
    def _():
        acc_ref[...] = jnp.zeros_like(acc_ref)

    acc_ref[...] += _swiglu_partial(h_ref[0], lambda c, n: wg_ref[:, c:c + n], lambda c, n: wu_ref[:, c:c + n],
                                    lambda c, n: wd_ref[c:c + n, :], wg_ref.shape[1])

    @pl.when(j == pl.num_programs(2) - 1)
    def _():
        o_ref[0] = x_ref[0] + gate_ref[0] * _rms(acc_ref[...], gn_ref[...])


def _ffn(h, wg, wu, wd, x, gn, gate, tm, tf):
    bsz, length, d = x.shape
    ff = wg.shape[1]
    row = lambda b, i, j: (b, i, 0)
    return pl.pallas_call(
        _ffn_kernel,
        grid=(bsz, length // tm, ff // tf),
        in_specs=[pl.BlockSpec((1, tm, d), row),
                  pl.BlockSpec((d, tf), lambda b, i, j: (0, j)),
                  pl.BlockSpec((d, tf), lambda b, i, j: (0, j)),
                  pl.BlockSpec((tf, d), lambda b, i, j: (j, 0)),
                  pl.BlockSpec((1, tm, d), row),
                  pl.BlockSpec((1, d), lambda b, i, j: (0, 0)),
                  pl.BlockSpec((1, 1, d), lambda b, i, j: (b, 0, 0))],
        out_specs=pl.BlockSpec((1, tm, d), row),
        out_shape=jax.ShapeDtypeStruct((bsz, length, d), F32),
        scratch_shapes=[pltpu.VMEM((tm, d), F32)],
        compiler_params=_params("parallel", "parallel", "arbitrary"),
        name="ffn",
    )(h, wg, wu, wd, x, gn, gate)


MOE_TILE = 512
MOE_BLOCK = 128
MOE_ALIGN = 16
MOE_TM = 1024


def _router_kernel(x_ref, gn_ref, sc_ref, sh_ref, w_ref, b_ref, tri_ref, routes_ref, routest_ref, cnt_ref):
    h = _rms(x_ref[0], gn_ref[...]) * (1.0 + sc_ref[0]) + sh_ref[0]
    logits = jnp.dot(h, w_ref[...], preferred_element_type=F32, precision=lax.Precision.HIGHEST) + b_ref[...]
    ne = float(logits.shape[1])
    idx = lax.broadcasted_iota(jnp.int32, logits.shape, 1).astype(F32)
    m1 = jnp.max(logits, axis=1, keepdims=True)
    i1 = jnp.min(jnp.where(logits == m1, idx, ne), axis=1, keepdims=True)
    rest = jnp.where(idx == i1, -jnp.inf, logits)
    m2 = jnp.max(rest, axis=1, keepdims=True)
    i2 = jnp.min(jnp.where(rest == m2, idx, ne), axis=1, keepdims=True)
    e2 = jnp.exp(m2 - m1)
    p1 = 1.0 / (1.0 + e2)
    p2 = e2 / (1.0 + e2)
    sel = jnp.where(idx == i1, 1.0, jnp.where(idx == i2, 1.0, 0.0))
    before = jnp.dot(tri_ref[...], sel.astype(BF16), preferred_element_type=F32)
    rank1 = jnp.sum(jnp.where(idx == i1, before, 0.0), axis=1, keepdims=True)
    rank2 = jnp.sum(jnp.where(idx == i2, before, 0.0), axis=1, keepdims=True)
    routes = jnp.zeros_like(logits)
    for lane, val in enumerate((i1, i2, rank1, rank2, p1, p2)):
        routes = jnp.where(idx == float(lane), val, routes)
    routes_ref[0] = routes
    routest_ref[0] = jnp.transpose(routes)[0:N_EXPERTS, :]
    cnt_ref[0, 0] = jnp.sum(sel, axis=0, keepdims=True)


def _router(x, gn, sc, sh, w, b):
    bsz, length, d = x.shape
    tm = MOE_TILE
    pad = LANES - w.shape[1]
    w = jnp.pad(w, ((0, 0), (0, pad)))
    b = jnp.pad(b, ((0, 0), (0, pad)), constant_values=-1e30)
    ne = LANES
    t = jnp.arange(tm, dtype=jnp.int32)
    tri = (t[None, :] < t[:, None]).astype(BF16)
    row = lambda bb, i: (bb, i, 0)
    vec = pl.BlockSpec((1, d), lambda bb, i: (0, 0))
    mod = pl.BlockSpec((1, 1, d), lambda bb, i: (bb, 0, 0))
    nt = length // tm
    return pl.pallas_call(
        _router_kernel,
        grid=(bsz, nt),
        in_specs=[pl.BlockSpec((1, tm, d), row), vec, mod, mod,
                  pl.BlockSpec((d, ne), lambda bb, i: (0, 0)), pl.BlockSpec((1, ne), lambda bb, i: (0, 0)),
                  pl.BlockSpec((tm, tm), lambda bb, i: (0, 0))],
        out_specs=[pl.BlockSpec((1, tm, ne), row),
                   pl.BlockSpec((1, N_EXPERTS, tm), lambda bb, i: (bb, 0, i)),
                   pl.BlockSpec((1, 1, 1, ne), lambda bb, i: (bb, i, 0, 0))],
        out_shape=[jax.ShapeDtypeStruct((bsz, length, ne), F32),
                   jax.ShapeDtypeStruct((bsz, N_EXPERTS, length), F32),
                   jax.ShapeDtypeStruct((bsz, nt, 1, ne), F32)],
        compiler_params=_params("parallel", "parallel"),
        name="router",
    )(x, gn, sc, sh, w, b, tri)


_MOE_SHIFT = MOE_BLOCK.bit_length() - 1
_MOE_STAGE_ROWS = -(-(2 * MOE_TILE + N_EXPERTS * (MOE_ALIGN - 1)) // LANES) * LANES


def _align_rows(c):
    return ((c + (MOE_ALIGN - 1)) >> (MOE_ALIGN.bit_length() - 1)) << (MOE_ALIGN.bit_length() - 1)


def _run_copies(start, to_sorted, tile, off_ref, cnt_ref, stage_ref, sorted_ref, sem):
    def run(stage_row, sorted_row, nrows):
        stage = stage_ref.at[pl.ds(pl.multiple_of(stage_row, MOE_ALIGN), nrows)]
        sort = sorted_ref.at[pl.ds(pl.multiple_of(sorted_row, MOE_ALIGN), nrows)]
        cp = pltpu.make_async_copy(stage, sort, sem) if to_sorted else pltpu.make_async_copy(sort, stage, sem)
        if start:
            cp.start()
        else:
            cp.wait()

    stage0 = jnp.int32(0)
    for e in range(N_EXPERTS):
        rows = _align_rows(cnt_ref[tile * N_EXPERTS + e])
        sorted0 = off_ref[tile * N_EXPERTS + e]
        nfull = rows >> _MOE_SHIFT

        def full(rb, carry, stage0=stage0, sorted0=sorted0):
            run(stage0 + rb * MOE_BLOCK, sorted0 + rb * MOE_BLOCK, MOE_BLOCK)
            return carry

        lax.fori_loop(0, nfull, full, 0)
        done = nfull * MOE_BLOCK
        piece = MOE_BLOCK // 2
        while piece >= MOE_ALIGN:
            has = (rows & piece) != 0

            @pl.when(has)
            def _(done=done, piece=piece, stage0=stage0, sorted0=sorted0):
                run(stage0 + done, sorted0 + done, piece)

            done = done + jnp.where(has, piece, 0)
            piece //= 2
        stage0 = stage0 + rows


def _staging_rows(tile, cnt_ref, i1, i2, rank1, rank2):
    row1 = jnp.zeros_like(i1)
    row2 = jnp.zeros_like(i2)
    stage0 = jnp.int32(0)
    for e in range(N_EXPERTS):
        first_row = stage0.astype(F32)
        row1 = jnp.where(i1 == float(e), first_row, row1)
        row2 = jnp.where(i2 == float(e), first_row, row2)
        stage0 = stage0 + _align_rows(cnt_ref[tile * N_EXPERTS + e])
    return row1 + rank1, row2 + rank2


def _dispatch_kernel(off_ref, cnt_ref, h_ref, routest_ref, hs_init_ref, hs_ref, buf_ref, sem):
    del hs_init_ref
    tile = pl.program_id(0)
    rt = routest_ref[0]
    row1, row2 = _staging_rows(tile, cnt_ref, rt[0:1], rt[1:2], rt[2:3], rt[3:4])
    rows = lax.broadcasted_iota(jnp.int32, (LANES, 1), 0)
    onehot = jnp.concatenate(
        [jnp.where(row1 == (rows + part * LANES).astype(F32), 1.0,
                   jnp.where(row2 == (rows + part * LANES).astype(F32), 1.0, 0.0)).astype(BF16)
         for part in range(_MOE_STAGE_ROWS // LANES)], axis=0)
    buf_ref[...] = jnp.dot(onehot, h_ref[...], preferred_element_type=F32).astype(BF16)
    _run_copies(True, True, tile, off_ref, cnt_ref, buf_ref, hs_ref, sem)
    _run_copies(False, True, tile, off_ref, cnt_ref, buf_ref, hs_ref, sem)


def _dispatch(h, rankt, off, cnt, nrows):
    n, d = h.shape
    ntb = rankt.shape[2] // MOE_TILE
    grid_spec = pltpu.PrefetchScalarGridSpec(
        num_scalar_prefetch=2, grid=(n // MOE_TILE,),
        in_specs=[pl.BlockSpec((MOE_TILE, d), lambda t, o, c: (t, 0)),
                  pl.BlockSpec((1, N_EXPERTS, MOE_TILE), lambda t, o, c: (t // ntb, 0, t % ntb)),
                  pl.BlockSpec(memory_space=pl.ANY)],
        out_specs=pl.BlockSpec(memory_space=pl.ANY),
        scratch_shapes=[pltpu.VMEM((_MOE_STAGE_ROWS, d), BF16), pltpu.SemaphoreType.DMA(())])
    return pl.pallas_call(
        _dispatch_kernel, grid_spec=grid_spec,
        out_shape=jax.ShapeDtypeStruct((nrows, d), BF16),
        input_output_aliases={4: 0},
        compiler_params=_params("arbitrary"),
        name="moe_dispatch",
    )(off, cnt, h, rankt, jnp.zeros((nrows, d), BF16))


def _gffn_kernel(te_ref, tv_ref, h_ref, wg_ref, wu_ref, wd_ref, o_ref, acc_ref):
    q = pl.program_id(0)
    j = pl.program_id(1)

    @pl.when(j == 0)
    def _():
        acc_ref[...] = jnp.zeros_like(acc_ref)

    @pl.when(tv_ref[q] > 0)
    def _():
        acc_ref[...] += _swiglu_partial(h_ref[...], lambda c, n: wg_ref[0, :, c:c + n],
                                        lambda c, n: wu_ref[0, :, c:c + n], lambda c, n: wd_ref[0, c:c + n, :],
                                        wg_ref.shape[2])

    @pl.when(j == pl.num_programs(1) - 1)
    def _():
        o_ref[...] = acc_ref[...].astype(BF16)


def _gffn(hs, te, tv, wg, wu, wd, tf):
    nrows, d = hs.shape
    ff = wg.shape[2]
    grid_spec = pltpu.PrefetchScalarGridSpec(
        num_scalar_prefetch=2, grid=(nrows // MOE_TM, ff // tf),
        in_specs=[pl.BlockSpec((MOE_TM, d), lambda q, j, te, tv: (q, 0)),
                  pl.BlockSpec((1, d, tf), lambda q, j, te, tv: (te[q], 0, j)),
                  pl.BlockSpec((1, d, tf), lambda q, j, te, tv: (te[q], 0, j)),
                  pl.BlockSpec((1, tf, d), lambda q, j, te, tv: (te[q], j, 0))],
        out_specs=pl.BlockSpec((MOE_TM, d), lambda q, j, te, tv: (q, 0)),
        scratch_shapes=[pltpu.VMEM((MOE_TM, d), F32)])
    return pl.pallas_call(
        _gffn_kernel, grid_spec=grid_spec,
        out_shape=jax.ShapeDtypeStruct((nrows, d), BF16),
        compiler_params=_params("arbitrary", "arbitrary"),
        name="moe_experts",
    )(te, tv, hs, wg, wu, wd)


def _combine_kernel(off_ref, cnt_ref, ys_ref, routes_ref, x_ref, gn_ref, gate_ref, o_ref, buf_ref, sem):
    tile = pl.program_id(0)

    @pl.when(tile == 0)
    def _():
        buf_ref[...] = jnp.zeros_like(buf_ref)

    _run_copies(True, False, tile, off_ref, cnt_ref, buf_ref, ys_ref, sem)
    routes = routes_ref[0]
    col = lambda j: routes[:, j:j + 1]
    row1, row2 = _staging_rows(tile, cnt_ref, col(0), col(1), col(2), col(3))
    lanes = lax.broadcasted_iota(jnp.int32, (1, LANES), 1)
    scatter = jnp.concatenate(
        [(jnp.where(row1 == (lanes + part * LANES).astype(F32), col(4), 0.0) +
          jnp.where(row2 == (lanes + part * LANES).astype(F32), col(5), 0.0)).astype(BF16)
         for part in range(_MOE_STAGE_ROWS // LANES)], axis=1)
    _run_copies(False, False, tile, off_ref, cnt_ref, buf_ref, ys_ref, sem)
    mixed = jnp.dot(scatter, buf_ref[...], preferred_element_type=F32)
    o_ref[0] = x_ref[0] + gate_ref[0] * _rms(mixed, gn_ref[...])


def _combine(ys, routes, x, gn, gate, off, cnt):
    bsz, length, d = x.shape
    ntb = length // MOE_TILE
    tok = lambda t, o, c: (t // ntb, t % ntb, 0)
    grid_spec = pltpu.PrefetchScalarGridSpec(
        num_scalar_prefetch=2, grid=(bsz * ntb,),
        in_specs=[pl.BlockSpec(memory_space=pl.ANY),
                  pl.BlockSpec((1, MOE_TILE, LANES), tok),
                  pl.BlockSpec((1, MOE_TILE, d), tok),
                  pl.BlockSpec((1, d), lambda t, o, c: (0, 0)),
                  pl.BlockSpec((1, 1, d), lambda t, o, c: (t // ntb, 0, 0))],
        out_specs=pl.BlockSpec((1, MOE_TILE, d), tok),
        scratch_shapes=[pltpu.VMEM((_MOE_STAGE_ROWS, d), BF16), pltpu.SemaphoreType.DMA(())])
    return pl.pallas_call(
        _combine_kernel, grid_spec=grid_spec,
        out_shape=jax.ShapeDtypeStruct((bsz, length, d), F32),
        compiler_params=_params("arbitrary"),
        name="moe_combine",
    )(off, cnt, ys, routes, x, gn, gate)


def _route_plan(counts, ntokens):
    ntiles = counts.shape[0]
    rows = (counts + (MOE_ALIGN - 1)) // MOE_ALIGN * MOE_ALIGN
    seg = (rows.sum(0) + MOE_BLOCK + MOE_TM - 1) // MOE_TM * MOE_TM
    seg_end = jnp.cumsum(seg)
    off = (seg_end - seg)[None, :] + jnp.cumsum(rows, axis=0) - rows
    nq = -(-(2 * ntokens + ntiles * N_EXPERTS * (MOE_ALIGN - 1) + N_EXPERTS * (MOE_BLOCK + MOE_TM - 1)) // MOE_TM)
    q = jnp.arange(nq, dtype=jnp.int32) * MOE_TM
    te = jnp.minimum(jnp.sum(q[:, None] >= seg_end[None, :], axis=1), N_EXPERTS - 1).astype(jnp.int32)
    tv = (q < seg_end[-1]).astype(jnp.int32)
    return off.reshape(-1).astype(jnp.int32), te, tv, nq * MOE_TM


def _moe(h, x, gn_pre, sc, sh, router_w, router_b, wg, wu, wd, gn_post, gate, tf):
    bsz, length, d = x.shape
    routes, routest, cnt = _router(x, gn_pre, sc, sh, router_w, router_b)
    counts = cnt[:, :, 0, :N_EXPERTS].astype(jnp.int32).reshape(-1, N_EXPERTS)
    off, te, tv, nrows = _route_plan(counts, bsz * length)
    cnt_flat = counts.reshape(-1)
    hs = _dispatch(h.reshape(bsz * length, d), routest, off, cnt_flat, nrows)
    ys = _gffn(hs, te, tv, wg, wu, wd, tf)
    return _combine(ys, routes, x, gn_post, gate, off, cnt_flat)


def _sincos_2d(rows, cols, dim):
    quarter = dim // 4
    omega = 1.0 / (POS_BASE ** (jnp.arange(quarter, dtype=F32) / quarter))

    def axis_emb(n):
        ang = jnp.arange(n, dtype=F32)[:, None] * omega[None, :]
        return jnp.concatenate([jnp.sin(ang), jnp.cos(ang)], axis=-1)

    er, ec = axis_emb(rows), axis_emb(cols)
    emb = jnp.concatenate([jnp.broadcast_to(er[:, None, :], (rows, cols, dim // 2)),
                           jnp.broadcast_to(ec[None, :, :], (rows, cols, dim // 2))], axis=-1)
    return emb.reshape(rows * cols, dim)


def _head_block_ones(width):
    h = jnp.arange(width, dtype=jnp.int32) // HEAD_DIM
    return (h[:, None] == h[None, :]).astype(BF16)


def _pad_lora(w2, d):
    z = jnp.zeros_like(w2[0])
    return jnp.concatenate([w2[0], z], axis=0) if d == 0 else jnp.concatenate([z, w2[1]], axis=0)


def _tile(length, pref):
    return pref if length % pref == 0 else length


def kernel(x, c, ctx, c_ctx, ada_w, ada_b, norm_g, w_in, w_out, rwkv_mu, rwkv_mu_w, rwkv_mu_a, rwkv_w0, rwkv_w2,
           rwkv_a0, rwkv_a2, rwkv_g2, rwkv_k_k, rwkv_k_a, rwkv_r_k, rwkv_gn_w, rwkv_gn_b, conv_w, ffn_w_gate,
           ffn_w_up, ffn_w_down, router_w, router_b, moe_w_gate, moe_w_up, moe_w_down):
    bsz, length, dim = x.shape
    ctx_len = ctx.shape[1]
    depth = ada_w.shape[0]
    d_rwkv = rwkv_w0.shape[-1]

    pos = _sincos_2d(length // GRID_W, GRID_W, dim).astype(x.dtype)
    bd = _head_block_ones(LANES)
    chan_w = _chan_dft_weights(N_FOUR_COLS // HEAD_DIM)
    dft_lat = _dft_tables(length)
    dft_ctx = _dft_tables(ctx_len)

    cvec = jnp.zeros((16, dim), F32).at[:bsz].set(c).at[bsz].set(c_ctx)
    mod = _adaln(cvec, ada_w, ada_b)
    zero_state = jnp.zeros((bsz, d_rwkv // LANES, LANES, LANES), F32)
    xc = ctx

    def mods(l, lo, hi):
        return [mod[l, lo:hi, i * dim:(i + 1) * dim][:, None, :] for i in range(6)]

    def mixers(l, zr, zc, zf, states0, period, dft, need_y=True):
        tm = _tile(zr.shape[1], 512)
        p = {"mu": rwkv_mu[l], "mu_w": rwkv_mu_w[l].reshape(1, -1), "mu_a": rwkv_mu_a[l].reshape(1, -1),
             "w0": rwkv_w0[l], "w2": jnp.stack([_pad_lora(rwkv_w2[l], d) for d in range(2)]), "a0": rwkv_a0[l],
             "a2": jnp.stack([_pad_lora(rwkv_a2[l], d) for d in range(2)]), "k_k": rwkv_k_k[l][None],
             "k_a": rwkv_k_a[l][None], "r_k": rwkv_r_k[l].reshape(1, -1), "gn_w": rwkv_gn_w[l][None],
             "gn_b": rwkv_gn_b[l][None], "bd": bd}
        ys, states = _scan(zr, states0[0], states0[1], p)
        if not need_y:
            return None, None, states
        yrc = _mix(ys[0], ys[1], zr, zc, rwkv_g2[l], conv_w[l], period, tm)
        yf = _fourier(zf, chan_w, dft, _tile(zf.shape[1], 512), _tile(2 * zf.shape[1], 2048))
        return yrc, yf, states

    def channel_mixer(l, h2, xn, gn3, gate2, sc2, sh2, gn2):
        tm = _tile(xn.shape[1], 512)
        i = l // 2
        if l % 2 == 0:
            return _ffn(h2, ffn_w_gate[i].astype(BF16), ffn_w_up[i].astype(BF16), ffn_w_down[i].astype(BF16),
                        xn, gn3, gate2, tm, 1408)
        return _moe(h2, xn, gn2, sc2, sh2, router_w[i], router_b[i][None], moe_w_gate[i].astype(BF16),
                    moe_w_up[i].astype(BF16), moe_w_down[i].astype(BF16), gn3, gate2, 1792)

    for l in range(depth):
        last = l == depth - 1
        w_in_l = w_in[l].astype(BF16)
        w_out_l = w_out[l].astype(BF16)
        gn = [norm_g[l, i][None] for i in range(4)]
        sh1, sc1, g1, sh2, sc2, g2 = mods(l, 0, bsz)
        csh1, csc1, cg1, csh2, csc2, cg2 = mods(l, bsz, bsz + 1)

        xc_flat = xc.reshape(1, bsz * ctx_len, dim)
        tmc = _tile(bsz * ctx_len, 512)
        zr, zc, zf = _inproj(xc_flat, None, gn[0], csc1, csh1, w_in_l, tmc)
        unflat = lambda t: t.reshape(bsz, ctx_len, t.shape[-1])
        yrc, yf, ctx_states = mixers(l, unflat(zr), unflat(zc), unflat(zf), (zero_state, zero_state), ctx_len,
                                     dft_ctx, need_y=not last)
        if not last:
            flat = lambda t: t.reshape(1, bsz * ctx_len, t.shape[-1])
            xcn, hc2 = _outproj(flat(yrc), flat(yf), w_out_l, xc_flat, gn[1], cg1, gn[2], csc2, csh2, tmc)
            xc = channel_mixer(l, hc2, xcn, gn[3], cg2, csc2, csh2, gn[2]).reshape(bsz, ctx_len, dim)

        tm = _tile(length, 512)
        if l == 0:
            zr, zc, zf, x = _inproj(x, pos, gn[0], sc1, sh1, w_in_l, tm)
        else:
            zr, zc, zf = _inproj(x, None, gn[0], sc1, sh1, w_in_l, tm)
        yrc, yf, _ = mixers(l, zr, zc, zf, ctx_states, GRID_W, dft_lat)
        xn, h2 = _outproj(yrc, yf, w_out_l, x, gn[1], g1, gn[2], sc2, sh2, _tile(length, 512))
        x = channel_mixer(l, h2, xn, gn[3], g2, sc2, sh2, gn[2])
    return x
```

```python
import functools

import jax
import jax.numpy as jnp
from jax import lax
from jax.experimental import pallas as pl
from jax.experimental.pallas import tpu as pltpu

F32 = jnp.float32
BF16 = jnp.bfloat16

HEAD_DIM = 64
GRID_W = 64
NORM_EPS = 1e-6
GN_EPS = 64e-5
POS_BASE = 10000.0
DECAY_SCALE = 0.6065306597126334
NORM_FLOOR = 1e-12
N_EXPERTS = 8
CHUNK = 64
LANES = 128
VMEM_LIMIT = 56 * 1024 * 1024


def _params(*sem):
    return pltpu.CompilerParams(dimension_semantics=sem, vmem_limit_bytes=VMEM_LIMIT)


def _dot(a, b):
    return jnp.dot(a.astype(BF16), b.astype(BF16), preferred_element_type=F32)


def _dot_nt(a, b):
    return lax.dot_general(a.astype(BF16), b.astype(BF16), (((1,), (1,)), ((), ())),
                           preferred_element_type=F32)


def _split(x):
    hi = x.astype(BF16)
    lo = (x - hi.astype(F32)).astype(BF16)
    return hi, lo


def _head_sums(x, ones_pair):
    rows = x.shape[0]
    tiles = x.shape[1] // LANES
    hi, lo = _split(x)
    stacked = jnp.concatenate([t[:, g * LANES:(g + 1) * LANES] for t in (hi, lo) for g in range(tiles)], axis=0)
    s = jnp.dot(stacked, ones_pair, preferred_element_type=F32)
    return jnp.concatenate([s[g * rows:(g + 1) * rows] + s[(tiles + g) * rows:(tiles + g + 1) * rows]
                            for g in range(tiles)], axis=1)


def _dot_split_rhs(w, x):
    hi, lo = _split(x)
    return (jnp.dot(w, hi, preferred_element_type=F32) + jnp.dot(w, lo, preferred_element_type=F32))


def _rms(x, g):
    ms = jnp.mean(x * x, axis=-1, keepdims=True)
    return x * lax.rsqrt(ms + NORM_EPS) * g


def _silu(x):
    return x * jax.nn.sigmoid(x)


def _adaln_kernel(c_ref, w_ref, b_ref, o_ref):
    s = _silu(c_ref[...])
    o_ref[0] = _dot(s, w_ref[0]) + b_ref[0]


def _adaln(cvec, ada_w, ada_b):
    nl, d, n = ada_w.shape
    rows = cvec.shape[0]
    tn = 1536
    return pl.pallas_call(
        _adaln_kernel,
        grid=(nl, n // tn),
        in_specs=[pl.BlockSpec((rows, d), lambda l, j: (0, 0)),
                  pl.BlockSpec((1, d, tn), lambda l, j: (l, 0, j)),
                  pl.BlockSpec((1, 1, tn), lambda l, j: (l, 0, j))],
        out_specs=pl.BlockSpec((1, rows, tn), lambda l, j: (l, 0, j)),
        out_shape=jax.ShapeDtypeStruct((nl, rows, n), F32),
        compiler_params=_params("arbitrary", "arbitrary"),
        name="adaln",
    )(cvec, ada_w, ada_b.reshape(nl, 1, n))


N_RWKV_COLS = 1920
N_CONV_COLS = 768
N_FOUR_COLS = 256


def _inproj_kernel(add_pos, *refs):
    if add_pos:
        x_ref, pos_ref, g_ref, sc_ref, sh_ref, w_ref, zr_ref, zc_ref, zf_ref, xp_ref = refs
        x = x_ref[0] + pos_ref[...]
        xp_ref[0] = x
    else:
        x_ref, g_ref, sc_ref, sh_ref, w_ref, zr_ref, zc_ref, zf_ref = refs
        x = x_ref[0]
    h = (_rms(x, g_ref[...]) * (1.0 + sc_ref[0]) + sh_ref[0]).astype(BF16)
    a, b = N_RWKV_COLS, N_RWKV_COLS + N_CONV_COLS
    zr_ref[0] = jnp.dot(h, w_ref[:, :a], preferred_element_type=F32)
    zc_ref[0] = jnp.dot(h, w_ref[:, a:b], preferred_element_type=F32).astype(BF16)
    zf_ref[0] = jnp.dot(h, w_ref[:, b:], preferred_element_type=F32).astype(BF16)


def _inproj(x, pos, g, sc, sh, w, tm):
    bsz, length, d = x.shape
    n = w.shape[1]
    add_pos = pos is not None
    row = lambda b, i: (b, i, 0)
    in_specs = [pl.BlockSpec((1, tm, d), row)]
    args = [x]
    if add_pos:
        in_specs.append(pl.BlockSpec((tm, d), lambda b, i: (i, 0)))
        args.append(pos)
    in_specs += [pl.BlockSpec((1, d), lambda b, i: (0, 0)),
                 pl.BlockSpec((1, 1, d), lambda b, i: (b, 0, 0)),
                 pl.BlockSpec((1, 1, d), lambda b, i: (b, 0, 0)),
                 pl.BlockSpec((d, n), lambda b, i: (0, 0))]
    args += [g, sc, sh, w]
    out_shape = [jax.ShapeDtypeStruct((bsz, length, N_RWKV_COLS), F32),
                 jax.ShapeDtypeStruct((bsz, length, N_CONV_COLS), BF16),
                 jax.ShapeDtypeStruct((bsz, length, N_FOUR_COLS), BF16)]
    out_specs = [pl.BlockSpec((1, tm, N_RWKV_COLS), row),
                 pl.BlockSpec((1, tm, N_CONV_COLS), row),
                 pl.BlockSpec((1, tm, N_FOUR_COLS), row)]
    if add_pos:
        out_shape.append(jax.ShapeDtypeStruct((bsz, length, d), F32))
        out_specs.append(pl.BlockSpec((1, tm, d), row))
    return pl.pallas_call(
        functools.partial(_inproj_kernel, add_pos),
        grid=(bsz, length // tm),
        in_specs=in_specs, out_specs=out_specs, out_shape=out_shape,
        compiler_params=_params("parallel", "parallel"),
        name="inproj",
    )(*args)


HALO = 8


def _rwkv_features(z, edge_row, d, mu_ref, muw_ref, mua_ref, w0_ref, w2_ref, a0_ref, a2_ref, kk_ref, ka_ref, bd):
    n = z.shape[0]
    rows = lax.broadcasted_iota(jnp.int32, (n, 1), 0)
    if d:
        zs = jnp.where(rows == n - 1, edge_row, pltpu.roll(z, n - 1, 0))
    else:
        zs = jnp.where(rows == 0, edge_row, pltpu.roll(z, 1, 0))
    dz = zs - z
    k = z[:, 0:512] + dz[:, 0:512] * mu_ref[d, 0:1, :]
    v = z[:, 512:1024] + dz[:, 512:1024] * mu_ref[d, 1:2, :]
    r = z[:, 1280:1792] + dz[:, 1280:1792] * mu_ref[d, 2:3, :]
    wl = z[:, 1024:1152] + dz[:, 1024:1152] * muw_ref[...]
    al = z[:, 1152:1280] + dz[:, 1152:1280] * mua_ref[...]
    lw = -DECAY_SCALE * jax.nn.sigmoid(w0_ref[d:d + 1, :] + _dot(jnp.tanh(wl), w2_ref[d]))
    a = jax.nn.sigmoid(a0_ref[d:d + 1, :] + _dot(al, a2_ref[d]))
    kk = k * kk_ref[...]
    ss = _head_sums(kk * kk, bd)
    kk = kk * lax.rsqrt(jnp.maximum(ss, NORM_FLOOR * NORM_FLOOR))
    return r, lw, k * (1.0 + (a - 1.0) * ka_ref[...]), v, -kk, kk * a


SOLVE_BLOCK = 8


def _unit_lower_solve(a_list, x_list, diag_blk, eye, first, side_work=()):
    side_work = list(side_work)

    def boundary():
        if side_work:
            side_work.pop(0)()

    def pair(m, w):
        return _dot(m, jnp.concatenate([jnp.where(first, w, 0.0), jnp.where(first, 0.0, w)], axis=0))

    def apply(m, x):
        zero = jnp.zeros((x.shape[0], LANES), x.dtype)
        return _dot(m, jnp.concatenate([jnp.concatenate([x[:, :LANES], zero], axis=1),
                                        jnp.concatenate([zero, x[:, LANES:]], axis=1)], axis=0))

    n = eye.shape[0]
    a_d = [jnp.where(diag_blk, a, 0.0) for a in a_list]
    a_o = [a - d for a, d in zip(a_list, a_d)]
    dinv = [eye + d for d in a_d]
    apow = a_d
    for _ in range(SOLVE_BLOCK.bit_length() - 2):
        apow = [pair(p, p) for p in apow]
        boundary()
        dinv = [d + pair(p, d) for p, d in zip(apow, dinv)]
        boundary()
    b = [pair(d, o) for d, o in zip(dinv, a_o)]
    boundary()
    x = [apply(d, x) for d, x in zip(dinv, x_list)]
    levels = (n // SOLVE_BLOCK).bit_length() - 1
    for level in range(levels):
        boundary()
        if level:
            b = [pair(m, m) for m in b]
        x = [v + apply(m, v) for m, v in zip(b, x)]
    while side_work:
        boundary()
    return x


def _scan_kernel(*refs):
    z_refs = (refs[0], refs[2])
    halo_refs = (refs[1], refs[3])
    zfirst_refs = refs[4:6]
    s0_refs = refs[6:8]
    feat_refs = refs[8:17]
    rk_ref, gw_ref, gb_ref, bd_ref = refs[17:21]
    y_refs = refs[21:23]
    st_refs = refs[23:25]
    feat_ref = refs[25]
    c = pl.program_id(1)

    @pl.when(c == 0)
    def _():
        for st_ref, s0_ref in zip(st_refs, s0_refs):
            st_ref[...] = s0_ref[...]
        no_token = jnp.zeros((1, zfirst_refs[0].shape[2]), F32)
        for d in range(2):
            for j, f in enumerate(_rwkv_features(zfirst_refs[d][0], no_token, d, *feat_refs, bd_ref[...])):
                feat_ref[d, j] = f

    n = CHUNK
    t_i = lax.broadcasted_iota(jnp.int32, (n, n), 0)
    s_i = lax.broadcasted_iota(jnp.int32, (n, n), 1)
    blk_shift = SOLVE_BLOCK.bit_length() - 1
    lane2 = lax.broadcasted_iota(jnp.int32, (n, LANES), 1)
    t_2 = lax.broadcasted_iota(jnp.int32, (n, LANES), 0)
    s_2 = lane2 & (n - 1)
    diag_blk2 = (s_2 >> blk_shift) == (t_2 >> blk_shift)
    eye2 = jnp.where(s_2 == t_2, 1.0, 0.0)
    first = lane2 < HEAD_DIM
    r_bd = lax.broadcasted_iota(jnp.int32, (LANES, LANES), 0) < HEAD_DIM
    c_bd = lax.broadcasted_iota(jnp.int32, (LANES, LANES), 1) < HEAD_DIM
    blockdiag = r_bd == c_bd
    incl2 = (s_2 <= t_2, s_2 >= t_2)
    strict2 = (s_2 < t_2, s_2 > t_2)

    npairs = st_refs[0].shape[1]
    sls = [slice(g * LANES, (g + 1) * LANES) for g in range(npairs)]
    items = [(d, g) for d in range(2) for g in range(npairs)]
    heads = [(i, j) for i in range(len(items)) for j in range(2)]
    feats = [[feat_ref[d, j] for j in range(6)] for d in range(2)]
    raw, scaled, w_tot = [], [], []
    for d in range(2):
        r, lw, k, v, a, b = feats[d]
        incl = (s_i >= t_i) if d else (s_i <= t_i)
        cum = _dot_split_rhs(jnp.where(incl, 1.0, 0.0).astype(BF16), lw)
        tot = cum[0:1, :] if d else cum[n - 1:n, :]
        w_inv = jnp.exp(-cum)
        w_all = jnp.exp(tot)
        b_inv = b * w_inv
        k_inv = k * w_inv
        raw.append((r, k, v))
        scaled.append((r * jnp.exp(cum), a * jnp.exp(cum - lw), b_inv, k_inv, b_inv * w_all, k_inv * w_all))
        w_tot.append(w_all)
    states = [st_refs[d][0, g] for d, g in items]

    v_g = [raw[d][2][:, sls[g]] for d, g in items]
    at_h = [(jnp.where(first, scaled[d][1][:, sls[g]], 0.0), jnp.where(first, 0.0, scaled[d][1][:, sls[g]]))
            for d, g in items]
    rt_h = [(jnp.where(first, scaled[d][0][:, sls[g]], 0.0), jnp.where(first, 0.0, scaled[d][0][:, sls[g]]))
            for d, g in items]
    s = [_dot_nt(jnp.concatenate([at_h[i][0], rt_h[i][0], at_h[i][1], rt_h[i][1]], axis=0),
                 jnp.concatenate([scaled[d][2][:, sls[g]], scaled[d][3][:, sls[g]]], axis=0))
         for i, (d, g) in enumerate(items)]
    sa = [(jnp.where(strict2[d], s[i][0:n], 0.0), jnp.where(strict2[d], s[i][2 * n:3 * n], 0.0))
          for i, (d, g) in enumerate(items)]
    sr = [(jnp.where(incl2[d], s[i][n:2 * n], 0.0), jnp.where(incl2[d], s[i][3 * n:4 * n], 0.0))
          for i, (d, g) in enumerate(items)]
    vv = [jnp.concatenate([x, x], axis=0) for x in (pltpu.roll(x, HEAD_DIM, 1) for x in v_g)]
    akv = [_dot(jnp.where(first, 0.0, sa[i][j]), vv[i]) for i, j in heads]
    rhs = [jnp.where(first, at_h[i][0], akv[2 * i]) if j == 0 else jnp.where(first, akv[2 * i + 1], at_h[i][1])
           for i, j in heads]
    edge_rows = (halo_refs[0][0, HALO - 1:HALO, :], halo_refs[1][0, 0:1, :])

    def next_features(d):
        def work():
            for j, f in enumerate(_rwkv_features(z_refs[d][0], edge_rows[d], d, *feat_refs, bd_ref[...])):
                feat_ref[d, j] = f
        return work

    idx = range(len(items))
    a_pair = [jnp.where(first, sa[i][0], pltpu.roll(sa[i][1], HEAD_DIM, 1)) for i in idx]
    sol = _unit_lower_solve(a_pair, [jnp.concatenate([rhs[2 * i], rhs[2 * i + 1]], axis=1) for i in idx],
                            diag_blk2, eye2, first,
                            side_work=[next_features(0), lambda: None, lambda: None, next_features(1)])
    pq = [sol[i // 2][:, (i % 2) * LANES:(i % 2 + 1) * LANES] for i in range(2 * len(items))]
    p_pair = [jnp.concatenate([jnp.where(first, pq[2 * i], 0.0), jnp.where(first, 0.0, pq[2 * i + 1])], axis=0)
              for i in idx]
    q_pair = [pltpu.roll(jnp.where(first, pq[2 * i + 1], pq[2 * i]), HEAD_DIM, 1) for i in idx]
    t_split = [jnp.concatenate(_split(states[i]), axis=0) for i in idx]
    pr = [jnp.concatenate([p_pair[i], rt_h[i][0], rt_h[i][1]], axis=0).astype(BF16) for i in idx]
    prt = [jnp.dot(jnp.concatenate([pr[i], pr[i]], axis=1), t_split[i], preferred_element_type=F32) for i in idx]
    pt = [m[0:2 * n] for m in prt]
    rtt = [m[2 * n:] for m in prt]
    u = [jnp.where(first, pt[i][0:n], pt[i][n:]) + q_pair[i] for i in idx]
    uv = [jnp.concatenate([u[i], v_g[i]], axis=0) for i in idx]
    yh = [_dot(sr[i][j], uv[i]) for i, j in heads]
    ys = [jnp.where(first, rtt[i][0:n] + yh[2 * i], rtt[i][n:] + yh[2 * i + 1]) for i in idx]
    t_new = [_dot(jnp.concatenate([scaled[d][4][:, sls[g]], scaled[d][5][:, sls[g]]], axis=0).T, uv[i])
             for i, (d, g) in enumerate(items)]
    for i, (d, g) in enumerate(items):
        w_col = jnp.broadcast_to(w_tot[d][:, sls[g]], (LANES, LANES)).T
        st_refs[d][0, g] = jnp.where(blockdiag, w_col * states[i] + t_new[i], 0.0)
    bd = bd_ref[...]
    inv = 1.0 / HEAD_DIM
    y = [jnp.concatenate(ys[d * npairs:(d + 1) * npairs], axis=1) for d in range(2)]
    mean = [_head_sums(y[d], bd) * inv for d in range(2)]
    dev = [y[d] - mean[d] for d in range(2)]
    var = [_head_sums(dev[d] * dev[d], bd) * inv for d in range(2)]
    bonus = [_head_sums(raw[d][0] * raw[d][1] * rk_ref[...], bd) * raw[d][2] for d in range(2)]
    for d in range(2):
        y_refs[d][0] = (dev[d] * lax.rsqrt(var[d] + GN_EPS) * gw_ref[...] + gb_ref[...] + bonus[d]).astype(BF16)


def _scan(zr, state_fwd, state_bwd, p):
    bsz, length, ncol = zr.shape
    width = p["w0"].shape[1]
    nc = length // CHUNK
    npairs = width // LANES
    per_chunk = CHUNK // HALO
    last_halo = length // HALO - 1
    fmap = lambda b, c: (b, c, 0)
    bmap = lambda b, c: (b, nc - 1 - c, 0)
    fnext = lambda c: jnp.minimum(c + 1, nc - 1)
    bnext = lambda c: jnp.maximum(nc - 2 - c, 0)
    fnext_map = lambda b, c: (b, fnext(c), 0)
    bnext_map = lambda b, c: (b, bnext(c), 0)
    prev_map = lambda b, c: (b, jnp.maximum(fnext(c) * per_chunk - 1, 0), 0)
    next_map = lambda b, c: (b, jnp.minimum((bnext(c) + 1) * per_chunk, last_halo), 0)
    full = lambda shape: pl.BlockSpec(shape, lambda b, c: (0,) * len(shape))
    st_spec = pl.BlockSpec((1, npairs, LANES, LANES), lambda b, c: (b, 0, 0, 0))
    y_shape = jax.ShapeDtypeStruct((bsz, length, width), BF16)
    st_shape = jax.ShapeDtypeStruct((bsz, npairs, LANES, LANES), F32)
    y_f, y_b, st_f, st_b = pl.pallas_call(
        _scan_kernel,
        grid=(bsz, nc),
        in_specs=[pl.BlockSpec((1, CHUNK, ncol), fnext_map), pl.BlockSpec((1, HALO, ncol), prev_map),
                  pl.BlockSpec((1, CHUNK, ncol), bnext_map), pl.BlockSpec((1, HALO, ncol), next_map),
                  pl.BlockSpec((1, CHUNK, ncol), lambda b, c: (b, 0, 0)),
                  pl.BlockSpec((1, CHUNK, ncol), lambda b, c: (b, nc - 1, 0)),
                  st_spec, st_spec,
                  full((2, 3, width)), full((1, LANES)), full((1, LANES)), full((2, width)),
                  full((2, LANES, width)), full((2, width)), full((2, LANES, width)), full((1, width)),
                  full((1, width)), full((1, width)), full((1, width)), full((1, width)), full((LANES, LANES))],
        out_specs=[pl.BlockSpec((1, CHUNK, width), fmap), pl.BlockSpec((1, CHUNK, width), bmap), st_spec, st_spec],
        out_shape=[y_shape, y_shape, st_shape, st_shape],
        scratch_shapes=[pltpu.VMEM((2, 6, CHUNK, width), F32)],
        compiler_params=_params("parallel", "arbitrary"),
        name="wkv_scan",
    )(zr, zr, zr, zr, zr, zr, state_fwd, state_bwd, p["mu"], p["mu_w"], p["mu_a"], p["w0"], p["w2"], p["a0"], p["a2"],
      p["k_k"], p["k_a"], p["r_k"], p["gn_w"], p["gn_b"], p["bd"])
    return (y_f, y_b), (st_f, st_b)


def _mix_kernel(period, tm, yf_ref, yb_ref, glo_ref, zc_ref, g2_ref, cw_ref, o_ref):
    gate = _dot(jax.nn.sigmoid(glo_ref[0]), g2_ref[...])
    yr = (yf_ref[0].astype(F32) + yb_ref[0].astype(F32)) * gate
    zc = zc_ref[0].astype(F32)
    u, gate_b, gate_c = zc[:, 0:256], zc[:, 256:512], zc[:, 512:768]
    hc = gate_c * u
    pos = lax.broadcasted_iota(jnp.int32, (tm, 1), 0) % period
    prev = jnp.where(pos == 0, 0.0, pltpu.roll(hc, 1, 0))
    nxt = jnp.where(pos == period - 1, 0.0, pltpu.roll(hc, tm - 1, 0))
    conv = prev * cw_ref[0:1, :] + hc * cw_ref[1:2, :] + nxt * cw_ref[2:3, :]
    o_ref[0, :, 0:512] = yr.astype(BF16)
    o_ref[0, :, 512:768] = (gate_b * conv).astype(BF16)


def _mix(ys_f, ys_b, zr, zc, g2, conv_w, period, tm):
    bsz, length, _ = zc.shape
    row = lambda b, i: (b, i, 0)
    glo_block = (N_RWKV_COLS - 128) // 128
    return pl.pallas_call(
        functools.partial(_mix_kernel, period, tm),
        grid=(bsz, length // tm),
        in_specs=[pl.BlockSpec((1, tm, 512), row), pl.BlockSpec((1, tm, 512), row),
                  pl.BlockSpec((1, tm, 128), lambda b, i: (b, i, glo_block)),
                  pl.BlockSpec((1, tm, N_CONV_COLS), row),
                  pl.BlockSpec((128, 512), lambda b, i: (0, 0)),
                  pl.BlockSpec((3, 256), lambda b, i: (0, 0))],
        out_specs=pl.BlockSpec((1, tm, 768), row),
        out_shape=jax.ShapeDtypeStruct((bsz, length, 768), BF16),
        compiler_params=_params("parallel", "parallel"),
        name="gate_conv",
    )(ys_f, ys_b, zr, zc, g2, conv_w)


def _chan_dft_kernel(f_ref, w_ref, o_ref):
    xcs = jnp.dot(f_ref[0].astype(BF16), w_ref[...], preferred_element_type=F32)
    o_ref[0, 0] = xcs[:, 0:256].astype(BF16)
    o_ref[0, 1] = xcs[:, 256:512].astype(BF16)


def _pos_dft_kernel(nb, cs_ref, x_ref, o_ref, acc_ref):
    kstep = pl.program_id(1)

    @pl.when(kstep == 0)
    def _():
        acc_ref[...] = jnp.zeros_like(acc_ref)

    cs = cs_ref[...]
    for b in range(nb):
        acc_ref[b] += jnp.dot(cs, x_ref[b], preferred_element_type=F32)

    @pl.when(kstep == pl.num_programs(1) - 1)
    def _():
        o_ref[...] = acc_ref[...].astype(BF16)


def _fourier(zf, chan_w, pos_cs, tm, tk):
    bsz, length, ch = zf.shape
    tr = min(512, length)
    xcs = pl.pallas_call(
        _chan_dft_kernel,
        grid=(bsz, length // tr),
        in_specs=[pl.BlockSpec((1, tr, ch), lambda b, i: (b, i, 0)),
                  pl.BlockSpec((ch, 2 * ch), lambda b, i: (0, 0))],
        out_specs=pl.BlockSpec((1, 2, tr, ch), lambda b, i: (b, 0, i, 0)),
        out_shape=jax.ShapeDtypeStruct((bsz, 2, length, ch), BF16),
        compiler_params=_params("parallel", "parallel"),
        name="chan_dft",
    )(zf, chan_w)
    xcs = xcs.reshape(bsz, 2 * length, ch)
    return pl.pallas_call(
        functools.partial(_pos_dft_kernel, bsz),
        grid=(length // tm, 2 * length // tk),
        in_specs=[pl.BlockSpec((tm, tk), lambda i, kk: (i, kk)),
                  pl.BlockSpec((bsz, tk, ch), lambda i, kk: (0, kk, 0))],
        out_specs=pl.BlockSpec((bsz, tm, ch), lambda i, kk: (0, i, 0)),
        out_shape=jax.ShapeDtypeStruct((bsz, length, ch), BF16),
        scratch_shapes=[pltpu.VMEM((bsz, tm, ch), F32)],
        compiler_params=_params("parallel", "arbitrary"),
        name="pos_dft",
    )(pos_cs, xcs)


def _dft_tables(length):
    m = jnp.arange(length, dtype=jnp.int32)[None, :]
    l1 = jnp.arange(length // 64, dtype=jnp.int32)[:, None] * 64
    l2 = jnp.arange(64, dtype=jnp.int32)[:, None]
    ang = lambda l: ((l * m) % length).astype(F32) * (2.0 * jnp.pi / length)
    c1, s1, c2, s2 = jnp.cos(ang(l1)), jnp.sin(ang(l1)), jnp.cos(ang(l2)), jnp.sin(ang(l2))
    scale = 1.0 / jnp.sqrt(jnp.float32(length))
    cos = (c1[:, None, :] * c2[None] - s1[:, None, :] * s2[None]).reshape(length, length) * scale
    sin = (s1[:, None, :] * c2[None] + c1[:, None, :] * s2[None]).reshape(length, length) * scale
    return jnp.concatenate([cos, sin], axis=1).astype(BF16)


def _chan_dft_weights(ngroups):
    j = jnp.arange(HEAD_DIM, dtype=jnp.int32)
    ang = ((j[:, None] * j[None, :]) % HEAD_DIM).astype(F32) * (2.0 * jnp.pi / HEAD_DIM)
    eye = jnp.eye(ngroups, dtype=F32)
    scale = 1.0 / jnp.sqrt(jnp.float32(HEAD_DIM))
    cc = jnp.kron(eye, jnp.cos(ang) * scale)
    sc = jnp.kron(eye, jnp.sin(ang) * scale)
    return jnp.concatenate([cc, -sc], axis=1).astype(BF16)


OUTPROJ_SPLIT = 4


def _outproj_kernel(yrc_ref, yf_ref, w_ref, x_ref, gn1_ref, gate_ref, gn2_ref, sc_ref, sh_ref, xo_ref, h_ref):
    nrc = yrc_ref.shape[2]
    tm = x_ref.shape[1]
    parts = [slice(s, s + tm // OUTPROJ_SPLIT) for s in range(0, tm, tm // OUTPROJ_SPLIT)]
    outs = [jnp.dot(yrc_ref[0, rows, :], w_ref[0:nrc, :], preferred_element_type=F32) +
            jnp.dot(yf_ref[0, rows, :], w_ref[nrc:, :], preferred_element_type=F32) for rows in parts]
    for rows, o in zip(parts, outs):
        xn = x_ref[0, rows, :] + gate_ref[0] * _rms(o, gn1_ref[...])
        xo_ref[0, rows, :] = xn
        h_ref[0, rows, :] = (_rms(xn, gn2_ref[...]) * (1.0 + sc_ref[0]) + sh_ref[0]).astype(BF16)


def _outproj(yrc, yf, w, x, gn1, gate, gn2, sc, sh, tm):
    bsz, length, d = x.shape
    row = lambda b, i: (b, i, 0)
    vec = pl.BlockSpec((1, d), lambda b, i: (0, 0))
    mod = pl.BlockSpec((1, 1, d), lambda b, i: (b, 0, 0))
    return pl.pallas_call(
        _outproj_kernel,
        grid=(bsz, length // tm),
        in_specs=[pl.BlockSpec((1, tm, yrc.shape[2]), row), pl.BlockSpec((1, tm, yf.shape[2]), row),
                  pl.BlockSpec(w.shape, lambda b, i: (0, 0)), pl.BlockSpec((1, tm, d), row),
                  vec, mod, vec, mod, mod],
        out_specs=[pl.BlockSpec((1, tm, d), row), pl.BlockSpec((1, tm, d), row)],
        out_shape=[jax.ShapeDtypeStruct((bsz, length, d), F32), jax.ShapeDtypeStruct((bsz, length, d), BF16)],
        compiler_params=_params("parallel", "parallel"),
        name="outproj",
    )(yrc, yf, w, x, gn1, gate, gn2, sc, sh)


FF_SUB = 256


def _swiglu_partial(h, wg, wu, wd, tf):
    def up(c, n):
        return (jnp.dot(h, wg(c, n), preferred_element_type=F32), jnp.dot(h, wu(c, n), preferred_element_type=F32))

    def down(gu, c, n):
        return jnp.dot((_silu(gu[0]) * gu[1]).astype(BF16), wd(c, n), preferred_element_type=F32)

    out = None
    prev = None
    for cut in [(c, min(FF_SUB, tf - c)) for c in range(0, tf, FF_SUB)]:
        gu = up(*cut)
        if prev is not None:
            part = down(*prev)
            out = part if out is None else out + part
        prev = (gu,) + cut
    part = down(*prev)
    return part if out is None else out + part


def _ffn_kernel(h_ref, wg_ref, wu_ref, wd_ref, x_ref, gn_ref, gate_ref, o_ref, acc_ref):
    j = pl.program_id(2)

    @pl.when(j == 0)
    def _():
        acc_ref[...] = jnp.zeros_like(acc_ref)

    acc_ref[...] += _swiglu_partial(h_ref[0], lambda c, n: wg_ref[:, c:c + n], lambda c, n: wu_ref[:, c:c + n],
                                    lambda c, n: wd_ref[c:c + n, :], wg_ref.shape[1])

    @pl.when(j == pl.num_programs(2) - 1)
    def _():
        o_ref[0] = x_ref[0] + gate_ref[0] * _rms(acc_ref[...], gn_ref[...])


def _ffn(h, wg, wu, wd, x, gn, gate, tm, tf):
    bsz, length, d = x.shape
    ff = wg.shape[1]
    row = lambda b, i, j: (b, i, 0)
    return pl.pallas_call(
        _ffn_kernel,
        grid=(bsz, length // tm, ff // tf),
        in_specs=[pl.BlockSpec((1, tm, d), row),
                  pl.BlockSpec((d, tf), lambda b, i, j: (0, j)),
                  pl.BlockSpec((d, tf), lambda b, i, j: (0, j)),
                  pl.BlockSpec((tf, d), lambda b, i, j: (j, 0)),
                  pl.BlockSpec((1, tm, d), row),
                  pl.BlockSpec((1, d), lambda b, i, j: (0, 0)),
                  pl.BlockSpec((1, 1, d), lambda b, i, j: (b, 0, 0))],
        out_specs=pl.BlockSpec((1, tm, d), row),
        out_shape=jax.ShapeDtypeStruct((bsz, length, d), F32),
        scratch_shapes=[pltpu.VMEM((tm, d), F32)],
        compiler_params=_params("parallel", "parallel", "arbitrary"),
        name="ffn",
    )(h, wg, wu, wd, x, gn, gate)


MOE_TILE = 512
MOE_BLOCK = 128
MOE_ALIGN = 16
MOE_TM = 1024


def _router_kernel(x_ref, gn_ref, sc_ref, sh_ref, w_ref, b_ref, tri_ref, routes_ref, routest_ref, cnt_ref):
    h = _rms(x_ref[0], gn_ref[...]) * (1.0 + sc_ref[0]) + sh_ref[0]
    logits = jnp.dot(h, w_ref[...], preferred_element_type=F32, precision=lax.Precision.HIGHEST) + b_ref[...]
    ne = float(logits.shape[1])
    idx = lax.broadcasted_iota(jnp.int32, logits.shape, 1).astype(F32)
    m1 = jnp.max(logits, axis=1, keepdims=True)
    i1 = jnp.min(jnp.where(logits == m1, idx, ne), axis=1, keepdims=True)
    rest = jnp.where(idx == i1, -jnp.inf, logits)
    m2 = jnp.max(rest, axis=1, keepdims=True)
    i2 = jnp.min(jnp.where(rest == m2, idx, ne), axis=1, keepdims=True)
    e2 = jnp.exp(m2 - m1)
    p1 = 1.0 / (1.0 + e2)
    p2 = e2 / (1.0 + e2)
    sel = jnp.where(idx == i1, 1.0, jnp.where(idx == i2, 1.0, 0.0))
    before = jnp.dot(tri_ref[...], sel.astype(BF16), preferred_element_type=F32)
    rank1 = jnp.sum(jnp.where(idx == i1, before, 0.0), axis=1, keepdims=True)
    rank2 = jnp.sum(jnp.where(idx == i2, before, 0.0), axis=1, keepdims=True)
    routes = jnp.zeros_like(logits)
    for lane, val in enumerate((i1, i2, rank1, rank2, p1, p2)):
        routes = jnp.where(idx == float(lane), val, routes)
    routes_ref[0] = routes
    routest_ref[0] = jnp.transpose(routes)[0:N_EXPERTS, :]
    cnt_ref[0, 0] = jnp.sum(sel, axis=0, keepdims=True)


def _router(x, gn, sc, sh, w, b):
    bsz, length, d = x.shape
    tm = MOE_TILE
    pad = LANES - w.shape[1]
    w = jnp.pad(w, ((0, 0), (0, pad)))
    b = jnp.pad(b, ((0, 0), (0, pad)), constant_values=-1e30)
    ne = LANES
    t = jnp.arange(tm, dtype=jnp.int32)
    tri = (t[None, :] < t[:, None]).astype(BF16)
    row = lambda bb, i: (bb, i, 0)
    vec = pl.BlockSpec((1, d), lambda bb, i: (0, 0))
    mod = pl.BlockSpec((1, 1, d), lambda bb, i: (bb, 0, 0))
    nt = length // tm
    return pl.pallas_call(
        _router_kernel,
        grid=(bsz, nt),
        in_specs=[pl.BlockSpec((1, tm, d), row), vec, mod, mod,
                  pl.BlockSpec((d, ne), lambda bb, i: (0, 0)), pl.BlockSpec((1, ne), lambda bb, i: (0, 0)),
                  pl.BlockSpec((tm, tm), lambda bb, i: (0, 0))],
        out_specs=[pl.BlockSpec((1, tm, ne), row),
                   pl.BlockSpec((1, N_EXPERTS, tm), lambda bb, i: (bb, 0, i)),
                   pl.BlockSpec((1, 1, 1, ne), lambda bb, i: (bb, i, 0, 0))],
        out_shape=[jax.ShapeDtypeStruct((bsz, length, ne), F32),
                   jax.ShapeDtypeStruct((bsz, N_EXPERTS, length), F32),
                   jax.ShapeDtypeStruct((bsz, nt, 1, ne), F32)],
        compiler_params=_params("parallel", "parallel"),
        name="router",
    )(x, gn, sc, sh, w, b, tri)


_MOE_SHIFT = MOE_BLOCK.bit_length() - 1
_MOE_STAGE_ROWS = -(-(2 * MOE_TILE + N_EXPERTS * (MOE_ALIGN - 1)) // LANES) * LANES


def _align_rows(c):
    return ((c + (MOE_ALIGN - 1)) >> (MOE_ALIGN.bit_length() - 1)) << (MOE_ALIGN.bit_length() - 1)


def _run_copies(start, to_sorted, tile, off_ref, cnt_ref, stage_ref, sorted_ref, sem):
    def run(stage_row, sorted_row, nrows):
        stage = stage_ref.at[pl.ds(pl.multiple_of(stage_row, MOE_ALIGN), nrows)]
        sort = sorted_ref.at[pl.ds(pl.multiple_of(sorted_row, MOE_ALIGN), nrows)]
        cp = pltpu.make_async_copy(stage, sort, sem) if to_sorted else pltpu.make_async_copy(sort, stage, sem)
        if start:
            cp.start()
        else:
            cp.wait()

    stage0 = jnp.int32(0)
    for e in range(N_EXPERTS):
        rows = _align_rows(cnt_ref[tile * N_EXPERTS + e])
        sorted0 = off_ref[tile * N_EXPERTS + e]
        nfull = rows >> _MOE_SHIFT

        def full(rb, carry, stage0=stage0, sorted0=sorted0):
            run(stage0 + rb * MOE_BLOCK, sorted0 + rb * MOE_BLOCK, MOE_BLOCK)
            return carry

        lax.fori_loop(0, nfull, full, 0)
        done = nfull * MOE_BLOCK
        piece = MOE_BLOCK // 2
        while piece >= MOE_ALIGN:
            has = (rows & piece) != 0

            @pl.when(has)
            def _(done=done, piece=piece, stage0=stage0, sorted0=sorted0):
                run(stage0 + done, sorted0 + done, piece)

            done = done + jnp.where(has, piece, 0)
            piece //= 2
        stage0 = stage0 + rows


def _staging_rows(tile, cnt_ref, i1, i2, rank1, rank2):
    row1 = jnp.zeros_like(i1)
    row2 = jnp.zeros_like(i2)
    stage0 = jnp.int32(0)
    for e in range(N_EXPERTS):
        first_row = stage0.astype(F32)
        row1 = jnp.where(i1 == float(e), first_row, row1)
        row2 = jnp.where(i2 == float(e), first_row, row2)
        stage0 = stage0 + _align_rows(cnt_ref[tile * N_EXPERTS + e])
    return row1 + rank1, row2 + rank2


def _dispatch_kernel(off_ref, cnt_ref, h_ref, routest_ref, hs_init_ref, hs_ref, buf_ref, sem):
    del hs_init_ref
    tile = pl.program_id(0)
    rt = routest_ref[0]
    row1, row2 = _staging_rows(tile, cnt_ref, rt[0:1], rt[1:2], rt[2:3], rt[3:4])
    rows = lax.broadcasted_iota(jnp.int32, (LANES, 1), 0)
    onehot = jnp.concatenate(
        [jnp.where(row1 == (rows + part * LANES).astype(F32), 1.0,
                   jnp.where(row2 == (rows + part * LANES).astype(F32), 1.0, 0.0)).astype(BF16)
         for part in range(_MOE_STAGE_ROWS // LANES)], axis=0)
    buf_ref[...] = jnp.dot(onehot, h_ref[...], preferred_element_type=F32).astype(BF16)
    _run_copies(True, True, tile, off_ref, cnt_ref, buf_ref, hs_ref, sem)
    _run_copies(False, True, tile, off_ref, cnt_ref, buf_ref, hs_ref, sem)


def _dispatch(h, rankt, off, cnt, nrows):
    n, d = h.shape
    ntb = rankt.shape[2] // MOE_TILE
    grid_spec = pltpu.PrefetchScalarGridSpec(
        num_scalar_prefetch=2, grid=(n // MOE_TILE,),
        in_specs=[pl.BlockSpec((MOE_TILE, d), lambda t, o, c: (t, 0)),
                  pl.BlockSpec((1, N_EXPERTS, MOE_TILE), lambda t, o, c: (t // ntb, 0, t % ntb)),
                  pl.BlockSpec(memory_space=pl.ANY)],
        out_specs=pl.BlockSpec(memory_space=pl.ANY),
        scratch_shapes=[pltpu.VMEM((_MOE_STAGE_ROWS, d), BF16), pltpu.SemaphoreType.DMA(())])
    return pl.pallas_call(
        _dispatch_kernel, grid_spec=grid_spec,
        out_shape=jax.ShapeDtypeStruct((nrows, d), BF16),
        input_output_aliases={4: 0},
        compiler_params=_params("arbitrary"),
        name="moe_dispatch",
    )(off, cnt, h, rankt, jnp.zeros((nrows, d), BF16))


def _gffn_kernel(te_ref, tv_ref, h_ref, wg_ref, wu_ref, wd_ref, o_ref, acc_ref):
    q = pl.program_id(0)
    j = pl.program_id(1)

    @pl.when(j == 0)
    def _():
        acc_ref[...] = jnp.zeros_like(acc_ref)

    @pl.when(tv_ref[q] > 0)
    def _():
        acc_ref[...] += _swiglu_partial(h_ref[...], lambda c, n: wg_ref[0, :, c:c + n],
                                        lambda c, n: wu_ref[0, :, c:c + n], lambda c, n: wd_ref[0, c:c + n, :],
                                        wg_ref.shape[2])

    @pl.when(j == pl.num_programs(1) - 1)
    def _():
        o_ref[...] = acc_ref[...].astype(BF16)


def _gffn(hs, te, tv, wg, wu, wd, tf):
    nrows, d = hs.shape
    ff = wg.shape[2]
    grid_spec = pltpu.PrefetchScalarGridSpec(
        num_scalar_prefetch=2, grid=(nrows // MOE_TM, ff // tf),
        in_specs=[pl.BlockSpec((MOE_TM, d), lambda q, j, te, tv: (q, 0)),
                  pl.BlockSpec((1, d, tf), lambda q, j, te, tv: (te[q], 0, j)),
                  pl.BlockSpec((1, d, tf), lambda q, j, te, tv: (te[q], 0, j)),
                  pl.BlockSpec((1, tf, d), lambda q, j, te, tv: (te[q], j, 0))],
        out_specs=pl.BlockSpec((MOE_TM, d), lambda q, j, te, tv: (q, 0)),
        scratch_shapes=[pltpu.VMEM((MOE_TM, d), F32)])
    return pl.pallas_call(
        _gffn_kernel, grid_spec=grid_spec,
        out_shape=jax.ShapeDtypeStruct((nrows, d), BF16),
        compiler_params=pltpu.CompilerParams(dimension_semantics=("arbitrary", "arbitrary"),
                                             vmem_limit_bytes=VMEM_LIMIT,
                                             allow_input_fusion=[False, False, False, True, True, True]),
        name="moe_experts",
    )(te, tv, hs, wg, wu, wd)


def _combine_kernel(off_ref, cnt_ref, ys_ref, routes_ref, x_ref, gn_ref, gate_ref, o_ref, buf_ref, sem):
    tile = pl.program_id(0)

    @pl.when(tile == 0)
    def _():
        buf_ref[...] = jnp.zeros_like(buf_ref)

    _run_copies(True, False, tile, off_ref, cnt_ref, buf_ref, ys_ref, sem)
    routes = routes_ref[0]
    col = lambda j: routes[:, j:j + 1]
    row1, row2 = _staging_rows(tile, cnt_ref, col(0), col(1), col(2), col(3))
    lanes = lax.broadcasted_iota(jnp.int32, (1, LANES), 1)
    scatter = jnp.concatenate(
        [(jnp.where(row1 == (lanes + part * LANES).astype(F32), col(4), 0.0) +
          jnp.where(row2 == (lanes + part * LANES).astype(F32), col(5), 0.0)).astype(BF16)
         for part in range(_MOE_STAGE_ROWS // LANES)], axis=1)
    _run_copies(False, False, tile, off_ref, cnt_ref, buf_ref, ys_ref, sem)
    mixed = jnp.dot(scatter, buf_ref[...], preferred_element_type=F32)
    o_ref[0] = x_ref[0] + gate_ref[0] * _rms(mixed, gn_ref[...])


def _combine(ys, routes, x, gn, gate, off, cnt):
    bsz, length, d = x.shape
    ntb = length // MOE_TILE
    tok = lambda t, o, c: (t // ntb, t % ntb, 0)
    grid_spec = pltpu.PrefetchScalarGridSpec(
        num_scalar_prefetch=2, grid=(bsz * ntb,),
        in_specs=[pl.BlockSpec(memory_space=pl.ANY),
                  pl.BlockSpec((1, MOE_TILE, LANES), tok),
                  pl.BlockSpec((1, MOE_TILE, d), tok),
                  pl.BlockSpec((1, d), lambda t, o, c: (0, 0)),
                  pl.BlockSpec((1, 1, d), lambda t, o, c: (t // ntb, 0, 0))],
        out_specs=pl.BlockSpec((1, MOE_TILE, d), tok),
        scratch_shapes=[pltpu.VMEM((_MOE_STAGE_ROWS, d), BF16), pltpu.SemaphoreType.DMA(())])
    return pl.pallas_call(
        _combine_kernel, grid_spec=grid_spec,
        out_shape=jax.ShapeDtypeStruct((bsz, length, d), F32),
        compiler_params=_params("arbitrary"),
        name="moe_combine",
    )(off, cnt, ys, routes, x, gn, gate)


def _route_plan(counts, ntokens):
    ntiles = counts.shape[0]
    rows = (counts + (MOE_ALIGN - 1)) // MOE_ALIGN * MOE_ALIGN
    seg = (rows.sum(0) + MOE_BLOCK + MOE_TM - 1) // MOE_TM * MOE_TM
    seg_end = jnp.cumsum(seg)
    off = (seg_end - seg)[None, :] + jnp.cumsum(rows, axis=0) - rows
    nq = -(-(2 * ntokens + ntiles * N_EXPERTS * (MOE_ALIGN - 1) + N_EXPERTS * (MOE_BLOCK + MOE_TM - 1)) // MOE_TM)
    q = jnp.arange(nq, dtype=jnp.int32) * MOE_TM
    te = jnp.minimum(jnp.sum(q[:, None] >= seg_end[None, :], axis=1), N_EXPERTS - 1).astype(jnp.int32)
    tv = (q < seg_end[-1]).astype(jnp.int32)
    return off.reshape(-1).astype(jnp.int32), te, tv, nq * MOE_TM


def _moe(h, x, gn_pre, sc, sh, router_w, router_b, wg, wu, wd, gn_post, gate, tf):
    bsz, length, d = x.shape
    routes, routest, cnt = _router(x, gn_pre, sc, sh, router_w, router_b)
    counts = cnt[:, :, 0, :N_EXPERTS].astype(jnp.int32).reshape(-1, N_EXPERTS)
    off, te, tv, nrows = _route_plan(counts, bsz * length)
    cnt_flat = counts.reshape(-1)
    hs = _dispatch(h.reshape(bsz * length, d), routest, off, cnt_flat, nrows)
    ys = _gffn(hs, te, tv, wg, wu, wd, tf)
    return _combine(ys, routes, x, gn_post, gate, off, cnt_flat)


def _sincos_2d(rows, cols, dim):
    quarter = dim // 4
    omega = 1.0 / (POS_BASE ** (jnp.arange(quarter, dtype=F32) / quarter))

    def axis_emb(n):
        ang = jnp.arange(n, dtype=F32)[:, None] * omega[None, :]
        return jnp.concatenate([jnp.sin(ang), jnp.cos(ang)], axis=-1)

    er, ec = axis_emb(rows), axis_emb(cols)
    emb = jnp.concatenate([jnp.broadcast_to(er[:, None, :], (rows, cols, dim // 2)),
                           jnp.broadcast_to(ec[None, :, :], (rows, cols, dim // 2))], axis=-1)
    return emb.reshape(rows * cols, dim)


def _head_block_ones(width):
    h = jnp.arange(width, dtype=jnp.int32) // HEAD_DIM
    return (h[:, None] == h[None, :]).astype(BF16)


def _pad_lora(w2, d):
    z = jnp.zeros_like(w2[0])
    return jnp.concatenate([w2[0], z], axis=0) if d == 0 else jnp.concatenate([z, w2[1]], axis=0)


def _tile(length, pref):
    return pref if length % pref == 0 else length


def kernel(x, c, ctx, c_ctx, ada_w, ada_b, norm_g, w_in, w_out, rwkv_mu, rwkv_mu_w, rwkv_mu_a, rwkv_w0, rwkv_w2,
           rwkv_a0, rwkv_a2, rwkv_g2, rwkv_k_k, rwkv_k_a, rwkv_r_k, rwkv_gn_w, rwkv_gn_b, conv_w, ffn_w_gate,
           ffn_w_up, ffn_w_down, router_w, router_b, moe_w_gate, moe_w_up, moe_w_down):
    bsz, length, dim = x.shape
    ctx_len = ctx.shape[1]
    depth = ada_w.shape[0]
    d_rwkv = rwkv_w0.shape[-1]

    pos = _sincos_2d(length // GRID_W, GRID_W, dim).astype(x.dtype)
    bd = _head_block_ones(LANES)
    chan_w = _chan_dft_weights(N_FOUR_COLS // HEAD_DIM)
    dft_lat = _dft_tables(length)
    dft_ctx = _dft_tables(ctx_len)

    cvec = jnp.zeros((16, dim), F32).at[:bsz].set(c).at[bsz].set(c_ctx)
    mod = _adaln(cvec, ada_w, ada_b)
    zero_state = jnp.zeros((bsz, d_rwkv // LANES, LANES, LANES), F32)
    xc = ctx

    def mods(l, lo, hi):
        return [mod[l, lo:hi, i * dim:(i + 1) * dim][:, None, :] for i in range(6)]

    def mixers(l, zr, zc, zf, states0, period, dft, need_y=True):
        tm = _tile(zr.shape[1], 512)
        p = {"mu": rwkv_mu[l], "mu_w": rwkv_mu_w[l].reshape(1, -1), "mu_a": rwkv_mu_a[l].reshape(1, -1),
             "w0": rwkv_w0[l], "w2": jnp.stack([_pad_lora(rwkv_w2[l], d) for d in range(2)]), "a0": rwkv_a0[l],
             "a2": jnp.stack([_pad_lora(rwkv_a2[l], d) for d in range(2)]), "k_k": rwkv_k_k[l][None],
             "k_a": rwkv_k_a[l][None], "r_k": rwkv_r_k[l].reshape(1, -1), "gn_w": rwkv_gn_w[l][None],
             "gn_b": rwkv_gn_b[l][None], "bd": bd}
        ys, states = _scan(zr, states0[0], states0[1], p)
        if not need_y:
            return None, None, states
        yrc = _mix(ys[0], ys[1], zr, zc, rwkv_g2[l], conv_w[l], period, tm)
        yf = _fourier(zf, chan_w, dft, _tile(zf.shape[1], 512), _tile(2 * zf.shape[1], 2048))
        return yrc, yf, states

    def channel_mixer(l, h2, xn, gn3, gate2, sc2, sh2, gn2):
        tm = _tile(xn.shape[1], 512)
        i = l // 2
        if l % 2 == 0:
            return _ffn(h2, ffn_w_gate[i].astype(BF16), ffn_w_up[i].astype(BF16), ffn_w_down[i].astype(BF16),
                        xn, gn3, gate2, tm, 1408)
        return _moe(h2, xn, gn2, sc2, sh2, router_w[i], router_b[i][None], moe_w_gate[i].astype(BF16),
                    moe_w_up[i].astype(BF16), moe_w_down[i].astype(BF16), gn3, gate2, 1792)

    for l in range(depth):
        last = l == depth - 1
        w_in_l = w_in[l].astype(BF16)
        w_out_l = w_out[l].astype(BF16)
        gn = [norm_g[l, i][None] for i in range(4)]
        sh1, sc1, g1, sh2, sc2, g2 = mods(l, 0, bsz)
        csh1, csc1, cg1, csh2, csc2, cg2 = mods(l, bsz, bsz + 1)

        xc_flat = xc.reshape(1, bsz * ctx_len, dim)
        tmc = _tile(bsz * ctx_len, 512)
        zr, zc, zf = _inproj(xc_flat, None, gn[0], csc1, csh1, w_in_l, tmc)
        unflat = lambda t: t.reshape(bsz, ctx_len, t.shape[-1])
        yrc, yf, ctx_states = mixers(l, unflat(zr), unflat(zc), unflat(zf), (zero_state, zero_state), ctx_len,
                                     dft_ctx, need_y=not last)
        if not last:
            flat = lambda t: t.reshape(1, bsz * ctx_len, t.shape[-1])
            xcn, hc2 = _outproj(flat(yrc), flat(yf), w_out_l, xc_flat, gn[1], cg1, gn[2], csc2, csh2, tmc)
            xc = channel_mixer(l, hc2, xcn, gn[3], cg2, csc2, csh2, gn[2]).reshape(bsz, ctx_len, dim)

        tm = _tile(length, 512)
        if l == 0:
            zr, zc, zf, x = _inproj(x, pos, gn[0], sc1, sh1, w_in_l, tm)
        else:
            zr, zc, zf = _inproj(x, None, gn[0], sc1, sh1, w_in_l, tm)
        yrc, yf, _ = mixers(l, zr, zc, zf, ctx_states, GRID_W, dft_lat)
        xn, h2 = _outproj(yrc, yf, w_out_l, x, gn[1], g1, gn[2], sc2, sh2, _tile(length, 512))
        x = channel_mixer(l, h2, xn, gn[3], g2, sc2, sh2, gn[2])
    return x
```
